```python
import math
import jax, jax.numpy as jnp
from jax import lax
import numpy as np

D_MODEL = 2048
BATCH = 8
SEQ = 2048
DEPTH = 2

ATT_HEADS = 8
ATT_KV_HEADS = 2
ATT_HEAD_DIM = 128
ATT_GROUP = ATT_HEADS // ATT_KV_HEADS
ATT_W = ATT_HEADS * ATT_HEAD_DIM
KV_W = ATT_KV_HEADS * ATT_HEAD_DIM
IDX_HEADS = 16
IDX_HEAD_DIM = 64
IDX_Q_RANK = 512
TOPK_MAX = 256
Q_BLOCK = 128
HGRN_HEADS = 8
HGRN_KEY_DIM = 128
HGRN_VAL_DIM = 128
HGRN_KW = HGRN_HEADS * HGRN_KEY_DIM
HGRN_VW = HGRN_HEADS * HGRN_VAL_DIM
HGRN_CHUNK = 64
REL_BUCKETS = 32
REL_MAX_DIST = 128
D_FF_DENSE = 5632
N_EXPERTS = 8
TOP_K_EXPERTS = 2
D_FF_EXPERT = 7168
EPS = 1e-6

N_DENSE = (DEPTH + 1) // 2
N_MOE = DEPTH // 2

kernel_name = "hybrid_dsa_hgrn2_gated_moe_block"


def _split_points():
    sizes = [ATT_W, KV_W, KV_W, IDX_Q_RANK, IDX_HEAD_DIM, IDX_HEADS,
             HGRN_KW, HGRN_KW, HGRN_VW, HGRN_VW, D_MODEL, D_MODEL]
    return [int(s) for s in np.cumsum(sizes)[:-1]], int(sum(sizes))


def rmsnorm(x, g):
    xf = x.astype(jnp.float32)
    y = xf * lax.rsqrt(jnp.mean(xf * xf, axis=-1, keepdims=True) + EPS)
    return (y * g.astype(jnp.float32)).astype(x.dtype)


def layernorm(x, g, b):
    xf = x.astype(jnp.float32)
    mu = jnp.mean(xf, axis=-1, keepdims=True)
    var = jnp.mean(jnp.square(xf - mu), axis=-1, keepdims=True)
    y = (xf - mu) * lax.rsqrt(var + EPS)
    return (y * g.astype(jnp.float32) + b.astype(jnp.float32)).astype(x.dtype)


def rel_bucket(dist):
    max_exact = REL_BUCKETS // 2
    n = jnp.maximum(dist, 0)
    nf = jnp.maximum(n, 1).astype(jnp.float32)
    large = max_exact + (jnp.log(nf / max_exact) / math.log(REL_MAX_DIST / max_exact)
                         * (REL_BUCKETS - max_exact)).astype(jnp.int32)
    large = jnp.minimum(large, REL_BUCKETS - 1)
    return jnp.where(n < max_exact, n, large)


def dsa_attention(q, k, v, q_idx, k_idx, w_idx, rel_bias):
    B, T = q.shape[0], q.shape[1]
    n_sel = min(TOPK_MAX, T // 4)
    nb = T // Q_BLOCK
    pos = jnp.arange(T)
    att_scale = ATT_HEAD_DIM ** -0.5
    idx_scale = IDX_HEAD_DIM ** -0.5
    w_scale = IDX_HEADS ** -0.5
    k_idx_f = k_idx.astype(jnp.float32)

    def to_blocks(a):
        return a.reshape((B, nb, Q_BLOCK) + a.shape[2:]).swapaxes(0, 1)

    def block(args):
        qb, qib, wb, t0 = args
        tq = t0 + jnp.arange(Q_BLOCK)
        s = jnp.einsum('bqhd,bsd->bhqs', qib.astype(jnp.float32), k_idx_f) * idx_scale
        score = jnp.einsum('bhqs,bqh->bqs', jax.nn.relu(s), wb.astype(jnp.float32) * w_scale)
        causal = pos[None, :] <= tq[:, None]
        score = jnp.where(causal[None], score, -jnp.inf)
        _, idx = lax.top_k(score, n_sel)
        valid = idx <= tq[None, :, None]
        kg = jax.vmap(lambda kk, ii: kk[ii])(k, idx)
        vg = jax.vmap(lambda vv, ii: vv[ii])(v, idx)
        qg = qb.reshape(B, Q_BLOCK, ATT_KV_HEADS, ATT_GROUP, ATT_HEAD_DIM)
        logits = jnp.einsum('bqgrd,bqngd->bgrqn', qg, kg).astype(jnp.float32) * att_scale
        bias = rel_bias[rel_bucket(tq[None, :, None] - idx)]
        bias = bias.transpose(0, 3, 1, 2).reshape(B, ATT_KV_HEADS, ATT_GROUP, Q_BLOCK, n_sel)
        logits = jnp.where(valid[:, None, None], logits + bias.astype(jnp.float32), -jnp.inf)
        p = jax.nn.softmax(logits, axis=-1).astype(vg.dtype)
        o = jnp.einsum('bgrqn,bqngd->bqgrd', p, vg)
        return o.reshape(B, Q_BLOCK, ATT_W)

    out = lax.map(block, (to_blocks(q), to_blocks(q_idx), to_blocks(w_idx),
                          jnp.arange(nb) * Q_BLOCK))
    return out.swapaxes(0, 1).reshape(B, T, ATT_W)


def hgrn2_branch(hq, hf, hi, hg, lb, g_norm):
    B, T = hq.shape[0], hq.shape[1]
    dt = hq.dtype
    H, dk, dv, C = HGRN_HEADS, HGRN_KEY_DIM, HGRN_VAL_DIM, HGRN_CHUNK
    nc = T // C
    q = jax.nn.silu(hq.astype(jnp.float32)).reshape(B, T, H, dk) * (dk ** -0.5)
    f = hf.astype(jnp.float32).reshape(B, T, H, dk)
    lbf = lb.astype(jnp.float32)
    log_f = jnp.logaddexp(jnp.log(lbf), jnp.log1p(-lbf) + jax.nn.log_sigmoid(f))
    kk = (1.0 - lbf) * jax.nn.sigmoid(-f)
    v = hi.astype(jnp.float32).reshape(B, T, H, dv)

    def chunks(a):
        return a.reshape(B, nc, C, H, a.shape[-1]).transpose(1, 0, 3, 2, 4)

    tri = jnp.tril(jnp.ones((C, C), dtype=bool))

    def step(S, xs):
        qc, kc, vc, gc = xs
        b = jnp.cumsum(gc, axis=2)
        diff = b[:, :, :, None, :] - b[:, :, None, :, :]
        decay = jnp.exp(jnp.where(tri[None, None, :, :, None], diff, -jnp.inf))
        A = jnp.einsum('bhtd,bhsd,bhtsd->bhts', qc, kc, decay)
        o = jnp.einsum('bhts,bhsv->bhtv', A, vc) + jnp.einsum('bhtd,bhdv->bhtv', qc * jnp.exp(b), S)
        b_last = b[:, :, -1:, :]
        S = jnp.exp(b_last[:, :, 0, :])[..., None] * S + jnp.einsum(
            'bhsd,bhsv->bhdv', kc * jnp.exp(b_last - b), vc)
        return S, o

    S0 = jnp.zeros((B, H, dk, dv), jnp.float32)
    _, o = lax.scan(step, S0, (chunks(q), chunks(kk), chunks(v), chunks(log_f)))
    o = o.transpose(1, 0, 3, 2, 4).reshape(B, T, H, dv)
    o = rmsnorm(o, g_norm) * jax.nn.silu(hg.astype(jnp.float32).reshape(B, T, H, dv))
    return o.reshape(B, T, HGRN_VW).astype(dt)


def hybrid_mixer(xn, w_in, q_norm_g, k_norm_g, idx_q_norm_g, w_idx_q, idx_k_ln_g, idx_k_ln_b,
                 lb, hgrn_out_norm_g, w_up_att, w_up_hgrn, w_o, rel_bias):
    B, T = xn.shape[0], xn.shape[1]
    splits, _ = _split_points()
    proj = jnp.einsum('btd,dn->btn', xn, w_in)
    (q, k, v, cq, kid, wid, hq, hf, hi, hg, ga, gh) = jnp.split(proj, splits, axis=-1)
    q = rmsnorm(q.reshape(B, T, ATT_HEADS, ATT_HEAD_DIM), q_norm_g)
    k = rmsnorm(k.reshape(B, T, ATT_KV_HEADS, ATT_HEAD_DIM), k_norm_g)
    v = v.reshape(B, T, ATT_KV_HEADS, ATT_HEAD_DIM)
    q_idx = jnp.einsum('btr,rn->btn', rmsnorm(cq, idx_q_norm_g), w_idx_q)
    q_idx = q_idx.reshape(B, T, IDX_HEADS, IDX_HEAD_DIM)
    k_idx = layernorm(kid, idx_k_ln_g, idx_k_ln_b)
    att = dsa_attention(q, k, v, q_idx, k_idx, wid, rel_bias)
    rec = hgrn2_branch(hq, hf, hi, hg, lb, hgrn_out_norm_g)
    merged = (jax.nn.sigmoid(ga) * jnp.einsum('btc,cd->btd', att, w_up_att)
              + jax.nn.sigmoid(gh) * jnp.einsum('btc,cd->btd', rec, w_up_hgrn))
    return jnp.einsum('btd,de->bte', merged, w_o)


def swiglu(x, w1, w3, w2):
    return jnp.einsum('btf,fd->btd', jax.nn.silu(jnp.einsum('btd,df->btf', x, w1))
                      * jnp.einsum('btd,df->btf', x, w3), w2)


def moe_swiglu(x, w_router, w1, w3, w2):
    logits = jnp.einsum('btd,de->bte', x, w_router).astype(jnp.float32)
    top_v, top_i = lax.top_k(logits, TOP_K_EXPERTS)
    top_p = jax.nn.softmax(top_v, axis=-1)
    gates = jnp.sum(jax.nn.one_hot(top_i, N_EXPERTS, dtype=jnp.float32) * top_p[..., None], axis=-2)
    y = jnp.zeros(x.shape, jnp.float32)
    for e in range(N_EXPERTS):
        y = y + gates[..., e:e + 1] * swiglu(x, w1[e], w3[e], w2[e]).astype(jnp.float32)
    return y.astype(x.dtype)


def setup_inputs(seed: int = 0) -> dict:
    key = jax.random.key(seed)
    ks = jax.random.split(key, 24)
    _, n_in = _split_points()
    nrm = lambda k, shape, s: jax.random.normal(k, shape, jnp.float32) * s
    gain = lambda k, shape: 1.0 + 0.02 * jax.random.normal(k, shape, jnp.float32)
    return {
        "x": nrm(ks[0], (BATCH, SEQ, D_MODEL), 1.0),
        "rel_bias": nrm(ks[1], (REL_BUCKETS, ATT_HEADS), 0.5),
        "norm_mix_g": gain(ks[2], (DEPTH, D_MODEL)),
        "norm_ffn_g": gain(ks[3], (DEPTH, D_MODEL)),
        "w_in": nrm(ks[4], (DEPTH, D_MODEL, n_in), D_MODEL ** -0.5),
        "q_norm_g": gain(ks[5], (DEPTH, ATT_HEAD_DIM)),
        "k_norm_g": gain(ks[6], (DEPTH, ATT_HEAD_DIM)),
        "idx_q_norm_g": gain(ks[7], (DEPTH, IDX_Q_RANK)),
        "w_idx_q": nrm(ks[8], (DEPTH, IDX_Q_RANK, IDX_HEADS * IDX_HEAD_DIM), IDX_Q_RANK ** -0.5),
        "idx_k_ln_g": gain(ks[9], (DEPTH, IDX_HEAD_DIM)),
        "idx_k_ln_b": nrm(ks[10], (DEPTH, IDX_HEAD_DIM), 0.02),
        "hgrn_lb_logits": nrm(ks[11], (DEPTH, HGRN_KW), 1.0),
        "hgrn_out_norm_g": gain(ks[12], (DEPTH, HGRN_VAL_DIM)),
        "w_up_att": nrm(ks[13], (DEPTH, ATT_W, D_MODEL), ATT_W ** -0.5),
        "w_up_hgrn": nrm(ks[14], (DEPTH, HGRN_VW, D_MODEL), HGRN_VW ** -0.5),
        "w_o": nrm(ks[15], (DEPTH, D_MODEL, D_MODEL), D_MODEL ** -0.5),
        "w1_dense": nrm(ks[16], (N_DENSE, D_MODEL, D_FF_DENSE), D_MODEL ** -0.5),
        "w3_dense": nrm(ks[17], (N_DENSE, D_MODEL, D_FF_DENSE), D_MODEL ** -0.5),
        "w2_dense": nrm(ks[18], (N_DENSE, D_FF_DENSE, D_MODEL), D_FF_DENSE ** -0.5),
        "w_router": nrm(ks[19], (N_MOE, D_MODEL, N_EXPERTS), D_MODEL ** -0.5),
        "w1_moe": nrm(ks[20], (N_MOE, N_EXPERTS, D_MODEL, D_FF_EXPERT), D_MODEL ** -0.5),
        "w3_moe": nrm(ks[21], (N_MOE, N_EXPERTS, D_MODEL, D_FF_EXPERT), D_MODEL ** -0.5),
        "w2_moe": nrm(ks[22], (N_MOE, N_EXPERTS, D_FF_EXPERT, D_MODEL), D_FF_EXPERT ** -0.5),
    }


def reference(x, rel_bias, norm_mix_g, norm_ffn_g, w_in, q_norm_g, k_norm_g, idx_q_norm_g, w_idx_q,
              idx_k_ln_g, idx_k_ln_b, hgrn_lb_logits, hgrn_out_norm_g, w_up_att, w_up_hgrn, w_o,
              w1_dense, w3_dense, w2_dense, w_router, w1_moe, w3_moe, w2_moe):
    lb_all = jnp.cumsum(jax.nn.softmax(hgrn_lb_logits.astype(jnp.float32), axis=0), axis=0)
    lb_all = lb_all - lb_all[0:1]
    for l in range(DEPTH):
        xn = rmsnorm(x, norm_mix_g[l])
        h = x + hybrid_mixer(xn, w_in[l], q_norm_g[l], k_norm_g[l], idx_q_norm_g[l], w_idx_q[l],
                             idx_k_ln_g[l], idx_k_ln_b[l],
                             lb_all[l].reshape(HGRN_HEADS, HGRN_KEY_DIM), hgrn_out_norm_g[l],
                             w_up_att[l], w_up_hgrn[l], w_o[l], rel_bias)
        hn = rmsnorm(h, norm_ffn_g[l])
        if l % 2 == 0:
            j = l // 2
            ffn = swiglu(hn, w1_dense[j], w3_dense[j], w2_dense[j])
        else:
            j = l // 2
            ffn = moe_swiglu(hn, w_router[j], w1_moe[j], w3_moe[j], w2_moe[j])
        x = h + ffn
    return x
```

```python
import functools
import math

import numpy as np
import jax
import jax.numpy as jnp
from jax import lax
from jax.experimental import pallas as pl
from jax.experimental.pallas import tpu as pltpu

F32 = jnp.float32
BF16 = jnp.bfloat16
I32 = jnp.int32

EPS = 1e-6
ATT_HEADS = 8
ATT_KV_HEADS = 2
ATT_GROUP = ATT_HEADS // ATT_KV_HEADS
HEAD_DIM = 128
ATT_W = ATT_HEADS * HEAD_DIM
KV_W = ATT_KV_HEADS * HEAD_DIM
IDX_HEADS = 16
IDX_HEAD_DIM = 64
IDX_Q_RANK = 512
TOPK_MAX = 256
HGRN_HEADS = 8
HGRN_DIM = 128
HGRN_W = HGRN_HEADS * HGRN_DIM
REL_BUCKETS = 32
REL_MAX_DIST = 128
N_EXPERTS = 8
LANES = 128
INT_MIN = -(2 ** 31)
NEG_BIG = -1e30

TQ = 256
TK = 256
HC = 128
HGRN_LEVELS = (1, 2, 4, 8, 16, 32, 64)
N_ARG_GROUPS = len(HGRN_LEVELS) + 2

COL_Q, COL_KVC, COL_HQ, COL_HF, COL_HI, COL_HG, COL_GA, COL_GH = 0, 1, 2, 3, 4, 5, 6, 8
MAIN_W = 10 * 1024


def _cparams(sem, vmem_mb):
    return pltpu.CompilerParams(dimension_semantics=sem, vmem_limit_bytes=vmem_mb << 20)


def _dot(a, b):
    return jnp.dot(a, b, preferred_element_type=F32)


def _dot_nt(a, b):
    return lax.dot_general(a, b, (((1,), (1,)), ((), ())), preferred_element_type=F32)


def _sigmoid(x):
    return 1.0 / (1.0 + jnp.exp(-x))


def _rmsnorm_kernel(x_ref, g_ref, o_ref):
    x = x_ref[...]
    ms = jnp.mean(x * x, axis=-1, keepdims=True)
    o_ref[...] = (x * lax.rsqrt(ms + EPS) * g_ref[...]).astype(o_ref.dtype)


def rmsnorm_rows(x, g, tm=512):
    m, d = x.shape
    return pl.pallas_call(
        _rmsnorm_kernel,
        grid=(m // tm,),
        in_specs=[pl.BlockSpec((tm, d), lambda i: (i, 0)),
                  pl.BlockSpec((1, d), lambda i: (0, 0))],
        out_specs=pl.BlockSpec((tm, d), lambda i: (i, 0)),
        out_shape=jax.ShapeDtypeStruct((m, d), BF16),
        compiler_params=_cparams(("parallel",), 32),
        name="rmsnorm",
    )(x, g.reshape(1, d))


def _rmsnorm_router_kernel(x_ref, g_ref, wr_ref, o_ref, gates_ref):
    x = x_ref[...]
    ms = jnp.mean(x * x, axis=-1, keepdims=True)
    hn = x * lax.rsqrt(ms + EPS) * g_ref[...]
    o_ref[...] = hn.astype(o_ref.dtype)
    logits = jnp.dot(hn, wr_ref[...], precision=lax.Precision.HIGHEST,
                     preferred_element_type=F32)
    lane = lax.broadcasted_iota(I32, logits.shape, 1)
    logits = jnp.where(lane < N_EXPERTS, logits, -jnp.inf)
    v1 = jnp.max(logits, axis=-1, keepdims=True)
    i1 = jnp.min(jnp.where(logits == v1, lane, LANES), axis=-1, keepdims=True)
    rest = jnp.where(lane == i1, -jnp.inf, logits)
    v2 = jnp.max(rest, axis=-1, keepdims=True)
    i2 = jnp.min(jnp.where(rest == v2, lane, LANES), axis=-1, keepdims=True)
    e = jnp.exp(v2 - v1)
    p1 = 1.0 / (1.0 + e)
    gates_ref[...] = jnp.where(lane == i1, p1, 0.0) + jnp.where(lane == i2, e * p1, 0.0)


def rmsnorm_router(x, g, w_router, tm=512):
    m, d = x.shape
    wr = jnp.zeros((d, LANES), F32).at[:, :N_EXPERTS].set(w_router)
    return pl.pallas_call(
        _rmsnorm_router_kernel,
        grid=(m // tm,),
        in_specs=[pl.BlockSpec((tm, d), lambda i: (i, 0)),
                  pl.BlockSpec((1, d), lambda i: (0, 0)),
                  pl.BlockSpec((d, LANES), lambda i: (0, 0))],
        out_specs=[pl.BlockSpec((tm, d), lambda i: (i, 0)),
                   pl.BlockSpec((tm, LANES), lambda i: (i, 0))],
        out_shape=[jax.ShapeDtypeStruct((m, d), BF16),
                   jax.ShapeDtypeStruct((m, LANES), F32)],
        compiler_params=_cparams(("parallel",), 32),
        name="rmsnorm_router",
    )(x, g.reshape(1, d), wr)


def _matmul_kernel(x_ref, w_ref, o_ref):
    o_ref[...] = _dot(x_ref[...], w_ref[...]).astype(o_ref.dtype)


def matmul(x, w, out_dtype, tm=1024, tn=1024):
    m, k = x.shape
    n = w.shape[1]
    tm, tn = min(tm, m), min(tn, n)
    return pl.pallas_call(
        _matmul_kernel,
        grid=(m // tm, n // tn),
        in_specs=[pl.BlockSpec((tm, k), lambda i, j: (i, 0)),
                  pl.BlockSpec((k, tn), lambda i, j: (0, j))],
        out_specs=pl.BlockSpec((tm, tn), lambda i, j: (i, j)),
        out_shape=jax.ShapeDtypeStruct((m, n), out_dtype),
        compiler_params=_cparams(("parallel", "parallel"), 48),
        name="matmul",
    )(x, w)


def _matmul_residual_kernel(x_ref, w_ref, r_ref, o_ref):
    o_ref[...] = r_ref[...] + _dot(x_ref[...], w_ref[...])


def matmul_residual(x, w, res, tm=1024, tn=1024):
    m, k = x.shape
    n = w.shape[1]
    tm = min(tm, m)
    return pl.pallas_call(
        _matmul_residual_kernel,
        grid=(m // tm, n // tn),
        in_specs=[pl.BlockSpec((tm, k), lambda i, j: (i, 0)),
                  pl.BlockSpec((k, tn), lambda i, j: (0, j)),
                  pl.BlockSpec((tm, tn), lambda i, j: (i, j))],
        out_specs=pl.BlockSpec((tm, tn), lambda i, j: (i, j)),
        out_shape=jax.ShapeDtypeStruct((m, n), F32),
        compiler_params=_cparams(("parallel", "parallel"), 48),
        name="matmul_residual",
    )(x, w, res)


def _attn_prep_kernel(q_ref, kvc_ref, pi_ref, qg_ref, kg_ref, cg_ref, wiq_ref, lng_ref, lnb_ref,
                      qn_ref, kn_ref, v_ref, qi_ref, ki_ref, w_ref):
    def head_rms(x, g):
        return x * lax.rsqrt(jnp.mean(x * x, axis=-1, keepdims=True) + EPS) * g

    att_scale = HEAD_DIM ** -0.5
    for h in range(ATT_HEADS):
        sl = slice(h * HEAD_DIM, (h + 1) * HEAD_DIM)
        qn_ref[:, sl] = (head_rms(q_ref[:, sl], qg_ref[...]) * att_scale).astype(BF16)
    for g in range(ATT_KV_HEADS):
        sl = slice(g * HEAD_DIM, (g + 1) * HEAD_DIM)
        kn_ref[:, sl] = head_rms(kvc_ref[:, sl], kg_ref[...]).astype(BF16)
    v_ref[...] = kvc_ref[:, KV_W:2 * KV_W].astype(BF16)
    cq = kvc_ref[:, 2 * KV_W:2 * KV_W + IDX_Q_RANK]
    cqn = head_rms(cq, cg_ref[...]).astype(BF16)
    qi_ref[...] = _dot(cqn, wiq_ref[...]).astype(BF16)
    pi = pi_ref[...]
    is_key = lax.broadcasted_iota(I32, pi.shape, 1) < IDX_HEAD_DIM
    mu = jnp.sum(jnp.where(is_key, pi, 0.0), axis=-1, keepdims=True) * (1.0 / IDX_HEAD_DIM)
    cen = jnp.where(is_key, pi - mu, 0.0)
    var = jnp.sum(cen * cen, axis=-1, keepdims=True) * (1.0 / IDX_HEAD_DIM)
    kidn = cen * lax.rsqrt(var + EPS) * lng_ref[...] + lnb_ref[...]
    ki_ref[:, 0:LANES] = kidn.astype(BF16)
    ki_ref[:, LANES:2 * LANES] = pltpu.roll(kidn, IDX_HEAD_DIM, 1).astype(BF16)
    w_ref[...] = pi * (IDX_HEAD_DIM ** -0.5 * IDX_HEADS ** -0.5)


def attn_prep(proj_main, proj_idx, q_norm_g, k_norm_g, idx_q_norm_g, w_idx_q, ln_g, ln_b, tm=512):
    m = proj_main.shape[0]
    row = lambda i: (i, 0)
    const = lambda i: (0, 0)
    lane_pad = lambda a: jnp.zeros((1, LANES), F32).at[0, :a.shape[0]].set(a)
    return pl.pallas_call(
        _attn_prep_kernel,
        grid=(m // tm,),
        in_specs=[pl.BlockSpec((tm, 1024), lambda i: (i, COL_Q)),
                  pl.BlockSpec((tm, 1024), lambda i: (i, COL_KVC)),
                  pl.BlockSpec((tm, LANES), row),
                  pl.BlockSpec((1, HEAD_DIM), const),
                  pl.BlockSpec((1, HEAD_DIM), const),
                  pl.BlockSpec((1, IDX_Q_RANK), const),
                  pl.BlockSpec((IDX_Q_RANK, IDX_HEADS * IDX_HEAD_DIM), const),
                  pl.BlockSpec((1, LANES), const),
                  pl.BlockSpec((1, LANES), const)],
        out_specs=[pl.BlockSpec((tm, ATT_W), row),
                   pl.BlockSpec((tm, KV_W), row),
                   pl.BlockSpec((tm, KV_W), row),
                   pl.BlockSpec((tm, IDX_HEADS * IDX_HEAD_DIM), row),
                   pl.BlockSpec((tm, 2 * LANES), row),
                   pl.BlockSpec((tm, LANES), row)],
        out_shape=[jax.ShapeDtypeStruct((m, ATT_W), BF16),
                   jax.ShapeDtypeStruct((m, KV_W), BF16),
                   jax.ShapeDtypeStruct((m, KV_W), BF16),
                   jax.ShapeDtypeStruct((m, IDX_HEADS * IDX_HEAD_DIM), BF16),
                   jax.ShapeDtypeStruct((m, 2 * LANES), BF16),
                   jax.ShapeDtypeStruct((m, LANES), F32)],
        compiler_params=_cparams(("parallel",), 40),
        name="attn_prep",
    )(proj_main, proj_main, proj_idx, q_norm_g.reshape(1, -1), k_norm_g.reshape(1, -1),
      idx_q_norm_g.reshape(1, -1), w_idx_q, lane_pad(ln_g), lane_pad(ln_b))


def _rel_bucket(dist):
    max_exact = REL_BUCKETS // 2
    n = jnp.maximum(dist, 0)
    nf = jnp.maximum(n, 1).astype(F32)
    large = max_exact + (jnp.log(nf / max_exact) / math.log(REL_MAX_DIST / max_exact)
                         * (REL_BUCKETS - max_exact)).astype(I32)
    large = jnp.minimum(large, REL_BUCKETS - 1)
    return jnp.where(n < max_exact, n, large)


def _bias_kernel(rb_ref, bkt_ref, o_ref):
    h = pl.program_id(0)
    for kind in range(3):
        b = bkt_ref[kind]
        acc = jnp.zeros(b.shape, F32)
        for n in range(REL_BUCKETS):
            acc = jnp.where(b == n, rb_ref[n, h], acc)
        o_ref[0, kind] = acc


def bias_tables(rel_bias):
    assert TK >= REL_MAX_DIST
    kpos = jnp.arange(TK, dtype=I32)[:, None]
    qpos = jnp.arange(TQ, dtype=I32)[None, :]
    dist = jnp.stack([qpos - kpos, qpos - kpos + TK, qpos - kpos + 2 * TK])
    bkt = _rel_bucket(dist)
    return pl.pallas_call(
        _bias_kernel,
        grid=(ATT_HEADS,),
        in_specs=[pl.BlockSpec(memory_space=pltpu.SMEM),
                  pl.BlockSpec((3, TK, TQ), lambda h: (0, 0, 0))],
        out_specs=pl.BlockSpec((1, 3, TK, TQ), lambda h: (h, 0, 0, 0)),
        out_shape=jax.ShapeDtypeStruct((ATT_HEADS, 3, TK, TQ), F32),
        compiler_params=_cparams(("arbitrary",), 32),
        name="bias_tables",
    )(rel_bias, bkt)


def _dsa_kernel(qi_ref, wt_ref, qn_ref, ki_ref, kn_ref, vt_ref, bias_ref, o_ref, key_ref, *, n_sel):
    i = pl.program_id(1)
    nch = i + 1
    qpos = i * TQ + lax.broadcasted_iota(I32, (TK, TQ), 1)

    def score_chunk(j, carry):
        k0 = pl.multiple_of(j * TK, TK)
        acc = jnp.zeros((TK, TQ), F32)
        for h in range(IDX_HEADS):
            par = h % 2
            kc = ki_ref[pl.ds(k0, TK), par * LANES:(par + 1) * LANES]
            qh = qi_ref[:, (h // 2) * LANES:(h // 2 + 1) * LANES]
            s = _dot_nt(kc, qh)
            acc = acc + jnp.maximum(s, 0.0) * wt_ref[0, h:h + 1, :]
        bits = lax.bitcast_convert_type(acc, I32)
        key = jnp.where(bits < 0, bits ^ jnp.int32(0x7FFFFFFF), bits)
        kpos = k0 + lax.broadcasted_iota(I32, (TK, TQ), 0)
        key_ref[pl.ds(k0, TK), :] = jnp.where(kpos <= qpos, key, INT_MIN)
        return carry

    lax.fori_loop(0, nch, score_chunk, 0)

    def count_ge(cand):
        def body(j, acc):
            k0 = pl.multiple_of(j * TK, TK)
            hit = jnp.where(key_ref[pl.ds(k0, TK), :] >= cand, 1, 0).astype(I32)
            return acc + jnp.sum(hit.reshape(TK // 8, 8, TQ), axis=0)
        acc = lax.fori_loop(0, nch, body, jnp.zeros((8, TQ), I32))
        return jnp.sum(acc, axis=0, keepdims=True)

    thr = jnp.where(count_ge(jnp.zeros((1, TQ), I32)) >= n_sel, 0, INT_MIN).astype(I32)

    def bit_body(b, thr):
        cand = thr | jnp.left_shift(jnp.int32(1), 30 - b)
        return jnp.where(count_ge(cand) >= n_sel, cand, thr)

    thr = lax.fori_loop(0, 31, bit_body, thr)
    thr = jnp.maximum(thr, INT_MIN + 1)

    for h in range(ATT_HEADS):
        g = h // ATT_GROUP
        qh = qn_ref[:, h * HEAD_DIM:(h + 1) * HEAD_DIM]

        def att_chunk(j, carry):
            m, l, acc = carry
            k0 = pl.multiple_of(j * TK, TK)
            kc = kn_ref[pl.ds(k0, TK), g * HEAD_DIM:(g + 1) * HEAD_DIM]
            lg = _dot_nt(kc, qh) + bias_ref[h, jnp.minimum(i - j, 2)]
            sel = key_ref[pl.ds(k0, TK), :] >= thr
            lg = jnp.where(sel, lg, NEG_BIG)
            m_new = jnp.maximum(m, jnp.max(lg, axis=0, keepdims=True))
            alpha = jnp.exp(m - m_new)
            p = jnp.where(sel, jnp.exp(lg - m_new), 0.0)
            l = alpha * l + jnp.sum(p, axis=0, keepdims=True)
            vc = vt_ref[0, j, g * HEAD_DIM:(g + 1) * HEAD_DIM, :]
            acc = alpha * acc + _dot(vc, p.astype(BF16))
            return m_new, l, acc

        init = (jnp.full((1, TQ), NEG_BIG, F32), jnp.zeros((1, TQ), F32),
                jnp.zeros((HEAD_DIM, TQ), F32))
        _, l, acc = lax.fori_loop(0, nch, att_chunk, init)
        o_ref[:, h * HEAD_DIM:(h + 1) * HEAD_DIM] = (acc * (1.0 / l)).T.astype(o_ref.dtype)


def dsa_attention(qi, wt, qn, ki, kn, vt, bias, batch, seq):
    nq = seq // TQ
    n_sel = min(TOPK_MAX, seq // 4)
    qrow = lambda b, i: (b * nq + i, 0)
    brow = lambda b, i: (b, 0)
    return pl.pallas_call(
        functools.partial(_dsa_kernel, n_sel=n_sel),
        grid=(batch, nq),
        in_specs=[pl.BlockSpec((TQ, IDX_HEADS * IDX_HEAD_DIM), qrow),
                  pl.BlockSpec((1, IDX_HEADS, TQ), lambda b, i: (b, 0, i)),
                  pl.BlockSpec((TQ, ATT_W), qrow),
                  pl.BlockSpec((seq, 2 * LANES), brow),
                  pl.BlockSpec((seq, KV_W), brow),
                  pl.BlockSpec((1, seq // TK, KV_W, TK), lambda b, i: (b, 0, 0, 0)),
                  pl.BlockSpec((ATT_HEADS, 3, TK, TQ), lambda b, i: (0, 0, 0, 0))],
        out_specs=pl.BlockSpec((TQ, ATT_W), qrow),
        out_shape=jax.ShapeDtypeStruct((batch * seq, ATT_W), BF16),
        scratch_shapes=[pltpu.VMEM((seq, TQ), I32)],
        compiler_params=_cparams(("parallel", "arbitrary"), 48),
        name="dsa_attention",
    )(qi, wt, qn, ki, kn, vt, bias)


def _hgrn_constants():
    t = np.arange(HC)
    rows = []
    masks = [np.eye(HC, dtype=np.float32)]
    for m in HGRN_LEVELS:
        upper = (t // m) % 2 == 1
        start = (t // m) * m
        end = start + m - 1
        u = t[None, :]
        q_side = upper[:, None] & (u >= start[:, None]) & (u <= t[:, None])
        k_side = (~upper)[:, None] & (u > t[:, None]) & (u <= end[:, None])
        rows.append((q_side | k_side).astype(np.float32))
        same = (t[:, None] // (2 * m)) == (t[None, :] // (2 * m))
        masks.append((upper[:, None] & (~upper)[None, :] & same).astype(np.float32))
    u = t[None, :]
    rows.append((u <= t[:, None]).astype(np.float32))
    rows.append((u > t[:, None]).astype(np.float32))
    mat = np.concatenate(rows, axis=0)
    return np.concatenate([mat, mat], axis=1), np.stack(masks)


def _hgrn_kernel(hq_ref, hf_ref, hi_ref, hg_ref, lb_ref, gn_ref, mat_ref, mask_ref, o_ref,
                 st_ref, arg_ref, kk_ref, lf_ref):
    c = pl.program_id(1)

    @pl.when(c == 0)
    def _():
        st_ref[...] = jnp.zeros_like(st_ref)

    lb = lb_ref[...]
    f = hf_ref[...]
    e = jnp.exp(-jnp.abs(f))
    r = 1.0 / (1.0 + e)
    log_sig = jnp.minimum(f, 0.0) - jnp.log(1.0 + e)
    la = jnp.log(lb)
    lc = jnp.log(1.0 - lb) + log_sig
    logf = jnp.maximum(la, lc) + jnp.log(1.0 + jnp.exp(-jnp.abs(la - lc)))
    kk_ref[...] = (1.0 - lb) * jnp.where(f >= 0, e * r, r)
    hi = logf.astype(BF16)
    lf_ref[0:HC, :] = hi
    lf_ref[HC:2 * HC, :] = (logf - hi.astype(F32)).astype(BF16)
    arg_ref[...] = _dot(mat_ref[...], lf_ref[...])

    scale = HGRN_DIM ** -0.5
    for h in range(HGRN_HEADS):
        sl = slice(h * HGRN_DIM, (h + 1) * HGRN_DIM)
        hq = hq_ref[:, sl]
        q = hq * _sigmoid(hq) * scale
        k = kk_ref[:, sl]
        v = hi_ref[:, sl]
        a = _dot_nt(q.astype(BF16), k.astype(BF16)) * mask_ref[0]
        for lv in range(len(HGRN_LEVELS)):
            ex = jnp.exp(arg_ref[lv * HC:(lv + 1) * HC, sl])
            a = a + _dot_nt((q * ex).astype(BF16), (k * ex).astype(BF16)) * mask_ref[lv + 1]
        nl = len(HGRN_LEVELS)
        eq = jnp.exp(arg_ref[nl * HC:(nl + 1) * HC, sl])
        ek = jnp.exp(arg_ref[(nl + 1) * HC:(nl + 2) * HC, sl])
        st = st_ref[h]
        vb = v.astype(BF16)
        o = _dot(a.astype(BF16), vb) + _dot_nt((q * eq).astype(BF16), st.astype(BF16))
        st_ref[h] = eq[HC - 1:HC, :] * st + _dot(v.T.astype(BF16), (k * ek).astype(BF16))
        on = o * lax.rsqrt(jnp.mean(o * o, axis=-1, keepdims=True) + EPS) * gn_ref[...]
        hg = hg_ref[:, sl]
        o_ref[:, sl] = (on * hg * _sigmoid(hg)).astype(o_ref.dtype)


def hgrn2(proj_main, lb, g_norm, batch, seq):
    nc = seq // HC
    mat, masks = _hgrn_constants()
    col = lambda cb: (lambda b, c: (b * nc + c, cb))
    return pl.pallas_call(
        _hgrn_kernel,
        grid=(batch, nc),
        in_specs=[pl.BlockSpec((HC, HGRN_W), col(COL_HQ)),
                  pl.BlockSpec((HC, HGRN_W), col(COL_HF)),
                  pl.BlockSpec((HC, HGRN_W), col(COL_HI)),
                  pl.BlockSpec((HC, HGRN_W), col(COL_HG)),
                  pl.BlockSpec((1, HGRN_W), lambda b, c: (0, 0)),
                  pl.BlockSpec((1, HGRN_DIM), lambda b, c: (0, 0)),
                  pl.BlockSpec((N_ARG_GROUPS * HC, 2 * HC), lambda b, c: (0, 0)),
                  pl.BlockSpec((len(HGRN_LEVELS) + 1, HC, HC), lambda b, c: (0, 0, 0))],
        out_specs=pl.BlockSpec((HC, HGRN_W), lambda b, c: (b * nc + c, 0)),
        out_shape=jax.ShapeDtypeStruct((batch * seq, HGRN_W), BF16),
        scratch_shapes=[pltpu.VMEM((HGRN_HEADS, HGRN_DIM, HGRN_DIM), F32),
                        pltpu.VMEM((N_ARG_GROUPS * HC, HGRN_W), F32),
                        pltpu.VMEM((HC, HGRN_W), F32),
                        pltpu.VMEM((2 * HC, HGRN_W), BF16)],
        compiler_params=_cparams(("parallel", "arbitrary"), 40),
        name="hgrn2",
    )(proj_main, proj_main, proj_main, proj_main, lb.reshape(1, -1), g_norm.reshape(1, -1),
      jnp.asarray(mat, BF16), jnp.asarray(masks, F32))


def _merge_kernel(att_ref, rec_ref, ga_ref, gh_ref, wa_ref, wh_ref, o_ref):
    a = _dot(att_ref[...], wa_ref[...])
    r = _dot(rec_ref[...], wh_ref[...])
    o_ref[...] = (_sigmoid(ga_ref[...]) * a + _sigmoid(gh_ref[...]) * r).astype(o_ref.dtype)


def gated_merge(att, rec, proj_main, w_up_att, w_up_hgrn, tm=512, tn=1024):
    m = att.shape[0]
    n = w_up_att.shape[1]
    return pl.pallas_call(
        _merge_kernel,
        grid=(m // tm, n // tn),
        in_specs=[pl.BlockSpec((tm, ATT_W), lambda i, j: (i, 0)),
                  pl.BlockSpec((tm, HGRN_W), lambda i, j: (i, 0)),
                  pl.BlockSpec((tm, tn), lambda i, j: (i, COL_GA + j)),
                  pl.BlockSpec((tm, tn), lambda i, j: (i, COL_GH + j)),
                  pl.BlockSpec((ATT_W, tn), lambda i, j: (0, j)),
                  pl.BlockSpec((HGRN_W, tn), lambda i, j: (0, j))],
        out_specs=pl.BlockSpec((tm, tn), lambda i, j: (i, j)),
        out_shape=jax.ShapeDtypeStruct((m, n), BF16),
        compiler_params=_cparams(("parallel", "parallel"), 48),
        name="gated_merge",
    )(att, rec, proj_main, proj_main, w_up_att, w_up_hgrn)


def _ffn_kernel(hn_ref, h_ref, w1_ref, w3_ref, w2_ref, o_ref):
    @pl.when(pl.program_id(1) == 0)
    def _():
        o_ref[...] = h_ref[...]

    x = hn_ref[...]
    a = _dot(x, w1_ref[...])
    b = _dot(x, w3_ref[...])
    act = (a * _sigmoid(a) * b).astype(BF16)
    o_ref[...] += _dot(act, w2_ref[...])


def ffn_dense(hn, h, w1, w3, w2, tm=512, tf=512):
    m, d = hn.shape
    f = w1.shape[1]
    return pl.pallas_call(
        _ffn_kernel,
        grid=(m // tm, f // tf),
        in_specs=[pl.BlockSpec((tm, d), lambda i, j: (i, 0)),
                  pl.BlockSpec((tm, d), lambda i, j: (i, 0)),
                  pl.BlockSpec((d, tf), lambda i, j: (0, j)),
                  pl.BlockSpec((d, tf), lambda i, j: (0, j)),
                  pl.BlockSpec((tf, d), lambda i, j: (j, 0))],
        out_specs=pl.BlockSpec((tm, d), lambda i, j: (i, 0)),
        out_shape=jax.ShapeDtypeStruct((m, d), F32),
        compiler_params=_cparams(("parallel", "arbitrary"), 48),
        name="ffn_dense",
    )(hn, h, w1, w3, w2)


def _moe_kernel(hn_ref, h_ref, gates_ref, w1_ref, w3_ref, w2_ref, o_ref):
    e = pl.program_id(1)

    @pl.when((e == 0) & (pl.program_id(2) == 0))
    def _():
        o_ref[...] = h_ref[...]

    gates = gates_ref[...]
    lane = lax.broadcasted_iota(I32, gates.shape, 1)
    ge = jnp.sum(jnp.where(lane == e, gates, 0.0), axis=-1, keepdims=True)
    x = hn_ref[...]
    a = _dot(x, w1_ref[0])
    b = _dot(x, w3_ref[0])
    act = (a * _sigmoid(a) * b * ge).astype(BF16)
    o_ref[...] += _dot(act, w2_ref[0])


def ffn_moe(hn, h, gates, w1, w3, w2, tm=512, tf=512):
    m, d = hn.shape
    ne, _, f = w1.shape
    return pl.pallas_call(
        _moe_kernel,
        grid=(m // tm, ne, f // tf),
        in_specs=[pl.BlockSpec((tm, d), lambda i, e, j: (i, 0)),
                  pl.BlockSpec((tm, d), lambda i, e, j: (i, 0)),
                  pl.BlockSpec((tm, LANES), lambda i, e, j: (i, 0)),
                  pl.BlockSpec((1, d, tf), lambda i, e, j: (e, 0, j)),
                  pl.BlockSpec((1, d, tf), lambda i, e, j: (e, 0, j)),
                  pl.BlockSpec((1, tf, d), lambda i, e, j: (e, j, 0))],
        out_specs=pl.BlockSpec((tm, d), lambda i, e, j: (i, 0)),
        out_shape=jax.ShapeDtypeStruct((m, d), F32),
        compiler_params=_cparams(("parallel", "arbitrary", "arbitrary"), 48),
        name="ffn_moe",
    )(hn, h, gates, w1, w3, w2)


def _split_w_in(w_in):
    sizes = [ATT_W, KV_W, KV_W, IDX_Q_RANK, IDX_HEAD_DIM, IDX_HEADS,
             HGRN_W, HGRN_W, HGRN_W, HGRN_W, w_in.shape[0], w_in.shape[0]]
    offs = np.cumsum([0] + sizes)
    part = lambda n: w_in[:, offs[n]:offs[n + 1]]
    main = jnp.concatenate([part(n) for n in (0, 1, 2, 3, 6, 7, 8, 9, 10, 11)], axis=1)
    pad = jnp.zeros((w_in.shape[0], LANES - IDX_HEAD_DIM - IDX_HEADS), w_in.dtype)
    idx = jnp.concatenate([part(4), part(5), pad], axis=1)
    return main.astype(BF16), idx.astype(BF16)


def _mixer(x2, batch, seq, norm_g, w_in, q_norm_g, k_norm_g, idx_q_norm_g, w_idx_q, ln_g, ln_b,
           lb, hgrn_norm_g, w_up_att, w_up_hgrn, w_o, bias):
    m = batch * seq
    xn = rmsnorm_rows(x2, norm_g)
    w_main, w_idx = _split_w_in(w_in)
    proj_main = matmul(xn, w_main, F32)
    proj_idx = matmul(xn, w_idx, F32)
    qn, kn, v, qi, ki, w = attn_prep(proj_main, proj_idx, q_norm_g, k_norm_g, idx_q_norm_g,
                                     w_idx_q.astype(BF16), ln_g, ln_b)
    wt = w[:, IDX_HEAD_DIM:IDX_HEAD_DIM + IDX_HEADS].reshape(batch, seq, IDX_HEADS).transpose(0, 2, 1)
    vt = v.reshape(batch, seq // TK, TK, KV_W).transpose(0, 1, 3, 2)
    att = dsa_attention(qi, wt, qn, ki, kn, vt, bias, batch, seq)
    rec = hgrn2(proj_main, lb, hgrn_norm_g, batch, seq)
    merged = gated_merge(att, rec, proj_main, w_up_att.astype(BF16), w_up_hgrn.astype(BF16))
    return matmul_residual(merged, w_o.astype(BF16), x2)


def kernel(x, rel_bias, norm_mix_g, norm_ffn_g, w_in, q_norm_g, k_norm_g, idx_q_norm_g, w_idx_q,
           idx_k_ln_g, idx_k_ln_b, hgrn_lb_logits, hgrn_out_norm_g, w_up_att, w_up_hgrn, w_o,
           w1_dense, w3_dense, w2_dense, w_router, w1_moe, w3_moe, w2_moe):
    batch, seq, d = x.shape
    depth = w_in.shape[0]
    lb_all = jnp.cumsum(jax.nn.softmax(hgrn_lb_logits.astype(F32), axis=0), axis=0)
    lb_all = lb_all - lb_all[0:1]
    bias = bias_tables(rel_bias)
    x2 = x.reshape(batch * seq, d)
    for l in range(depth):
        h = _mixer(x2, batch, seq, norm_mix_g[l], w_in[l], q_norm_g[l], k_norm_g[l],
                   idx_q_norm_g[l], w_idx_q[l], idx_k_ln_g[l], idx_k_ln_b[l], lb_all[l],
                   hgrn_out_norm_g[l], w_up_att[l], w_up_hgrn[l], w_o[l], bias)
        j = l // 2
        if l % 2 == 0:
            hn = rmsnorm_rows(h, norm_ffn_g[l])
            x2 = ffn_dense(hn, h, w1_dense[j].astype(BF16), w3_dense[j].astype(BF16),
                           w2_dense[j].astype(BF16))
        else:
            hn, gates = rmsnorm_router(h, norm_ffn_g[l], w_router[j])
            x2 = ffn_moe(hn, h, gates, w1_moe[j].astype(BF16), w3_moe[j].astype(BF16),
                         w2_moe[j].astype(BF16))
    return x2.reshape(batch, seq, d)
```

```python
import functools
import math

import numpy as np
import jax
import jax.numpy as jnp
from jax import lax
from jax.experimental import pallas as pl
from jax.experimental.pallas import tpu as pltpu

F32 = jnp.float32
BF16 = jnp.bfloat16
I32 = jnp.int32

EPS = 1e-6
ATT_HEADS = 8
ATT_KV_HEADS = 2
ATT_GROUP = ATT_HEADS // ATT_KV_HEADS
HEAD_DIM = 128
ATT_W = ATT_HEADS * HEAD_DIM
KV_W = ATT_KV_HEADS * HEAD_DIM
IDX_HEADS = 16
IDX_HEAD_DIM = 64
IDX_Q_RANK = 512
TOPK_MAX = 256
HGRN_HEADS = 8
HGRN_DIM = 128
HGRN_W = HGRN_HEADS * HGRN_DIM
REL_BUCKETS = 32
REL_MAX_DIST = 128
N_EXPERTS = 8
LANES = 128
INT_MIN = -(2 ** 31)
NEG_BIG = -1e30

TQ = 256
TK = 256
HC = 128
HGRN_LEVELS = (1, 2, 4, 8, 16, 32, 64)
N_ARG_GROUPS = len(HGRN_LEVELS) + 2

COL_Q, COL_KVC, COL_HQ, COL_HF, COL_HI, COL_HG, COL_GA, COL_GH = 0, 1, 2, 3, 4, 5, 6, 8
MAIN_W = 10 * 1024


def _cparams(sem, vmem_mb):
    return pltpu.CompilerParams(dimension_semantics=sem, vmem_limit_bytes=vmem_mb << 20)


def _dot(a, b):
    return jnp.dot(a, b, preferred_element_type=F32)


def _dot_nt(a, b):
    return lax.dot_general(a, b, (((1,), (1,)), ((), ())), preferred_element_type=F32)


def _sigmoid(x):
    return 1.0 / (1.0 + jnp.exp(-x))


def _rmsnorm_kernel(x_ref, g_ref, o_ref):
    x = x_ref[...]
    ms = jnp.mean(x * x, axis=-1, keepdims=True)
    o_ref[...] = (x * lax.rsqrt(ms + EPS) * g_ref[...]).astype(o_ref.dtype)


def rmsnorm_rows(x, g, tm=512):
    m, d = x.shape
    return pl.pallas_call(
        _rmsnorm_kernel,
        grid=(m // tm,),
        in_specs=[pl.BlockSpec((tm, d), lambda i: (i, 0)),
                  pl.BlockSpec((1, d), lambda i: (0, 0))],
        out_specs=pl.BlockSpec((tm, d), lambda i: (i, 0)),
        out_shape=jax.ShapeDtypeStruct((m, d), BF16),
        compiler_params=_cparams(("parallel",), 32),
        name="rmsnorm",
    )(x, g.reshape(1, d))


def _pack_bf16_pairs(x):
    c = x.shape[1] // 2
    lo = lax.bitcast_convert_type(x[:, :c].astype(BF16).astype(F32), I32)
    hi = lax.bitcast_convert_type(x[:, c:].astype(BF16).astype(F32), I32)
    return (hi & jnp.int32(-65536)) | lax.shift_right_logical(lo, 16)


def _unpack_bf16_pairs(p):
    lo = lax.bitcast_convert_type(lax.shift_left(p, 16), F32).astype(BF16)
    hi = lax.bitcast_convert_type(p & jnp.int32(-65536), F32).astype(BF16)
    return lo, hi


def _rmsnorm_router_kernel(x_ref, g_ref, wr_ref, tri_ref, o_ref, meta_ref, prob_ref, cnt_ref, carry_ref):
    @pl.when(pl.program_id(0) == 0)
    def _():
        carry_ref[...] = jnp.zeros_like(carry_ref)

    x = x_ref[...]
    ms = jnp.mean(x * x, axis=-1, keepdims=True)
    hn = x * lax.rsqrt(ms + EPS) * g_ref[...]
    o_ref[...] = _pack_bf16_pairs(hn)
    logits = jnp.dot(hn, wr_ref[...], precision=lax.Precision.HIGHEST,
                     preferred_element_type=F32)
    lane = lax.broadcasted_iota(I32, logits.shape, 1)
    logits = jnp.where(lane < N_EXPERTS, logits, -jnp.inf)
    v1 = jnp.max(logits, axis=-1, keepdims=True)
    i1 = jnp.min(jnp.where(logits == v1, lane, LANES), axis=-1, keepdims=True)
    rest = jnp.where(lane == i1, -jnp.inf, logits)
    v2 = jnp.max(rest, axis=-1, keepdims=True)
    i2 = jnp.min(jnp.where(rest == v2, lane, LANES), axis=-1, keepdims=True)
    e = jnp.exp(v2 - v1)
    p1 = 1.0 / (1.0 + e)
    prob_ref[...] = jnp.where(lane == 0, p1, 0.0) + jnp.where(lane == 1, e * p1, 0.0)
    hot = jnp.where((lane == i1) | (lane == i2), 1.0, 0.0)
    rank = _dot(tri_ref[...], hot.astype(BF16)) + carry_ref[0:1, :]
    carry_ref[...] = carry_ref[...] + jnp.sum(hot, axis=0, keepdims=True)
    cnt_ref[...] = carry_ref[...]
    r1 = jnp.sum(jnp.where(lane == i1, rank, 0.0), axis=-1, keepdims=True)
    r2 = jnp.sum(jnp.where(lane == i2, rank, 0.0), axis=-1, keepdims=True)
    meta = (jnp.where(lane == 0, i1, 0) + jnp.where(lane == 1, i2, 0)
            + jnp.where(lane == 2, r1.astype(I32), 0) + jnp.where(lane == 3, r2.astype(I32), 0))
    meta_ref[...] = meta


def rmsnorm_router(x, g, w_router, tm=512):
    m, d = x.shape
    wr = jnp.zeros((d, LANES), F32).at[:, :N_EXPERTS].set(w_router)
    tri = jnp.asarray(np.tril(np.ones((tm, tm), np.float32), -1), BF16)
    return pl.pallas_call(
        _rmsnorm_router_kernel,
        grid=(m // tm,),
        in_specs=[pl.BlockSpec((tm, d), lambda i: (i, 0)),
                  pl.BlockSpec((1, d), lambda i: (0, 0)),
                  pl.BlockSpec((d, LANES), lambda i: (0, 0)),
                  pl.BlockSpec((tm, tm), lambda i: (0, 0))],
        out_specs=[pl.BlockSpec((tm, d // 2), lambda i: (i, 0)),
                   pl.BlockSpec((tm, LANES), lambda i: (i, 0)),
                   pl.BlockSpec((tm, LANES), lambda i: (i, 0)),
                   pl.BlockSpec((8, LANES), lambda i: (0, 0))],
        out_shape=[jax.ShapeDtypeStruct((m, d // 2), I32),
                   jax.ShapeDtypeStruct((m, LANES), I32),
                   jax.ShapeDtypeStruct((m, LANES), F32),
                   jax.ShapeDtypeStruct((8, LANES), F32)],
        scratch_shapes=[pltpu.VMEM((8, LANES), F32)],
        compiler_params=_cparams(("arbitrary",), 40),
        name="rmsnorm_router",
    )(x, g.reshape(1, d), wr, tri)


def _matmul_kernel(x_ref, w_ref, o_ref):
    o_ref[...] = _dot(x_ref[...], w_ref[...]).astype(o_ref.dtype)


def matmul(x, w, out_dtype, tm=1024, tn=1024):
    m, k = x.shape
    n = w.shape[1]
    tm, tn = min(tm, m), min(tn, n)
    return pl.pallas_call(
        _matmul_kernel,
        grid=(m // tm, n // tn),
        in_specs=[pl.BlockSpec((tm, k), lambda i, j: (i, 0)),
                  pl.BlockSpec((k, tn), lambda i, j: (0, j))],
        out_specs=pl.BlockSpec((tm, tn), lambda i, j: (i, j)),
        out_shape=jax.ShapeDtypeStruct((m, n), out_dtype),
        compiler_params=_cparams(("parallel", "parallel"), 48),
        name="matmul",
    )(x, w)


def _matmul_residual_kernel(x_ref, w_ref, r_ref, o_ref):
    o_ref[...] = r_ref[...] + _dot(x_ref[...], w_ref[...])


def matmul_residual(x, w, res, tm=1024, tn=1024):
    m, k = x.shape
    n = w.shape[1]
    tm = min(tm, m)
    return pl.pallas_call(
        _matmul_residual_kernel,
        grid=(m // tm, n // tn),
        in_specs=[pl.BlockSpec((tm, k), lambda i, j: (i, 0)),
                  pl.BlockSpec((k, tn), lambda i, j: (0, j)),
                  pl.BlockSpec((tm, tn), lambda i, j: (i, j))],
        out_specs=pl.BlockSpec((tm, tn), lambda i, j: (i, j)),
        out_shape=jax.ShapeDtypeStruct((m, n), F32),
        compiler_params=_cparams(("parallel", "parallel"), 48),
        name="matmul_residual",
    )(x, w, res)


def _attn_prep_kernel(q_ref, kvc_ref, pi_ref, qg_ref, kg_ref, cg_ref, wiq_ref, lng_ref, lnb_ref,
                      qn_ref, kn_ref, v_ref, qi_ref, ki_ref, w_ref):
    def head_rms(x, g):
        return x * lax.rsqrt(jnp.mean(x * x, axis=-1, keepdims=True) + EPS) * g

    att_scale = HEAD_DIM ** -0.5
    for h in range(ATT_HEADS):
        sl = slice(h * HEAD_DIM, (h + 1) * HEAD_DIM)
        qn_ref[:, sl] = (head_rms(q_ref[:, sl], qg_ref[...]) * att_scale).astype(BF16)
    for g in range(ATT_KV_HEADS):
        sl = slice(g * HEAD_DIM, (g + 1) * HEAD_DIM)
        kn_ref[:, sl] = head_rms(kvc_ref[:, sl], kg_ref[...]).astype(BF16)
    v_ref[...] = kvc_ref[:, KV_W:2 * KV_W].astype(BF16)
    cq = kvc_ref[:, 2 * KV_W:2 * KV_W + IDX_Q_RANK]
    cqn = head_rms(cq, cg_ref[...]).astype(BF16)
    qi_ref[...] = _dot(cqn, wiq_ref[...]).astype(BF16)
    pi = pi_ref[...]
    is_key = lax.broadcasted_iota(I32, pi.shape, 1) < IDX_HEAD_DIM
    mu = jnp.sum(jnp.where(is_key, pi, 0.0), axis=-1, keepdims=True) * (1.0 / IDX_HEAD_DIM)
    cen = jnp.where(is_key, pi - mu, 0.0)
    var = jnp.sum(cen * cen, axis=-1, keepdims=True) * (1.0 / IDX_HEAD_DIM)
    kidn = cen * lax.rsqrt(var + EPS) * lng_ref[...] + lnb_ref[...]
    ki_ref[:, 0:LANES] = kidn.astype(BF16)
    ki_ref[:, LANES:2 * LANES] = pltpu.roll(kidn, IDX_HEAD_DIM, 1).astype(BF16)
    w_ref[...] = pi * (IDX_HEAD_DIM ** -0.5 * IDX_HEADS ** -0.5)


def attn_prep(proj_main, proj_idx, q_norm_g, k_norm_g, idx_q_norm_g, w_idx_q, ln_g, ln_b, tm=512):
    m = proj_main.shape[0]
    row = lambda i: (i, 0)
    const = lambda i: (0, 0)
    lane_pad = lambda a: jnp.zeros((1, LANES), F32).at[0, :a.shape[0]].set(a)
    return pl.pallas_call(
        _attn_prep_kernel,
        grid=(m // tm,),
        in_specs=[pl.BlockSpec((tm, 1024), lambda i: (i, COL_Q)),
                  pl.BlockSpec((tm, 1024), lambda i: (i, COL_KVC)),
                  pl.BlockSpec((tm, LANES), row),
                  pl.BlockSpec((1, HEAD_DIM), const),
                  pl.BlockSpec((1, HEAD_DIM), const),
                  pl.BlockSpec((1, IDX_Q_RANK), const),
                  pl.BlockSpec((IDX_Q_RANK, IDX_HEADS * IDX_HEAD_DIM), const),
                  pl.BlockSpec((1, LANES), const),
                  pl.BlockSpec((1, LANES), const)],
        out_specs=[pl.BlockSpec((tm, ATT_W), row),
                   pl.BlockSpec((tm, KV_W), row),
                   pl.BlockSpec((tm, KV_W), row),
                   pl.BlockSpec((tm, IDX_HEADS * IDX_HEAD_DIM), row),
                   pl.BlockSpec((tm, 2 * LANES), row),
                   pl.BlockSpec((tm, LANES), row)],
        out_shape=[jax.ShapeDtypeStruct((m, ATT_W), BF16),
                   jax.ShapeDtypeStruct((m, KV_W), BF16),
                   jax.ShapeDtypeStruct((m, KV_W), BF16),
                   jax.ShapeDtypeStruct((m, IDX_HEADS * IDX_HEAD_DIM), BF16),
                   jax.ShapeDtypeStruct((m, 2 * LANES), BF16),
                   jax.ShapeDtypeStruct((m, LANES), F32)],
        compiler_params=_cparams(("parallel",), 40),
        name="attn_prep",
    )(proj_main, proj_main, proj_idx, q_norm_g.reshape(1, -1), k_norm_g.reshape(1, -1),
      idx_q_norm_g.reshape(1, -1), w_idx_q, lane_pad(ln_g), lane_pad(ln_b))


def _rel_bucket(dist):
    max_exact = REL_BUCKETS // 2
    n = jnp.maximum(dist, 0)
    nf = jnp.maximum(n, 1).astype(F32)
    large = max_exact + (jnp.log(nf / max_exact) / math.log(REL_MAX_DIST / max_exact)
                         * (REL_BUCKETS - max_exact)).astype(I32)
    large = jnp.minimum(large, REL_BUCKETS - 1)
    return jnp.where(n < max_exact, n, large)


def _bias_kernel(rb_ref, bkt_ref, o_ref):
    h = pl.program_id(0)
    for kind in range(3):
        b = bkt_ref[kind]
        acc = jnp.zeros(b.shape, F32)
        for n in range(REL_BUCKETS):
            acc = jnp.where(b == n, rb_ref[n, h], acc)
        o_ref[0, kind] = acc


def bias_tables(rel_bias):
    assert TK >= REL_MAX_DIST
    kpos = jnp.arange(TK, dtype=I32)[:, None]
    qpos = jnp.arange(TQ, dtype=I32)[None, :]
    dist = jnp.stack([qpos - kpos, qpos - kpos + TK, qpos - kpos + 2 * TK])
    bkt = _rel_bucket(dist)
    return pl.pallas_call(
        _bias_kernel,
        grid=(ATT_HEADS,),
        in_specs=[pl.BlockSpec(memory_space=pltpu.SMEM),
                  pl.BlockSpec((3, TK, TQ), lambda h: (0, 0, 0))],
        out_specs=pl.BlockSpec((1, 3, TK, TQ), lambda h: (h, 0, 0, 0)),
        out_shape=jax.ShapeDtypeStruct((ATT_HEADS, 3, TK, TQ), F32),
        compiler_params=_cparams(("arbitrary",), 32),
        name="bias_tables",
    )(rel_bias, bkt)


def _dsa_kernel(qi_ref, wt_ref, qn_ref, ki_ref, kn_ref, vt_ref, bias_ref, o_ref, key_ref, *, n_sel):
    i = pl.program_id(1)
    nch = i + 1
    qpos = i * TQ + lax.broadcasted_iota(I32, (TK, TQ), 1)

    def score_chunk(j, carry):
        k0 = pl.multiple_of(j * TK, TK)
        acc = jnp.zeros((TK, TQ), F32)
        for h in range(IDX_HEADS):
            par = h % 2
            kc = ki_ref[pl.ds(k0, TK), par * LANES:(par + 1) * LANES]
            qh = qi_ref[:, (h // 2) * LANES:(h // 2 + 1) * LANES]
            s = _dot_nt(kc, qh)
            acc = acc + jnp.maximum(s, 0.0) * wt_ref[0, h:h + 1, :]
        bits = lax.bitcast_convert_type(acc, I32)
        key = jnp.where(bits < 0, bits ^ jnp.int32(0x7FFFFFFF), bits)
        kpos = k0 + lax.broadcasted_iota(I32, (TK, TQ), 0)
        key_ref[pl.ds(k0, TK), :] = jnp.where(kpos <= qpos, key, INT_MIN)
        return carry

    lax.fori_loop(0, nch, score_chunk, 0)

    def count_ge(cand):
        def body(j, acc):
            k0 = pl.multiple_of(j * TK, TK)
            hit = jnp.where(key_ref[pl.ds(k0, TK), :] >= cand, 1, 0).astype(I32)
            return acc + jnp.sum(hit.reshape(TK // 8, 8, TQ), axis=0)
        acc = lax.fori_loop(0, nch, body, jnp.zeros((8, TQ), I32))
        return jnp.sum(acc, axis=0, keepdims=True)

    thr = jnp.where(count_ge(jnp.zeros((1, TQ), I32)) >= n_sel, 0, INT_MIN).astype(I32)

    def bit_body(b, thr):
        cand = thr | jnp.left_shift(jnp.int32(1), 30 - b)
        return jnp.where(count_ge(cand) >= n_sel, cand, thr)

    thr = lax.fori_loop(0, 31, bit_body, thr)
    thr = jnp.maximum(thr, INT_MIN + 1)

    for h in range(ATT_HEADS):
        g = h // ATT_GROUP
        qh = qn_ref[:, h * HEAD_DIM:(h + 1) * HEAD_DIM]

        def att_chunk(j, carry):
            m, l, acc = carry
            k0 = pl.multiple_of(j * TK, TK)
            kc = kn_ref[pl.ds(k0, TK), g * HEAD_DIM:(g + 1) * HEAD_DIM]
            lg = _dot_nt(kc, qh) + bias_ref[h, jnp.minimum(i - j, 2)]
            sel = key_ref[pl.ds(k0, TK), :] >= thr
            lg = jnp.where(sel, lg, NEG_BIG)
            m_new = jnp.maximum(m, jnp.max(lg, axis=0, keepdims=True))
            alpha = jnp.exp(m - m_new)
            p = jnp.where(sel, jnp.exp(lg - m_new), 0.0)
            l = alpha * l + jnp.sum(p, axis=0, keepdims=True)
            vc = vt_ref[0, j, g * HEAD_DIM:(g + 1) * HEAD_DIM, :]
            acc = alpha * acc + _dot(vc, p.astype(BF16))
            return m_new, l, acc

        init = (jnp.full((1, TQ), NEG_BIG, F32), jnp.zeros((1, TQ), F32),
                jnp.zeros((HEAD_DIM, TQ), F32))
        _, l, acc = lax.fori_loop(0, nch, att_chunk, init)
        o_ref[:, h * HEAD_DIM:(h + 1) * HEAD_DIM] = (acc * (1.0 / l)).T.astype(o_ref.dtype)


def dsa_attention(qi, wt, qn, ki, kn, vt, bias, batch, seq):
    nq = seq // TQ
    n_sel = min(TOPK_MAX, seq // 4)
    qrow = lambda b, i: (b * nq + i, 0)
    brow = lambda b, i: (b, 0)
    return pl.pallas_call(
        functools.partial(_dsa_kernel, n_sel=n_sel),
        grid=(batch, nq),
        in_specs=[pl.BlockSpec((TQ, IDX_HEADS * IDX_HEAD_DIM), qrow),
                  pl.BlockSpec((1, IDX_HEADS, TQ), lambda b, i: (b, 0, i)),
                  pl.BlockSpec((TQ, ATT_W), qrow),
                  pl.BlockSpec((seq, 2 * LANES), brow),
                  pl.BlockSpec((seq, KV_W), brow),
                  pl.BlockSpec((1, seq // TK, KV_W, TK), lambda b, i: (b, 0, 0, 0)),
                  pl.BlockSpec((ATT_HEADS, 3, TK, TQ), lambda b, i: (0, 0, 0, 0))],
        out_specs=pl.BlockSpec((TQ, ATT_W), qrow),
        out_shape=jax.ShapeDtypeStruct((batch * seq, ATT_W), BF16),
        scratch_shapes=[pltpu.VMEM((seq, TQ), I32)],
        compiler_params=_cparams(("parallel", "arbitrary"), 48),
        name="dsa_attention",
    )(qi, wt, qn, ki, kn, vt, bias)


def _hgrn_constants():
    t = np.arange(HC)
    rows = []
    masks = [np.eye(HC, dtype=np.float32)]
    for m in HGRN_LEVELS:
        upper = (t // m) % 2 == 1
        start = (t // m) * m
        end = start + m - 1
        u = t[None, :]
        q_side = upper[:, None] & (u >= start[:, None]) & (u <= t[:, None])
        k_side = (~upper)[:, None] & (u > t[:, None]) & (u <= end[:, None])
        rows.append((q_side | k_side).astype(np.float32))
        same = (t[:, None] // (2 * m)) == (t[None, :] // (2 * m))
        masks.append((upper[:, None] & (~upper)[None, :] & same).astype(np.float32))
    u = t[None, :]
    rows.append((u <= t[:, None]).astype(np.float32))
    rows.append((u > t[:, None]).astype(np.float32))
    mat = np.concatenate(rows, axis=0)
    return np.concatenate([mat, mat], axis=1), np.stack(masks)


def _hgrn_kernel(hq_ref, hf_ref, hi_ref, hg_ref, lb_ref, gn_ref, mat_ref, mask_ref, o_ref,
                 st_ref, arg_ref, kk_ref, lf_ref):
    c = pl.program_id(1)

    @pl.when(c == 0)
    def _():
        st_ref[...] = jnp.zeros_like(st_ref)

    lb = lb_ref[...]
    f = hf_ref[...]
    e = jnp.exp(-jnp.abs(f))
    r = 1.0 / (1.0 + e)
    log_sig = jnp.minimum(f, 0.0) - jnp.log(1.0 + e)
    la = jnp.log(lb)
    lc = jnp.log(1.0 - lb) + log_sig
    logf = jnp.maximum(la, lc) + jnp.log(1.0 + jnp.exp(-jnp.abs(la - lc)))
    kk_ref[...] = (1.0 - lb) * jnp.where(f >= 0, e * r, r)
    hi = logf.astype(BF16)
    lf_ref[0:HC, :] = hi
    lf_ref[HC:2 * HC, :] = (logf - hi.astype(F32)).astype(BF16)
    arg_ref[...] = _dot(mat_ref[...], lf_ref[...])

    scale = HGRN_DIM ** -0.5
    for h in range(HGRN_HEADS):
        sl = slice(h * HGRN_DIM, (h + 1) * HGRN_DIM)
        hq = hq_ref[:, sl]
        q = hq * _sigmoid(hq) * scale
        k = kk_ref[:, sl]
        v = hi_ref[:, sl]
        a = _dot_nt(q.astype(BF16), k.astype(BF16)) * mask_ref[0]
        for lv in range(len(HGRN_LEVELS)):
            ex = jnp.exp(arg_ref[lv * HC:(lv + 1) * HC, sl])
            a = a + _dot_nt((q * ex).astype(BF16), (k * ex).astype(BF16)) * mask_ref[lv + 1]
        nl = len(HGRN_LEVELS)
        eq = jnp.exp(arg_ref[nl * HC:(nl + 1) * HC, sl])
        ek = jnp.exp(arg_ref[(nl + 1) * HC:(nl + 2) * HC, sl])
        st = st_ref[h]
        vb = v.astype(BF16)
        o = _dot(a.astype(BF16), vb) + _dot_nt((q * eq).astype(BF16), st.astype(BF16))
        st_ref[h] = eq[HC - 1:HC, :] * st + _dot(v.T.astype(BF16), (k * ek).astype(BF16))
        on = o * lax.rsqrt(jnp.mean(o * o, axis=-1, keepdims=True) + EPS) * gn_ref[...]
        hg = hg_ref[:, sl]
        o_ref[:, sl] = (on * hg * _sigmoid(hg)).astype(o_ref.dtype)


def hgrn2(proj_main, lb, g_norm, batch, seq):
    nc = seq // HC
    mat, masks = _hgrn_constants()
    col = lambda cb: (lambda b, c: (b * nc + c, cb))
    return pl.pallas_call(
        _hgrn_kernel,
        grid=(batch, nc),
        in_specs=[pl.BlockSpec((HC, HGRN_W), col(COL_HQ)),
                  pl.BlockSpec((HC, HGRN_W), col(COL_HF)),
                  pl.BlockSpec((HC, HGRN_W), col(COL_HI)),
                  pl.BlockSpec((HC, HGRN_W), col(COL_HG)),
                  pl.BlockSpec((1, HGRN_W), lambda b, c: (0, 0)),
                  pl.BlockSpec((1, HGRN_DIM), lambda b, c: (0, 0)),
                  pl.BlockSpec((N_ARG_GROUPS * HC, 2 * HC), lambda b, c: (0, 0)),
                  pl.BlockSpec((len(HGRN_LEVELS) + 1, HC, HC), lambda b, c: (0, 0, 0))],
        out_specs=pl.BlockSpec((HC, HGRN_W), lambda b, c: (b * nc + c, 0)),
        out_shape=jax.ShapeDtypeStruct((batch * seq, HGRN_W), BF16),
        scratch_shapes=[pltpu.VMEM((HGRN_HEADS, HGRN_DIM, HGRN_DIM), F32),
                        pltpu.VMEM((N_ARG_GROUPS * HC, HGRN_W), F32),
                        pltpu.VMEM((HC, HGRN_W), F32),
                        pltpu.VMEM((2 * HC, HGRN_W), BF16)],
        compiler_params=_cparams(("parallel", "arbitrary"), 40),
        name="hgrn2",
    )(proj_main, proj_main, proj_main, proj_main, lb.reshape(1, -1), g_norm.reshape(1, -1),
      jnp.asarray(mat, BF16), jnp.asarray(masks, F32))


def _merge_kernel(att_ref, rec_ref, ga_ref, gh_ref, wa_ref, wh_ref, o_ref):
    a = _dot(att_ref[...], wa_ref[...])
    r = _dot(rec_ref[...], wh_ref[...])
    o_ref[...] = (_sigmoid(ga_ref[...]) * a + _sigmoid(gh_ref[...]) * r).astype(o_ref.dtype)


def gated_merge(att, rec, proj_main, w_up_att, w_up_hgrn, tm=512, tn=1024):
    m = att.shape[0]
    n = w_up_att.shape[1]
    return pl.pallas_call(
        _merge_kernel,
        grid=(m // tm, n // tn),
        in_specs=[pl.BlockSpec((tm, ATT_W), lambda i, j: (i, 0)),
                  pl.BlockSpec((tm, HGRN_W), lambda i, j: (i, 0)),
                  pl.BlockSpec((tm, tn), lambda i, j: (i, COL_GA + j)),
                  pl.BlockSpec((tm, tn), lambda i, j: (i, COL_GH + j)),
                  pl.BlockSpec((ATT_W, tn), lambda i, j: (0, j)),
                  pl.BlockSpec((HGRN_W, tn), lambda i, j: (0, j))],
        out_specs=pl.BlockSpec((tm, tn), lambda i, j: (i, j)),
        out_shape=jax.ShapeDtypeStruct((m, n), BF16),
        compiler_params=_cparams(("parallel", "parallel"), 48),
        name="gated_merge",
    )(att, rec, proj_main, proj_main, w_up_att, w_up_hgrn)


def _ffn_kernel(hn_ref, h_ref, w1_ref, w3_ref, w2_ref, o_ref):
    @pl.when(pl.program_id(1) == 0)
    def _():
        o_ref[...] = h_ref[...]

    x = hn_ref[...]
    a = _dot(x, w1_ref[...])
    b = _dot(x, w3_ref[...])
    act = (a * _sigmoid(a) * b).astype(BF16)
    o_ref[...] += _dot(act, w2_ref[...])


def ffn_dense(hn, h, w1, w3, w2, tm=512, tf=512):
    m, d = hn.shape
    f = w1.shape[1]
    return pl.pallas_call(
        _ffn_kernel,
        grid=(m // tm, f // tf),
        in_specs=[pl.BlockSpec((tm, d), lambda i, j: (i, 0)),
                  pl.BlockSpec((tm, d), lambda i, j: (i, 0)),
                  pl.BlockSpec((d, tf), lambda i, j: (0, j)),
                  pl.BlockSpec((d, tf), lambda i, j: (0, j)),
                  pl.BlockSpec((tf, d), lambda i, j: (j, 0))],
        out_specs=pl.BlockSpec((tm, d), lambda i, j: (i, 0)),
        out_shape=jax.ShapeDtypeStruct((m, d), F32),
        compiler_params=_cparams(("parallel", "arbitrary"), 48),
        name="ffn_dense",
    )(hn, h, w1, w3, w2)


def _row_copy(src_ref, src_row, dst_ref, dst_row, sem):
    return pltpu.make_async_copy(src_ref.at[pl.ds(src_row, 1)], dst_ref.at[pl.ds(dst_row, 1)], sem)


def _moe_scatter_kernel(dest_ref, hn_ref, xs_init_ref, xs_ref, sem, *, tm, m):
    del xs_init_ref
    base = pl.program_id(0) * tm

    def issue(r, carry):
        for k in range(2):
            _row_copy(hn_ref, r, xs_ref, dest_ref[k * m + base + r], sem).start()
        return carry

    lax.fori_loop(0, tm, issue, 0, unroll=8)
    for k in range(2):
        pltpu.make_async_copy(hn_ref, xs_ref.at[pl.ds(0, tm)], sem).wait()


def moe_scatter(hn_packed, dest, n_rows, tm=512):
    m, c = hn_packed.shape
    xs_init = jnp.zeros((n_rows, c), I32)
    return pl.pallas_call(
        functools.partial(_moe_scatter_kernel, tm=tm, m=m),
        grid_spec=pltpu.PrefetchScalarGridSpec(
            num_scalar_prefetch=1,
            grid=(m // tm,),
            in_specs=[pl.BlockSpec((tm, c), lambda i, dest: (i, 0)),
                      pl.BlockSpec(memory_space=pl.ANY)],
            out_specs=pl.BlockSpec(memory_space=pl.ANY),
            scratch_shapes=[pltpu.SemaphoreType.DMA(())]),
        out_shape=jax.ShapeDtypeStruct((n_rows, c), I32),
        input_output_aliases={2: 0},
        compiler_params=_cparams(("arbitrary",), 32),
        name="moe_scatter",
    )(dest, hn_packed, xs_init)


def _moe_ffn_kernel(te_ref, tv_ref, xs_ref, w1_ref, w3_ref, w2_ref, o_ref, xb_ref):
    del te_ref
    i = pl.program_id(0)
    half = xs_ref.shape[1]

    @pl.when(pl.program_id(1) == 0)
    def _():
        o_ref[...] = jnp.zeros_like(o_ref)
        lo, hi = _unpack_bf16_pairs(xs_ref[...])
        xb_ref[:, 0:half] = lo
        xb_ref[:, half:2 * half] = hi

    @pl.when(tv_ref[i] == 1)
    def _():
        x = xb_ref[...]
        a = _dot(x, w1_ref[0])
        b = _dot(x, w3_ref[0])
        act = (a * _sigmoid(a) * b).astype(BF16)
        o_ref[...] += _dot(act, w2_ref[0])


def moe_ffn(xs, tile_expert, tile_valid, w1, w3, w2, tm=512, tf=512):
    p, c = xs.shape
    ne, d, f = w1.shape
    nf = f // tf
    col = lambda i, j, te, tv: jnp.where(tv[i] == 1, j, nf - 1)
    return pl.pallas_call(
        _moe_ffn_kernel,
        grid_spec=pltpu.PrefetchScalarGridSpec(
            num_scalar_prefetch=2,
            grid=(p // tm, nf),
            in_specs=[pl.BlockSpec((tm, c), lambda i, j, te, tv: (i, 0)),
                      pl.BlockSpec((1, d, tf), lambda i, j, te, tv: (te[i], 0, col(i, j, te, tv))),
                      pl.BlockSpec((1, d, tf), lambda i, j, te, tv: (te[i], 0, col(i, j, te, tv))),
                      pl.BlockSpec((1, tf, d), lambda i, j, te, tv: (te[i], col(i, j, te, tv), 0))],
            out_specs=pl.BlockSpec((tm, d), lambda i, j, te, tv: (i, 0)),
            scratch_shapes=[pltpu.VMEM((tm, d), BF16)]),
        out_shape=jax.ShapeDtypeStruct((p, d), F32),
        compiler_params=_cparams(("arbitrary", "arbitrary"), 48),
        name="moe_ffn",
    )(tile_expert, tile_valid, xs, w1, w3, w2)


def _moe_combine_kernel(dest_ref, h_ref, prob_ref, ys_ref, o_ref, ybuf_ref, sem, *, tm, m):
    base = pl.program_id(0) * tm

    def issue(r, carry):
        for k in range(2):
            _row_copy(ys_ref, dest_ref[k * m + base + r], ybuf_ref.at[k], r, sem).start()
        return carry

    lax.fori_loop(0, tm, issue, 0, unroll=8)
    for k in range(2):
        pltpu.make_async_copy(ys_ref.at[pl.ds(0, tm)], ybuf_ref.at[k], sem).wait()
    prob = prob_ref[...]
    o_ref[...] = h_ref[...] + prob[:, 0:1] * ybuf_ref[0] + prob[:, 1:2] * ybuf_ref[1]


def moe_combine(h, prob, ys, dest, tm=512):
    m, d = h.shape
    return pl.pallas_call(
        functools.partial(_moe_combine_kernel, tm=tm, m=m),
        grid_spec=pltpu.PrefetchScalarGridSpec(
            num_scalar_prefetch=1,
            grid=(m // tm,),
            in_specs=[pl.BlockSpec((tm, d), lambda i, dest: (i, 0)),
                      pl.BlockSpec((tm, LANES), lambda i, dest: (i, 0)),
                      pl.BlockSpec(memory_space=pl.ANY)],
            out_specs=pl.BlockSpec((tm, d), lambda i, dest: (i, 0)),
            scratch_shapes=[pltpu.VMEM((2, tm, d), F32), pltpu.SemaphoreType.DMA(())]),
        out_shape=jax.ShapeDtypeStruct((m, d), F32),
        compiler_params=_cparams(("arbitrary",), 40),
        name="moe_combine",
    )(dest, h, prob, ys)


def ffn_moe(h, norm_g, w_router, w1, w3, w2, tm=512):
    m, d = h.shape
    ne = w1.shape[0]
    hn_packed, meta, prob, cnt = rmsnorm_router(h, norm_g, w_router)
    counts = cnt[0, :ne].astype(I32)
    tiles_per = (counts + tm - 1) // tm
    tile_end = jnp.cumsum(tiles_per)
    offset = (tile_end - tiles_per) * tm
    dest = jnp.concatenate([offset[meta[:, 0]] + meta[:, 2], offset[meta[:, 1]] + meta[:, 3]])
    n_tiles = (2 * m) // tm + ne
    tile_ids = jnp.arange(n_tiles, dtype=I32)
    tile_valid = (tile_ids < tile_end[-1]).astype(I32)
    tile_expert = jnp.searchsorted(tile_end, jnp.minimum(tile_ids, tile_end[-1] - 1), side="right").astype(I32)
    xs = moe_scatter(hn_packed, dest, n_tiles * tm)
    ys = moe_ffn(xs, tile_expert, tile_valid, w1, w3, w2, tm=tm)
    return moe_combine(h, prob, ys, dest)


def _split_w_in(w_in):
    sizes = [ATT_W, KV_W, KV_W, IDX_Q_RANK, IDX_HEAD_DIM, IDX_HEADS,
             HGRN_W, HGRN_W, HGRN_W, HGRN_W, w_in.shape[0], w_in.shape[0]]
    offs = np.cumsum([0] + sizes)
    part = lambda n: w_in[:, offs[n]:offs[n + 1]]
    main = jnp.concatenate([part(n) for n in (0, 1, 2, 3, 6, 7, 8, 9, 10, 11)], axis=1)
    pad = jnp.zeros((w_in.shape[0], LANES - IDX_HEAD_DIM - IDX_HEADS), w_in.dtype)
    idx = jnp.concatenate([part(4), part(5), pad], axis=1)
    return main.astype(BF16), idx.astype(BF16)


def _mixer(x2, batch, seq, norm_g, w_in, q_norm_g, k_norm_g, idx_q_norm_g, w_idx_q, ln_g, ln_b,
           lb, hgrn_norm_g, w_up_att, w_up_hgrn, w_o, bias):
    m = batch * seq
    xn = rmsnorm_rows(x2, norm_g)
    w_main, w_idx = _split_w_in(w_in)
    proj_main = matmul(xn, w_main, F32)
    proj_idx = matmul(xn, w_idx, F32)
    qn, kn, v, qi, ki, w = attn_prep(proj_main, proj_idx, q_norm_g, k_norm_g, idx_q_norm_g,
                                     w_idx_q.astype(BF16), ln_g, ln_b)
    wt = w[:, IDX_HEAD_DIM:IDX_HEAD_DIM + IDX_HEADS].reshape(batch, seq, IDX_HEADS).transpose(0, 2, 1)
    vt = v.reshape(batch, seq // TK, TK, KV_W).transpose(0, 1, 3, 2)
    att = dsa_attention(qi, wt, qn, ki, kn, vt, bias, batch, seq)
    rec = hgrn2(proj_main, lb, hgrn_norm_g, batch, seq)
    merged = gated_merge(att, rec, proj_main, w_up_att.astype(BF16), w_up_hgrn.astype(BF16))
    return matmul_residual(merged, w_o.astype(BF16), x2)


def kernel(x, rel_bias, norm_mix_g, norm_ffn_g, w_in, q_norm_g, k_norm_g, idx_q_norm_g, w_idx_q,
           idx_k_ln_g, idx_k_ln_b, hgrn_lb_logits, hgrn_out_norm_g, w_up_att, w_up_hgrn, w_o,
           w1_dense, w3_dense, w2_dense, w_router, w1_moe, w3_moe, w2_moe):
    batch, seq, d = x.shape
    depth = w_in.shape[0]
    lb_all = jnp.cumsum(jax.nn.softmax(hgrn_lb_logits.astype(F32), axis=0), axis=0)
    lb_all = lb_all - lb_all[0:1]
    bias = bias_tables(rel_bias)
    x2 = x.reshape(batch * seq, d)
    for l in range(depth):
        h = _mixer(x2, batch, seq, norm_mix_g[l], w_in[l], q_norm_g[l], k_norm_g[l],
                   idx_q_norm_g[l], w_idx_q[l], idx_k_ln_g[l], idx_k_ln_b[l], lb_all[l],
                   hgrn_out_norm_g[l], w_up_att[l], w_up_hgrn[l], w_o[l], bias)
        j = l // 2
        if l % 2 == 0:
            hn = rmsnorm_rows(h, norm_ffn_g[l])
            x2 = ffn_dense(hn, h, w1_dense[j].astype(BF16), w3_dense[j].astype(BF16),
                           w2_dense[j].astype(BF16))
        else:
            x2 = ffn_moe(h, norm_ffn_g[l], w_router[j], w1_moe[j].astype(BF16),
                         w3_moe[j].astype(BF16), w2_moe[j].astype(BF16))
    return x2.reshape(batch, seq, d)
```

```python
import functools
import math

import numpy as np
import jax
import jax.numpy as jnp
from jax import lax
from jax.experimental import pallas as pl
from jax.experimental.pallas import tpu as pltpu

F32 = jnp.float32
BF16 = jnp.bfloat16
I32 = jnp.int32

EPS = 1e-6
ATT_HEADS = 8
ATT_KV_HEADS = 2
ATT_GROUP = ATT_HEADS // ATT_KV_HEADS
HEAD_DIM = 128
ATT_W = ATT_HEADS * HEAD_DIM
KV_W = ATT_KV_HEADS * HEAD_DIM
IDX_HEADS = 16
IDX_HEAD_DIM = 64
IDX_Q_RANK = 512
TOPK_MAX = 256
HGRN_HEADS = 8
HGRN_DIM = 128
HGRN_W = HGRN_HEADS * HGRN_DIM
REL_BUCKETS = 32
REL_MAX_DIST = 128
N_EXPERTS = 8
LANES = 128
INT_MIN = -(2 ** 31)
NEG_BIG = -1e30
LOG2E = math.log2(math.e)

TQ = 256
TK = 256
HC = 128
HGRN_LEVELS = (1, 2, 4, 8, 16, 32, 64)
N_ARG_GROUPS = len(HGRN_LEVELS) + 2

COL_Q, COL_KVC, COL_HQ, COL_HF, COL_HI, COL_HG, COL_GA, COL_GH = 0, 1, 2, 3, 4, 5, 6, 8
MAIN_W = 10 * 1024


def _cparams(sem, vmem_mb):
    return pltpu.CompilerParams(dimension_semantics=sem, vmem_limit_bytes=vmem_mb << 20)


def _dot(a, b):
    return jnp.dot(a, b, preferred_element_type=F32)


def _dot_nt(a, b):
    return lax.dot_general(a, b, (((1,), (1,)), ((), ())), preferred_element_type=F32)


def _sigmoid(x):
    return 1.0 / (1.0 + jnp.exp(-x))


def _rmsnorm_kernel(x_ref, g_ref, o_ref):
    x = x_ref[...]
    ms = jnp.mean(x * x, axis=-1, keepdims=True)
    o_ref[...] = (x * lax.rsqrt(ms + EPS) * g_ref[...]).astype(o_ref.dtype)


def rmsnorm_rows(x, g, tm=512):
    m, d = x.shape
    return pl.pallas_call(
        _rmsnorm_kernel,
        grid=(m // tm,),
        in_specs=[pl.BlockSpec((tm, d), lambda i: (i, 0)),
                  pl.BlockSpec((1, d), lambda i: (0, 0))],
        out_specs=pl.BlockSpec((tm, d), lambda i: (i, 0)),
        out_shape=jax.ShapeDtypeStruct((m, d), BF16),
        compiler_params=_cparams(("parallel",), 32),
        name="rmsnorm",
    )(x, g.reshape(1, d))


def _pack_bf16_pairs(x):
    c = x.shape[1] // 2
    lo = lax.bitcast_convert_type(x[:, :c].astype(BF16).astype(F32), I32)
    hi = lax.bitcast_convert_type(x[:, c:].astype(BF16).astype(F32), I32)
    return (hi & jnp.int32(-65536)) | lax.shift_right_logical(lo, 16)


def _unpack_bf16_pairs(p):
    lo = lax.bitcast_convert_type(lax.shift_left(p, 16), F32).astype(BF16)
    hi = lax.bitcast_convert_type(p & jnp.int32(-65536), F32).astype(BF16)
    return lo, hi


def _rmsnorm_router_kernel(x_ref, g_ref, wr_ref, tri_ref, o_ref, meta_ref, prob_ref, cnt_ref, carry_ref):
    @pl.when(pl.program_id(0) == 0)
    def _():
        carry_ref[...] = jnp.zeros_like(carry_ref)

    x = x_ref[...]
    ms = jnp.mean(x * x, axis=-1, keepdims=True)
    hn = x * lax.rsqrt(ms + EPS) * g_ref[...]
    o_ref[...] = _pack_bf16_pairs(hn)
    logits = jnp.dot(hn, wr_ref[...], precision=lax.Precision.HIGHEST,
                     preferred_element_type=F32)
    lane = lax.broadcasted_iota(I32, logits.shape, 1)
    logits = jnp.where(lane < N_EXPERTS, logits, -jnp.inf)
    v1 = jnp.max(logits, axis=-1, keepdims=True)
    i1 = jnp.min(jnp.where(logits == v1, lane, LANES), axis=-1, keepdims=True)
    rest = jnp.where(lane == i1, -jnp.inf, logits)
    v2 = jnp.max(rest, axis=-1, keepdims=True)
    i2 = jnp.min(jnp.where(rest == v2, lane, LANES), axis=-1, keepdims=True)
    e = jnp.exp(v2 - v1)
    p1 = 1.0 / (1.0 + e)
    prob_ref[...] = jnp.where(lane == 0, p1, 0.0) + jnp.where(lane == 1, e * p1, 0.0)
    hot = jnp.where((lane == i1) | (lane == i2), 1.0, 0.0)
    rank = _dot(tri_ref[...], hot.astype(BF16)) + carry_ref[0:1, :]
    carry_ref[...] = carry_ref[...] + jnp.sum(hot, axis=0, keepdims=True)
    cnt_ref[...] = carry_ref[...]
    r1 = jnp.sum(jnp.where(lane == i1, rank, 0.0), axis=-1, keepdims=True)
    r2 = jnp.sum(jnp.where(lane == i2, rank, 0.0), axis=-1, keepdims=True)
    meta = (jnp.where(lane == 0, i1, 0) + jnp.where(lane == 1, i2, 0)
            + jnp.where(lane == 2, r1.astype(I32), 0) + jnp.where(lane == 3, r2.astype(I32), 0))
    meta_ref[...] = meta


def rmsnorm_router(x, g, w_router, tm=512):
    m, d = x.shape
    wr = jnp.zeros((d, LANES), F32).at[:, :N_EXPERTS].set(w_router)
    tri = jnp.asarray(np.tril(np.ones((tm, tm), np.float32), -1), BF16)
    return pl.pallas_call(
        _rmsnorm_router_kernel,
        grid=(m // tm,),
        in_specs=[pl.BlockSpec((tm, d), lambda i: (i, 0)),
                  pl.BlockSpec((1, d), lambda i: (0, 0)),
                  pl.BlockSpec((d, LANES), lambda i: (0, 0)),
                  pl.BlockSpec((tm, tm), lambda i: (0, 0))],
        out_specs=[pl.BlockSpec((tm, d // 2), lambda i: (i, 0)),
                   pl.BlockSpec((tm, LANES), lambda i: (i, 0)),
                   pl.BlockSpec((tm, LANES), lambda i: (i, 0)),
                   pl.BlockSpec((8, LANES), lambda i: (0, 0))],
        out_shape=[jax.ShapeDtypeStruct((m, d // 2), I32),
                   jax.ShapeDtypeStruct((m, LANES), I32),
                   jax.ShapeDtypeStruct((m, LANES), F32),
                   jax.ShapeDtypeStruct((8, LANES), F32)],
        scratch_shapes=[pltpu.VMEM((8, LANES), F32)],
        compiler_params=_cparams(("arbitrary",), 40),
        name="rmsnorm_router",
    )(x, g.reshape(1, d), wr, tri)


def _matmul_kernel(x_ref, w_ref, o_ref):
    o_ref[...] = _dot(x_ref[...], w_ref[...]).astype(o_ref.dtype)


def matmul(x, w, out_dtype, tm=1024, tn=1024):
    m, k = x.shape
    n = w.shape[1]
    tm, tn = min(tm, m), min(tn, n)
    return pl.pallas_call(
        _matmul_kernel,
        grid=(m // tm, n // tn),
        in_specs=[pl.BlockSpec((tm, k), lambda i, j: (i, 0)),
                  pl.BlockSpec((k, tn), lambda i, j: (0, j))],
        out_specs=pl.BlockSpec((tm, tn), lambda i, j: (i, j)),
        out_shape=jax.ShapeDtypeStruct((m, n), out_dtype),
        compiler_params=_cparams(("parallel", "parallel"), 48),
        name="matmul",
    )(x, w)


def _matmul_residual_kernel(x_ref, w_ref, r_ref, o_ref):
    o_ref[...] = r_ref[...] + _dot(x_ref[...], w_ref[...])


def matmul_residual(x, w, res, tm=1024, tn=1024):
    m, k = x.shape
    n = w.shape[1]
    tm = min(tm, m)
    return pl.pallas_call(
        _matmul_residual_kernel,
        grid=(m // tm, n // tn),
        in_specs=[pl.BlockSpec((tm, k), lambda i, j: (i, 0)),
                  pl.BlockSpec((k, tn), lambda i, j: (0, j)),
                  pl.BlockSpec((tm, tn), lambda i, j: (i, j))],
        out_specs=pl.BlockSpec((tm, tn), lambda i, j: (i, j)),
        out_shape=jax.ShapeDtypeStruct((m, n), F32),
        compiler_params=_cparams(("parallel", "parallel"), 48),
        name="matmul_residual",
    )(x, w, res)


def _attn_prep_kernel(q_ref, kvc_ref, pi_ref, qg_ref, kg_ref, cg_ref, wiq_ref, lng_ref, lnb_ref,
                      qn_ref, kn_ref, v_ref, qi_ref, ki_ref, w_ref):
    def head_rms(x, g):
        return x * lax.rsqrt(jnp.mean(x * x, axis=-1, keepdims=True) + EPS) * g

    att_scale = HEAD_DIM ** -0.5 * LOG2E
    for h in range(ATT_HEADS):
        sl = slice(h * HEAD_DIM, (h + 1) * HEAD_DIM)
        qn_ref[:, sl] = (head_rms(q_ref[:, sl], qg_ref[...]) * att_scale).astype(BF16)
    for g in range(ATT_KV_HEADS):
        sl = slice(g * HEAD_DIM, (g + 1) * HEAD_DIM)
        kn_ref[:, sl] = head_rms(kvc_ref[:, sl], kg_ref[...]).astype(BF16)
    v_ref[...] = kvc_ref[:, KV_W:2 * KV_W].astype(BF16)
    cq = kvc_ref[:, 2 * KV_W:2 * KV_W + IDX_Q_RANK]
    cqn = head_rms(cq, cg_ref[...]).astype(BF16)
    qi_ref[...] = _dot(cqn, wiq_ref[...]).astype(BF16)
    pi = pi_ref[...]
    is_key = lax.broadcasted_iota(I32, pi.shape, 1) < IDX_HEAD_DIM
    mu = jnp.sum(jnp.where(is_key, pi, 0.0), axis=-1, keepdims=True) * (1.0 / IDX_HEAD_DIM)
    cen = jnp.where(is_key, pi - mu, 0.0)
    var = jnp.sum(cen * cen, axis=-1, keepdims=True) * (1.0 / IDX_HEAD_DIM)
    kidn = cen * lax.rsqrt(var + EPS) * lng_ref[...] + lnb_ref[...]
    ki_ref[:, 0:LANES] = kidn.astype(BF16)
    ki_ref[:, LANES:2 * LANES] = pltpu.roll(kidn, IDX_HEAD_DIM, 1).astype(BF16)
    w_ref[...] = pi * (IDX_HEAD_DIM ** -0.5 * IDX_HEADS ** -0.5)


def attn_prep(proj_main, proj_idx, q_norm_g, k_norm_g, idx_q_norm_g, w_idx_q, ln_g, ln_b, tm=512):
    m = proj_main.shape[0]
    row = lambda i: (i, 0)
    const = lambda i: (0, 0)
    lane_pad = lambda a: jnp.zeros((1, LANES), F32).at[0, :a.shape[0]].set(a)
    return pl.pallas_call(
        _attn_prep_kernel,
        grid=(m // tm,),
        in_specs=[pl.BlockSpec((tm, 1024), lambda i: (i, COL_Q)),
                  pl.BlockSpec((tm, 1024), lambda i: (i, COL_KVC)),
                  pl.BlockSpec((tm, LANES), row),
                  pl.BlockSpec((1, HEAD_DIM), const),
                  pl.BlockSpec((1, HEAD_DIM), const),
                  pl.BlockSpec((1, IDX_Q_RANK), const),
                  pl.BlockSpec((IDX_Q_RANK, IDX_HEADS * IDX_HEAD_DIM), const),
                  pl.BlockSpec((1, LANES), const),
                  pl.BlockSpec((1, LANES), const)],
        out_specs=[pl.BlockSpec((tm, ATT_W), row),
                   pl.BlockSpec((tm, KV_W), row),
                   pl.BlockSpec((tm, KV_W), row),
                   pl.BlockSpec((tm, IDX_HEADS * IDX_HEAD_DIM), row),
                   pl.BlockSpec((tm, 2 * LANES), row),
                   pl.BlockSpec((tm, LANES), row)],
        out_shape=[jax.ShapeDtypeStruct((m, ATT_W), BF16),
                   jax.ShapeDtypeStruct((m, KV_W), BF16),
                   jax.ShapeDtypeStruct((m, KV_W), BF16),
                   jax.ShapeDtypeStruct((m, IDX_HEADS * IDX_HEAD_DIM), BF16),
                   jax.ShapeDtypeStruct((m, 2 * LANES), BF16),
                   jax.ShapeDtypeStruct((m, LANES), F32)],
        compiler_params=_cparams(("parallel",), 40),
        name="attn_prep",
    )(proj_main, proj_main, proj_idx, q_norm_g.reshape(1, -1), k_norm_g.reshape(1, -1),
      idx_q_norm_g.reshape(1, -1), w_idx_q, lane_pad(ln_g), lane_pad(ln_b))


def _rel_bucket(dist):
    max_exact = REL_BUCKETS // 2
    n = jnp.maximum(dist, 0)
    nf = jnp.maximum(n, 1).astype(F32)
    large = max_exact + (jnp.log(nf / max_exact) / math.log(REL_MAX_DIST / max_exact)
                         * (REL_BUCKETS - max_exact)).astype(I32)
    large = jnp.minimum(large, REL_BUCKETS - 1)
    return jnp.where(n < max_exact, n, large)


def _bias_kernel(rb_ref, bkt_ref, o_ref):
    h = pl.program_id(0)
    for kind in range(3):
        b = bkt_ref[kind]
        acc = jnp.zeros(b.shape, F32)
        for n in range(REL_BUCKETS):
            acc = jnp.where(b == n, rb_ref[n, h], acc)
        o_ref[0, kind] = acc * LOG2E


def bias_tables(rel_bias):
    assert TK >= REL_MAX_DIST
    kpos = jnp.arange(TK, dtype=I32)[:, None]
    qpos = jnp.arange(TQ, dtype=I32)[None, :]
    dist = jnp.stack([qpos - kpos, qpos - kpos + TK, qpos - kpos + 2 * TK])
    bkt = _rel_bucket(dist)
    return pl.pallas_call(
        _bias_kernel,
        grid=(ATT_HEADS,),
        in_specs=[pl.BlockSpec(memory_space=pltpu.SMEM),
                  pl.BlockSpec((3, TK, TQ), lambda h: (0, 0, 0))],
        out_specs=pl.BlockSpec((1, 3, TK, TQ), lambda h: (h, 0, 0, 0)),
        out_shape=jax.ShapeDtypeStruct((ATT_HEADS, 3, TK, TQ), F32),
        compiler_params=_cparams(("arbitrary",), 32),
        name="bias_tables",
    )(rel_bias, bkt)


def _dsa_kernel(qi_ref, wt_ref, qn_ref, ki_ref, kn_ref, vt_ref, bias_ref, o_ref,
                key_ref, m_ref, l_ref, a_ref, acc_ref, s_ref, p_ref, *, n_sel):
    i = pl.program_id(1)
    nch = i + 1
    qpos = i * TQ + lax.broadcasted_iota(I32, (TK, TQ), 1)

    def score_chunk(j, carry):
        k0 = pl.multiple_of(j * TK, TK)
        acc = jnp.zeros((TK, TQ), F32)
        for h in range(IDX_HEADS):
            par = h % 2
            kc = ki_ref[pl.ds(k0, TK), par * LANES:(par + 1) * LANES]
            qh = qi_ref[:, (h // 2) * LANES:(h // 2 + 1) * LANES]
            s = _dot_nt(kc, qh)
            acc = acc + jnp.maximum(s, 0.0) * wt_ref[0, h:h + 1, :]
        bits = lax.bitcast_convert_type(acc, I32)
        key = jnp.where(bits < 0, bits ^ jnp.int32(0x7FFFFFFF), bits)
        kpos = k0 + lax.broadcasted_iota(I32, (TK, TQ), 0)
        key_ref[pl.ds(k0, TK), :] = jnp.where(kpos <= qpos, key, INT_MIN)
        return carry

    lax.fori_loop(0, nch, score_chunk, 0)

    def count_ge(cand):
        def body(j, acc):
            k0 = pl.multiple_of(j * TK, TK)
            hit = jnp.where(key_ref[pl.ds(k0, TK), :] >= cand, 1, 0).astype(I32)
            return acc + jnp.sum(hit.reshape(TK // 8, 8, TQ), axis=0)
        acc = lax.fori_loop(0, nch, body, jnp.zeros((8, TQ), I32))
        return jnp.sum(acc, axis=0, keepdims=True)

    thr = jnp.where(count_ge(jnp.zeros((1, TQ), I32)) >= n_sel, 0, INT_MIN).astype(I32)

    def bit_body(b, thr):
        cand = thr | jnp.left_shift(jnp.int32(1), 30 - b)
        return jnp.where(count_ge(cand) >= n_sel, cand, thr)

    thr = lax.fori_loop(0, 31, bit_body, thr)
    thr = jnp.maximum(thr, INT_MIN + 1)

    m_ref[...] = jnp.full(m_ref.shape, NEG_BIG, F32)
    l_ref[...] = jnp.zeros(l_ref.shape, F32)
    acc_ref[...] = jnp.zeros(acc_ref.shape, F32)

    def att_chunk(j, carry):
        k0 = pl.multiple_of(j * TK, TK)
        neg = jnp.where(key_ref[pl.ds(k0, TK), :] >= thr, 0.0, NEG_BIG)
        kind = jnp.minimum(i - j, 2)
        for h in range(ATT_HEADS):
            g = h // ATT_GROUP
            kc = kn_ref[pl.ds(k0, TK), g * HEAD_DIM:(g + 1) * HEAD_DIM]
            qh = qn_ref[:, h * HEAD_DIM:(h + 1) * HEAD_DIM]
            lg = _dot_nt(kc, qh) + bias_ref[h, kind] + neg
            s_ref[h] = lg
            m_old = m_ref[h:h + 1, :]
            m_new = jnp.maximum(m_old, jnp.max(lg, axis=0, keepdims=True))
            a_ref[h:h + 1, :] = jnp.exp2(m_old - m_new)
            m_ref[h:h + 1, :] = m_new
        for h in range(ATT_HEADS):
            p = jnp.exp2(s_ref[h] - m_ref[h:h + 1, :])
            l_ref[h:h + 1, :] = (a_ref[h:h + 1, :] * l_ref[h:h + 1, :]
                                 + jnp.sum(p, axis=0, keepdims=True))
            p_ref[h] = p.astype(BF16)
        for h in range(ATT_HEADS):
            g = h // ATT_GROUP
            vc = vt_ref[0, j, g * HEAD_DIM:(g + 1) * HEAD_DIM, :]
            acc_ref[h] = a_ref[h:h + 1, :] * acc_ref[h] + _dot(vc, p_ref[h])
        return carry

    lax.fori_loop(0, nch, att_chunk, 0)
    for h in range(ATT_HEADS):
        o = acc_ref[h] * (1.0 / l_ref[h:h + 1, :])
        o_ref[:, h * HEAD_DIM:(h + 1) * HEAD_DIM] = o.T.astype(o_ref.dtype)


def dsa_attention(qi, wt, qn, ki, kn, vt, bias, batch, seq):
    nq = seq // TQ
    n_sel = min(TOPK_MAX, seq // 4)
    qrow = lambda b, i: (b * nq + i, 0)
    brow = lambda b, i: (b, 0)
    return pl.pallas_call(
        functools.partial(_dsa_kernel, n_sel=n_sel),
        grid=(batch, nq),
        in_specs=[pl.BlockSpec((TQ, IDX_HEADS * IDX_HEAD_DIM), qrow),
                  pl.BlockSpec((1, IDX_HEADS, TQ), lambda b, i: (b, 0, i)),
                  pl.BlockSpec((TQ, ATT_W), qrow),
                  pl.BlockSpec((seq, 2 * LANES), brow),
                  pl.BlockSpec((seq, KV_W), brow),
                  pl.BlockSpec((1, seq // TK, KV_W, TK), lambda b, i: (b, 0, 0, 0)),
                  pl.BlockSpec((ATT_HEADS, 3, TK, TQ), lambda b, i: (0, 0, 0, 0))],
        out_specs=pl.BlockSpec((TQ, ATT_W), qrow),
        out_shape=jax.ShapeDtypeStruct((batch * seq, ATT_W), BF16),
        scratch_shapes=[pltpu.VMEM((seq, TQ), I32),
                        pltpu.VMEM((ATT_HEADS, TQ), F32),
                        pltpu.VMEM((ATT_HEADS, TQ), F32),
                        pltpu.VMEM((ATT_HEADS, TQ), F32),
                        pltpu.VMEM((ATT_HEADS, HEAD_DIM, TQ), F32),
                        pltpu.VMEM((ATT_HEADS, TK, TQ), F32),
                        pltpu.VMEM((ATT_HEADS, TK, TQ), BF16)],
        compiler_params=_cparams(("parallel", "arbitrary"), 48),
        name="dsa_attention",
    )(qi, wt, qn, ki, kn, vt, bias)


def _hgrn_constants():
    t = np.arange(HC)
    rows = []
    masks = [np.eye(HC, dtype=np.float32)]
    for m in HGRN_LEVELS:
        upper = (t // m) % 2 == 1
        start = (t // m) * m
        end = start + m - 1
        u = t[None, :]
        q_side = upper[:, None] & (u >= start[:, None]) & (u <= t[:, None])
        k_side = (~upper)[:, None] & (u > t[:, None]) & (u <= end[:, None])
        rows.append((q_side | k_side).astype(np.float32))
        same = (t[:, None] // (2 * m)) == (t[None, :] // (2 * m))
        masks.append((upper[:, None] & (~upper)[None, :] & same).astype(np.float32))
    u = t[None, :]
    rows.append((u <= t[:, None]).astype(np.float32))
    rows.append((u > t[:, None]).astype(np.float32))
    mat = np.concatenate(rows, axis=0)
    return np.concatenate([mat, mat], axis=1), np.stack(masks)


def _hgrn_kernel(hq_ref, hf_ref, hi_ref, hg_ref, lb_ref, gn_ref, mat_ref, mask_ref, o_ref,
                 st_ref, arg_ref, kk_ref, lf_ref):
    c = pl.program_id(1)

    @pl.when(c == 0)
    def _():
        st_ref[...] = jnp.zeros_like(st_ref)

    lb = lb_ref[...]
    f = hf_ref[...]
    e = jnp.exp(-jnp.abs(f))
    r = 1.0 / (1.0 + e)
    log_sig = jnp.minimum(f, 0.0) - jnp.log(1.0 + e)
    la = jnp.log(lb)
    lc = jnp.log(1.0 - lb) + log_sig
    logf = (jnp.maximum(la, lc) + jnp.log(1.0 + jnp.exp(-jnp.abs(la - lc)))) * LOG2E
    kk_ref[...] = (1.0 - lb) * jnp.where(f >= 0, e * r, r)
    hi = logf.astype(BF16)
    lf_ref[0:HC, :] = hi
    lf_ref[HC:2 * HC, :] = (logf - hi.astype(F32)).astype(BF16)
    arg_ref[...] = _dot(mat_ref[...], lf_ref[...])

    scale = HGRN_DIM ** -0.5
    for h in range(HGRN_HEADS):
        sl = slice(h * HGRN_DIM, (h + 1) * HGRN_DIM)
        hq = hq_ref[:, sl]
        q = hq * _sigmoid(hq) * scale
        k = kk_ref[:, sl]
        v = hi_ref[:, sl]
        a = _dot_nt(q.astype(BF16), k.astype(BF16)) * mask_ref[0]
        for lv in range(len(HGRN_LEVELS)):
            ex = jnp.exp2(arg_ref[lv * HC:(lv + 1) * HC, sl])
            a = a + _dot_nt((q * ex).astype(BF16), (k * ex).astype(BF16)) * mask_ref[lv + 1]
        nl = len(HGRN_LEVELS)
        eq = jnp.exp2(arg_ref[nl * HC:(nl + 1) * HC, sl])
        ek = jnp.exp2(arg_ref[(nl + 1) * HC:(nl + 2) * HC, sl])
        st = st_ref[h]
        vb = v.astype(BF16)
        o = _dot(a.astype(BF16), vb) + _dot_nt((q * eq).astype(BF16), st.astype(BF16))
        st_ref[h] = eq[HC - 1:HC, :] * st + _dot(v.T.astype(BF16), (k * ek).astype(BF16))
        on = o * lax.rsqrt(jnp.mean(o * o, axis=-1, keepdims=True) + EPS) * gn_ref[...]
        hg = hg_ref[:, sl]
        o_ref[:, sl] = (on * hg * _sigmoid(hg)).astype(o_ref.dtype)


def hgrn2(proj_main, lb, g_norm, batch, seq):
    nc = seq // HC
    mat, masks = _hgrn_constants()
    col = lambda cb: (lambda b, c: (b * nc + c, cb))
    return pl.pallas_call(
        _hgrn_kernel,
        grid=(batch, nc),
        in_specs=[pl.BlockSpec((HC, HGRN_W), col(COL_HQ)),
                  pl.BlockSpec((HC, HGRN_W), col(COL_HF)),
                  pl.BlockSpec((HC, HGRN_W), col(COL_HI)),
                  pl.BlockSpec((HC, HGRN_W), col(COL_HG)),
                  pl.BlockSpec((1, HGRN_W), lambda b, c: (0, 0)),
                  pl.BlockSpec((1, HGRN_DIM), lambda b, c: (0, 0)),
                  pl.BlockSpec((N_ARG_GROUPS * HC, 2 * HC), lambda b, c: (0, 0)),
                  pl.BlockSpec((len(HGRN_LEVELS) + 1, HC, HC), lambda b, c: (0, 0, 0))],
        out_specs=pl.BlockSpec((HC, HGRN_W), lambda b, c: (b * nc + c, 0)),
        out_shape=jax.ShapeDtypeStruct((batch * seq, HGRN_W), BF16),
        scratch_shapes=[pltpu.VMEM((HGRN_HEADS, HGRN_DIM, HGRN_DIM), F32),
                        pltpu.VMEM((N_ARG_GROUPS * HC, HGRN_W), F32),
                        pltpu.VMEM((HC, HGRN_W), F32),
                        pltpu.VMEM((2 * HC, HGRN_W), BF16)],
        compiler_params=_cparams(("parallel", "arbitrary"), 40),
        name="hgrn2",
    )(proj_main, proj_main, proj_main, proj_main, lb.reshape(1, -1), g_norm.reshape(1, -1),
      jnp.asarray(mat, BF16), jnp.asarray(masks, F32))


def _merge_kernel(att_ref, rec_ref, ga_ref, gh_ref, wa_ref, wh_ref, o_ref):
    a = _dot(att_ref[...], wa_ref[...])
    r = _dot(rec_ref[...], wh_ref[...])
    o_ref[...] = (_sigmoid(ga_ref[...]) * a + _sigmoid(gh_ref[...]) * r).astype(o_ref.dtype)


def gated_merge(att, rec, proj_main, w_up_att, w_up_hgrn, tm=512, tn=1024):
    m = att.shape[0]
    n = w_up_att.shape[1]
    return pl.pallas_call(
        _merge_kernel,
        grid=(m // tm, n // tn),
        in_specs=[pl.BlockSpec((tm, ATT_W), lambda i, j: (i, 0)),
                  pl.BlockSpec((tm, HGRN_W), lambda i, j: (i, 0)),
                  pl.BlockSpec((tm, tn), lambda i, j: (i, COL_GA + j)),
                  pl.BlockSpec((tm, tn), lambda i, j: (i, COL_GH + j)),
                  pl.BlockSpec((ATT_W, tn), lambda i, j: (0, j)),
                  pl.BlockSpec((HGRN_W, tn), lambda i, j: (0, j))],
        out_specs=pl.BlockSpec((tm, tn), lambda i, j: (i, j)),
        out_shape=jax.ShapeDtypeStruct((m, n), BF16),
        compiler_params=_cparams(("parallel", "parallel"), 48),
        name="gated_merge",
    )(att, rec, proj_main, proj_main, w_up_att, w_up_hgrn)


def _ffn_kernel(hn_ref, h_ref, w1_ref, w3_ref, w2_ref, o_ref):
    @pl.when(pl.program_id(1) == 0)
    def _():
        o_ref[...] = h_ref[...]

    x = hn_ref[...]
    a = _dot(x, w1_ref[...])
    b = _dot(x, w3_ref[...])
    act = (a * _sigmoid(a) * b).astype(BF16)
    o_ref[...] += _dot(act, w2_ref[...])


def ffn_dense(hn, h, w1, w3, w2, tm=512, tf=512):
    m, d = hn.shape
    f = w1.shape[1]
    return pl.pallas_call(
        _ffn_kernel,
        grid=(m // tm, f // tf),
        in_specs=[pl.BlockSpec((tm, d), lambda i, j: (i, 0)),
                  pl.BlockSpec((tm, d), lambda i, j: (i, 0)),
                  pl.BlockSpec((d, tf), lambda i, j: (0, j)),
                  pl.BlockSpec((d, tf), lambda i, j: (0, j)),
                  pl.BlockSpec((tf, d), lambda i, j: (j, 0))],
        out_specs=pl.BlockSpec((tm, d), lambda i, j: (i, 0)),
        out_shape=jax.ShapeDtypeStruct((m, d), F32),
        compiler_params=_cparams(("parallel", "arbitrary"), 48),
        name="ffn_dense",
    )(hn, h, w1, w3, w2)


def _row_copy(src_ref, src_row, dst_ref, dst_row, sem):
    return pltpu.make_async_copy(src_ref.at[pl.ds(src_row, 1)], dst_ref.at[pl.ds(dst_row, 1)], sem)


def _moe_scatter_kernel(dest_ref, hn_ref, xs_init_ref, xs_ref, sem, *, tm, m):
    del xs_init_ref
    base = pl.program_id(0) * tm

    def issue(r, carry):
        for k in range(2):
            _row_copy(hn_ref, r, xs_ref, dest_ref[k * m + base + r], sem).start()
        return carry

    lax.fori_loop(0, tm, issue, 0, unroll=8)
    for k in range(2):
        pltpu.make_async_copy(hn_ref, xs_ref.at[pl.ds(0, tm)], sem).wait()


def moe_scatter(hn_packed, dest, n_rows, tm=512):
    m, c = hn_packed.shape
    xs_init = jnp.zeros((n_rows, c), I32)
    return pl.pallas_call(
        functools.partial(_moe_scatter_kernel, tm=tm, m=m),
        grid_spec=pltpu.PrefetchScalarGridSpec(
            num_scalar_prefetch=1,
            grid=(m // tm,),
            in_specs=[pl.BlockSpec((tm, c), lambda i, dest: (i, 0)),
                      pl.BlockSpec(memory_space=pl.ANY)],
            out_specs=pl.BlockSpec(memory_space=pl.ANY),
            scratch_shapes=[pltpu.SemaphoreType.DMA(())]),
        out_shape=jax.ShapeDtypeStruct((n_rows, c), I32),
        input_output_aliases={2: 0},
        compiler_params=_cparams(("arbitrary",), 32),
        name="moe_scatter",
    )(dest, hn_packed, xs_init)


def _moe_ffn_kernel(te_ref, tv_ref, xs_ref, w1_ref, w3_ref, w2_ref, o_ref, xb_ref):
    del te_ref
    i = pl.program_id(0)
    half = xs_ref.shape[1]

    @pl.when(pl.program_id(1) == 0)
    def _():
        o_ref[...] = jnp.zeros_like(o_ref)
        lo, hi = _unpack_bf16_pairs(xs_ref[...])
        xb_ref[:, 0:half] = lo
        xb_ref[:, half:2 * half] = hi

    @pl.when(tv_ref[i] == 1)
    def _():
        x = xb_ref[...]
        a = _dot(x, w1_ref[0])
        b = _dot(x, w3_ref[0])
        act = (a * _sigmoid(a) * b).astype(BF16)
        o_ref[...] += _dot(act, w2_ref[0])


def moe_ffn(xs, tile_expert, tile_valid, w1, w3, w2, tm=512, tf=1024):
    p, c = xs.shape
    ne, d, f = w1.shape
    nf = f // tf
    col = lambda i, j, te, tv: jnp.where(tv[i] == 1, j, nf - 1)
    return pl.pallas_call(
        _moe_ffn_kernel,
        grid_spec=pltpu.PrefetchScalarGridSpec(
            num_scalar_prefetch=2,
            grid=(p // tm, nf),
            in_specs=[pl.BlockSpec((tm, c), lambda i, j, te, tv: (i, 0)),
                      pl.BlockSpec((1, d, tf), lambda i, j, te, tv: (te[i], 0, col(i, j, te, tv))),
                      pl.BlockSpec((1, d, tf), lambda i, j, te, tv: (te[i], 0, col(i, j, te, tv))),
                      pl.BlockSpec((1, tf, d), lambda i, j, te, tv: (te[i], col(i, j, te, tv), 0))],
            out_specs=pl.BlockSpec((tm, d), lambda i, j, te, tv: (i, 0)),
            scratch_shapes=[pltpu.VMEM((tm, d), BF16)]),
        out_shape=jax.ShapeDtypeStruct((p, d), F32),
        compiler_params=_cparams(("arbitrary", "arbitrary"), 56),
        name="moe_ffn",
    )(tile_expert, tile_valid, xs, w1, w3, w2)


def _moe_combine_kernel(dest_ref, h_ref, prob_ref, ys_ref, o_ref, ybuf_ref, sem, *, tm, m):
    base = pl.program_id(0) * tm

    def issue(r, carry):
        for k in range(2):
            _row_copy(ys_ref, dest_ref[k * m + base + r], ybuf_ref.at[k], r, sem).start()
        return carry

    lax.fori_loop(0, tm, issue, 0, unroll=8)
    for k in range(2):
        pltpu.make_async_copy(ys_ref.at[pl.ds(0, tm)], ybuf_ref.at[k], sem).wait()
    prob = prob_ref[...]
    o_ref[...] = h_ref[...] + prob[:, 0:1] * ybuf_ref[0] + prob[:, 1:2] * ybuf_ref[1]


def moe_combine(h, prob, ys, dest, tm=512):
    m, d = h.shape
    return pl.pallas_call(
        functools.partial(_moe_combine_kernel, tm=tm, m=m),
        grid_spec=pltpu.PrefetchScalarGridSpec(
            num_scalar_prefetch=1,
            grid=(m // tm,),
            in_specs=[pl.BlockSpec((tm, d), lambda i, dest: (i, 0)),
                      pl.BlockSpec((tm, LANES), lambda i, dest: (i, 0)),
                      pl.BlockSpec(memory_space=pl.ANY)],
            out_specs=pl.BlockSpec((tm, d), lambda i, dest: (i, 0)),
            scratch_shapes=[pltpu.VMEM((2, tm, d), F32), pltpu.SemaphoreType.DMA(())]),
        out_shape=jax.ShapeDtypeStruct((m, d), F32),
        compiler_params=_cparams(("arbitrary",), 40),
        name="moe_combine",
    )(dest, h, prob, ys)


def ffn_moe(h, norm_g, w_router, w1, w3, w2, tm=512):
    m, d = h.shape
    ne = w1.shape[0]
    hn_packed, meta, prob, cnt = rmsnorm_router(h, norm_g, w_router)
    counts = cnt[0, :ne].astype(I32)
    tiles_per = (counts + tm - 1) // tm
    tile_end = jnp.cumsum(tiles_per)
    offset = (tile_end - tiles_per) * tm
    dest = jnp.concatenate([offset[meta[:, 0]] + meta[:, 2], offset[meta[:, 1]] + meta[:, 3]])
    n_tiles = (2 * m) // tm + ne
    tile_ids = jnp.arange(n_tiles, dtype=I32)
    tile_valid = (tile_ids < tile_end[-1]).astype(I32)
    tile_expert = jnp.searchsorted(tile_end, jnp.minimum(tile_ids, tile_end[-1] - 1), side="right").astype(I32)
    xs = moe_scatter(hn_packed, dest, n_tiles * tm)
    ys = moe_ffn(xs, tile_expert, tile_valid, w1, w3, w2, tm=tm)
    return moe_combine(h, prob, ys, dest)


def _split_w_in(w_in):
    sizes = [ATT_W, KV_W, KV_W, IDX_Q_RANK, IDX_HEAD_DIM, IDX_HEADS,
             HGRN_W, HGRN_W, HGRN_W, HGRN_W, w_in.shape[0], w_in.shape[0]]
    offs = np.cumsum([0] + sizes)
    part = lambda n: w_in[:, offs[n]:offs[n + 1]]
    main = jnp.concatenate([part(n) for n in (0, 1, 2, 3, 6, 7, 8, 9, 10, 11)], axis=1)
    pad = jnp.zeros((w_in.shape[0], LANES - IDX_HEAD_DIM - IDX_HEADS), w_in.dtype)
    idx = jnp.concatenate([part(4), part(5), pad], axis=1)
    return main.astype(BF16), idx.astype(BF16)


def _mixer(x2, batch, seq, norm_g, w_in, q_norm_g, k_norm_g, idx_q_norm_g, w_idx_q, ln_g, ln_b,
           lb, hgrn_norm_g, w_up_att, w_up_hgrn, w_o, bias):
    m = batch * seq
    xn = rmsnorm_rows(x2, norm_g)
    w_main, w_idx = _split_w_in(w_in)
    proj_main = matmul(xn, w_main, F32)
    proj_idx = matmul(xn, w_idx, F32)
    qn, kn, v, qi, ki, w = attn_prep(proj_main, proj_idx, q_norm_g, k_norm_g, idx_q_norm_g,
                                     w_idx_q.astype(BF16), ln_g, ln_b)
    wt = w[:, IDX_HEAD_DIM:IDX_HEAD_DIM + IDX_HEADS].reshape(batch, seq, IDX_HEADS).transpose(0, 2, 1)
    vt = v.reshape(batch, seq // TK, TK, KV_W).transpose(0, 1, 3, 2)
    att = dsa_attention(qi, wt, qn, ki, kn, vt, bias, batch, seq)
    rec = hgrn2(proj_main, lb, hgrn_norm_g, batch, seq)
    merged = gated_merge(att, rec, proj_main, w_up_att.astype(BF16), w_up_hgrn.astype(BF16))
    return matmul_residual(merged, w_o.astype(BF16), x2)


def kernel(x, rel_bias, norm_mix_g, norm_ffn_g, w_in, q_norm_g, k_norm_g, idx_q_norm_g, w_idx_q,
           idx_k_ln_g, idx_k_ln_b, hgrn_lb_logits, hgrn_out_norm_g, w_up_att, w_up_hgrn, w_o,
           w1_dense, w3_dense, w2_dense, w_router, w1_moe, w3_moe, w2_moe):
    batch, seq, d = x.shape
    depth = w_in.shape[0]
    lb_all = jnp.cumsum(jax.nn.softmax(hgrn_lb_logits.astype(F32), axis=0), axis=0)
    lb_all = lb_all - lb_all[0:1]
    bias = bias_tables(rel_bias)
    x2 = x.reshape(batch * seq, d)
    for l in range(depth):
        h = _mixer(x2, batch, seq, norm_mix_g[l], w_in[l], q_norm_g[l], k_norm_g[l],
                   idx_q_norm_g[l], w_idx_q[l], idx_k_ln_g[l], idx_k_ln_b[l], lb_all[l],
                   hgrn_out_norm_g[l], w_up_att[l], w_up_hgrn[l], w_o[l], bias)
        j = l // 2
        if l % 2 == 0:
            hn = rmsnorm_rows(h, norm_ffn_g[l])
            x2 = ffn_dense(hn, h, w1_dense[j].astype(BF16), w3_dense[j].astype(BF16),
                           w2_dense[j].astype(BF16))
        else:
            x2 = ffn_moe(h, norm_ffn_g[l], w_router[j], w1_moe[j].astype(BF16),
                         w3_moe[j].astype(BF16), w2_moe[j].astype(BF16))
    return x2.reshape(batch, seq, d)
```

```python
import functools
import math

import numpy as np
import jax
import jax.numpy as jnp
from jax import lax
from jax.experimental import pallas as pl
from jax.experimental.pallas import tpu as pltpu

F32 = jnp.float32
BF16 = jnp.bfloat16
I32 = jnp.int32

EPS = 1e-6
ATT_HEADS = 8
ATT_KV_HEADS = 2
ATT_GROUP = ATT_HEADS // ATT_KV_HEADS
HEAD_DIM = 128
ATT_W = ATT_HEADS * HEAD_DIM
KV_W = ATT_KV_HEADS * HEAD_DIM
IDX_HEADS = 16
IDX_HEAD_DIM = 64
IDX_Q_RANK = 512
TOPK_MAX = 256
HGRN_HEADS = 8
HGRN_DIM = 128
HGRN_W = HGRN_HEADS * HGRN_DIM
REL_BUCKETS = 32
REL_MAX_DIST = 128
N_EXPERTS = 8
LANES = 128
INT_MIN = -(2 ** 31)
NEG_BIG = -1e30
LOG2E = math.log2(math.e)

TQ = 256
TK = 256
HC = 128
HGRN_LEVELS = (1, 2, 4, 8, 16, 32, 64)
N_ARG_GROUPS = len(HGRN_LEVELS) + 2

COL_Q, COL_KVC, COL_HQ, COL_HF, COL_HI, COL_HG, COL_GA, COL_GH = 0, 1, 2, 3, 4, 5, 6, 8
MAIN_W = 10 * 1024


def _cparams(sem, vmem_mb):
    return pltpu.CompilerParams(dimension_semantics=sem, vmem_limit_bytes=vmem_mb << 20)


def _dot(a, b):
    return jnp.dot(a, b, preferred_element_type=F32)


def _dot_nt(a, b):
    return lax.dot_general(a, b, (((1,), (1,)), ((), ())), preferred_element_type=F32)


def _sigmoid(x):
    return 1.0 / (1.0 + jnp.exp(-x))


def _rmsnorm_kernel(x_ref, g_ref, o_ref):
    x = x_ref[...]
    ms = jnp.mean(x * x, axis=-1, keepdims=True)
    o_ref[...] = (x * lax.rsqrt(ms + EPS) * g_ref[...]).astype(o_ref.dtype)


def rmsnorm_rows(x, g, tm=512):
    m, d = x.shape
    return pl.pallas_call(
        _rmsnorm_kernel,
        grid=(m // tm,),
        in_specs=[pl.BlockSpec((tm, d), lambda i: (i, 0)),
                  pl.BlockSpec((1, d), lambda i: (0, 0))],
        out_specs=pl.BlockSpec((tm, d), lambda i: (i, 0)),
        out_shape=jax.ShapeDtypeStruct((m, d), BF16),
        compiler_params=_cparams(("parallel",), 32),
        name="rmsnorm",
    )(x, g.reshape(1, d))


def _pack_bf16_pairs(x):
    c = x.shape[1] // 2
    lo = lax.bitcast_convert_type(x[:, :c].astype(BF16).astype(F32), I32)
    hi = lax.bitcast_convert_type(x[:, c:].astype(BF16).astype(F32), I32)
    return (hi & jnp.int32(-65536)) | lax.shift_right_logical(lo, 16)


def _unpack_bf16_pairs(p):
    lo = lax.bitcast_convert_type(lax.shift_left(p, 16), F32).astype(BF16)
    hi = lax.bitcast_convert_type(p & jnp.int32(-65536), F32).astype(BF16)
    return lo, hi


def _rmsnorm_router_kernel(x_ref, g_ref, wr_ref, tri_ref, o_ref, meta_ref, prob_ref, cnt_ref, carry_ref):
    @pl.when(pl.program_id(0) == 0)
    def _():
        carry_ref[...] = jnp.zeros_like(carry_ref)

    x = x_ref[...]
    ms = jnp.mean(x * x, axis=-1, keepdims=True)
    hn = x * lax.rsqrt(ms + EPS) * g_ref[...]
    o_ref[...] = _pack_bf16_pairs(hn)
    logits = jnp.dot(hn, wr_ref[...], precision=lax.Precision.HIGHEST,
                     preferred_element_type=F32)
    lane = lax.broadcasted_iota(I32, logits.shape, 1)
    logits = jnp.where(lane < N_EXPERTS, logits, -jnp.inf)
    v1 = jnp.max(logits, axis=-1, keepdims=True)
    i1 = jnp.min(jnp.where(logits == v1, lane, LANES), axis=-1, keepdims=True)
    rest = jnp.where(lane == i1, -jnp.inf, logits)
    v2 = jnp.max(rest, axis=-1, keepdims=True)
    i2 = jnp.min(jnp.where(rest == v2, lane, LANES), axis=-1, keepdims=True)
    e = jnp.exp(v2 - v1)
    p1 = 1.0 / (1.0 + e)
    prob_ref[...] = jnp.where(lane == 0, p1, 0.0) + jnp.where(lane == 1, e * p1, 0.0)
    hot = jnp.where((lane == i1) | (lane == i2), 1.0, 0.0)
    rank = _dot(tri_ref[...], hot.astype(BF16)) + carry_ref[0:1, :]
    carry_ref[...] = carry_ref[...] + jnp.sum(hot, axis=0, keepdims=True)
    cnt_ref[...] = carry_ref[...]
    r1 = jnp.sum(jnp.where(lane == i1, rank, 0.0), axis=-1, keepdims=True)
    r2 = jnp.sum(jnp.where(lane == i2, rank, 0.0), axis=-1, keepdims=True)
    meta = (jnp.where(lane == 0, i1, 0) + jnp.where(lane == 1, i2, 0)
            + jnp.where(lane == 2, r1.astype(I32), 0) + jnp.where(lane == 3, r2.astype(I32), 0))
    meta_ref[...] = meta


def rmsnorm_router(x, g, w_router, tm=512):
    m, d = x.shape
    wr = jnp.zeros((d, LANES), F32).at[:, :N_EXPERTS].set(w_router)
    tri = jnp.asarray(np.tril(np.ones((tm, tm), np.float32), -1), BF16)
    return pl.pallas_call(
        _rmsnorm_router_kernel,
        grid=(m // tm,),
        in_specs=[pl.BlockSpec((tm, d), lambda i: (i, 0)),
                  pl.BlockSpec((1, d), lambda i: (0, 0)),
                  pl.BlockSpec((d, LANES), lambda i: (0, 0)),
                  pl.BlockSpec((tm, tm), lambda i: (0, 0))],
        out_specs=[pl.BlockSpec((tm, d // 2), lambda i: (i, 0)),
                   pl.BlockSpec((tm, LANES), lambda i: (i, 0)),
                   pl.BlockSpec((tm, LANES), lambda i: (i, 0)),
                   pl.BlockSpec((8, LANES), lambda i: (0, 0))],
        out_shape=[jax.ShapeDtypeStruct((m, d // 2), I32),
                   jax.ShapeDtypeStruct((m, LANES), I32),
                   jax.ShapeDtypeStruct((m, LANES), F32),
                   jax.ShapeDtypeStruct((8, LANES), F32)],
        scratch_shapes=[pltpu.VMEM((8, LANES), F32)],
        compiler_params=_cparams(("arbitrary",), 40),
        name="rmsnorm_router",
    )(x, g.reshape(1, d), wr, tri)


def _matmul_kernel(x_ref, w_ref, o_ref):
    o_ref[...] = _dot(x_ref[...], w_ref[...]).astype(o_ref.dtype)


def matmul(x, w, out_dtype, tm=1024, tn=1024):
    m, k = x.shape
    n = w.shape[1]
    tm, tn = min(tm, m), min(tn, n)
    return pl.pallas_call(
        _matmul_kernel,
        grid=(m // tm, n // tn),
        in_specs=[pl.BlockSpec((tm, k), lambda i, j: (i, 0)),
                  pl.BlockSpec((k, tn), lambda i, j: (0, j))],
        out_specs=pl.BlockSpec((tm, tn), lambda i, j: (i, j)),
        out_shape=jax.ShapeDtypeStruct((m, n), out_dtype),
        compiler_params=_cparams(("parallel", "parallel"), 48),
        name="matmul",
    )(x, w)


def _matmul_residual_kernel(x_ref, w_ref, r_ref, o_ref):
    o_ref[...] = r_ref[...] + _dot(x_ref[...], w_ref[...])


def matmul_residual(x, w, res, tm=1024, tn=1024):
    m, k = x.shape
    n = w.shape[1]
    tm = min(tm, m)
    return pl.pallas_call(
        _matmul_residual_kernel,
        grid=(m // tm, n // tn),
        in_specs=[pl.BlockSpec((tm, k), lambda i, j: (i, 0)),
                  pl.BlockSpec((k, tn), lambda i, j: (0, j)),
                  pl.BlockSpec((tm, tn), lambda i, j: (i, j))],
        out_specs=pl.BlockSpec((tm, tn), lambda i, j: (i, j)),
        out_shape=jax.ShapeDtypeStruct((m, n), F32),
        compiler_params=_cparams(("parallel", "parallel"), 48),
        name="matmul_residual",
    )(x, w, res)


def _attn_prep_kernel(q_ref, kvc_ref, pi_ref, qg_ref, kg_ref, cg_ref, wiq_ref, lng_ref, lnb_ref,
                      qn_ref, kn_ref, v_ref, qi_ref, ki_ref, w_ref):
    def head_rms(x, g):
        return x * lax.rsqrt(jnp.mean(x * x, axis=-1, keepdims=True) + EPS) * g

    att_scale = HEAD_DIM ** -0.5 * LOG2E
    for h in range(ATT_HEADS):
        sl = slice(h * HEAD_DIM, (h + 1) * HEAD_DIM)
        qn_ref[:, sl] = (head_rms(q_ref[:, sl], qg_ref[...]) * att_scale).astype(BF16)
    for g in range(ATT_KV_HEADS):
        sl = slice(g * HEAD_DIM, (g + 1) * HEAD_DIM)
        kn_ref[:, sl] = head_rms(kvc_ref[:, sl], kg_ref[...]).astype(BF16)
    v_ref[...] = kvc_ref[:, KV_W:2 * KV_W].astype(BF16)
    cq = kvc_ref[:, 2 * KV_W:2 * KV_W + IDX_Q_RANK]
    cqn = head_rms(cq, cg_ref[...]).astype(BF16)
    qi_ref[...] = _dot(cqn, wiq_ref[...]).astype(BF16)
    pi = pi_ref[...]
    is_key = lax.broadcasted_iota(I32, pi.shape, 1) < IDX_HEAD_DIM
    mu = jnp.sum(jnp.where(is_key, pi, 0.0), axis=-1, keepdims=True) * (1.0 / IDX_HEAD_DIM)
    cen = jnp.where(is_key, pi - mu, 0.0)
    var = jnp.sum(cen * cen, axis=-1, keepdims=True) * (1.0 / IDX_HEAD_DIM)
    kidn = cen * lax.rsqrt(var + EPS) * lng_ref[...] + lnb_ref[...]
    ki_ref[:, 0:LANES] = kidn.astype(BF16)
    ki_ref[:, LANES:2 * LANES] = pltpu.roll(kidn, IDX_HEAD_DIM, 1).astype(BF16)
    w_ref[...] = pi * (IDX_HEAD_DIM ** -0.5 * IDX_HEADS ** -0.5)


def attn_prep(proj_main, proj_idx, q_norm_g, k_norm_g, idx_q_norm_g, w_idx_q, ln_g, ln_b, tm=512):
    m = proj_main.shape[0]
    row = lambda i: (i, 0)
    const = lambda i: (0, 0)
    lane_pad = lambda a: jnp.zeros((1, LANES), F32).at[0, :a.shape[0]].set(a)
    return pl.pallas_call(
        _attn_prep_kernel,
        grid=(m // tm,),
        in_specs=[pl.BlockSpec((tm, 1024), lambda i: (i, COL_Q)),
                  pl.BlockSpec((tm, 1024), lambda i: (i, COL_KVC)),
                  pl.BlockSpec((tm, LANES), row),
                  pl.BlockSpec((1, HEAD_DIM), const),
                  pl.BlockSpec((1, HEAD_DIM), const),
                  pl.BlockSpec((1, IDX_Q_RANK), const),
                  pl.BlockSpec((IDX_Q_RANK, IDX_HEADS * IDX_HEAD_DIM), const),
                  pl.BlockSpec((1, LANES), const),
                  pl.BlockSpec((1, LANES), const)],
        out_specs=[pl.BlockSpec((tm, ATT_W), row),
                   pl.BlockSpec((tm, KV_W), row),
                   pl.BlockSpec((tm, KV_W), row),
                   pl.BlockSpec((tm, IDX_HEADS * IDX_HEAD_DIM), row),
                   pl.BlockSpec((tm, 2 * LANES), row),
                   pl.BlockSpec((tm, LANES), row)],
        out_shape=[jax.ShapeDtypeStruct((m, ATT_W), BF16),
                   jax.ShapeDtypeStruct((m, KV_W), BF16),
                   jax.ShapeDtypeStruct((m, KV_W), BF16),
                   jax.ShapeDtypeStruct((m, IDX_HEADS * IDX_HEAD_DIM), BF16),
                   jax.ShapeDtypeStruct((m, 2 * LANES), BF16),
                   jax.ShapeDtypeStruct((m, LANES), F32)],
        compiler_params=_cparams(("parallel",), 40),
        name="attn_prep",
    )(proj_main, proj_main, proj_idx, q_norm_g.reshape(1, -1), k_norm_g.reshape(1, -1),
      idx_q_norm_g.reshape(1, -1), w_idx_q, lane_pad(ln_g), lane_pad(ln_b))


def _rel_bucket(dist):
    max_exact = REL_BUCKETS // 2
    n = jnp.maximum(dist, 0)
    nf = jnp.maximum(n, 1).astype(F32)
    large = max_exact + (jnp.log(nf / max_exact) / math.log(REL_MAX_DIST / max_exact)
                         * (REL_BUCKETS - max_exact)).astype(I32)
    large = jnp.minimum(large, REL_BUCKETS - 1)
    return jnp.where(n < max_exact, n, large)


def _bias_kernel(rb_ref, bkt_ref, o_ref):
    h = pl.program_id(0)
    for kind in range(3):
        b = bkt_ref[kind]
        acc = jnp.zeros(b.shape, F32)
        for n in range(REL_BUCKETS):
            acc = jnp.where(b == n, rb_ref[n, h], acc)
        o_ref[0, kind] = acc * LOG2E


def bias_tables(rel_bias):
    assert TK >= REL_MAX_DIST
    kpos = jnp.arange(TK, dtype=I32)[:, None]
    qpos = jnp.arange(TQ, dtype=I32)[None, :]
    dist = jnp.stack([qpos - kpos, qpos - kpos + TK, qpos - kpos + 2 * TK])
    bkt = _rel_bucket(dist)
    return pl.pallas_call(
        _bias_kernel,
        grid=(ATT_HEADS,),
        in_specs=[pl.BlockSpec(memory_space=pltpu.SMEM),
                  pl.BlockSpec((3, TK, TQ), lambda h: (0, 0, 0))],
        out_specs=pl.BlockSpec((1, 3, TK, TQ), lambda h: (h, 0, 0, 0)),
        out_shape=jax.ShapeDtypeStruct((ATT_HEADS, 3, TK, TQ), F32),
        compiler_params=_cparams(("arbitrary",), 32),
        name="bias_tables",
    )(rel_bias, bkt)


def _dsa_kernel(qi_ref, wt_ref, qn_ref, ki_ref, kn_ref, vt_ref, bias_ref, o_ref,
                key_ref, m_ref, l_ref, a_ref, acc_ref, s_ref, p_ref, *, n_sel):
    i = pl.program_id(1)
    nch = i + 1
    qpos = i * TQ + lax.broadcasted_iota(I32, (TK, TQ), 1)

    def score_chunk(j, carry):
        k0 = pl.multiple_of(j * TK, TK)
        acc = jnp.zeros((TK, TQ), F32)
        for h in range(IDX_HEADS):
            par = h % 2
            kc = ki_ref[pl.ds(k0, TK), par * LANES:(par + 1) * LANES]
            qh = qi_ref[:, (h // 2) * LANES:(h // 2 + 1) * LANES]
            s = _dot_nt(kc, qh)
            acc = acc + jnp.maximum(s, 0.0) * wt_ref[0, h:h + 1, :]
        bits = lax.bitcast_convert_type(acc, I32)
        key = jnp.where(bits < 0, bits ^ jnp.int32(0x7FFFFFFF), bits)
        kpos = k0 + lax.broadcasted_iota(I32, (TK, TQ), 0)
        key_ref[pl.ds(k0, TK), :] = jnp.where(kpos <= qpos, key, INT_MIN)
        return carry

    lax.fori_loop(0, nch, score_chunk, 0)

    def count_ge(cand):
        def body(j, acc):
            k0 = pl.multiple_of(j * TK, TK)
            hit = jnp.where(key_ref[pl.ds(k0, TK), :] >= cand, 1, 0).astype(I32)
            return acc + jnp.sum(hit.reshape(TK // 8, 8, TQ), axis=0)
        acc = lax.fori_loop(0, nch, body, jnp.zeros((8, TQ), I32))
        return jnp.sum(acc, axis=0, keepdims=True)

    thr = jnp.where(count_ge(jnp.zeros((1, TQ), I32)) >= n_sel, 0, INT_MIN).astype(I32)

    def bit_body(b, thr):
        cand = thr | jnp.left_shift(jnp.int32(1), 30 - b)
        return jnp.where(count_ge(cand) >= n_sel, cand, thr)

    thr = lax.fori_loop(0, 31, bit_body, thr)
    thr = jnp.maximum(thr, INT_MIN + 1)

    m_ref[...] = jnp.full(m_ref.shape, NEG_BIG, F32)
    l_ref[...] = jnp.zeros(l_ref.shape, F32)
    acc_ref[...] = jnp.zeros(acc_ref.shape, F32)

    def att_chunk(j, carry):
        k0 = pl.multiple_of(j * TK, TK)
        neg = jnp.where(key_ref[pl.ds(k0, TK), :] >= thr, 0.0, NEG_BIG)
        kind = jnp.minimum(i - j, 2)
        for h in range(ATT_HEADS):
            g = h // ATT_GROUP
            kc = kn_ref[pl.ds(k0, TK), g * HEAD_DIM:(g + 1) * HEAD_DIM]
            qh = qn_ref[:, h * HEAD_DIM:(h + 1) * HEAD_DIM]
            lg = _dot_nt(kc, qh) + bias_ref[h, kind] + neg
            s_ref[h] = lg
            m_old = m_ref[h:h + 1, :]
            m_new = jnp.maximum(m_old, jnp.max(lg, axis=0, keepdims=True))
            a_ref[h:h + 1, :] = jnp.exp2(m_old - m_new)
            m_ref[h:h + 1, :] = m_new
        for h in range(ATT_HEADS):
            p = jnp.exp2(s_ref[h] - m_ref[h:h + 1, :])
            l_ref[h:h + 1, :] = (a_ref[h:h + 1, :] * l_ref[h:h + 1, :]
                                 + jnp.sum(p, axis=0, keepdims=True))
            p_ref[h] = p.astype(BF16)
        for h in range(ATT_HEADS):
            g = h // ATT_GROUP
            vc = vt_ref[0, j, g * HEAD_DIM:(g + 1) * HEAD_DIM, :]
            acc_ref[h] = a_ref[h:h + 1, :] * acc_ref[h] + _dot(vc, p_ref[h])
        return carry

    lax.fori_loop(0, nch, att_chunk, 0)
    for h in range(ATT_HEADS):
        o = acc_ref[h] * (1.0 / l_ref[h:h + 1, :])
        o_ref[:, h * HEAD_DIM:(h + 1) * HEAD_DIM] = o.T.astype(o_ref.dtype)


def dsa_attention(qi, wt, qn, ki, kn, vt, bias, batch, seq):
    nq = seq // TQ
    n_sel = min(TOPK_MAX, seq // 4)
    qrow = lambda b, i: (b * nq + i, 0)
    brow = lambda b, i: (b, 0)
    return pl.pallas_call(
        functools.partial(_dsa_kernel, n_sel=n_sel),
        grid=(batch, nq),
        in_specs=[pl.BlockSpec((TQ, IDX_HEADS * IDX_HEAD_DIM), qrow),
                  pl.BlockSpec((1, IDX_HEADS, TQ), lambda b, i: (b, 0, i)),
                  pl.BlockSpec((TQ, ATT_W), qrow),
                  pl.BlockSpec((seq, 2 * LANES), brow),
                  pl.BlockSpec((seq, KV_W), brow),
                  pl.BlockSpec((1, seq // TK, KV_W, TK), lambda b, i: (b, 0, 0, 0)),
                  pl.BlockSpec((ATT_HEADS, 3, TK, TQ), lambda b, i: (0, 0, 0, 0))],
        out_specs=pl.BlockSpec((TQ, ATT_W), qrow),
        out_shape=jax.ShapeDtypeStruct((batch * seq, ATT_W), BF16),
        scratch_shapes=[pltpu.VMEM((seq, TQ), I32),
                        pltpu.VMEM((ATT_HEADS, TQ), F32),
                        pltpu.VMEM((ATT_HEADS, TQ), F32),
                        pltpu.VMEM((ATT_HEADS, TQ), F32),
                        pltpu.VMEM((ATT_HEADS, HEAD_DIM, TQ), F32),
                        pltpu.VMEM((ATT_HEADS, TK, TQ), F32),
                        pltpu.VMEM((ATT_HEADS, TK, TQ), BF16)],
        compiler_params=_cparams(("parallel", "arbitrary"), 48),
        name="dsa_attention",
    )(qi, wt, qn, ki, kn, vt, bias)


def _hgrn_constants():
    t = np.arange(HC)
    rows = []
    masks = [np.eye(HC, dtype=np.float32)]
    for m in HGRN_LEVELS:
        upper = (t // m) % 2 == 1
        start = (t // m) * m
        end = start + m - 1
        u = t[None, :]
        q_side = upper[:, None] & (u >= start[:, None]) & (u <= t[:, None])
        k_side = (~upper)[:, None] & (u > t[:, None]) & (u <= end[:, None])
        rows.append((q_side | k_side).astype(np.float32))
        same = (t[:, None] // (2 * m)) == (t[None, :] // (2 * m))
        masks.append((upper[:, None] & (~upper)[None, :] & same).astype(np.float32))
    u = t[None, :]
    rows.append((u <= t[:, None]).astype(np.float32))
    rows.append((u > t[:, None]).astype(np.float32))
    mat = np.concatenate(rows, axis=0)
    return np.concatenate([mat, mat], axis=1), np.stack(masks)


def _hgrn_kernel(hq_ref, hf_ref, hi_ref, hg_ref, lb_ref, gn_ref, mat_ref, mask_ref, o_ref,
                 st_ref, arg_ref, kk_ref, lf_ref, zq_ref, zk_ref):
    c = pl.program_id(1)

    @pl.when(c == 0)
    def _():
        st_ref[...] = jnp.zeros_like(st_ref)

    lb = lb_ref[...]
    f = hf_ref[...]
    e = jnp.exp(-jnp.abs(f))
    r = 1.0 / (1.0 + e)
    log_sig = jnp.minimum(f, 0.0) - jnp.log(1.0 + e)
    la = jnp.log(lb)
    lc = jnp.log(1.0 - lb) + log_sig
    logf = (jnp.maximum(la, lc) + jnp.log(1.0 + jnp.exp(-jnp.abs(la - lc)))) * LOG2E
    kk_ref[...] = (1.0 - lb) * jnp.where(f >= 0, e * r, r)
    hi = logf.astype(BF16)
    lf_ref[0:HC, :] = hi
    lf_ref[HC:2 * HC, :] = (logf - hi.astype(F32)).astype(BF16)
    arg_ref[...] = _dot(mat_ref[...], lf_ref[...])

    scale = HGRN_DIM ** -0.5
    nl = len(HGRN_LEVELS)
    for h in range(HGRN_HEADS):
        sl = slice(h * HGRN_DIM, (h + 1) * HGRN_DIM)
        hq = hq_ref[:, sl]
        q = hq * _sigmoid(hq) * scale
        k = kk_ref[:, sl]
        zq_ref[h, 0] = q.astype(BF16)
        zk_ref[h, 0] = k.astype(BF16)
        for lv in range(nl):
            ex = jnp.exp2(arg_ref[lv * HC:(lv + 1) * HC, sl])
            zq_ref[h, lv + 1] = (q * ex).astype(BF16)
            zk_ref[h, lv + 1] = (k * ex).astype(BF16)
        zq_ref[h, nl + 1] = (q * jnp.exp2(arg_ref[nl * HC:(nl + 1) * HC, sl])).astype(BF16)
        zk_ref[h, nl + 1] = (k * jnp.exp2(arg_ref[(nl + 1) * HC:(nl + 2) * HC, sl])).astype(BF16)

    for h in range(HGRN_HEADS):
        sl = slice(h * HGRN_DIM, (h + 1) * HGRN_DIM)
        a = _dot_nt(zq_ref[h, 0], zk_ref[h, 0]) * mask_ref[0]
        for lv in range(nl):
            a = a + _dot_nt(zq_ref[h, lv + 1], zk_ref[h, lv + 1]) * mask_ref[lv + 1]
        v = hi_ref[:, sl]
        st = st_ref[h]
        o = _dot(a.astype(BF16), v.astype(BF16)) + _dot_nt(zq_ref[h, nl + 1], st.astype(BF16))
        decay = jnp.exp2(arg_ref[(nl + 1) * HC - 1:(nl + 1) * HC, sl])
        st_ref[h] = decay * st + _dot(v.T.astype(BF16), zk_ref[h, nl + 1])
        on = o * lax.rsqrt(jnp.mean(o * o, axis=-1, keepdims=True) + EPS) * gn_ref[...]
        hg = hg_ref[:, sl]
        o_ref[:, sl] = (on * hg * _sigmoid(hg)).astype(o_ref.dtype)


def hgrn2(proj_main, lb, g_norm, batch, seq):
    nc = seq // HC
    mat, masks = _hgrn_constants()
    col = lambda cb: (lambda b, c: (b * nc + c, cb))
    return pl.pallas_call(
        _hgrn_kernel,
        grid=(batch, nc),
        in_specs=[pl.BlockSpec((HC, HGRN_W), col(COL_HQ)),
                  pl.BlockSpec((HC, HGRN_W), col(COL_HF)),
                  pl.BlockSpec((HC, HGRN_W), col(COL_HI)),
                  pl.BlockSpec((HC, HGRN_W), col(COL_HG)),
                  pl.BlockSpec((1, HGRN_W), lambda b, c: (0, 0)),
                  pl.BlockSpec((1, HGRN_DIM), lambda b, c: (0, 0)),
                  pl.BlockSpec((N_ARG_GROUPS * HC, 2 * HC), lambda b, c: (0, 0)),
                  pl.BlockSpec((len(HGRN_LEVELS) + 1, HC, HC), lambda b, c: (0, 0, 0))],
        out_specs=pl.BlockSpec((HC, HGRN_W), lambda b, c: (b * nc + c, 0)),
        out_shape=jax.ShapeDtypeStruct((batch * seq, HGRN_W), BF16),
        scratch_shapes=[pltpu.VMEM((HGRN_HEADS, HGRN_DIM, HGRN_DIM), F32),
                        pltpu.VMEM((N_ARG_GROUPS * HC, HGRN_W), F32),
                        pltpu.VMEM((HC, HGRN_W), F32),
                        pltpu.VMEM((2 * HC, HGRN_W), BF16),
                        pltpu.VMEM((HGRN_HEADS, N_ARG_GROUPS, HC, HGRN_DIM), BF16),
                        pltpu.VMEM((HGRN_HEADS, N_ARG_GROUPS, HC, HGRN_DIM), BF16)],
        compiler_params=_cparams(("parallel", "arbitrary"), 40),
        name="hgrn2",
    )(proj_main, proj_main, proj_main, proj_main, lb.reshape(1, -1), g_norm.reshape(1, -1),
      jnp.asarray(mat, BF16), jnp.asarray(masks, F32))


def _merge_kernel(att_ref, rec_ref, ga_ref, gh_ref, wa_ref, wh_ref, o_ref):
    a = _dot(att_ref[...], wa_ref[...])
    r = _dot(rec_ref[...], wh_ref[...])
    o_ref[...] = (_sigmoid(ga_ref[...]) * a + _sigmoid(gh_ref[...]) * r).astype(o_ref.dtype)


def gated_merge(att, rec, proj_main, w_up_att, w_up_hgrn, tm=512, tn=1024):
    m = att.shape[0]
    n = w_up_att.shape[1]
    return pl.pallas_call(
        _merge_kernel,
        grid=(m // tm, n // tn),
        in_specs=[pl.BlockSpec((tm, ATT_W), lambda i, j: (i, 0)),
                  pl.BlockSpec((tm, HGRN_W), lambda i, j: (i, 0)),
                  pl.BlockSpec((tm, tn), lambda i, j: (i, COL_GA + j)),
                  pl.BlockSpec((tm, tn), lambda i, j: (i, COL_GH + j)),
                  pl.BlockSpec((ATT_W, tn), lambda i, j: (0, j)),
                  pl.BlockSpec((HGRN_W, tn), lambda i, j: (0, j))],
        out_specs=pl.BlockSpec((tm, tn), lambda i, j: (i, j)),
        out_shape=jax.ShapeDtypeStruct((m, n), BF16),
        compiler_params=_cparams(("parallel", "parallel"), 48),
        name="gated_merge",
    )(att, rec, proj_main, proj_main, w_up_att, w_up_hgrn)


def _ffn_kernel(hn_ref, h_ref, w1_ref, w3_ref, w2_ref, o_ref):
    @pl.when(pl.program_id(1) == 0)
    def _():
        o_ref[...] = h_ref[...]

    x = hn_ref[...]
    a = _dot(x, w1_ref[...])
    b = _dot(x, w3_ref[...])
    act = (a * _sigmoid(a) * b).astype(BF16)
    o_ref[...] += _dot(act, w2_ref[...])


def ffn_dense(hn, h, w1, w3, w2, tm=1024, tf=512):
    m, d = hn.shape
    f = w1.shape[1]
    tm = min(tm, m)
    return pl.pallas_call(
        _ffn_kernel,
        grid=(m // tm, f // tf),
        in_specs=[pl.BlockSpec((tm, d), lambda i, j: (i, 0)),
                  pl.BlockSpec((tm, d), lambda i, j: (i, 0), pipeline_mode=pl.Buffered(1)),
                  pl.BlockSpec((d, tf), lambda i, j: (0, j)),
                  pl.BlockSpec((d, tf), lambda i, j: (0, j)),
                  pl.BlockSpec((tf, d), lambda i, j: (j, 0))],
        out_specs=pl.BlockSpec((tm, d), lambda i, j: (i, 0)),
        out_shape=jax.ShapeDtypeStruct((m, d), F32),
        compiler_params=_cparams(("parallel", "arbitrary"), 58),
        name="ffn_dense",
    )(hn, h, w1, w3, w2)


def _row_copy(src_ref, src_row, dst_ref, dst_row, sem):
    return pltpu.make_async_copy(src_ref.at[pl.ds(src_row, 1)], dst_ref.at[pl.ds(dst_row, 1)], sem)


def _moe_scatter_kernel(dest_ref, hn_ref, xs_init_ref, xs_ref, sem, *, tm, m):
    del xs_init_ref
    base = pl.program_id(0) * tm

    def issue(r, carry):
        for k in range(2):
            _row_copy(hn_ref, r, xs_ref, dest_ref[k * m + base + r], sem).start()
        return carry

    lax.fori_loop(0, tm, issue, 0, unroll=8)
    for k in range(2):
        pltpu.make_async_copy(hn_ref, xs_ref.at[pl.ds(0, tm)], sem).wait()


def moe_scatter(hn_packed, dest, n_rows, tm=512):
    m, c = hn_packed.shape
    xs_init = jnp.zeros((n_rows, c), I32)
    return pl.pallas_call(
        functools.partial(_moe_scatter_kernel, tm=tm, m=m),
        grid_spec=pltpu.PrefetchScalarGridSpec(
            num_scalar_prefetch=1,
            grid=(m // tm,),
            in_specs=[pl.BlockSpec((tm, c), lambda i, dest: (i, 0)),
                      pl.BlockSpec(memory_space=pl.ANY)],
            out_specs=pl.BlockSpec(memory_space=pl.ANY),
            scratch_shapes=[pltpu.SemaphoreType.DMA(())]),
        out_shape=jax.ShapeDtypeStruct((n_rows, c), I32),
        input_output_aliases={2: 0},
        compiler_params=_cparams(("arbitrary",), 32),
        name="moe_scatter",
    )(dest, hn_packed, xs_init)


def _moe_ffn_kernel(te_ref, tv_ref, xs_ref, w1_ref, w3_ref, w2_ref, o_ref, xb_ref):
    del te_ref
    i = pl.program_id(0)
    half = xs_ref.shape[1]

    @pl.when(pl.program_id(1) == 0)
    def _():
        o_ref[...] = jnp.zeros_like(o_ref)
        lo, hi = _unpack_bf16_pairs(xs_ref[...])
        xb_ref[:, 0:half] = lo
        xb_ref[:, half:2 * half] = hi

    @pl.when(tv_ref[i] == 1)
    def _():
        x = xb_ref[...]
        a = _dot(x, w1_ref[0])
        b = _dot(x, w3_ref[0])
        act = (a * _sigmoid(a) * b).astype(BF16)
        o_ref[...] += _dot(act, w2_ref[0])


def moe_ffn(xs, tile_expert, tile_valid, w1, w3, w2, tm=512, tf=1024):
    p, c = xs.shape
    ne, d, f = w1.shape
    nf = f // tf
    col = lambda i, j, te, tv: jnp.where(tv[i] == 1, j, nf - 1)
    return pl.pallas_call(
        _moe_ffn_kernel,
        grid_spec=pltpu.PrefetchScalarGridSpec(
            num_scalar_prefetch=2,
            grid=(p // tm, nf),
            in_specs=[pl.BlockSpec((tm, c), lambda i, j, te, tv: (i, 0)),
                      pl.BlockSpec((1, d, tf), lambda i, j, te, tv: (te[i], 0, col(i, j, te, tv))),
                      pl.BlockSpec((1, d, tf), lambda i, j, te, tv: (te[i], 0, col(i, j, te, tv))),
                      pl.BlockSpec((1, tf, d), lambda i, j, te, tv: (te[i], col(i, j, te, tv), 0))],
            out_specs=pl.BlockSpec((tm, d), lambda i, j, te, tv: (i, 0)),
            scratch_shapes=[pltpu.VMEM((tm, d), BF16)]),
        out_shape=jax.ShapeDtypeStruct((p, d), F32),
        compiler_params=_cparams(("arbitrary", "arbitrary"), 56),
        name="moe_ffn",
    )(tile_expert, tile_valid, xs, w1, w3, w2)


def _moe_combine_kernel(dest_ref, h_ref, prob_ref, ys_ref, o_ref, ybuf_ref, sem, *, tm, m):
    i = pl.program_id(0)
    slot = i % 2

    def gather_tile(tile, slot_):
        base = tile * tm

        def issue(r, carry):
            for k in range(2):
                _row_copy(ys_ref, dest_ref[k * m + base + r], ybuf_ref.at[slot_, k], r,
                          sem.at[slot_]).start()
            return carry

        lax.fori_loop(0, tm, issue, 0, unroll=8)

    @pl.when(i == 0)
    def _():
        gather_tile(0, 0)

    @pl.when(i + 1 < pl.num_programs(0))
    def _():
        gather_tile(i + 1, 1 - slot)

    for k in range(2):
        pltpu.make_async_copy(ys_ref.at[pl.ds(0, tm)], ybuf_ref.at[slot, k], sem.at[slot]).wait()
    prob = prob_ref[...]
    o_ref[...] = h_ref[...] + prob[:, 0:1] * ybuf_ref[slot, 0] + prob[:, 1:2] * ybuf_ref[slot, 1]


def moe_combine(h, prob, ys, dest, tm=512):
    m, d = h.shape
    return pl.pallas_call(
        functools.partial(_moe_combine_kernel, tm=tm, m=m),
        grid_spec=pltpu.PrefetchScalarGridSpec(
            num_scalar_prefetch=1,
            grid=(m // tm,),
            in_specs=[pl.BlockSpec((tm, d), lambda i, dest: (i, 0)),
                      pl.BlockSpec((tm, LANES), lambda i, dest: (i, 0)),
                      pl.BlockSpec(memory_space=pl.ANY)],
            out_specs=pl.BlockSpec((tm, d), lambda i, dest: (i, 0)),
            scratch_shapes=[pltpu.VMEM((2, 2, tm, d), F32), pltpu.SemaphoreType.DMA((2,))]),
        out_shape=jax.ShapeDtypeStruct((m, d), F32),
        compiler_params=_cparams(("arbitrary",), 48),
        name="moe_combine",
    )(dest, h, prob, ys)


def ffn_moe(h, norm_g, w_router, w1, w3, w2, tm=512):
    m, d = h.shape
    ne = w1.shape[0]
    hn_packed, meta, prob, cnt = rmsnorm_router(h, norm_g, w_router)
    counts = cnt[0, :ne].astype(I32)
    tiles_per = (counts + tm - 1) // tm
    tile_end = jnp.cumsum(tiles_per)
    offset = (tile_end - tiles_per) * tm
    dest = jnp.concatenate([offset[meta[:, 0]] + meta[:, 2], offset[meta[:, 1]] + meta[:, 3]])
    n_tiles = (2 * m) // tm + ne
    tile_ids = jnp.arange(n_tiles, dtype=I32)
    tile_valid = (tile_ids < tile_end[-1]).astype(I32)
    tile_expert = jnp.searchsorted(tile_end, jnp.minimum(tile_ids, tile_end[-1] - 1), side="right").astype(I32)
    xs = moe_scatter(hn_packed, dest, n_tiles * tm)
    ys = moe_ffn(xs, tile_expert, tile_valid, w1, w3, w2, tm=tm)
    return moe_combine(h, prob, ys, dest)


def _split_w_in(w_in):
    sizes = [ATT_W, KV_W, KV_W, IDX_Q_RANK, IDX_HEAD_DIM, IDX_HEADS,
             HGRN_W, HGRN_W, HGRN_W, HGRN_W, w_in.shape[0], w_in.shape[0]]
    offs = np.cumsum([0] + sizes)
    part = lambda n: w_in[:, offs[n]:offs[n + 1]]
    main = jnp.concatenate([part(n) for n in (0, 1, 2, 3, 6, 7, 8, 9, 10, 11)], axis=1)
    pad = jnp.zeros((w_in.shape[0], LANES - IDX_HEAD_DIM - IDX_HEADS), w_in.dtype)
    idx = jnp.concatenate([part(4), part(5), pad], axis=1)
    return main.astype(BF16), idx.astype(BF16)


def _mixer(x2, batch, seq, norm_g, w_in, q_norm_g, k_norm_g, idx_q_norm_g, w_idx_q, ln_g, ln_b,
           lb, hgrn_norm_g, w_up_att, w_up_hgrn, w_o, bias):
    m = batch * seq
    xn = rmsnorm_rows(x2, norm_g)
    w_main, w_idx = _split_w_in(w_in)
    proj_main = matmul(xn, w_main, F32)
    proj_idx = matmul(xn, w_idx, F32)
    qn, kn, v, qi, ki, w = attn_prep(proj_main, proj_idx, q_norm_g, k_norm_g, idx_q_norm_g,
                                     w_idx_q.astype(BF16), ln_g, ln_b)
    wt = w[:, IDX_HEAD_DIM:IDX_HEAD_DIM + IDX_HEADS].reshape(batch, seq, IDX_HEADS).transpose(0, 2, 1)
    vt = v.reshape(batch, seq // TK, TK, KV_W).transpose(0, 1, 3, 2)
    att = dsa_attention(qi, wt, qn, ki, kn, vt, bias, batch, seq)
    rec = hgrn2(proj_main, lb, hgrn_norm_g, batch, seq)
    merged = gated_merge(att, rec, proj_main, w_up_att.astype(BF16), w_up_hgrn.astype(BF16))
    return matmul_residual(merged, w_o.astype(BF16), x2)


def kernel(x, rel_bias, norm_mix_g, norm_ffn_g, w_in, q_norm_g, k_norm_g, idx_q_norm_g, w_idx_q,
           idx_k_ln_g, idx_k_ln_b, hgrn_lb_logits, hgrn_out_norm_g, w_up_att, w_up_hgrn, w_o,
           w1_dense, w3_dense, w2_dense, w_router, w1_moe, w3_moe, w2_moe):
    batch, seq, d = x.shape
    depth = w_in.shape[0]
    lb_all = jnp.cumsum(jax.nn.softmax(hgrn_lb_logits.astype(F32), axis=0), axis=0)
    lb_all = lb_all - lb_all[0:1]
    bias = bias_tables(rel_bias)
    x2 = x.reshape(batch * seq, d)
    for l in range(depth):
        h = _mixer(x2, batch, seq, norm_mix_g[l], w_in[l], q_norm_g[l], k_norm_g[l],
                   idx_q_norm_g[l], w_idx_q[l], idx_k_ln_g[l], idx_k_ln_b[l], lb_all[l],
                   hgrn_out_norm_g[l], w_up_att[l], w_up_hgrn[l], w_o[l], bias)
        j = l // 2
        if l % 2 == 0:
            hn = rmsnorm_rows(h, norm_ffn_g[l])
            x2 = ffn_dense(hn, h, w1_dense[j].astype(BF16), w3_dense[j].astype(BF16),
                           w2_dense[j].astype(BF16))
        else:
            x2 = ffn_moe(h, norm_ffn_g[l], w_router[j], w1_moe[j].astype(BF16),
                         w3_moe[j].astype(BF16), w2_moe[j].astype(BF16))
    return x2.reshape(batch, seq, d)
```

```python
import functools
import math

import numpy as np
import jax
import jax.numpy as jnp
from jax import lax
from jax.experimental import pallas as pl
from jax.experimental.pallas import tpu as pltpu

F32 = jnp.float32
BF16 = jnp.bfloat16
I32 = jnp.int32

EPS = 1e-6
ATT_HEADS = 8
ATT_KV_HEADS = 2
ATT_GROUP = ATT_HEADS // ATT_KV_HEADS
HEAD_DIM = 128
ATT_W = ATT_HEADS * HEAD_DIM
KV_W = ATT_KV_HEADS * HEAD_DIM
IDX_HEADS = 16
IDX_HEAD_DIM = 64
IDX_Q_RANK = 512
TOPK_MAX = 256
HGRN_HEADS = 8
HGRN_DIM = 128
HGRN_W = HGRN_HEADS * HGRN_DIM
REL_BUCKETS = 32
REL_MAX_DIST = 128
N_EXPERTS = 8
LANES = 128
INT_MIN = -(2 ** 31)
NEG_BIG = -1e30
LOG2E = math.log2(math.e)

TQ = 256
TK = 256
HC = 128
HGRN_LEVELS = (1, 2, 4, 8, 16, 32, 64)
N_ARG_GROUPS = len(HGRN_LEVELS) + 2

COL_Q, COL_KVC, COL_HQ, COL_HF, COL_HI, COL_HG, COL_GA, COL_GH = 0, 1, 2, 3, 4, 5, 6, 8
MAIN_W = 10 * 1024


def _cparams(sem, vmem_mb):
    return pltpu.CompilerParams(dimension_semantics=sem, vmem_limit_bytes=vmem_mb << 20)


def _dot(a, b):
    return jnp.dot(a, b, preferred_element_type=F32)


def _dot_nt(a, b):
    return lax.dot_general(a, b, (((1,), (1,)), ((), ())), preferred_element_type=F32)


def _sigmoid(x):
    return 1.0 / (1.0 + jnp.exp(-x))


def _rmsnorm_kernel(x_ref, g_ref, o_ref):
    x = x_ref[...]
    ms = jnp.mean(x * x, axis=-1, keepdims=True)
    o_ref[...] = (x * lax.rsqrt(ms + EPS) * g_ref[...]).astype(o_ref.dtype)


def rmsnorm_rows(x, g, tm=512):
    m, d = x.shape
    return pl.pallas_call(
        _rmsnorm_kernel,
        grid=(m // tm,),
        in_specs=[pl.BlockSpec((tm, d), lambda i: (i, 0)),
                  pl.BlockSpec((1, d), lambda i: (0, 0))],
        out_specs=pl.BlockSpec((tm, d), lambda i: (i, 0)),
        out_shape=jax.ShapeDtypeStruct((m, d), BF16),
        compiler_params=_cparams(("parallel",), 32),
        name="rmsnorm",
    )(x, g.reshape(1, d))


def _pack_bf16_pairs(x):
    c = x.shape[1] // 2
    lo = lax.bitcast_convert_type(x[:, :c].astype(BF16).astype(F32), I32)
    hi = lax.bitcast_convert_type(x[:, c:].astype(BF16).astype(F32), I32)
    return (hi & jnp.int32(-65536)) | lax.shift_right_logical(lo, 16)


def _unpack_bf16_pairs(p):
    lo = lax.bitcast_convert_type(lax.shift_left(p, 16), F32).astype(BF16)
    hi = lax.bitcast_convert_type(p & jnp.int32(-65536), F32).astype(BF16)
    return lo, hi


def _rmsnorm_router_kernel(x_ref, g_ref, wr_ref, tri_ref, o_ref, meta_ref, prob_ref, cnt_ref, carry_ref):
    @pl.when(pl.program_id(0) == 0)
    def _():
        carry_ref[...] = jnp.zeros_like(carry_ref)

    x = x_ref[...]
    ms = jnp.mean(x * x, axis=-1, keepdims=True)
    hn = x * lax.rsqrt(ms + EPS) * g_ref[...]
    o_ref[...] = _pack_bf16_pairs(hn)
    logits = jnp.dot(hn, wr_ref[...], precision=lax.Precision.HIGHEST,
                     preferred_element_type=F32)
    lane = lax.broadcasted_iota(I32, logits.shape, 1)
    logits = jnp.where(lane < N_EXPERTS, logits, -jnp.inf)
    v1 = jnp.max(logits, axis=-1, keepdims=True)
    i1 = jnp.min(jnp.where(logits == v1, lane, LANES), axis=-1, keepdims=True)
    rest = jnp.where(lane == i1, -jnp.inf, logits)
    v2 = jnp.max(rest, axis=-1, keepdims=True)
    i2 = jnp.min(jnp.where(rest == v2, lane, LANES), axis=-1, keepdims=True)
    e = jnp.exp(v2 - v1)
    p1 = 1.0 / (1.0 + e)
    prob_ref[...] = jnp.where(lane == 0, p1, 0.0) + jnp.where(lane == 1, e * p1, 0.0)
    hot = jnp.where((lane == i1) | (lane == i2), 1.0, 0.0)
    rank = _dot(tri_ref[...], hot.astype(BF16)) + carry_ref[0:1, :]
    carry_ref[...] = carry_ref[...] + jnp.sum(hot, axis=0, keepdims=True)
    cnt_ref[...] = carry_ref[...]
    r1 = jnp.sum(jnp.where(lane == i1, rank, 0.0), axis=-1, keepdims=True)
    r2 = jnp.sum(jnp.where(lane == i2, rank, 0.0), axis=-1, keepdims=True)
    meta = (jnp.where(lane == 0, i1, 0) + jnp.where(lane == 1, i2, 0)
            + jnp.where(lane == 2, r1.astype(I32), 0) + jnp.where(lane == 3, r2.astype(I32), 0))
    meta_ref[...] = meta


def rmsnorm_router(x, g, w_router, tm=512):
    m, d = x.shape
    wr = jnp.zeros((d, LANES), F32).at[:, :N_EXPERTS].set(w_router)
    tri = jnp.asarray(np.tril(np.ones((tm, tm), np.float32), -1), BF16)
    return pl.pallas_call(
        _rmsnorm_router_kernel,
        grid=(m // tm,),
        in_specs=[pl.BlockSpec((tm, d), lambda i: (i, 0)),
                  pl.BlockSpec((1, d), lambda i: (0, 0)),
                  pl.BlockSpec((d, LANES), lambda i: (0, 0)),
                  pl.BlockSpec((tm, tm), lambda i: (0, 0))],
        out_specs=[pl.BlockSpec((tm, d // 2), lambda i: (i, 0)),
                   pl.BlockSpec((tm, LANES), lambda i: (i, 0)),
                   pl.BlockSpec((tm, LANES), lambda i: (i, 0)),
                   pl.BlockSpec((8, LANES), lambda i: (0, 0))],
        out_shape=[jax.ShapeDtypeStruct((m, d // 2), I32),
                   jax.ShapeDtypeStruct((m, LANES), I32),
                   jax.ShapeDtypeStruct((m, LANES), F32),
                   jax.ShapeDtypeStruct((8, LANES), F32)],
        scratch_shapes=[pltpu.VMEM((8, LANES), F32)],
        compiler_params=_cparams(("arbitrary",), 40),
        name="rmsnorm_router",
    )(x, g.reshape(1, d), wr, tri)


def _matmul_kernel(x_ref, w_ref, o_ref):
    o_ref[...] = _dot(x_ref[...], w_ref[...]).astype(o_ref.dtype)


def matmul(x, w, out_dtype, tm=1024, tn=1024):
    m, k = x.shape
    n = w.shape[1]
    tm, tn = min(tm, m), min(tn, n)
    return pl.pallas_call(
        _matmul_kernel,
        grid=(m // tm, n // tn),
        in_specs=[pl.BlockSpec((tm, k), lambda i, j: (i, 0)),
                  pl.BlockSpec((k, tn), lambda i, j: (0, j))],
        out_specs=pl.BlockSpec((tm, tn), lambda i, j: (i, j)),
        out_shape=jax.ShapeDtypeStruct((m, n), out_dtype),
        compiler_params=_cparams(("parallel", "parallel"), 48),
        name="matmul",
    )(x, w)


def _attn_prep_kernel(q_ref, kvc_ref, pi_ref, qg_ref, kg_ref, cg_ref, wiq_ref, lng_ref, lnb_ref,
                      qn_ref, kn_ref, v_ref, qi_ref, ki_ref, w_ref):
    def head_rms(x, g):
        x = x.astype(F32)
        return x * lax.rsqrt(jnp.mean(x * x, axis=-1, keepdims=True) + EPS) * g

    att_scale = HEAD_DIM ** -0.5 * LOG2E
    for h in range(ATT_HEADS):
        sl = slice(h * HEAD_DIM, (h + 1) * HEAD_DIM)
        qn_ref[:, sl] = (head_rms(q_ref[:, sl], qg_ref[...]) * att_scale).astype(BF16)
    for g in range(ATT_KV_HEADS):
        sl = slice(g * HEAD_DIM, (g + 1) * HEAD_DIM)
        kn_ref[:, sl] = head_rms(kvc_ref[:, sl], kg_ref[...]).astype(BF16)
    v_ref[...] = kvc_ref[:, KV_W:2 * KV_W].astype(v_ref.dtype)
    cq = kvc_ref[:, 2 * KV_W:2 * KV_W + IDX_Q_RANK]
    cqn = head_rms(cq, cg_ref[...]).astype(BF16)
    qi_ref[...] = _dot(cqn, wiq_ref[...]).astype(BF16)
    pi = pi_ref[...]
    is_key = lax.broadcasted_iota(I32, pi.shape, 1) < IDX_HEAD_DIM
    mu = jnp.sum(jnp.where(is_key, pi, 0.0), axis=-1, keepdims=True) * (1.0 / IDX_HEAD_DIM)
    cen = jnp.where(is_key, pi - mu, 0.0)
    var = jnp.sum(cen * cen, axis=-1, keepdims=True) * (1.0 / IDX_HEAD_DIM)
    kidn = cen * lax.rsqrt(var + EPS) * lng_ref[...] + lnb_ref[...]
    ki_ref[:, 0:LANES] = kidn.astype(BF16)
    ki_ref[:, LANES:2 * LANES] = pltpu.roll(kidn, IDX_HEAD_DIM, 1).astype(BF16)
    w_ref[...] = pi * (IDX_HEAD_DIM ** -0.5 * IDX_HEADS ** -0.5)


def attn_prep(proj_main, proj_idx, q_norm_g, k_norm_g, idx_q_norm_g, w_idx_q, ln_g, ln_b, tm=512):
    m = proj_main.shape[0]
    row = lambda i: (i, 0)
    const = lambda i: (0, 0)
    lane_pad = lambda a: jnp.zeros((1, LANES), F32).at[0, :a.shape[0]].set(a)
    return pl.pallas_call(
        _attn_prep_kernel,
        grid=(m // tm,),
        in_specs=[pl.BlockSpec((tm, 1024), lambda i: (i, COL_Q)),
                  pl.BlockSpec((tm, 1024), lambda i: (i, COL_KVC)),
                  pl.BlockSpec((tm, LANES), row),
                  pl.BlockSpec((1, HEAD_DIM), const),
                  pl.BlockSpec((1, HEAD_DIM), const),
                  pl.BlockSpec((1, IDX_Q_RANK), const),
                  pl.BlockSpec((IDX_Q_RANK, IDX_HEADS * IDX_HEAD_DIM), const),
                  pl.BlockSpec((1, LANES), const),
                  pl.BlockSpec((1, LANES), const)],
        out_specs=[pl.BlockSpec((tm, ATT_W), row),
                   pl.BlockSpec((tm, KV_W), row),
                   pl.BlockSpec((tm, KV_W), row),
                   pl.BlockSpec((tm, IDX_HEADS * IDX_HEAD_DIM), row),
                   pl.BlockSpec((tm, 2 * LANES), row),
                   pl.BlockSpec((tm, LANES), row)],
        out_shape=[jax.ShapeDtypeStruct((m, ATT_W), BF16),
                   jax.ShapeDtypeStruct((m, KV_W), BF16),
                   jax.ShapeDtypeStruct((m, KV_W), BF16),
                   jax.ShapeDtypeStruct((m, IDX_HEADS * IDX_HEAD_DIM), BF16),
                   jax.ShapeDtypeStruct((m, 2 * LANES), BF16),
                   jax.ShapeDtypeStruct((m, LANES), F32)],
        compiler_params=_cparams(("parallel",), 40),
        name="attn_prep",
    )(proj_main, proj_main, proj_idx, q_norm_g.reshape(1, -1), k_norm_g.reshape(1, -1),
      idx_q_norm_g.reshape(1, -1), w_idx_q, lane_pad(ln_g), lane_pad(ln_b))


def _rel_bucket(dist):
    max_exact = REL_BUCKETS // 2
    n = jnp.maximum(dist, 0)
    nf = jnp.maximum(n, 1).astype(F32)
    large = max_exact + (jnp.log(nf / max_exact) / math.log(REL_MAX_DIST / max_exact)
                         * (REL_BUCKETS - max_exact)).astype(I32)
    large = jnp.minimum(large, REL_BUCKETS - 1)
    return jnp.where(n < max_exact, n, large)


def _bias_kernel(rb_ref, bkt_ref, o_ref):
    h = pl.program_id(0)
    for kind in range(3):
        b = bkt_ref[kind]
        acc = jnp.zeros(b.shape, F32)
        for n in range(REL_BUCKETS):
            acc = jnp.where(b == n, rb_ref[n, h], acc)
        o_ref[0, kind] = acc * LOG2E


def bias_tables(rel_bias):
    assert TK >= REL_MAX_DIST
    kpos = jnp.arange(TK, dtype=I32)[:, None]
    qpos = jnp.arange(TQ, dtype=I32)[None, :]
    dist = jnp.stack([qpos - kpos, qpos - kpos + TK, qpos - kpos + 2 * TK])
    bkt = _rel_bucket(dist)
    return pl.pallas_call(
        _bias_kernel,
        grid=(ATT_HEADS,),
        in_specs=[pl.BlockSpec(memory_space=pltpu.SMEM),
                  pl.BlockSpec((3, TK, TQ), lambda h: (0, 0, 0))],
        out_specs=pl.BlockSpec((1, 3, TK, TQ), lambda h: (h, 0, 0, 0)),
        out_shape=jax.ShapeDtypeStruct((ATT_HEADS, 3, TK, TQ), F32),
        compiler_params=_cparams(("arbitrary",), 32),
        name="bias_tables",
    )(rel_bias, bkt)


def _dsa_kernel(qi_ref, wt_ref, qn_ref, ki_ref, kn_ref, vt_ref, bias_ref, o_ref,
                key_ref, m_ref, l_ref, a_ref, acc_ref, s_ref, p_ref, *, n_sel):
    i = pl.program_id(1)
    nch = i + 1
    qpos = i * TQ + lax.broadcasted_iota(I32, (TK, TQ), 1)

    def score_chunk(j, carry):
        k0 = pl.multiple_of(j * TK, TK)
        acc = jnp.zeros((TK, TQ), F32)
        for h in range(IDX_HEADS):
            par = h % 2
            kc = ki_ref[pl.ds(k0, TK), par * LANES:(par + 1) * LANES]
            qh = qi_ref[:, (h // 2) * LANES:(h // 2 + 1) * LANES]
            s = _dot_nt(kc, qh)
            acc = acc + jnp.maximum(s, 0.0) * wt_ref[0, h:h + 1, :]
        bits = lax.bitcast_convert_type(acc, I32)
        key = jnp.where(bits < 0, bits ^ jnp.int32(0x7FFFFFFF), bits)
        kpos = k0 + lax.broadcasted_iota(I32, (TK, TQ), 0)
        key_ref[pl.ds(k0, TK), :] = jnp.where(kpos <= qpos, key, INT_MIN)
        return carry

    lax.fori_loop(0, nch, score_chunk, 0)

    def count_ge(cand):
        def body(j, acc):
            k0 = pl.multiple_of(j * TK, TK)
            hit = jnp.where(key_ref[pl.ds(k0, TK), :] >= cand, 1, 0).astype(I32)
            return acc + jnp.sum(hit.reshape(TK // 8, 8, TQ), axis=0)
        acc = lax.fori_loop(0, nch, body, jnp.zeros((8, TQ), I32))
        return jnp.sum(acc, axis=0, keepdims=True)

    thr = jnp.where(count_ge(jnp.zeros((1, TQ), I32)) >= n_sel, 0, INT_MIN).astype(I32)

    def bit_body(b, thr):
        cand = thr | jnp.left_shift(jnp.int32(1), 30 - b)
        return jnp.where(count_ge(cand) >= n_sel, cand, thr)

    thr = lax.fori_loop(0, 31, bit_body, thr)
    thr = jnp.maximum(thr, INT_MIN + 1)

    m_ref[...] = jnp.full(m_ref.shape, NEG_BIG, F32)
    l_ref[...] = jnp.zeros(l_ref.shape, F32)
    acc_ref[...] = jnp.zeros(acc_ref.shape, F32)

    def att_chunk(j, carry):
        k0 = pl.multiple_of(j * TK, TK)
        neg = jnp.where(key_ref[pl.ds(k0, TK), :] >= thr, 0.0, NEG_BIG)
        kind = jnp.minimum(i - j, 2)
        for h in range(ATT_HEADS):
            g = h // ATT_GROUP
            kc = kn_ref[pl.ds(k0, TK), g * HEAD_DIM:(g + 1) * HEAD_DIM]
            qh = qn_ref[:, h * HEAD_DIM:(h + 1) * HEAD_DIM]
            lg = _dot_nt(kc, qh) + bias_ref[h, kind] + neg
            s_ref[h] = lg
            m_old = m_ref[h:h + 1, :]
            m_new = jnp.maximum(m_old, jnp.max(lg, axis=0, keepdims=True))
            a_ref[h:h + 1, :] = jnp.exp2(m_old - m_new)
            m_ref[h:h + 1, :] = m_new
        for h in range(ATT_HEADS):
            p = jnp.exp2(s_ref[h] - m_ref[h:h + 1, :])
            l_ref[h:h + 1, :] = (a_ref[h:h + 1, :] * l_ref[h:h + 1, :]
                                 + jnp.sum(p, axis=0, keepdims=True))
            p_ref[h] = p.astype(BF16)
        for h in range(ATT_HEADS):
            g = h // ATT_GROUP
            vc = vt_ref[0, j, g * HEAD_DIM:(g + 1) * HEAD_DIM, :]
            acc_ref[h] = a_ref[h:h + 1, :] * acc_ref[h] + _dot(vc, p_ref[h])
        return carry

    lax.fori_loop(0, nch, att_chunk, 0)
    for h in range(ATT_HEADS):
        o = acc_ref[h] * (1.0 / l_ref[h:h + 1, :])
        o_ref[:, h * HEAD_DIM:(h + 1) * HEAD_DIM] = o.T.astype(o_ref.dtype)


def dsa_attention(qi, wt, qn, ki, kn, vt, bias, batch, seq):
    nq = seq // TQ
    n_sel = min(TOPK_MAX, seq // 4)
    qrow = lambda b, i: (b * nq + i, 0)
    brow = lambda b, i: (b, 0)
    return pl.pallas_call(
        functools.partial(_dsa_kernel, n_sel=n_sel),
        grid=(batch, nq),
        in_specs=[pl.BlockSpec((TQ, IDX_HEADS * IDX_HEAD_DIM), qrow),
                  pl.BlockSpec((1, IDX_HEADS, TQ), lambda b, i: (b, 0, i)),
                  pl.BlockSpec((TQ, ATT_W), qrow),
                  pl.BlockSpec((seq, 2 * LANES), brow),
                  pl.BlockSpec((seq, KV_W), brow),
                  pl.BlockSpec((1, seq // TK, KV_W, TK), lambda b, i: (b, 0, 0, 0)),
                  pl.BlockSpec((ATT_HEADS, 3, TK, TQ), lambda b, i: (0, 0, 0, 0))],
        out_specs=pl.BlockSpec((TQ, ATT_W), qrow),
        out_shape=jax.ShapeDtypeStruct((batch * seq, ATT_W), BF16),
        scratch_shapes=[pltpu.VMEM((seq, TQ), I32),
                        pltpu.VMEM((ATT_HEADS, TQ), F32),
                        pltpu.VMEM((ATT_HEADS, TQ), F32),
                        pltpu.VMEM((ATT_HEADS, TQ), F32),
                        pltpu.VMEM((ATT_HEADS, HEAD_DIM, TQ), F32),
                        pltpu.VMEM((ATT_HEADS, TK, TQ), F32),
                        pltpu.VMEM((ATT_HEADS, TK, TQ), BF16)],
        compiler_params=_cparams(("parallel", "arbitrary"), 48),
        name="dsa_attention",
    )(qi, wt, qn, ki, kn, vt, bias)


def _hgrn_constants():
    t = np.arange(HC)
    rows = []
    masks = [np.eye(HC, dtype=np.float32)]
    for m in HGRN_LEVELS:
        upper = (t // m) % 2 == 1
        start = (t // m) * m
        end = start + m - 1
        u = t[None, :]
        q_side = upper[:, None] & (u >= start[:, None]) & (u <= t[:, None])
        k_side = (~upper)[:, None] & (u > t[:, None]) & (u <= end[:, None])
        rows.append((q_side | k_side).astype(np.float32))
        same = (t[:, None] // (2 * m)) == (t[None, :] // (2 * m))
        masks.append((upper[:, None] & (~upper)[None, :] & same).astype(np.float32))
    u = t[None, :]
    rows.append((u <= t[:, None]).astype(np.float32))
    rows.append((u > t[:, None]).astype(np.float32))
    mat = np.concatenate(rows, axis=0)
    return np.concatenate([mat, mat], axis=1), np.stack(masks)


def _hgrn_kernel(hq_ref, hf_ref, hi_ref, hg_ref, lb_ref, gn_ref, mat_ref, mask_ref, o_ref,
                 st_ref, arg_ref, kk_ref, lf_ref, zq_ref, zk_ref):
    c = pl.program_id(1)

    @pl.when(c == 0)
    def _():
        st_ref[...] = jnp.zeros_like(st_ref)

    lb = lb_ref[...]
    f = hf_ref[...].astype(F32)
    e = jnp.exp(-jnp.abs(f))
    r = 1.0 / (1.0 + e)
    log_sig = jnp.minimum(f, 0.0) - jnp.log(1.0 + e)
    la = jnp.log(lb)
    lc = jnp.log(1.0 - lb) + log_sig
    logf = (jnp.maximum(la, lc) + jnp.log(1.0 + jnp.exp(-jnp.abs(la - lc)))) * LOG2E
    kk_ref[...] = (1.0 - lb) * jnp.where(f >= 0, e * r, r)
    hi = logf.astype(BF16)
    lf_ref[0:HC, :] = hi
    lf_ref[HC:2 * HC, :] = (logf - hi.astype(F32)).astype(BF16)
    arg_ref[...] = _dot(mat_ref[...], lf_ref[...])

    scale = HGRN_DIM ** -0.5
    nl = len(HGRN_LEVELS)
    for h in range(HGRN_HEADS):
        sl = slice(h * HGRN_DIM, (h + 1) * HGRN_DIM)
        hq = hq_ref[:, sl].astype(F32)
        q = hq * _sigmoid(hq) * scale
        k = kk_ref[:, sl]
        zq_ref[h, 0] = q.astype(BF16)
        zk_ref[h, 0] = k.astype(BF16)
        for lv in range(nl):
            ex = jnp.exp2(arg_ref[lv * HC:(lv + 1) * HC, sl])
            zq_ref[h, lv + 1] = (q * ex).astype(BF16)
            zk_ref[h, lv + 1] = (k * ex).astype(BF16)
        zq_ref[h, nl + 1] = (q * jnp.exp2(arg_ref[nl * HC:(nl + 1) * HC, sl])).astype(BF16)
        zk_ref[h, nl + 1] = (k * jnp.exp2(arg_ref[(nl + 1) * HC:(nl + 2) * HC, sl])).astype(BF16)

    for h in range(HGRN_HEADS):
        sl = slice(h * HGRN_DIM, (h + 1) * HGRN_DIM)
        a = _dot_nt(zq_ref[h, 0], zk_ref[h, 0]) * mask_ref[0]
        for lv in range(nl):
            a = a + _dot_nt(zq_ref[h, lv + 1], zk_ref[h, lv + 1]) * mask_ref[lv + 1]
        v = hi_ref[:, sl].astype(BF16)
        st = st_ref[h]
        o = _dot(a.astype(BF16), v) + _dot_nt(zq_ref[h, nl + 1], st.astype(BF16))
        decay = jnp.exp2(arg_ref[(nl + 1) * HC - 1:(nl + 1) * HC, sl])
        st_ref[h] = decay * st + _dot(v.astype(F32).T.astype(BF16), zk_ref[h, nl + 1])
        on = o * lax.rsqrt(jnp.mean(o * o, axis=-1, keepdims=True) + EPS) * gn_ref[...]
        hg = hg_ref[:, sl].astype(F32)
        o_ref[:, sl] = (on * hg * _sigmoid(hg)).astype(o_ref.dtype)


def hgrn2(proj_main, lb, g_norm, batch, seq):
    nc = seq // HC
    mat, masks = _hgrn_constants()
    col = lambda cb: (lambda b, c: (b * nc + c, cb))
    return pl.pallas_call(
        _hgrn_kernel,
        grid=(batch, nc),
        in_specs=[pl.BlockSpec((HC, HGRN_W), col(COL_HQ)),
                  pl.BlockSpec((HC, HGRN_W), col(COL_HF)),
                  pl.BlockSpec((HC, HGRN_W), col(COL_HI)),
                  pl.BlockSpec((HC, HGRN_W), col(COL_HG)),
                  pl.BlockSpec((1, HGRN_W), lambda b, c: (0, 0)),
                  pl.BlockSpec((1, HGRN_DIM), lambda b, c: (0, 0)),
                  pl.BlockSpec((N_ARG_GROUPS * HC, 2 * HC), lambda b, c: (0, 0)),
                  pl.BlockSpec((len(HGRN_LEVELS) + 1, HC, HC), lambda b, c: (0, 0, 0))],
        out_specs=pl.BlockSpec((HC, HGRN_W), lambda b, c: (b * nc + c, 0)),
        out_shape=jax.ShapeDtypeStruct((batch * seq, HGRN_W), BF16),
        scratch_shapes=[pltpu.VMEM((HGRN_HEADS, HGRN_DIM, HGRN_DIM), F32),
                        pltpu.VMEM((N_ARG_GROUPS * HC, HGRN_W), F32),
                        pltpu.VMEM((HC, HGRN_W), F32),
                        pltpu.VMEM((2 * HC, HGRN_W), BF16),
                        pltpu.VMEM((HGRN_HEADS, N_ARG_GROUPS, HC, HGRN_DIM), BF16),
                        pltpu.VMEM((HGRN_HEADS, N_ARG_GROUPS, HC, HGRN_DIM), BF16)],
        compiler_params=_cparams(("parallel", "arbitrary"), 40),
        name="hgrn2",
    )(proj_main, proj_main, proj_main, proj_main, lb.reshape(1, -1), g_norm.reshape(1, -1),
      jnp.asarray(mat, BF16), jnp.asarray(masks, F32))


def _merge_out_kernel(att_ref, rec_ref, ga_ref, gh_ref, x_ref, wa_ref, wh_ref, wo_ref, g_ref,
                      h_ref, *maybe_hn_ref):
    j = pl.program_id(1)

    @pl.when(j == 0)
    def _():
        h_ref[...] = x_ref[...]

    a = _dot(att_ref[...], wa_ref[...])
    r = _dot(rec_ref[...], wh_ref[...])
    merged = (_sigmoid(ga_ref[...].astype(F32)) * a + _sigmoid(gh_ref[...].astype(F32)) * r).astype(BF16)
    h_ref[...] += _dot(merged, wo_ref[...])

    if maybe_hn_ref:
        @pl.when(j == pl.num_programs(1) - 1)
        def _():
            h = h_ref[...]
            ms = jnp.mean(h * h, axis=-1, keepdims=True)
            maybe_hn_ref[0][...] = (h * lax.rsqrt(ms + EPS) * g_ref[...]).astype(BF16)


def merge_out(att, rec, proj_main, x, w_up_att, w_up_hgrn, w_o, norm_g, emit_hn, tm=512, tn=512):
    m, d = x.shape
    ga0, gh0 = COL_GA * 1024 // tn, COL_GH * 1024 // tn
    row = lambda i, j: (i, 0)
    out_specs = [pl.BlockSpec((tm, d), row)]
    out_shape = [jax.ShapeDtypeStruct((m, d), F32)]
    if emit_hn:
        out_specs.append(pl.BlockSpec((tm, d), row))
        out_shape.append(jax.ShapeDtypeStruct((m, d), BF16))
    return pl.pallas_call(
        _merge_out_kernel,
        grid=(m // tm, d // tn),
        in_specs=[pl.BlockSpec((tm, ATT_W), row),
                  pl.BlockSpec((tm, HGRN_W), row),
                  pl.BlockSpec((tm, tn), lambda i, j: (i, ga0 + j)),
                  pl.BlockSpec((tm, tn), lambda i, j: (i, gh0 + j)),
                  pl.BlockSpec((tm, d), row),
                  pl.BlockSpec((ATT_W, tn), lambda i, j: (0, j)),
                  pl.BlockSpec((HGRN_W, tn), lambda i, j: (0, j)),
                  pl.BlockSpec((tn, d), lambda i, j: (j, 0)),
                  pl.BlockSpec((1, d), lambda i, j: (0, 0))],
        out_specs=out_specs,
        out_shape=out_shape,
        compiler_params=_cparams(("parallel", "arbitrary"), 48),
        name="merge_out",
    )(att, rec, proj_main, proj_main, x, w_up_att, w_up_hgrn, w_o, norm_g.reshape(1, d))


def _ffn_kernel(hn_ref, h_ref, w1_ref, w3_ref, w2_ref, g_ref, o_ref, *maybe_xn_ref):
    j = pl.program_id(1)

    @pl.when(j == 0)
    def _():
        o_ref[...] = h_ref[...]

    x = hn_ref[...]
    a = _dot(x, w1_ref[...])
    b = _dot(x, w3_ref[...])
    act = (a * _sigmoid(a) * b).astype(BF16)
    o_ref[...] += _dot(act, w2_ref[...])

    if maybe_xn_ref:
        @pl.when(j == pl.num_programs(1) - 1)
        def _():
            y = o_ref[...]
            ms = jnp.mean(y * y, axis=-1, keepdims=True)
            maybe_xn_ref[0][...] = (y * lax.rsqrt(ms + EPS) * g_ref[...]).astype(BF16)


def ffn_dense(hn, h, w1, w3, w2, next_norm_g, emit_xn, tm=512, tf=512):
    m, d = hn.shape
    f = w1.shape[1]
    row = lambda i, j: (i, 0)
    out_specs = [pl.BlockSpec((tm, d), row)]
    out_shape = [jax.ShapeDtypeStruct((m, d), F32)]
    if emit_xn:
        out_specs.append(pl.BlockSpec((tm, d), row))
        out_shape.append(jax.ShapeDtypeStruct((m, d), BF16))
    return pl.pallas_call(
        _ffn_kernel,
        grid=(m // tm, f // tf),
        in_specs=[pl.BlockSpec((tm, d), row),
                  pl.BlockSpec((tm, d), row),
                  pl.BlockSpec((d, tf), lambda i, j: (0, j)),
                  pl.BlockSpec((d, tf), lambda i, j: (0, j)),
                  pl.BlockSpec((tf, d), lambda i, j: (j, 0)),
                  pl.BlockSpec((1, d), lambda i, j: (0, 0))],
        out_specs=out_specs,
        out_shape=out_shape,
        compiler_params=_cparams(("parallel", "arbitrary"), 52),
        name="ffn_dense",
    )(hn, h, w1, w3, w2, next_norm_g.reshape(1, d))


def _row_copy(src_ref, src_row, dst_ref, dst_row, sem):
    return pltpu.make_async_copy(src_ref.at[pl.ds(src_row, 1)], dst_ref.at[pl.ds(dst_row, 1)], sem)


def _moe_scatter_kernel(dest_ref, hn_ref, xs_init_ref, xs_ref, sem, *, tm, m):
    del xs_init_ref
    base = pl.program_id(0) * tm

    def issue(r, carry):
        for k in range(2):
            _row_copy(hn_ref, r, xs_ref, dest_ref[k * m + base + r], sem).start()
        return carry

    lax.fori_loop(0, tm, issue, 0, unroll=8)
    for k in range(2):
        pltpu.make_async_copy(hn_ref, xs_ref.at[pl.ds(0, tm)], sem).wait()


def moe_scatter(hn_packed, dest, n_rows, tm=512):
    m, c = hn_packed.shape
    xs_init = jnp.zeros((n_rows, c), I32)
    return pl.pallas_call(
        functools.partial(_moe_scatter_kernel, tm=tm, m=m),
        grid_spec=pltpu.PrefetchScalarGridSpec(
            num_scalar_prefetch=1,
            grid=(m // tm,),
            in_specs=[pl.BlockSpec((tm, c), lambda i, dest: (i, 0)),
                      pl.BlockSpec(memory_space=pl.ANY)],
            out_specs=pl.BlockSpec(memory_space=pl.ANY),
            scratch_shapes=[pltpu.SemaphoreType.DMA(())]),
        out_shape=jax.ShapeDtypeStruct((n_rows, c), I32),
        input_output_aliases={2: 0},
        compiler_params=_cparams(("arbitrary",), 32),
        name="moe_scatter",
    )(dest, hn_packed, xs_init)


def _moe_ffn_kernel(te_ref, tv_ref, xs_ref, w1_ref, w3_ref, w2_ref, o_ref, xb_ref):
    del te_ref
    i = pl.program_id(0)
    half = xs_ref.shape[1]

    @pl.when(pl.program_id(1) == 0)
    def _():
        o_ref[...] = jnp.zeros_like(o_ref)
        lo, hi = _unpack_bf16_pairs(xs_ref[...])
        xb_ref[:, 0:half] = lo
        xb_ref[:, half:2 * half] = hi

    @pl.when(tv_ref[i] == 1)
    def _():
        x = xb_ref[...]
        a = _dot(x, w1_ref[0])
        b = _dot(x, w3_ref[0])
        act = (a * _sigmoid(a) * b).astype(BF16)
        o_ref[...] += _dot(act, w2_ref[0])


def moe_ffn(xs, tile_expert, tile_valid, w1, w3, w2, tm=512, tf=1024):
    p, c = xs.shape
    ne, d, f = w1.shape
    nf = f // tf
    col = lambda i, j, te, tv: jnp.where(tv[i] == 1, j, nf - 1)
    return pl.pallas_call(
        _moe_ffn_kernel,
        grid_spec=pltpu.PrefetchScalarGridSpec(
            num_scalar_prefetch=2,
            grid=(p // tm, nf),
            in_specs=[pl.BlockSpec((tm, c), lambda i, j, te, tv: (i, 0)),
                      pl.BlockSpec((1, d, tf), lambda i, j, te, tv: (te[i], 0, col(i, j, te, tv))),
                      pl.BlockSpec((1, d, tf), lambda i, j, te, tv: (te[i], 0, col(i, j, te, tv))),
                      pl.BlockSpec((1, tf, d), lambda i, j, te, tv: (te[i], col(i, j, te, tv), 0))],
            out_specs=pl.BlockSpec((tm, d), lambda i, j, te, tv: (i, 0)),
            scratch_shapes=[pltpu.VMEM((tm, d), BF16)]),
        out_shape=jax.ShapeDtypeStruct((p, d), F32),
        compiler_params=_cparams(("arbitrary", "arbitrary"), 56),
        name="moe_ffn",
    )(tile_expert, tile_valid, xs, w1, w3, w2)


def _moe_combine_kernel(dest_ref, h_ref, prob_ref, ys_ref, o_ref, ybuf_ref, sem, *, tm, m):
    i = pl.program_id(0)
    slot = i % 2

    def gather_tile(tile, slot_):
        base = tile * tm

        def issue(r, carry):
            for k in range(2):
                _row_copy(ys_ref, dest_ref[k * m + base + r], ybuf_ref.at[slot_, k], r,
                          sem.at[slot_]).start()
            return carry

        lax.fori_loop(0, tm, issue, 0, unroll=8)

    @pl.when(i == 0)
    def _():
        gather_tile(0, 0)

    @pl.when(i + 1 < pl.num_programs(0))
    def _():
        gather_tile(i + 1, 1 - slot)

    for k in range(2):
        pltpu.make_async_copy(ys_ref.at[pl.ds(0, tm)], ybuf_ref.at[slot, k], sem.at[slot]).wait()
    prob = prob_ref[...]
    o_ref[...] = h_ref[...] + prob[:, 0:1] * ybuf_ref[slot, 0] + prob[:, 1:2] * ybuf_ref[slot, 1]


def moe_combine(h, prob, ys, dest, tm=512):
    m, d = h.shape
    return pl.pallas_call(
        functools.partial(_moe_combine_kernel, tm=tm, m=m),
        grid_spec=pltpu.PrefetchScalarGridSpec(
            num_scalar_prefetch=1,
            grid=(m // tm,),
            in_specs=[pl.BlockSpec((tm, d), lambda i, dest: (i, 0)),
                      pl.BlockSpec((tm, LANES), lambda i, dest: (i, 0)),
                      pl.BlockSpec(memory_space=pl.ANY)],
            out_specs=pl.BlockSpec((tm, d), lambda i, dest: (i, 0)),
            scratch_shapes=[pltpu.VMEM((2, 2, tm, d), F32), pltpu.SemaphoreType.DMA((2,))]),
        out_shape=jax.ShapeDtypeStruct((m, d), F32),
        compiler_params=_cparams(("arbitrary",), 48),
        name="moe_combine",
    )(dest, h, prob, ys)


def ffn_moe(h, norm_g, w_router, w1, w3, w2, tm=512):
    m, d = h.shape
    ne = w1.shape[0]
    hn_packed, meta, prob, cnt = rmsnorm_router(h, norm_g, w_router)
    counts = cnt[0, :ne].astype(I32)
    tiles_per = (counts + tm - 1) // tm
    tile_end = jnp.cumsum(tiles_per)
    offset = (tile_end - tiles_per) * tm
    dest = jnp.concatenate([offset[meta[:, 0]] + meta[:, 2], offset[meta[:, 1]] + meta[:, 3]])
    n_tiles = (2 * m) // tm + ne
    tile_ids = jnp.arange(n_tiles, dtype=I32)
    tile_valid = (tile_ids < tile_end[-1]).astype(I32)
    tile_expert = jnp.searchsorted(tile_end, jnp.minimum(tile_ids, tile_end[-1] - 1), side="right").astype(I32)
    xs = moe_scatter(hn_packed, dest, n_tiles * tm)
    ys = moe_ffn(xs, tile_expert, tile_valid, w1, w3, w2, tm=tm)
    return moe_combine(h, prob, ys, dest)


def _w_in_prep_kernel(w_ref, main_ref, idx_ref):
    head = ATT_W + 2 * KV_W + IDX_Q_RANK
    small = IDX_HEAD_DIM + IDX_HEADS
    main_ref[:, 0:head] = w_ref[:, 0:head].astype(BF16)
    group = w_ref[:, head:head + LANES]
    lane = lax.broadcasted_iota(I32, group.shape, 1)
    idx_ref[...] = jnp.where(lane < small, group, 0.0).astype(BF16)
    step = 1024
    for c in range((MAIN_W - head) // step):
        src = head + small + c * step
        main_ref[:, head + c * step:head + (c + 1) * step] = w_ref[:, src:src + step].astype(BF16)


def _split_w_in(w_in, tr=256):
    d, n = w_in.shape
    assert n == MAIN_W + IDX_HEAD_DIM + IDX_HEADS
    return pl.pallas_call(
        _w_in_prep_kernel,
        grid=(d // tr,),
        in_specs=[pl.BlockSpec((tr, n), lambda i: (i, 0))],
        out_specs=[pl.BlockSpec((tr, MAIN_W), lambda i: (i, 0)),
                   pl.BlockSpec((tr, LANES), lambda i: (i, 0))],
        out_shape=[jax.ShapeDtypeStruct((d, MAIN_W), BF16),
                   jax.ShapeDtypeStruct((d, LANES), BF16)],
        compiler_params=_cparams(("parallel",), 48),
        name="w_in_prep",
    )(w_in)


def _mixer(x2, maybe_xn, batch, seq, norm_g, w_in, q_norm_g, k_norm_g, idx_q_norm_g, w_idx_q, ln_g,
           ln_b, lb, hgrn_norm_g, w_up_att, w_up_hgrn, w_o, bias, norm_ffn_g, emit_hn):
    xn = rmsnorm_rows(x2, norm_g) if maybe_xn is None else maybe_xn
    w_main, w_idx = _split_w_in(w_in)
    proj_main = matmul(xn, w_main, BF16)
    proj_idx = matmul(xn, w_idx, F32)
    qn, kn, v, qi, ki, w = attn_prep(proj_main, proj_idx, q_norm_g, k_norm_g, idx_q_norm_g,
                                     w_idx_q.astype(BF16), ln_g, ln_b)
    wt = w[:, IDX_HEAD_DIM:IDX_HEAD_DIM + IDX_HEADS].reshape(batch, seq, IDX_HEADS).transpose(0, 2, 1)
    vt = v.reshape(batch, seq // TK, TK, KV_W).transpose(0, 1, 3, 2)
    att = dsa_attention(qi, wt, qn, ki, kn, vt, bias, batch, seq)
    rec = hgrn2(proj_main, lb, hgrn_norm_g, batch, seq)
    return merge_out(att, rec, proj_main, x2, w_up_att.astype(BF16), w_up_hgrn.astype(BF16),
                     w_o.astype(BF16), norm_ffn_g, emit_hn)


def kernel(x, rel_bias, norm_mix_g, norm_ffn_g, w_in, q_norm_g, k_norm_g, idx_q_norm_g, w_idx_q,
           idx_k_ln_g, idx_k_ln_b, hgrn_lb_logits, hgrn_out_norm_g, w_up_att, w_up_hgrn, w_o,
           w1_dense, w3_dense, w2_dense, w_router, w1_moe, w3_moe, w2_moe):
    batch, seq, d = x.shape
    depth = w_in.shape[0]
    lb_all = jnp.cumsum(jax.nn.softmax(hgrn_lb_logits.astype(F32), axis=0), axis=0)
    lb_all = lb_all - lb_all[0:1]
    bias = bias_tables(rel_bias)
    x2 = x.reshape(batch * seq, d)
    maybe_xn = None
    for l in range(depth):
        dense = l % 2 == 0
        h, *maybe_hn = _mixer(x2, maybe_xn, batch, seq, norm_mix_g[l], w_in[l], q_norm_g[l],
                              k_norm_g[l], idx_q_norm_g[l], w_idx_q[l], idx_k_ln_g[l],
                              idx_k_ln_b[l], lb_all[l], hgrn_out_norm_g[l], w_up_att[l],
                              w_up_hgrn[l], w_o[l], bias, norm_ffn_g[l], emit_hn=dense)
        maybe_xn = None
        j = l // 2
        if dense:
            more = l + 1 < depth
            x2, *rest = ffn_dense(maybe_hn[0], h, w1_dense[j].astype(BF16), w3_dense[j].astype(BF16),
                                  w2_dense[j].astype(BF16), norm_mix_g[l + 1 if more else l], more)
            maybe_xn = rest[0] if more else None
        else:
            x2 = ffn_moe(h, norm_ffn_g[l], w_router[j], w1_moe[j].astype(BF16),
                         w3_moe[j].astype(BF16), w2_moe[j].astype(BF16))
    return x2.reshape(batch, seq, d)
```

```python
import functools
import math

import numpy as np
import jax
import jax.numpy as jnp
from jax import lax
from jax.experimental import pallas as pl
from jax.experimental.pallas import tpu as pltpu

F32 = jnp.float32
BF16 = jnp.bfloat16
I32 = jnp.int32

EPS = 1e-6
ATT_HEADS = 8
ATT_KV_HEADS = 2
ATT_GROUP = ATT_HEADS // ATT_KV_HEADS
HEAD_DIM = 128
ATT_W = ATT_HEADS * HEAD_DIM
KV_W = ATT_KV_HEADS * HEAD_DIM
IDX_HEADS = 16
IDX_HEAD_DIM = 64
IDX_Q_RANK = 512
TOPK_MAX = 256
HGRN_HEADS = 8
HGRN_DIM = 128
HGRN_W = HGRN_HEADS * HGRN_DIM
REL_BUCKETS = 32
REL_MAX_DIST = 128
N_EXPERTS = 8
LANES = 128
INT_MIN = -(2 ** 31)
NEG_BIG = -1e30
LOG2E = math.log2(math.e)

TQ = 256
TK = 256
HC = 128
HGRN_LEVELS = (1, 2, 4, 8, 16, 32, 64)
N_ARG_GROUPS = len(HGRN_LEVELS) + 2

COL_Q, COL_KVC, COL_HQ, COL_HF, COL_HI, COL_HG, COL_GA, COL_GH = 0, 1, 2, 3, 4, 5, 6, 8
MAIN_W = 10 * 1024


def _cparams(sem, vmem_mb):
    return pltpu.CompilerParams(dimension_semantics=sem, vmem_limit_bytes=vmem_mb << 20)


def _dot(a, b):
    return jnp.dot(a, b, preferred_element_type=F32)


def _dot_nt(a, b):
    return lax.dot_general(a, b, (((1,), (1,)), ((), ())), preferred_element_type=F32)


def _sigmoid(x):
    return 1.0 / (1.0 + jnp.exp(-x))


def _rmsnorm_kernel(x_ref, g_ref, o_ref):
    x = x_ref[...]
    ms = jnp.mean(x * x, axis=-1, keepdims=True)
    o_ref[...] = (x * lax.rsqrt(ms + EPS) * g_ref[...]).astype(o_ref.dtype)


def rmsnorm_rows(x, g, tm=512):
    m, d = x.shape
    return pl.pallas_call(
        _rmsnorm_kernel,
        grid=(m // tm,),
        in_specs=[pl.BlockSpec((tm, d), lambda i: (i, 0)),
                  pl.BlockSpec((1, d), lambda i: (0, 0))],
        out_specs=pl.BlockSpec((tm, d), lambda i: (i, 0)),
        out_shape=jax.ShapeDtypeStruct((m, d), BF16),
        compiler_params=_cparams(("parallel",), 32),
        name="rmsnorm",
    )(x, g.reshape(1, d))


def _pack_bf16_pairs(x):
    c = x.shape[1] // 2
    lo = lax.bitcast_convert_type(x[:, :c].astype(BF16).astype(F32), I32)
    hi = lax.bitcast_convert_type(x[:, c:].astype(BF16).astype(F32), I32)
    return (hi & jnp.int32(-65536)) | lax.shift_right_logical(lo, 16)


def _unpack_bf16_pairs(p):
    lo = lax.bitcast_convert_type(lax.shift_left(p, 16), F32).astype(BF16)
    hi = lax.bitcast_convert_type(p & jnp.int32(-65536), F32).astype(BF16)
    return lo, hi


def _rmsnorm_router_kernel(x_ref, g_ref, wr_ref, tri_ref, o_ref, meta_ref, prob_ref, cnt_ref, carry_ref):
    @pl.when(pl.program_id(0) == 0)
    def _():
        carry_ref[...] = jnp.zeros_like(carry_ref)

    x = x_ref[...]
    ms = jnp.mean(x * x, axis=-1, keepdims=True)
    hn = x * lax.rsqrt(ms + EPS) * g_ref[...]
    o_ref[...] = _pack_bf16_pairs(hn)
    logits = jnp.dot(hn, wr_ref[...], precision=lax.Precision.HIGHEST,
                     preferred_element_type=F32)
    lane = lax.broadcasted_iota(I32, logits.shape, 1)
    logits = jnp.where(lane < N_EXPERTS, logits, -jnp.inf)
    v1 = jnp.max(logits, axis=-1, keepdims=True)
    i1 = jnp.min(jnp.where(logits == v1, lane, LANES), axis=-1, keepdims=True)
    rest = jnp.where(lane == i1, -jnp.inf, logits)
    v2 = jnp.max(rest, axis=-1, keepdims=True)
    i2 = jnp.min(jnp.where(rest == v2, lane, LANES), axis=-1, keepdims=True)
    e = jnp.exp(v2 - v1)
    p1 = 1.0 / (1.0 + e)
    prob_ref[...] = jnp.where(lane == 0, p1, 0.0) + jnp.where(lane == 1, e * p1, 0.0)
    hot = jnp.where((lane == i1) | (lane == i2), 1.0, 0.0)
    rank = _dot(tri_ref[...], hot.astype(BF16)) + carry_ref[0:1, :]
    carry_ref[...] = carry_ref[...] + jnp.sum(hot, axis=0, keepdims=True)
    cnt_ref[...] = carry_ref[...]
    r1 = jnp.sum(jnp.where(lane == i1, rank, 0.0), axis=-1, keepdims=True)
    r2 = jnp.sum(jnp.where(lane == i2, rank, 0.0), axis=-1, keepdims=True)
    meta = (jnp.where(lane == 0, i1, 0) + jnp.where(lane == 1, i2, 0)
            + jnp.where(lane == 2, r1.astype(I32), 0) + jnp.where(lane == 3, r2.astype(I32), 0))
    meta_ref[...] = meta


def rmsnorm_router(x, g, w_router, tm=512):
    m, d = x.shape
    wr = jnp.zeros((d, LANES), F32).at[:, :N_EXPERTS].set(w_router)
    tri = jnp.asarray(np.tril(np.ones((tm, tm), np.float32), -1), BF16)
    return pl.pallas_call(
        _rmsnorm_router_kernel,
        grid=(m // tm,),
        in_specs=[pl.BlockSpec((tm, d), lambda i: (i, 0)),
                  pl.BlockSpec((1, d), lambda i: (0, 0)),
                  pl.BlockSpec((d, LANES), lambda i: (0, 0)),
                  pl.BlockSpec((tm, tm), lambda i: (0, 0))],
        out_specs=[pl.BlockSpec((tm, d // 2), lambda i: (i, 0)),
                   pl.BlockSpec((tm, LANES), lambda i: (i, 0)),
                   pl.BlockSpec((tm, LANES), lambda i: (i, 0)),
                   pl.BlockSpec((8, LANES), lambda i: (0, 0))],
        out_shape=[jax.ShapeDtypeStruct((m, d // 2), I32),
                   jax.ShapeDtypeStruct((m, LANES), I32),
                   jax.ShapeDtypeStruct((m, LANES), F32),
                   jax.ShapeDtypeStruct((8, LANES), F32)],
        scratch_shapes=[pltpu.VMEM((8, LANES), F32)],
        compiler_params=_cparams(("arbitrary",), 40),
        name="rmsnorm_router",
    )(x, g.reshape(1, d), wr, tri)


def _matmul_kernel(x_ref, w_ref, o_ref):
    o_ref[...] = _dot(x_ref[...], w_ref[...]).astype(o_ref.dtype)


def matmul(x, w, out_dtype, tm=1024, tn=1024):
    m, k = x.shape
    n = w.shape[1]
    tm, tn = min(tm, m), min(tn, n)
    return pl.pallas_call(
        _matmul_kernel,
        grid=(m // tm, n // tn),
        in_specs=[pl.BlockSpec((tm, k), lambda i, j: (i, 0)),
                  pl.BlockSpec((k, tn), lambda i, j: (0, j))],
        out_specs=pl.BlockSpec((tm, tn), lambda i, j: (i, j)),
        out_shape=jax.ShapeDtypeStruct((m, n), out_dtype),
        compiler_params=_cparams(("parallel", "parallel"), 48),
        name="matmul",
    )(x, w)


def _attn_prep_kernel(q_ref, kvc_ref, pi_ref, qg_ref, kg_ref, cg_ref, wiq_ref, lng_ref, lnb_ref,
                      qn_ref, kn_ref, v_ref, qi_ref, ki_ref, w_ref):
    def head_rms(x, g):
        x = x.astype(F32)
        return x * lax.rsqrt(jnp.mean(x * x, axis=-1, keepdims=True) + EPS) * g

    att_scale = HEAD_DIM ** -0.5 * LOG2E
    for h in range(ATT_HEADS):
        sl = slice(h * HEAD_DIM, (h + 1) * HEAD_DIM)
        qn_ref[:, sl] = (head_rms(q_ref[:, sl], qg_ref[...]) * att_scale).astype(BF16)
    for g in range(ATT_KV_HEADS):
        sl = slice(g * HEAD_DIM, (g + 1) * HEAD_DIM)
        kn_ref[:, sl] = head_rms(kvc_ref[:, sl], kg_ref[...]).astype(BF16)
    v_ref[...] = kvc_ref[:, KV_W:2 * KV_W].astype(v_ref.dtype)
    cq = kvc_ref[:, 2 * KV_W:2 * KV_W + IDX_Q_RANK]
    cqn = head_rms(cq, cg_ref[...]).astype(BF16)
    qi_ref[...] = _dot(cqn, wiq_ref[...]).astype(BF16)
    pi = pi_ref[...]
    is_key = lax.broadcasted_iota(I32, pi.shape, 1) < IDX_HEAD_DIM
    mu = jnp.sum(jnp.where(is_key, pi, 0.0), axis=-1, keepdims=True) * (1.0 / IDX_HEAD_DIM)
    cen = jnp.where(is_key, pi - mu, 0.0)
    var = jnp.sum(cen * cen, axis=-1, keepdims=True) * (1.0 / IDX_HEAD_DIM)
    kidn = cen * lax.rsqrt(var + EPS) * lng_ref[...] + lnb_ref[...]
    ki_ref[:, 0:LANES] = kidn.astype(BF16)
    ki_ref[:, LANES:2 * LANES] = pltpu.roll(kidn, IDX_HEAD_DIM, 1).astype(BF16)
    w_ref[...] = pi * (IDX_HEAD_DIM ** -0.5 * IDX_HEADS ** -0.5)


def attn_prep(proj_main, proj_idx, q_norm_g, k_norm_g, idx_q_norm_g, w_idx_q, ln_g, ln_b, tm=512):
    m = proj_main.shape[0]
    row = lambda i: (i, 0)
    const = lambda i: (0, 0)
    lane_pad = lambda a: jnp.zeros((1, LANES), F32).at[0, :a.shape[0]].set(a)
    return pl.pallas_call(
        _attn_prep_kernel,
        grid=(m // tm,),
        in_specs=[pl.BlockSpec((tm, 1024), lambda i: (i, COL_Q)),
                  pl.BlockSpec((tm, 1024), lambda i: (i, COL_KVC)),
                  pl.BlockSpec((tm, LANES), row),
                  pl.BlockSpec((1, HEAD_DIM), const),
                  pl.BlockSpec((1, HEAD_DIM), const),
                  pl.BlockSpec((1, IDX_Q_RANK), const),
                  pl.BlockSpec((IDX_Q_RANK, IDX_HEADS * IDX_HEAD_DIM), const),
                  pl.BlockSpec((1, LANES), const),
                  pl.BlockSpec((1, LANES), const)],
        out_specs=[pl.BlockSpec((tm, ATT_W), row),
                   pl.BlockSpec((tm, KV_W), row),
                   pl.BlockSpec((tm, KV_W), row),
                   pl.BlockSpec((tm, IDX_HEADS * IDX_HEAD_DIM), row),
                   pl.BlockSpec((tm, 2 * LANES), row),
                   pl.BlockSpec((tm, LANES), row)],
        out_shape=[jax.ShapeDtypeStruct((m, ATT_W), BF16),
                   jax.ShapeDtypeStruct((m, KV_W), BF16),
                   jax.ShapeDtypeStruct((m, KV_W), BF16),
                   jax.ShapeDtypeStruct((m, IDX_HEADS * IDX_HEAD_DIM), BF16),
                   jax.ShapeDtypeStruct((m, 2 * LANES), BF16),
                   jax.ShapeDtypeStruct((m, LANES), F32)],
        compiler_params=_cparams(("parallel",), 40),
        name="attn_prep",
    )(proj_main, proj_main, proj_idx, q_norm_g.reshape(1, -1), k_norm_g.reshape(1, -1),
      idx_q_norm_g.reshape(1, -1), w_idx_q, lane_pad(ln_g), lane_pad(ln_b))


def _rel_bucket(dist):
    max_exact = REL_BUCKETS // 2
    n = jnp.maximum(dist, 0)
    nf = jnp.maximum(n, 1).astype(F32)
    large = max_exact + (jnp.log(nf / max_exact) / math.log(REL_MAX_DIST / max_exact)
                         * (REL_BUCKETS - max_exact)).astype(I32)
    large = jnp.minimum(large, REL_BUCKETS - 1)
    return jnp.where(n < max_exact, n, large)


def _bias_kernel(rb_ref, bkt_ref, o_ref):
    h = pl.program_id(0)
    for kind in range(3):
        b = bkt_ref[kind]
        acc = jnp.zeros(b.shape, F32)
        for n in range(REL_BUCKETS):
            acc = jnp.where(b == n, rb_ref[n, h], acc)
        o_ref[0, kind] = acc * LOG2E


def bias_tables(rel_bias):
    assert TK >= REL_MAX_DIST
    kpos = jnp.arange(TK, dtype=I32)[:, None]
    qpos = jnp.arange(TQ, dtype=I32)[None, :]
    dist = jnp.stack([qpos - kpos, qpos - kpos + TK, qpos - kpos + 2 * TK])
    bkt = _rel_bucket(dist)
    return pl.pallas_call(
        _bias_kernel,
        grid=(ATT_HEADS,),
        in_specs=[pl.BlockSpec(memory_space=pltpu.SMEM),
                  pl.BlockSpec((3, TK, TQ), lambda h: (0, 0, 0))],
        out_specs=pl.BlockSpec((1, 3, TK, TQ), lambda h: (h, 0, 0, 0)),
        out_shape=jax.ShapeDtypeStruct((ATT_HEADS, 3, TK, TQ), F32),
        compiler_params=_cparams(("arbitrary",), 32),
        name="bias_tables",
    )(rel_bias, bkt)


def _side_cast_specs(src, split_axis, n_split, n_steps, step_of):
    ne = src.shape[0]
    n_blocks = ne * n_split
    every = n_steps // n_blocks
    assert every >= 1, "not enough grid steps to walk the weight stack"
    block = [1, src.shape[1], src.shape[2]]
    block[split_axis] //= n_split

    def index_map(*grid_idx):
        t = jnp.minimum(step_of(*grid_idx) // every, n_blocks - 1)
        idx = [t // n_split, 0, 0]
        idx[split_axis] = t % n_split
        return tuple(idx)

    spec = pl.BlockSpec(tuple(block), index_map)
    return spec, spec, jax.ShapeDtypeStruct(src.shape, BF16)


def _dsa_kernel(qi_ref, wt_ref, qn_ref, ki_ref, kn_ref, vt_ref, bias_ref, *rest, n_sel, side_cast):
    if side_cast:
        wf_ref, o_ref, wb_ref, key_ref, m_ref, l_ref, a_ref, acc_ref, s_ref, p_ref = rest
        wb_ref[...] = wf_ref[...].astype(BF16)
    else:
        o_ref, key_ref, m_ref, l_ref, a_ref, acc_ref, s_ref, p_ref = rest
    i = pl.program_id(1)
    nch = i + 1
    qpos = i * TQ + lax.broadcasted_iota(I32, (TK, TQ), 1)

    def score_chunk(j, carry):
        k0 = pl.multiple_of(j * TK, TK)
        acc = jnp.zeros((TK, TQ), F32)
        for h in range(IDX_HEADS):
            par = h % 2
            kc = ki_ref[pl.ds(k0, TK), par * LANES:(par + 1) * LANES]
            qh = qi_ref[:, (h // 2) * LANES:(h // 2 + 1) * LANES]
            s = _dot_nt(kc, qh)
            acc = acc + jnp.maximum(s, 0.0) * wt_ref[0, h:h + 1, :]
        bits = lax.bitcast_convert_type(acc, I32)
        key = jnp.where(bits < 0, bits ^ jnp.int32(0x7FFFFFFF), bits)
        kpos = k0 + lax.broadcasted_iota(I32, (TK, TQ), 0)
        key_ref[pl.ds(k0, TK), :] = jnp.where(kpos <= qpos, key, INT_MIN)
        return carry

    lax.fori_loop(0, nch, score_chunk, 0)

    def count_ge(cand):
        def body(j, acc):
            k0 = pl.multiple_of(j * TK, TK)
            hit = jnp.where(key_ref[pl.ds(k0, TK), :] >= cand, 1, 0).astype(I32)
            return acc + jnp.sum(hit.reshape(TK // 8, 8, TQ), axis=0)
        acc = lax.fori_loop(0, nch, body, jnp.zeros((8, TQ), I32))
        return jnp.sum(acc, axis=0, keepdims=True)

    thr = jnp.where(count_ge(jnp.zeros((1, TQ), I32)) >= n_sel, 0, INT_MIN).astype(I32)

    def bit_body(b, thr):
        cand = thr | jnp.left_shift(jnp.int32(1), 30 - b)
        return jnp.where(count_ge(cand) >= n_sel, cand, thr)

    thr = lax.fori_loop(0, 31, bit_body, thr)
    thr = jnp.maximum(thr, INT_MIN + 1)

    m_ref[...] = jnp.full(m_ref.shape, NEG_BIG, F32)
    l_ref[...] = jnp.zeros(l_ref.shape, F32)
    acc_ref[...] = jnp.zeros(acc_ref.shape, F32)

    def att_chunk(j, carry):
        k0 = pl.multiple_of(j * TK, TK)
        neg = jnp.where(key_ref[pl.ds(k0, TK), :] >= thr, 0.0, NEG_BIG)
        kind = jnp.minimum(i - j, 2)
        for h in range(ATT_HEADS):
            g = h // ATT_GROUP
            kc = kn_ref[pl.ds(k0, TK), g * HEAD_DIM:(g + 1) * HEAD_DIM]
            qh = qn_ref[:, h * HEAD_DIM:(h + 1) * HEAD_DIM]
            lg = _dot_nt(kc, qh) + bias_ref[h, kind] + neg
            s_ref[h] = lg
            m_old = m_ref[h:h + 1, :]
            m_new = jnp.maximum(m_old, jnp.max(lg, axis=0, keepdims=True))
            a_ref[h:h + 1, :] = jnp.exp2(m_old - m_new)
            m_ref[h:h + 1, :] = m_new
        for h in range(ATT_HEADS):
            p = jnp.exp2(s_ref[h] - m_ref[h:h + 1, :])
            l_ref[h:h + 1, :] = (a_ref[h:h + 1, :] * l_ref[h:h + 1, :]
                                 + jnp.sum(p, axis=0, keepdims=True))
            p_ref[h] = p.astype(BF16)
        for h in range(ATT_HEADS):
            g = h // ATT_GROUP
            vc = vt_ref[0, j, g * HEAD_DIM:(g + 1) * HEAD_DIM, :]
            acc_ref[h] = a_ref[h:h + 1, :] * acc_ref[h] + _dot(vc, p_ref[h])
        return carry

    lax.fori_loop(0, nch, att_chunk, 0)
    for h in range(ATT_HEADS):
        o = acc_ref[h] * (1.0 / l_ref[h:h + 1, :])
        o_ref[:, h * HEAD_DIM:(h + 1) * HEAD_DIM] = o.T.astype(o_ref.dtype)


def dsa_attention(qi, wt, qn, ki, kn, vt, bias, batch, seq, side_cast=None):
    nq = seq // TQ
    n_sel = min(TOPK_MAX, seq // 4)
    qrow = lambda b, i: (b * nq + i, 0)
    brow = lambda b, i: (b, 0)
    in_specs = [pl.BlockSpec((TQ, IDX_HEADS * IDX_HEAD_DIM), qrow),
                pl.BlockSpec((1, IDX_HEADS, TQ), lambda b, i: (b, 0, i)),
                pl.BlockSpec((TQ, ATT_W), qrow),
                pl.BlockSpec((seq, 2 * LANES), brow),
                pl.BlockSpec((seq, KV_W), brow),
                pl.BlockSpec((1, seq // TK, KV_W, TK), lambda b, i: (b, 0, 0, 0)),
                pl.BlockSpec((ATT_HEADS, 3, TK, TQ), lambda b, i: (0, 0, 0, 0))]
    out_specs = [pl.BlockSpec((TQ, ATT_W), qrow)]
    out_shape = [jax.ShapeDtypeStruct((batch * seq, ATT_W), BF16)]
    operands = [qi, wt, qn, ki, kn, vt, bias]
    if side_cast is not None:
        src, split_axis, n_split = side_cast
        spec_in, spec_out, shape_out = _side_cast_specs(src, split_axis, n_split, batch * nq,
                                                        lambda b, i: b * nq + i)
        in_specs.append(spec_in)
        out_specs.append(spec_out)
        out_shape.append(shape_out)
        operands.append(src)
    return pl.pallas_call(
        functools.partial(_dsa_kernel, n_sel=n_sel, side_cast=side_cast is not None),
        grid=(batch, nq),
        in_specs=in_specs,
        out_specs=out_specs,
        out_shape=out_shape,
        scratch_shapes=[pltpu.VMEM((seq, TQ), I32),
                        pltpu.VMEM((ATT_HEADS, TQ), F32),
                        pltpu.VMEM((ATT_HEADS, TQ), F32),
                        pltpu.VMEM((ATT_HEADS, TQ), F32),
                        pltpu.VMEM((ATT_HEADS, HEAD_DIM, TQ), F32),
                        pltpu.VMEM((ATT_HEADS, TK, TQ), F32),
                        pltpu.VMEM((ATT_HEADS, TK, TQ), BF16)],
        compiler_params=_cparams(("arbitrary", "arbitrary"), 56),
        name="dsa_attention",
    )(*operands)


def _hgrn_constants():
    t = np.arange(HC)
    rows = []
    masks = [np.eye(HC, dtype=np.float32)]
    for m in HGRN_LEVELS:
        upper = (t // m) % 2 == 1
        start = (t // m) * m
        end = start + m - 1
        u = t[None, :]
        q_side = upper[:, None] & (u >= start[:, None]) & (u <= t[:, None])
        k_side = (~upper)[:, None] & (u > t[:, None]) & (u <= end[:, None])
        rows.append((q_side | k_side).astype(np.float32))
        same = (t[:, None] // (2 * m)) == (t[None, :] // (2 * m))
        masks.append((upper[:, None] & (~upper)[None, :] & same).astype(np.float32))
    u = t[None, :]
    rows.append((u <= t[:, None]).astype(np.float32))
    rows.append((u > t[:, None]).astype(np.float32))
    mat = np.concatenate(rows, axis=0)
    return np.concatenate([mat, mat], axis=1), np.stack(masks)


def _hgrn_kernel(hq_ref, hf_ref, hi_ref, hg_ref, lb_ref, gn_ref, mat_ref, mask_ref, o_ref,
                 st_ref, arg_ref, kk_ref, lf_ref, zq_ref, zk_ref):
    c = pl.program_id(1)

    @pl.when(c == 0)
    def _():
        st_ref[...] = jnp.zeros_like(st_ref)

    lb = lb_ref[...]
    f = hf_ref[...].astype(F32)
    e = jnp.exp(-jnp.abs(f))
    r = 1.0 / (1.0 + e)
    log_sig = jnp.minimum(f, 0.0) - jnp.log(1.0 + e)
    la = jnp.log(lb)
    lc = jnp.log(1.0 - lb) + log_sig
    logf = (jnp.maximum(la, lc) + jnp.log(1.0 + jnp.exp(-jnp.abs(la - lc)))) * LOG2E
    kk_ref[...] = (1.0 - lb) * jnp.where(f >= 0, e * r, r)
    hi = logf.astype(BF16)
    lf_ref[0:HC, :] = hi
    lf_ref[HC:2 * HC, :] = (logf - hi.astype(F32)).astype(BF16)
    arg_ref[...] = _dot(mat_ref[...], lf_ref[...])

    scale = HGRN_DIM ** -0.5
    nl = len(HGRN_LEVELS)
    for h in range(HGRN_HEADS):
        sl = slice(h * HGRN_DIM, (h + 1) * HGRN_DIM)
        hq = hq_ref[:, sl].astype(F32)
        q = hq * _sigmoid(hq) * scale
        k = kk_ref[:, sl]
        zq_ref[h, 0] = q.astype(BF16)
        zk_ref[h, 0] = k.astype(BF16)
        for lv in range(nl):
            ex = jnp.exp2(arg_ref[lv * HC:(lv + 1) * HC, sl])
            zq_ref[h, lv + 1] = (q * ex).astype(BF16)
            zk_ref[h, lv + 1] = (k * ex).astype(BF16)
        zq_ref[h, nl + 1] = (q * jnp.exp2(arg_ref[nl * HC:(nl + 1) * HC, sl])).astype(BF16)
        zk_ref[h, nl + 1] = (k * jnp.exp2(arg_ref[(nl + 1) * HC:(nl + 2) * HC, sl])).astype(BF16)

    for h in range(HGRN_HEADS):
        sl = slice(h * HGRN_DIM, (h + 1) * HGRN_DIM)
        a = _dot_nt(zq_ref[h, 0], zk_ref[h, 0]) * mask_ref[0]
        for lv in range(nl):
            a = a + _dot_nt(zq_ref[h, lv + 1], zk_ref[h, lv + 1]) * mask_ref[lv + 1]
        v = hi_ref[:, sl].astype(BF16)
        st = st_ref[h]
        o = _dot(a.astype(BF16), v) + _dot_nt(zq_ref[h, nl + 1], st.astype(BF16))
        decay = jnp.exp2(arg_ref[(nl + 1) * HC - 1:(nl + 1) * HC, sl])
        st_ref[h] = decay * st + _dot(v.astype(F32).T.astype(BF16), zk_ref[h, nl + 1])
        on = o * lax.rsqrt(jnp.mean(o * o, axis=-1, keepdims=True) + EPS) * gn_ref[...]
        hg = hg_ref[:, sl].astype(F32)
        o_ref[:, sl] = (on * hg * _sigmoid(hg)).astype(o_ref.dtype)


def hgrn2(proj_main, lb, g_norm, batch, seq):
    nc = seq // HC
    mat, masks = _hgrn_constants()
    col = lambda cb: (lambda b, c: (b * nc + c, cb))
    return pl.pallas_call(
        _hgrn_kernel,
        grid=(batch, nc),
        in_specs=[pl.BlockSpec((HC, HGRN_W), col(COL_HQ)),
                  pl.BlockSpec((HC, HGRN_W), col(COL_HF)),
                  pl.BlockSpec((HC, HGRN_W), col(COL_HI)),
                  pl.BlockSpec((HC, HGRN_W), col(COL_HG)),
                  pl.BlockSpec((1, HGRN_W), lambda b, c: (0, 0)),
                  pl.BlockSpec((1, HGRN_DIM), lambda b, c: (0, 0)),
                  pl.BlockSpec((N_ARG_GROUPS * HC, 2 * HC), lambda b, c: (0, 0)),
                  pl.BlockSpec((len(HGRN_LEVELS) + 1, HC, HC), lambda b, c: (0, 0, 0))],
        out_specs=pl.BlockSpec((HC, HGRN_W), lambda b, c: (b * nc + c, 0)),
        out_shape=jax.ShapeDtypeStruct((batch * seq, HGRN_W), BF16),
        scratch_shapes=[pltpu.VMEM((HGRN_HEADS, HGRN_DIM, HGRN_DIM), F32),
                        pltpu.VMEM((N_ARG_GROUPS * HC, HGRN_W), F32),
                        pltpu.VMEM((HC, HGRN_W), F32),
                        pltpu.VMEM((2 * HC, HGRN_W), BF16),
                        pltpu.VMEM((HGRN_HEADS, N_ARG_GROUPS, HC, HGRN_DIM), BF16),
                        pltpu.VMEM((HGRN_HEADS, N_ARG_GROUPS, HC, HGRN_DIM), BF16)],
        compiler_params=_cparams(("parallel", "arbitrary"), 40),
        name="hgrn2",
    )(proj_main, proj_main, proj_main, proj_main, lb.reshape(1, -1), g_norm.reshape(1, -1),
      jnp.asarray(mat, BF16), jnp.asarray(masks, F32))


def _merge_out_kernel(att_ref, rec_ref, ga_ref, gh_ref, x_ref, wa_ref, wh_ref, wo_ref, g_ref,
                      h_ref, *maybe_hn_ref):
    j = pl.program_id(1)

    @pl.when(j == 0)
    def _():
        h_ref[...] = x_ref[...]

    a = _dot(att_ref[...], wa_ref[...])
    r = _dot(rec_ref[...], wh_ref[...])
    merged = (_sigmoid(ga_ref[...].astype(F32)) * a + _sigmoid(gh_ref[...].astype(F32)) * r).astype(BF16)
    h_ref[...] += _dot(merged, wo_ref[...])

    if maybe_hn_ref:
        @pl.when(j == pl.num_programs(1) - 1)
        def _():
            h = h_ref[...]
            ms = jnp.mean(h * h, axis=-1, keepdims=True)
            maybe_hn_ref[0][...] = (h * lax.rsqrt(ms + EPS) * g_ref[...]).astype(BF16)


def merge_out(att, rec, proj_main, x, w_up_att, w_up_hgrn, w_o, norm_g, emit_hn, tm=512, tn=512):
    m, d = x.shape
    ga0, gh0 = COL_GA * 1024 // tn, COL_GH * 1024 // tn
    row = lambda i, j: (i, 0)
    out_specs = [pl.BlockSpec((tm, d), row)]
    out_shape = [jax.ShapeDtypeStruct((m, d), F32)]
    if emit_hn:
        out_specs.append(pl.BlockSpec((tm, d), row))
        out_shape.append(jax.ShapeDtypeStruct((m, d), BF16))
    return pl.pallas_call(
        _merge_out_kernel,
        grid=(m // tm, d // tn),
        in_specs=[pl.BlockSpec((tm, ATT_W), row),
                  pl.BlockSpec((tm, HGRN_W), row),
                  pl.BlockSpec((tm, tn), lambda i, j: (i, ga0 + j)),
                  pl.BlockSpec((tm, tn), lambda i, j: (i, gh0 + j)),
                  pl.BlockSpec((tm, d), row),
                  pl.BlockSpec((ATT_W, tn), lambda i, j: (0, j)),
                  pl.BlockSpec((HGRN_W, tn), lambda i, j: (0, j)),
                  pl.BlockSpec((tn, d), lambda i, j: (j, 0)),
                  pl.BlockSpec((1, d), lambda i, j: (0, 0))],
        out_specs=out_specs,
        out_shape=out_shape,
        compiler_params=_cparams(("parallel", "arbitrary"), 48),
        name="merge_out",
    )(att, rec, proj_main, proj_main, x, w_up_att, w_up_hgrn, w_o, norm_g.reshape(1, d))


def _ffn_kernel(hn_ref, h_ref, w1_ref, w3_ref, w2_ref, g_ref, *rest, emit_xn, side_cast_every):
    rest = list(rest)
    wf_ref = rest.pop(0) if side_cast_every else None
    o_ref = rest.pop(0)
    maybe_xn_ref = [rest.pop(0)] if emit_xn else []
    j = pl.program_id(1)
    if side_cast_every:
        wb_ref = rest.pop(0)
        step = pl.program_id(0) * pl.num_programs(1) + j

        @pl.when(step % side_cast_every == 0)
        def _():
            wb_ref[...] = wf_ref[...].astype(BF16)

    @pl.when(j == 0)
    def _():
        o_ref[...] = h_ref[...]

    x = hn_ref[...]
    a = _dot(x, w1_ref[...])
    b = _dot(x, w3_ref[...])
    act = (a * _sigmoid(a) * b).astype(BF16)
    o_ref[...] += _dot(act, w2_ref[...])

    if maybe_xn_ref:
        @pl.when(j == pl.num_programs(1) - 1)
        def _():
            y = o_ref[...]
            ms = jnp.mean(y * y, axis=-1, keepdims=True)
            maybe_xn_ref[0][...] = (y * lax.rsqrt(ms + EPS) * g_ref[...]).astype(BF16)


def ffn_dense(hn, h, w1, w3, w2, next_norm_g, emit_xn, side_cast=None, tm=512, tf=512):
    m, d = hn.shape
    f = w1.shape[1]
    nf = f // tf
    row = lambda i, j: (i, 0)
    in_specs = [pl.BlockSpec((tm, d), row),
                pl.BlockSpec((tm, d), row),
                pl.BlockSpec((d, tf), lambda i, j: (0, j)),
                pl.BlockSpec((d, tf), lambda i, j: (0, j)),
                pl.BlockSpec((tf, d), lambda i, j: (j, 0)),
                pl.BlockSpec((1, d), lambda i, j: (0, 0))]
    operands = [hn, h, w1, w3, w2, next_norm_g.reshape(1, d)]
    out_specs = [pl.BlockSpec((tm, d), row)]
    out_shape = [jax.ShapeDtypeStruct((m, d), F32)]
    if emit_xn:
        out_specs.append(pl.BlockSpec((tm, d), row))
        out_shape.append(jax.ShapeDtypeStruct((m, d), BF16))
    every = 0
    if side_cast is not None:
        src, split_axis, n_split = side_cast
        n_steps = (m // tm) * nf
        every = n_steps // (src.shape[0] * n_split)
        spec_in, spec_out, shape_out = _side_cast_specs(src, split_axis, n_split, n_steps,
                                                        lambda i, j: i * nf + j)
        in_specs.append(spec_in)
        operands.append(src)
        out_specs.append(spec_out)
        out_shape.append(shape_out)
    return pl.pallas_call(
        functools.partial(_ffn_kernel, emit_xn=emit_xn, side_cast_every=every),
        grid=(m // tm, nf),
        in_specs=in_specs,
        out_specs=out_specs,
        out_shape=out_shape,
        compiler_params=_cparams(("arbitrary", "arbitrary"), 56),
        name="ffn_dense",
    )(*operands)


def _row_copy(src_ref, src_row, dst_ref, dst_row, sem):
    return pltpu.make_async_copy(src_ref.at[pl.ds(src_row, 1)], dst_ref.at[pl.ds(dst_row, 1)], sem)


def _moe_scatter_kernel(dest_ref, hn_ref, xs_init_ref, xs_ref, sem, *, tm, m):
    del xs_init_ref
    base = pl.program_id(0) * tm

    def issue(r, carry):
        for k in range(2):
            _row_copy(hn_ref, r, xs_ref, dest_ref[k * m + base + r], sem).start()
        return carry

    lax.fori_loop(0, tm, issue, 0, unroll=8)
    for k in range(2):
        pltpu.make_async_copy(hn_ref, xs_ref.at[pl.ds(0, tm)], sem).wait()


def moe_scatter(hn_packed, dest, n_rows, tm=512):
    m, c = hn_packed.shape
    xs_init = jnp.zeros((n_rows, c), I32)
    return pl.pallas_call(
        functools.partial(_moe_scatter_kernel, tm=tm, m=m),
        grid_spec=pltpu.PrefetchScalarGridSpec(
            num_scalar_prefetch=1,
            grid=(m // tm,),
            in_specs=[pl.BlockSpec((tm, c), lambda i, dest: (i, 0)),
                      pl.BlockSpec(memory_space=pl.ANY)],
            out_specs=pl.BlockSpec(memory_space=pl.ANY),
            scratch_shapes=[pltpu.SemaphoreType.DMA(())]),
        out_shape=jax.ShapeDtypeStruct((n_rows, c), I32),
        input_output_aliases={2: 0},
        compiler_params=_cparams(("arbitrary",), 32),
        name="moe_scatter",
    )(dest, hn_packed, xs_init)


def _moe_ffn_kernel(te_ref, tv_ref, xs_ref, w1_ref, w3_ref, w2_ref, o_ref, xb_ref):
    del te_ref
    i = pl.program_id(0)
    half = xs_ref.shape[1]

    @pl.when(pl.program_id(1) == 0)
    def _():
        o_ref[...] = jnp.zeros_like(o_ref)
        lo, hi = _unpack_bf16_pairs(xs_ref[...])
        xb_ref[:, 0:half] = lo
        xb_ref[:, half:2 * half] = hi

    @pl.when(tv_ref[i] == 1)
    def _():
        x = xb_ref[...]
        a = _dot(x, w1_ref[0])
        b = _dot(x, w3_ref[0])
        act = (a * _sigmoid(a) * b).astype(BF16)
        o_ref[...] += _dot(act, w2_ref[0])


def moe_ffn(xs, tile_expert, tile_valid, w1, w3, w2, tm=512, tf=1024):
    p, c = xs.shape
    ne, d, f = w1.shape
    nf = f // tf
    col = lambda i, j, te, tv: jnp.where(tv[i] == 1, j, nf - 1)
    return pl.pallas_call(
        _moe_ffn_kernel,
        grid_spec=pltpu.PrefetchScalarGridSpec(
            num_scalar_prefetch=2,
            grid=(p // tm, nf),
            in_specs=[pl.BlockSpec((tm, c), lambda i, j, te, tv: (i, 0)),
                      pl.BlockSpec((1, d, tf), lambda i, j, te, tv: (te[i], 0, col(i, j, te, tv))),
                      pl.BlockSpec((1, d, tf), lambda i, j, te, tv: (te[i], 0, col(i, j, te, tv))),
                      pl.BlockSpec((1, tf, d), lambda i, j, te, tv: (te[i], col(i, j, te, tv), 0))],
            out_specs=pl.BlockSpec((tm, d), lambda i, j, te, tv: (i, 0)),
            scratch_shapes=[pltpu.VMEM((tm, d), BF16)]),
        out_shape=jax.ShapeDtypeStruct((p, d), F32),
        compiler_params=_cparams(("arbitrary", "arbitrary"), 56),
        name="moe_ffn",
    )(tile_expert, tile_valid, xs, w1, w3, w2)


def _moe_combine_kernel(dest_ref, h_ref, prob_ref, ys_ref, o_ref, ybuf_ref, sem, *, tm, m):
    i = pl.program_id(0)
    slot = i % 2

    def gather_tile(tile, slot_):
        base = tile * tm

        def issue(r, carry):
            for k in range(2):
                _row_copy(ys_ref, dest_ref[k * m + base + r], ybuf_ref.at[slot_, k], r,
                          sem.at[slot_]).start()
            return carry

        lax.fori_loop(0, tm, issue, 0, unroll=8)

    @pl.when(i == 0)
    def _():
        gather_tile(0, 0)

    @pl.when(i + 1 < pl.num_programs(0))
    def _():
        gather_tile(i + 1, 1 - slot)

    for k in range(2):
        pltpu.make_async_copy(ys_ref.at[pl.ds(0, tm)], ybuf_ref.at[slot, k], sem.at[slot]).wait()
    prob = prob_ref[...]
    o_ref[...] = h_ref[...] + prob[:, 0:1] * ybuf_ref[slot, 0] + prob[:, 1:2] * ybuf_ref[slot, 1]


def moe_combine(h, prob, ys, dest, tm=512):
    m, d = h.shape
    return pl.pallas_call(
        functools.partial(_moe_combine_kernel, tm=tm, m=m),
        grid_spec=pltpu.PrefetchScalarGridSpec(
            num_scalar_prefetch=1,
            grid=(m // tm,),
            in_specs=[pl.BlockSpec((tm, d), lambda i, dest: (i, 0)),
                      pl.BlockSpec((tm, LANES), lambda i, dest: (i, 0)),
                      pl.BlockSpec(memory_space=pl.ANY)],
            out_specs=pl.BlockSpec((tm, d), lambda i, dest: (i, 0)),
            scratch_shapes=[pltpu.VMEM((2, 2, tm, d), F32), pltpu.SemaphoreType.DMA((2,))]),
        out_shape=jax.ShapeDtypeStruct((m, d), F32),
        compiler_params=_cparams(("arbitrary",), 48),
        name="moe_combine",
    )(dest, h, prob, ys)


def ffn_moe(h, norm_g, w_router, w1, w3, w2, tm=512):
    m, d = h.shape
    ne = w1.shape[0]
    hn_packed, meta, prob, cnt = rmsnorm_router(h, norm_g, w_router)
    counts = cnt[0, :ne].astype(I32)
    tiles_per = (counts + tm - 1) // tm
    tile_end = jnp.cumsum(tiles_per)
    offset = (tile_end - tiles_per) * tm
    dest = jnp.concatenate([offset[meta[:, 0]] + meta[:, 2], offset[meta[:, 1]] + meta[:, 3]])
    n_tiles = (2 * m) // tm + ne
    tile_ids = jnp.arange(n_tiles, dtype=I32)
    tile_valid = (tile_ids < tile_end[-1]).astype(I32)
    tile_expert = jnp.searchsorted(tile_end, jnp.minimum(tile_ids, tile_end[-1] - 1), side="right").astype(I32)
    xs = moe_scatter(hn_packed, dest, n_tiles * tm)
    ys = moe_ffn(xs, tile_expert, tile_valid, w1, w3, w2, tm=tm)
    return moe_combine(h, prob, ys, dest)


def _w_in_prep_kernel(w_ref, main_ref, idx_ref):
    head = ATT_W + 2 * KV_W + IDX_Q_RANK
    small = IDX_HEAD_DIM + IDX_HEADS
    main_ref[:, 0:head] = w_ref[:, 0:head].astype(BF16)
    group = w_ref[:, head:head + LANES]
    lane = lax.broadcasted_iota(I32, group.shape, 1)
    idx_ref[...] = jnp.where(lane < small, group, 0.0).astype(BF16)
    step = 1024
    for c in range((MAIN_W - head) // step):
        src = head + small + c * step
        main_ref[:, head + c * step:head + (c + 1) * step] = w_ref[:, src:src + step].astype(BF16)


def _split_w_in(w_in, tr=256):
    d, n = w_in.shape
    assert n == MAIN_W + IDX_HEAD_DIM + IDX_HEADS
    return pl.pallas_call(
        _w_in_prep_kernel,
        grid=(d // tr,),
        in_specs=[pl.BlockSpec((tr, n), lambda i: (i, 0))],
        out_specs=[pl.BlockSpec((tr, MAIN_W), lambda i: (i, 0)),
                   pl.BlockSpec((tr, LANES), lambda i: (i, 0))],
        out_shape=[jax.ShapeDtypeStruct((d, MAIN_W), BF16),
                   jax.ShapeDtypeStruct((d, LANES), BF16)],
        compiler_params=_cparams(("parallel",), 48),
        name="w_in_prep",
    )(w_in)


def _mixer(x2, maybe_xn, batch, seq, norm_g, w_in, q_norm_g, k_norm_g, idx_q_norm_g, w_idx_q, ln_g,
           ln_b, lb, hgrn_norm_g, w_up_att, w_up_hgrn, w_o, bias, norm_ffn_g, emit_hn,
           side_cast=None):
    xn = rmsnorm_rows(x2, norm_g) if maybe_xn is None else maybe_xn
    w_main, w_idx = _split_w_in(w_in)
    proj_main = matmul(xn, w_main, BF16)
    proj_idx = matmul(xn, w_idx, F32)
    qn, kn, v, qi, ki, w = attn_prep(proj_main, proj_idx, q_norm_g, k_norm_g, idx_q_norm_g,
                                     w_idx_q.astype(BF16), ln_g, ln_b)
    wt = w[:, IDX_HEAD_DIM:IDX_HEAD_DIM + IDX_HEADS].reshape(batch, seq, IDX_HEADS).transpose(0, 2, 1)
    vt = v.reshape(batch, seq // TK, TK, KV_W).transpose(0, 1, 3, 2)
    att, *maybe_cast = dsa_attention(qi, wt, qn, ki, kn, vt, bias, batch, seq, side_cast)
    rec = hgrn2(proj_main, lb, hgrn_norm_g, batch, seq)
    outs = merge_out(att, rec, proj_main, x2, w_up_att.astype(BF16), w_up_hgrn.astype(BF16),
                     w_o.astype(BF16), norm_ffn_g, emit_hn)
    return outs, (maybe_cast[0] if maybe_cast else None)


def kernel(x, rel_bias, norm_mix_g, norm_ffn_g, w_in, q_norm_g, k_norm_g, idx_q_norm_g, w_idx_q,
           idx_k_ln_g, idx_k_ln_b, hgrn_lb_logits, hgrn_out_norm_g, w_up_att, w_up_hgrn, w_o,
           w1_dense, w3_dense, w2_dense, w_router, w1_moe, w3_moe, w2_moe):
    batch, seq, d = x.shape
    depth = w_in.shape[0]
    lb_all = jnp.cumsum(jax.nn.softmax(hgrn_lb_logits.astype(F32), axis=0), axis=0)
    lb_all = lb_all - lb_all[0:1]
    bias = bias_tables(rel_bias)
    x2 = x.reshape(batch * seq, d)
    maybe_xn = None
    n_split_ff = w1_moe.shape[-1] // 1024
    moe_bf16 = {}
    for l in range(depth):
        dense = l % 2 == 0
        j = l // 2
        if dense and l + 1 < depth:
            side = (w3_moe[j], 2, n_split_ff)
        elif not dense:
            side = (w2_moe[j], 1, n_split_ff)
        else:
            side = None
        (h, *maybe_hn), cast = _mixer(x2, maybe_xn, batch, seq, norm_mix_g[l], w_in[l], q_norm_g[l],
                                      k_norm_g[l], idx_q_norm_g[l], w_idx_q[l], idx_k_ln_g[l],
                                      idx_k_ln_b[l], lb_all[l], hgrn_out_norm_g[l], w_up_att[l],
                                      w_up_hgrn[l], w_o[l], bias, norm_ffn_g[l], emit_hn=dense,
                                      side_cast=side)
        maybe_xn = None
        if dense:
            more = l + 1 < depth
            moe_bf16["w3"] = cast
            x2, *rest = ffn_dense(maybe_hn[0], h, w1_dense[j].astype(BF16), w3_dense[j].astype(BF16),
                                  w2_dense[j].astype(BF16), norm_mix_g[l + 1 if more else l], more,
                                  side_cast=(w1_moe[j], 2, 2 * n_split_ff) if more else None)
            if more:
                maybe_xn, moe_bf16["w1"] = rest
        else:
            x2 = ffn_moe(h, norm_ffn_g[l], w_router[j], moe_bf16["w1"], moe_bf16["w3"], cast)
    return x2.reshape(batch, seq, d)
```

```python
import functools
import math

import numpy as np
import jax
import jax.numpy as jnp
from jax import lax
from jax.experimental import pallas as pl
from jax.experimental.pallas import tpu as pltpu

F32 = jnp.float32
BF16 = jnp.bfloat16
I32 = jnp.int32

EPS = 1e-6
ATT_HEADS = 8
ATT_KV_HEADS = 2
ATT_GROUP = ATT_HEADS // ATT_KV_HEADS
HEAD_DIM = 128
ATT_W = ATT_HEADS * HEAD_DIM
KV_W = ATT_KV_HEADS * HEAD_DIM
IDX_HEADS = 16
IDX_HEAD_DIM = 64
IDX_Q_RANK = 512
TOPK_MAX = 256
HGRN_HEADS = 8
HGRN_DIM = 128
HGRN_W = HGRN_HEADS * HGRN_DIM
REL_BUCKETS = 32
REL_MAX_DIST = 128
N_EXPERTS = 8
LANES = 128
INT_MIN = -(2 ** 31)
NEG_BIG = -1e30
LOG2E = math.log2(math.e)

TQ = 256
TK = 256
HC = 128
HGRN_LEVELS = (1, 2, 4, 8, 16, 32, 64)
N_ARG_GROUPS = len(HGRN_LEVELS) + 2

COL_Q, COL_KVC, COL_HQ, COL_HF, COL_HI, COL_HG, COL_GA, COL_GH = 0, 1, 2, 3, 4, 5, 6, 8
MAIN_W = 10 * 1024


def _cparams(sem, vmem_mb):
    return pltpu.CompilerParams(dimension_semantics=sem, vmem_limit_bytes=vmem_mb << 20)


def _dot(a, b):
    return jnp.dot(a, b, preferred_element_type=F32)


def _dot_nt(a, b):
    return lax.dot_general(a, b, (((1,), (1,)), ((), ())), preferred_element_type=F32)


def _sigmoid(x):
    return 1.0 / (1.0 + jnp.exp(-x))


def _rmsnorm_kernel(x_ref, g_ref, o_ref):
    x = x_ref[...]
    ms = jnp.mean(x * x, axis=-1, keepdims=True)
    o_ref[...] = (x * lax.rsqrt(ms + EPS) * g_ref[...]).astype(o_ref.dtype)


def rmsnorm_rows(x, g, tm=512):
    m, d = x.shape
    return pl.pallas_call(
        _rmsnorm_kernel,
        grid=(m // tm,),
        in_specs=[pl.BlockSpec((tm, d), lambda i: (i, 0)),
                  pl.BlockSpec((1, d), lambda i: (0, 0))],
        out_specs=pl.BlockSpec((tm, d), lambda i: (i, 0)),
        out_shape=jax.ShapeDtypeStruct((m, d), BF16),
        compiler_params=_cparams(("parallel",), 32),
        name="rmsnorm",
    )(x, g.reshape(1, d))


def _pack_bf16_pairs(x):
    c = x.shape[1] // 2
    lo = lax.bitcast_convert_type(x[:, :c].astype(BF16).astype(F32), I32)
    hi = lax.bitcast_convert_type(x[:, c:].astype(BF16).astype(F32), I32)
    return (hi & jnp.int32(-65536)) | lax.shift_right_logical(lo, 16)


def _unpack_bf16_pairs(p):
    lo = lax.bitcast_convert_type(lax.shift_left(p, 16), F32).astype(BF16)
    hi = lax.bitcast_convert_type(p & jnp.int32(-65536), F32).astype(BF16)
    return lo, hi


def _rmsnorm_router_kernel(x_ref, g_ref, wr_ref, tri_ref, o_ref, meta_ref, prob_ref, cnt_ref, carry_ref):
    @pl.when(pl.program_id(0) == 0)
    def _():
        carry_ref[...] = jnp.zeros_like(carry_ref)

    x = x_ref[...]
    ms = jnp.mean(x * x, axis=-1, keepdims=True)
    hn = x * lax.rsqrt(ms + EPS) * g_ref[...]
    o_ref[...] = _pack_bf16_pairs(hn)
    logits = jnp.dot(hn, wr_ref[...], precision=lax.Precision.HIGHEST,
                     preferred_element_type=F32)
    lane = lax.broadcasted_iota(I32, logits.shape, 1)
    logits = jnp.where(lane < N_EXPERTS, logits, -jnp.inf)
    v1 = jnp.max(logits, axis=-1, keepdims=True)
    i1 = jnp.min(jnp.where(logits == v1, lane, LANES), axis=-1, keepdims=True)
    rest = jnp.where(lane == i1, -jnp.inf, logits)
    v2 = jnp.max(rest, axis=-1, keepdims=True)
    i2 = jnp.min(jnp.where(rest == v2, lane, LANES), axis=-1, keepdims=True)
    e = jnp.exp(v2 - v1)
    p1 = 1.0 / (1.0 + e)
    prob_ref[...] = jnp.where(lane == 0, p1, 0.0) + jnp.where(lane == 1, e * p1, 0.0)
    hot = jnp.where((lane == i1) | (lane == i2), 1.0, 0.0)
    rank = _dot(tri_ref[...], hot.astype(BF16)) + carry_ref[0:1, :]
    carry_ref[...] = carry_ref[...] + jnp.sum(hot, axis=0, keepdims=True)
    cnt_ref[...] = carry_ref[...]
    r1 = jnp.sum(jnp.where(lane == i1, rank, 0.0), axis=-1, keepdims=True)
    r2 = jnp.sum(jnp.where(lane == i2, rank, 0.0), axis=-1, keepdims=True)
    meta = (jnp.where(lane == 0, i1, 0) + jnp.where(lane == 1, i2, 0)
            + jnp.where(lane == 2, r1.astype(I32), 0) + jnp.where(lane == 3, r2.astype(I32), 0))
    meta_ref[...] = meta


def rmsnorm_router(x, g, w_router, tm=512):
    m, d = x.shape
    wr = jnp.zeros((d, LANES), F32).at[:, :N_EXPERTS].set(w_router)
    tri = jnp.asarray(np.tril(np.ones((tm, tm), np.float32), -1), BF16)
    return pl.pallas_call(
        _rmsnorm_router_kernel,
        grid=(m // tm,),
        in_specs=[pl.BlockSpec((tm, d), lambda i: (i, 0)),
                  pl.BlockSpec((1, d), lambda i: (0, 0)),
                  pl.BlockSpec((d, LANES), lambda i: (0, 0)),
                  pl.BlockSpec((tm, tm), lambda i: (0, 0))],
        out_specs=[pl.BlockSpec((tm, d // 2), lambda i: (i, 0)),
                   pl.BlockSpec((tm, LANES), lambda i: (i, 0)),
                   pl.BlockSpec((tm, LANES), lambda i: (i, 0)),
                   pl.BlockSpec((8, LANES), lambda i: (0, 0))],
        out_shape=[jax.ShapeDtypeStruct((m, d // 2), I32),
                   jax.ShapeDtypeStruct((m, LANES), I32),
                   jax.ShapeDtypeStruct((m, LANES), F32),
                   jax.ShapeDtypeStruct((8, LANES), F32)],
        scratch_shapes=[pltpu.VMEM((8, LANES), F32)],
        compiler_params=_cparams(("arbitrary",), 40),
        name="rmsnorm_router",
    )(x, g.reshape(1, d), wr, tri)


def _matmul_kernel(x_ref, w_ref, o_ref):
    o_ref[...] = _dot(x_ref[...], w_ref[...]).astype(o_ref.dtype)


def matmul(x, w, out_dtype, tm=1024, tn=1024):
    m, k = x.shape
    n = w.shape[1]
    tm, tn = min(tm, m), min(tn, n)
    return pl.pallas_call(
        _matmul_kernel,
        grid=(m // tm, n // tn),
        in_specs=[pl.BlockSpec((tm, k), lambda i, j: (i, 0)),
                  pl.BlockSpec((k, tn), lambda i, j: (0, j))],
        out_specs=pl.BlockSpec((tm, tn), lambda i, j: (i, j)),
        out_shape=jax.ShapeDtypeStruct((m, n), out_dtype),
        compiler_params=_cparams(("parallel", "parallel"), 48),
        name="matmul",
    )(x, w)


def _attn_prep_kernel(q_ref, kvc_ref, pi_ref, qg_ref, kg_ref, cg_ref, wiq_ref, lng_ref, lnb_ref,
                      qn_ref, kn_ref, v_ref, qi_ref, ki_ref, w_ref):
    def head_rms(x, g):
        x = x.astype(F32)
        return x * lax.rsqrt(jnp.mean(x * x, axis=-1, keepdims=True) + EPS) * g

    att_scale = HEAD_DIM ** -0.5 * LOG2E
    for h in range(ATT_HEADS):
        sl = slice(h * HEAD_DIM, (h + 1) * HEAD_DIM)
        qn_ref[:, sl] = (head_rms(q_ref[:, sl], qg_ref[...]) * att_scale).astype(BF16)
    for g in range(ATT_KV_HEADS):
        sl = slice(g * HEAD_DIM, (g + 1) * HEAD_DIM)
        kn_ref[:, sl] = head_rms(kvc_ref[:, sl], kg_ref[...]).astype(BF16)
    v_ref[...] = kvc_ref[:, KV_W:2 * KV_W].astype(v_ref.dtype)
    cq = kvc_ref[:, 2 * KV_W:2 * KV_W + IDX_Q_RANK]
    cqn = head_rms(cq, cg_ref[...]).astype(BF16)
    qi_ref[...] = _dot(cqn, wiq_ref[...]).astype(BF16)
    pi = pi_ref[...]
    is_key = lax.broadcasted_iota(I32, pi.shape, 1) < IDX_HEAD_DIM
    mu = jnp.sum(jnp.where(is_key, pi, 0.0), axis=-1, keepdims=True) * (1.0 / IDX_HEAD_DIM)
    cen = jnp.where(is_key, pi - mu, 0.0)
    var = jnp.sum(cen * cen, axis=-1, keepdims=True) * (1.0 / IDX_HEAD_DIM)
    kidn = cen * lax.rsqrt(var + EPS) * lng_ref[...] + lnb_ref[...]
    ki_ref[:, 0:LANES] = kidn.astype(BF16)
    ki_ref[:, LANES:2 * LANES] = pltpu.roll(kidn, IDX_HEAD_DIM, 1).astype(BF16)
    w_ref[...] = pi * (IDX_HEAD_DIM ** -0.5 * IDX_HEADS ** -0.5)


def attn_prep(proj_main, proj_idx, q_norm_g, k_norm_g, idx_q_norm_g, w_idx_q, ln_g, ln_b, tm=512):
    m = proj_main.shape[0]
    row = lambda i: (i, 0)
    const = lambda i: (0, 0)
    lane_pad = lambda a: jnp.zeros((1, LANES), F32).at[0, :a.shape[0]].set(a)
    return pl.pallas_call(
        _attn_prep_kernel,
        grid=(m // tm,),
        in_specs=[pl.BlockSpec((tm, 1024), lambda i: (i, COL_Q)),
                  pl.BlockSpec((tm, 1024), lambda i: (i, COL_KVC)),
                  pl.BlockSpec((tm, LANES), row),
                  pl.BlockSpec((1, HEAD_DIM), const),
                  pl.BlockSpec((1, HEAD_DIM), const),
                  pl.BlockSpec((1, IDX_Q_RANK), const),
                  pl.BlockSpec((IDX_Q_RANK, IDX_HEADS * IDX_HEAD_DIM), const),
                  pl.BlockSpec((1, LANES), const),
                  pl.BlockSpec((1, LANES), const)],
        out_specs=[pl.BlockSpec((tm, ATT_W), row),
                   pl.BlockSpec((tm, KV_W), row),
                   pl.BlockSpec((tm, KV_W), row),
                   pl.BlockSpec((tm, IDX_HEADS * IDX_HEAD_DIM), row),
                   pl.BlockSpec((tm, 2 * LANES), row),
                   pl.BlockSpec((tm, LANES), row)],
        out_shape=[jax.ShapeDtypeStruct((m, ATT_W), BF16),
                   jax.ShapeDtypeStruct((m, KV_W), BF16),
                   jax.ShapeDtypeStruct((m, KV_W), BF16),
                   jax.ShapeDtypeStruct((m, IDX_HEADS * IDX_HEAD_DIM), BF16),
                   jax.ShapeDtypeStruct((m, 2 * LANES), BF16),
                   jax.ShapeDtypeStruct((m, LANES), F32)],
        compiler_params=_cparams(("parallel",), 40),
        name="attn_prep",
    )(proj_main, proj_main, proj_idx, q_norm_g.reshape(1, -1), k_norm_g.reshape(1, -1),
      idx_q_norm_g.reshape(1, -1), w_idx_q, lane_pad(ln_g), lane_pad(ln_b))


def _rel_bucket(dist):
    max_exact = REL_BUCKETS // 2
    n = jnp.maximum(dist, 0)
    nf = jnp.maximum(n, 1).astype(F32)
    large = max_exact + (jnp.log(nf / max_exact) / math.log(REL_MAX_DIST / max_exact)
                         * (REL_BUCKETS - max_exact)).astype(I32)
    large = jnp.minimum(large, REL_BUCKETS - 1)
    return jnp.where(n < max_exact, n, large)


def _bias_kernel(rb_ref, bkt_ref, o_ref):
    h = pl.program_id(0)
    for kind in range(3):
        b = bkt_ref[kind]
        acc = jnp.zeros(b.shape, F32)
        for n in range(REL_BUCKETS):
            acc = jnp.where(b == n, rb_ref[n, h], acc)
        o_ref[0, kind] = acc * LOG2E


def bias_tables(rel_bias):
    assert TK >= REL_MAX_DIST
    kpos = jnp.arange(TK, dtype=I32)[:, None]
    qpos = jnp.arange(TQ, dtype=I32)[None, :]
    dist = jnp.stack([qpos - kpos, qpos - kpos + TK, qpos - kpos + 2 * TK])
    bkt = _rel_bucket(dist)
    return pl.pallas_call(
        _bias_kernel,
        grid=(ATT_HEADS,),
        in_specs=[pl.BlockSpec(memory_space=pltpu.SMEM),
                  pl.BlockSpec((3, TK, TQ), lambda h: (0, 0, 0))],
        out_specs=pl.BlockSpec((1, 3, TK, TQ), lambda h: (h, 0, 0, 0)),
        out_shape=jax.ShapeDtypeStruct((ATT_HEADS, 3, TK, TQ), F32),
        compiler_params=_cparams(("arbitrary",), 32),
        name="bias_tables",
    )(rel_bias, bkt)


def _side_cast_specs(src, split_axis, n_split, n_steps, step_of):
    ne = src.shape[0]
    n_blocks = ne * n_split
    every = n_steps // n_blocks
    assert every >= 1, "not enough grid steps to walk the weight stack"
    block = [1, src.shape[1], src.shape[2]]
    block[split_axis] //= n_split

    def index_map(*grid_idx):
        t = jnp.minimum(step_of(*grid_idx) // every, n_blocks - 1)
        idx = [t // n_split, 0, 0]
        idx[split_axis] = t % n_split
        return tuple(idx)

    spec = pl.BlockSpec(tuple(block), index_map)
    return spec, spec, jax.ShapeDtypeStruct(src.shape, BF16)


def _dsa_kernel(qi_ref, wt_ref, qn_ref, ki_ref, kn_ref, vt_ref, bias_ref, *rest, n_sel, side_cast):
    if side_cast:
        wf_ref, o_ref, wb_ref, key_ref, m_ref, l_ref, a_ref, acc_ref, s_ref, p_ref = rest
        wb_ref[...] = wf_ref[...].astype(BF16)
    else:
        o_ref, key_ref, m_ref, l_ref, a_ref, acc_ref, s_ref, p_ref = rest
    i = pl.program_id(1)
    nch = i + 1
    qpos = i * TQ + lax.broadcasted_iota(I32, (TK, TQ), 1)

    def score_chunk(j, carry):
        k0 = pl.multiple_of(j * TK, TK)
        acc = jnp.zeros((TK, TQ), F32)
        for h in range(IDX_HEADS):
            par = h % 2
            kc = ki_ref[pl.ds(k0, TK), par * LANES:(par + 1) * LANES]
            qh = qi_ref[:, (h // 2) * LANES:(h // 2 + 1) * LANES]
            s = _dot_nt(kc, qh)
            acc = acc + jnp.maximum(s, 0.0) * wt_ref[0, h:h + 1, :]
        bits = lax.bitcast_convert_type(acc, I32)
        key = jnp.where(bits < 0, bits ^ jnp.int32(0x7FFFFFFF), bits)
        kpos = k0 + lax.broadcasted_iota(I32, (TK, TQ), 0)
        key_ref[pl.ds(k0, TK), :] = jnp.where(kpos <= qpos, key, INT_MIN)
        return carry

    lax.fori_loop(0, nch, score_chunk, 0)

    def count_ge(cand):
        def body(j, acc):
            k0 = pl.multiple_of(j * TK, TK)
            hit = jnp.where(key_ref[pl.ds(k0, TK), :] >= cand, 1, 0).astype(I32)
            return acc + jnp.sum(hit.reshape(TK // 8, 8, TQ), axis=0)
        acc = lax.fori_loop(0, nch, body, jnp.zeros((8, TQ), I32))
        return jnp.sum(acc, axis=0, keepdims=True)

    thr = jnp.where(count_ge(jnp.zeros((1, TQ), I32)) >= n_sel, 0, INT_MIN).astype(I32)

    def bit_body(b, thr):
        cand = thr | jnp.left_shift(jnp.int32(1), 30 - b)
        return jnp.where(count_ge(cand) >= n_sel, cand, thr)

    thr = lax.fori_loop(0, 31, bit_body, thr)
    thr = jnp.maximum(thr, INT_MIN + 1)

    m_ref[...] = jnp.full(m_ref.shape, NEG_BIG, F32)
    l_ref[...] = jnp.zeros(l_ref.shape, F32)
    acc_ref[...] = jnp.zeros(acc_ref.shape, F32)

    def att_chunk(j, carry):
        k0 = pl.multiple_of(j * TK, TK)
        neg = jnp.where(key_ref[pl.ds(k0, TK), :] >= thr, 0.0, NEG_BIG)
        kind = jnp.minimum(i - j, 2)
        for h in range(ATT_HEADS):
            g = h // ATT_GROUP
            kc = kn_ref[pl.ds(k0, TK), g * HEAD_DIM:(g + 1) * HEAD_DIM]
            qh = qn_ref[:, h * HEAD_DIM:(h + 1) * HEAD_DIM]
            lg = _dot_nt(kc, qh) + bias_ref[h, kind] + neg
            s_ref[h] = lg
            m_old = m_ref[h:h + 1, :]
            m_new = jnp.maximum(m_old, jnp.max(lg, axis=0, keepdims=True))
            a_ref[h:h + 1, :] = jnp.exp2(m_old - m_new)
            m_ref[h:h + 1, :] = m_new
        for h in range(ATT_HEADS):
            p = jnp.exp2(s_ref[h] - m_ref[h:h + 1, :])
            l_ref[h:h + 1, :] = (a_ref[h:h + 1, :] * l_ref[h:h + 1, :]
                                 + jnp.sum(p, axis=0, keepdims=True))
            p_ref[h] = p.astype(BF16)
        for h in range(ATT_HEADS):
            g = h // ATT_GROUP
            vc = vt_ref[0, j, g * HEAD_DIM:(g + 1) * HEAD_DIM, :]
            acc_ref[h] = a_ref[h:h + 1, :] * acc_ref[h] + _dot(vc, p_ref[h])
        return carry

    lax.fori_loop(0, nch, att_chunk, 0)
    for h in range(ATT_HEADS):
        o = acc_ref[h] * (1.0 / l_ref[h:h + 1, :])
        o_ref[:, h * HEAD_DIM:(h + 1) * HEAD_DIM] = o.T.astype(o_ref.dtype)


def dsa_attention(qi, wt, qn, ki, kn, vt, bias, batch, seq, side_cast=None):
    nq = seq // TQ
    n_sel = min(TOPK_MAX, seq // 4)
    qrow = lambda b, i: (b * nq + i, 0)
    brow = lambda b, i: (b, 0)
    in_specs = [pl.BlockSpec((TQ, IDX_HEADS * IDX_HEAD_DIM), qrow),
                pl.BlockSpec((1, IDX_HEADS, TQ), lambda b, i: (b, 0, i)),
                pl.BlockSpec((TQ, ATT_W), qrow),
                pl.BlockSpec((seq, 2 * LANES), brow),
                pl.BlockSpec((seq, KV_W), brow),
                pl.BlockSpec((1, seq // TK, KV_W, TK), lambda b, i: (b, 0, 0, 0)),
                pl.BlockSpec((ATT_HEADS, 3, TK, TQ), lambda b, i: (0, 0, 0, 0))]
    out_specs = [pl.BlockSpec((TQ, ATT_W), qrow)]
    out_shape = [jax.ShapeDtypeStruct((batch * seq, ATT_W), BF16)]
    operands = [qi, wt, qn, ki, kn, vt, bias]
    if side_cast is not None:
        src, split_axis, n_split = side_cast
        spec_in, spec_out, shape_out = _side_cast_specs(src, split_axis, n_split, batch * nq,
                                                        lambda b, i: b * nq + i)
        in_specs.append(spec_in)
        out_specs.append(spec_out)
        out_shape.append(shape_out)
        operands.append(src)
    return pl.pallas_call(
        functools.partial(_dsa_kernel, n_sel=n_sel, side_cast=side_cast is not None),
        grid=(batch, nq),
        in_specs=in_specs,
        out_specs=out_specs,
        out_shape=out_shape,
        scratch_shapes=[pltpu.VMEM((seq, TQ), I32),
                        pltpu.VMEM((ATT_HEADS, TQ), F32),
                        pltpu.VMEM((ATT_HEADS, TQ), F32),
                        pltpu.VMEM((ATT_HEADS, TQ), F32),
                        pltpu.VMEM((ATT_HEADS, HEAD_DIM, TQ), F32),
                        pltpu.VMEM((ATT_HEADS, TK, TQ), F32),
                        pltpu.VMEM((ATT_HEADS, TK, TQ), BF16)],
        compiler_params=_cparams(("arbitrary", "arbitrary"), 56),
        name="dsa_attention",
    )(*operands)


def _hgrn_constants():
    t = np.arange(HC)
    rows = []
    masks = [np.eye(HC, dtype=np.float32)]
    for m in HGRN_LEVELS:
        upper = (t // m) % 2 == 1
        start = (t // m) * m
        end = start + m - 1
        u = t[None, :]
        q_side = upper[:, None] & (u >= start[:, None]) & (u <= t[:, None])
        k_side = (~upper)[:, None] & (u > t[:, None]) & (u <= end[:, None])
        rows.append((q_side | k_side).astype(np.float32))
        same = (t[:, None] // (2 * m)) == (t[None, :] // (2 * m))
        masks.append((upper[:, None] & (~upper)[None, :] & same).astype(np.float32))
    u = t[None, :]
    rows.append((u <= t[:, None]).astype(np.float32))
    rows.append((u > t[:, None]).astype(np.float32))
    mat = np.concatenate(rows, axis=0)
    return np.concatenate([mat, mat], axis=1), np.stack(masks)


def _hgrn_kernel(hq_ref, hf_ref, hi_ref, hg_ref, lb_ref, gn_ref, mat_ref, mask_ref, *rest,
                 side_cast):
    if side_cast:
        wf_ref, o_ref, wb_ref, st_ref, arg_ref, kk_ref, lf_ref, zq_ref, zk_ref = rest
        wb_ref[...] = wf_ref[...].astype(BF16)
    else:
        o_ref, st_ref, arg_ref, kk_ref, lf_ref, zq_ref, zk_ref = rest
    c = pl.program_id(1)

    @pl.when(c == 0)
    def _():
        st_ref[...] = jnp.zeros_like(st_ref)

    lb = lb_ref[...]
    f = hf_ref[...].astype(F32)
    e = jnp.exp(-jnp.abs(f))
    r = 1.0 / (1.0 + e)
    log_sig = jnp.minimum(f, 0.0) - jnp.log(1.0 + e)
    la = jnp.log(lb)
    lc = jnp.log(1.0 - lb) + log_sig
    logf = (jnp.maximum(la, lc) + jnp.log(1.0 + jnp.exp(-jnp.abs(la - lc)))) * LOG2E
    kk_ref[...] = (1.0 - lb) * jnp.where(f >= 0, e * r, r)
    hi = logf.astype(BF16)
    lf_ref[0:HC, :] = hi
    lf_ref[HC:2 * HC, :] = (logf - hi.astype(F32)).astype(BF16)
    arg_ref[...] = _dot(mat_ref[...], lf_ref[...])

    scale = HGRN_DIM ** -0.5
    nl = len(HGRN_LEVELS)
    for h in range(HGRN_HEADS):
        sl = slice(h * HGRN_DIM, (h + 1) * HGRN_DIM)
        hq = hq_ref[:, sl].astype(F32)
        q = hq * _sigmoid(hq) * scale
        k = kk_ref[:, sl]
        zq_ref[h, 0] = q.astype(BF16)
        zk_ref[h, 0] = k.astype(BF16)
        for lv in range(nl):
            ex = jnp.exp2(arg_ref[lv * HC:(lv + 1) * HC, sl])
            zq_ref[h, lv + 1] = (q * ex).astype(BF16)
            zk_ref[h, lv + 1] = (k * ex).astype(BF16)
        zq_ref[h, nl + 1] = (q * jnp.exp2(arg_ref[nl * HC:(nl + 1) * HC, sl])).astype(BF16)
        zk_ref[h, nl + 1] = (k * jnp.exp2(arg_ref[(nl + 1) * HC:(nl + 2) * HC, sl])).astype(BF16)

    for h in range(HGRN_HEADS):
        sl = slice(h * HGRN_DIM, (h + 1) * HGRN_DIM)
        a = _dot_nt(zq_ref[h, 0], zk_ref[h, 0]) * mask_ref[0]
        for lv in range(nl):
            a = a + _dot_nt(zq_ref[h, lv + 1], zk_ref[h, lv + 1]) * mask_ref[lv + 1]
        v = hi_ref[:, sl].astype(BF16)
        st = st_ref[h]
        o = _dot(a.astype(BF16), v) + _dot_nt(zq_ref[h, nl + 1], st.astype(BF16))
        decay = jnp.exp2(arg_ref[(nl + 1) * HC - 1:(nl + 1) * HC, sl])
        st_ref[h] = decay * st + _dot(v.astype(F32).T.astype(BF16), zk_ref[h, nl + 1])
        on = o * lax.rsqrt(jnp.mean(o * o, axis=-1, keepdims=True) + EPS) * gn_ref[...]
        hg = hg_ref[:, sl].astype(F32)
        o_ref[:, sl] = (on * hg * _sigmoid(hg)).astype(o_ref.dtype)


def hgrn2(proj_main, lb, g_norm, batch, seq, side_cast=None):
    nc = seq // HC
    mat, masks = _hgrn_constants()
    col = lambda cb: (lambda b, c: (b * nc + c, cb))
    in_specs = [pl.BlockSpec((HC, HGRN_W), col(COL_HQ)),
                pl.BlockSpec((HC, HGRN_W), col(COL_HF)),
                pl.BlockSpec((HC, HGRN_W), col(COL_HI)),
                pl.BlockSpec((HC, HGRN_W), col(COL_HG)),
                pl.BlockSpec((1, HGRN_W), lambda b, c: (0, 0)),
                pl.BlockSpec((1, HGRN_DIM), lambda b, c: (0, 0)),
                pl.BlockSpec((N_ARG_GROUPS * HC, 2 * HC), lambda b, c: (0, 0)),
                pl.BlockSpec((len(HGRN_LEVELS) + 1, HC, HC), lambda b, c: (0, 0, 0))]
    operands = [proj_main, proj_main, proj_main, proj_main, lb.reshape(1, -1), g_norm.reshape(1, -1),
                jnp.asarray(mat, BF16), jnp.asarray(masks, F32)]
    out_specs = [pl.BlockSpec((HC, HGRN_W), lambda b, c: (b * nc + c, 0))]
    out_shape = [jax.ShapeDtypeStruct((batch * seq, HGRN_W), BF16)]
    if side_cast is not None:
        src, split_axis, n_split = side_cast
        spec_in, spec_out, shape_out = _side_cast_specs(src, split_axis, n_split, batch * nc,
                                                        lambda b, c: b * nc + c)
        in_specs.append(spec_in)
        operands.append(src)
        out_specs.append(spec_out)
        out_shape.append(shape_out)
    return pl.pallas_call(
        functools.partial(_hgrn_kernel, side_cast=side_cast is not None),
        grid=(batch, nc),
        in_specs=in_specs,
        out_specs=out_specs,
        out_shape=out_shape,
        scratch_shapes=[pltpu.VMEM((HGRN_HEADS, HGRN_DIM, HGRN_DIM), F32),
                        pltpu.VMEM((N_ARG_GROUPS * HC, HGRN_W), F32),
                        pltpu.VMEM((HC, HGRN_W), F32),
                        pltpu.VMEM((2 * HC, HGRN_W), BF16),
                        pltpu.VMEM((HGRN_HEADS, N_ARG_GROUPS, HC, HGRN_DIM), BF16),
                        pltpu.VMEM((HGRN_HEADS, N_ARG_GROUPS, HC, HGRN_DIM), BF16)],
        compiler_params=_cparams(("arbitrary", "arbitrary"), 48),
        name="hgrn2",
    )(*operands)


def _merge_out_kernel(att_ref, rec_ref, ga_ref, gh_ref, x_ref, wa_ref, wh_ref, wo_ref, g_ref,
                      h_ref, *maybe_hn_ref):
    j = pl.program_id(1)

    @pl.when(j == 0)
    def _():
        h_ref[...] = x_ref[...]

    a = _dot(att_ref[...], wa_ref[...])
    r = _dot(rec_ref[...], wh_ref[...])
    merged = (_sigmoid(ga_ref[...].astype(F32)) * a + _sigmoid(gh_ref[...].astype(F32)) * r).astype(BF16)
    h_ref[...] += _dot(merged, wo_ref[...])

    if maybe_hn_ref:
        @pl.when(j == pl.num_programs(1) - 1)
        def _():
            h = h_ref[...]
            ms = jnp.mean(h * h, axis=-1, keepdims=True)
            maybe_hn_ref[0][...] = (h * lax.rsqrt(ms + EPS) * g_ref[...]).astype(BF16)


def merge_out(att, rec, proj_main, x, w_up_att, w_up_hgrn, w_o, norm_g, emit_hn, tm=512, tn=1024):
    m, d = x.shape
    ga0, gh0 = COL_GA * 1024 // tn, COL_GH * 1024 // tn
    row = lambda i, j: (i, 0)
    out_specs = [pl.BlockSpec((tm, d), row)]
    out_shape = [jax.ShapeDtypeStruct((m, d), F32)]
    if emit_hn:
        out_specs.append(pl.BlockSpec((tm, d), row))
        out_shape.append(jax.ShapeDtypeStruct((m, d), BF16))
    return pl.pallas_call(
        _merge_out_kernel,
        grid=(m // tm, d // tn),
        in_specs=[pl.BlockSpec((tm, ATT_W), row),
                  pl.BlockSpec((tm, HGRN_W), row),
                  pl.BlockSpec((tm, tn), lambda i, j: (i, ga0 + j)),
                  pl.BlockSpec((tm, tn), lambda i, j: (i, gh0 + j)),
                  pl.BlockSpec((tm, d), row),
                  pl.BlockSpec((ATT_W, tn), lambda i, j: (0, j)),
                  pl.BlockSpec((HGRN_W, tn), lambda i, j: (0, j)),
                  pl.BlockSpec((tn, d), lambda i, j: (j, 0)),
                  pl.BlockSpec((1, d), lambda i, j: (0, 0))],
        out_specs=out_specs,
        out_shape=out_shape,
        compiler_params=_cparams(("parallel", "arbitrary"), 48),
        name="merge_out",
    )(att, rec, proj_main, proj_main, x, w_up_att, w_up_hgrn, w_o, norm_g.reshape(1, d))


def _ffn_kernel(hn_ref, h_ref, w1_ref, w3_ref, w2_ref, g_ref, o_ref, *maybe_xn_ref):
    j = pl.program_id(1)

    @pl.when(j == 0)
    def _():
        o_ref[...] = h_ref[...]

    x = hn_ref[...]
    a = _dot(x, w1_ref[...])
    b = _dot(x, w3_ref[...])
    act = (a * _sigmoid(a) * b).astype(BF16)
    o_ref[...] += _dot(act, w2_ref[...])

    if maybe_xn_ref:
        @pl.when(j == pl.num_programs(1) - 1)
        def _():
            y = o_ref[...]
            ms = jnp.mean(y * y, axis=-1, keepdims=True)
            maybe_xn_ref[0][...] = (y * lax.rsqrt(ms + EPS) * g_ref[...]).astype(BF16)


def ffn_dense(hn, h, w1, w3, w2, next_norm_g, emit_xn, tm=512, tf=512):
    m, d = hn.shape
    f = w1.shape[1]
    row = lambda i, j: (i, 0)
    out_specs = [pl.BlockSpec((tm, d), row)]
    out_shape = [jax.ShapeDtypeStruct((m, d), F32)]
    if emit_xn:
        out_specs.append(pl.BlockSpec((tm, d), row))
        out_shape.append(jax.ShapeDtypeStruct((m, d), BF16))
    return pl.pallas_call(
        _ffn_kernel,
        grid=(m // tm, f // tf),
        in_specs=[pl.BlockSpec((tm, d), row),
                  pl.BlockSpec((tm, d), row),
                  pl.BlockSpec((d, tf), lambda i, j: (0, j)),
                  pl.BlockSpec((d, tf), lambda i, j: (0, j)),
                  pl.BlockSpec((tf, d), lambda i, j: (j, 0)),
                  pl.BlockSpec((1, d), lambda i, j: (0, 0))],
        out_specs=out_specs,
        out_shape=out_shape,
        compiler_params=_cparams(("parallel", "arbitrary"), 52),
        name="ffn_dense",
    )(hn, h, w1, w3, w2, next_norm_g.reshape(1, d))


def _row_copy(src_ref, src_row, dst_ref, dst_row, sem):
    return pltpu.make_async_copy(src_ref.at[pl.ds(src_row, 1)], dst_ref.at[pl.ds(dst_row, 1)], sem)


def _moe_scatter_kernel(dest_ref, hn_ref, xs_init_ref, xs_ref, sem, *, tm, m):
    del xs_init_ref
    base = pl.program_id(0) * tm

    def issue(r, carry):
        for k in range(2):
            _row_copy(hn_ref, r, xs_ref, dest_ref[k * m + base + r], sem).start()
        return carry

    lax.fori_loop(0, tm, issue, 0, unroll=8)
    for k in range(2):
        pltpu.make_async_copy(hn_ref, xs_ref.at[pl.ds(0, tm)], sem).wait()


def moe_scatter(hn_packed, dest, n_rows, tm=512):
    m, c = hn_packed.shape
    xs_init = jnp.zeros((n_rows, c), I32)
    return pl.pallas_call(
        functools.partial(_moe_scatter_kernel, tm=tm, m=m),
        grid_spec=pltpu.PrefetchScalarGridSpec(
            num_scalar_prefetch=1,
            grid=(m // tm,),
            in_specs=[pl.BlockSpec((tm, c), lambda i, dest: (i, 0)),
                      pl.BlockSpec(memory_space=pl.ANY)],
            out_specs=pl.BlockSpec(memory_space=pl.ANY),
            scratch_shapes=[pltpu.SemaphoreType.DMA(())]),
        out_shape=jax.ShapeDtypeStruct((n_rows, c), I32),
        input_output_aliases={2: 0},
        compiler_params=_cparams(("arbitrary",), 32),
        name="moe_scatter",
    )(dest, hn_packed, xs_init)


def _moe_ffn_kernel(te_ref, tv_ref, xs_ref, w1_ref, w3_ref, w2_ref, o_ref, xb_ref):
    del te_ref
    i = pl.program_id(0)
    half = xs_ref.shape[1]

    @pl.when(pl.program_id(1) == 0)
    def _():
        o_ref[...] = jnp.zeros_like(o_ref)
        lo, hi = _unpack_bf16_pairs(xs_ref[...])
        xb_ref[:, 0:half] = lo
        xb_ref[:, half:2 * half] = hi

    @pl.when(tv_ref[i] == 1)
    def _():
        x = xb_ref[...]
        a = _dot(x, w1_ref[0])
        b = _dot(x, w3_ref[0])
        act = (a * _sigmoid(a) * b).astype(BF16)
        o_ref[...] += _dot(act, w2_ref[0])


def moe_ffn(xs, tile_expert, tile_valid, w1, w3, w2, tm=512, tf=1024):
    p, c = xs.shape
    ne, d, f = w1.shape
    nf = f // tf
    col = lambda i, j, te, tv: jnp.where(tv[i] == 1, j, nf - 1)
    return pl.pallas_call(
        _moe_ffn_kernel,
        grid_spec=pltpu.PrefetchScalarGridSpec(
            num_scalar_prefetch=2,
            grid=(p // tm, nf),
            in_specs=[pl.BlockSpec((tm, c), lambda i, j, te, tv: (i, 0)),
                      pl.BlockSpec((1, d, tf), lambda i, j, te, tv: (te[i], 0, col(i, j, te, tv))),
                      pl.BlockSpec((1, d, tf), lambda i, j, te, tv: (te[i], 0, col(i, j, te, tv))),
                      pl.BlockSpec((1, tf, d), lambda i, j, te, tv: (te[i], col(i, j, te, tv), 0))],
            out_specs=pl.BlockSpec((tm, d), lambda i, j, te, tv: (i, 0)),
            scratch_shapes=[pltpu.VMEM((tm, d), BF16)]),
        out_shape=jax.ShapeDtypeStruct((p, d), F32),
        compiler_params=_cparams(("arbitrary", "arbitrary"), 56),
        name="moe_ffn",
    )(tile_expert, tile_valid, xs, w1, w3, w2)


def _moe_combine_kernel(dest_ref, h_ref, prob_ref, ys_ref, o_ref, ybuf_ref, sem, *, tm, m):
    i = pl.program_id(0)
    slot = i % 2

    def gather_tile(tile, slot_):
        base = tile * tm

        def issue(r, carry):
            for k in range(2):
                _row_copy(ys_ref, dest_ref[k * m + base + r], ybuf_ref.at[slot_, k], r,
                          sem.at[slot_]).start()
            return carry

        lax.fori_loop(0, tm, issue, 0, unroll=8)

    @pl.when(i == 0)
    def _():
        gather_tile(0, 0)

    @pl.when(i + 1 < pl.num_programs(0))
    def _():
        gather_tile(i + 1, 1 - slot)

    for k in range(2):
        pltpu.make_async_copy(ys_ref.at[pl.ds(0, tm)], ybuf_ref.at[slot, k], sem.at[slot]).wait()
    prob = prob_ref[...]
    o_ref[...] = h_ref[...] + prob[:, 0:1] * ybuf_ref[slot, 0] + prob[:, 1:2] * ybuf_ref[slot, 1]


def moe_combine(h, prob, ys, dest, tm=512):
    m, d = h.shape
    return pl.pallas_call(
        functools.partial(_moe_combine_kernel, tm=tm, m=m),
        grid_spec=pltpu.PrefetchScalarGridSpec(
            num_scalar_prefetch=1,
            grid=(m // tm,),
            in_specs=[pl.BlockSpec((tm, d), lambda i, dest: (i, 0)),
                      pl.BlockSpec((tm, LANES), lambda i, dest: (i, 0)),
                      pl.BlockSpec(memory_space=pl.ANY)],
            out_specs=pl.BlockSpec((tm, d), lambda i, dest: (i, 0)),
            scratch_shapes=[pltpu.VMEM((2, 2, tm, d), F32), pltpu.SemaphoreType.DMA((2,))]),
        out_shape=jax.ShapeDtypeStruct((m, d), F32),
        compiler_params=_cparams(("arbitrary",), 48),
        name="moe_combine",
    )(dest, h, prob, ys)


def ffn_moe(h, norm_g, w_router, w1, w3, w2, tm=512):
    m, d = h.shape
    ne = w1.shape[0]
    hn_packed, meta, prob, cnt = rmsnorm_router(h, norm_g, w_router)
    counts = cnt[0, :ne].astype(I32)
    tiles_per = (counts + tm - 1) // tm
    tile_end = jnp.cumsum(tiles_per)
    offset = (tile_end - tiles_per) * tm
    dest = jnp.concatenate([offset[meta[:, 0]] + meta[:, 2], offset[meta[:, 1]] + meta[:, 3]])
    n_tiles = (2 * m) // tm + ne
    tile_ids = jnp.arange(n_tiles, dtype=I32)
    tile_valid = (tile_ids < tile_end[-1]).astype(I32)
    tile_expert = jnp.searchsorted(tile_end, jnp.minimum(tile_ids, tile_end[-1] - 1), side="right").astype(I32)
    xs = moe_scatter(hn_packed, dest, n_tiles * tm)
    ys = moe_ffn(xs, tile_expert, tile_valid, w1, w3, w2, tm=tm)
    return moe_combine(h, prob, ys, dest)


def _w_in_prep_kernel(w_ref, main_ref, idx_ref):
    head = ATT_W + 2 * KV_W + IDX_Q_RANK
    small = IDX_HEAD_DIM + IDX_HEADS
    main_ref[:, 0:head] = w_ref[:, 0:head].astype(BF16)
    group = w_ref[:, head:head + LANES]
    lane = lax.broadcasted_iota(I32, group.shape, 1)
    idx_ref[...] = jnp.where(lane < small, group, 0.0).astype(BF16)
    step = 1024
    for c in range((MAIN_W - head) // step):
        src = head + small + c * step
        main_ref[:, head + c * step:head + (c + 1) * step] = w_ref[:, src:src + step].astype(BF16)


def _split_w_in(w_in, tr=256):
    d, n = w_in.shape
    assert n == MAIN_W + IDX_HEAD_DIM + IDX_HEADS
    return pl.pallas_call(
        _w_in_prep_kernel,
        grid=(d // tr,),
        in_specs=[pl.BlockSpec((tr, n), lambda i: (i, 0))],
        out_specs=[pl.BlockSpec((tr, MAIN_W), lambda i: (i, 0)),
                   pl.BlockSpec((tr, LANES), lambda i: (i, 0))],
        out_shape=[jax.ShapeDtypeStruct((d, MAIN_W), BF16),
                   jax.ShapeDtypeStruct((d, LANES), BF16)],
        compiler_params=_cparams(("parallel",), 48),
        name="w_in_prep",
    )(w_in)


def _mixer(x2, maybe_xn, batch, seq, norm_g, w_in, q_norm_g, k_norm_g, idx_q_norm_g, w_idx_q, ln_g,
           ln_b, lb, hgrn_norm_g, w_up_att, w_up_hgrn, w_o, bias, norm_ffn_g, emit_hn,
           side_cast_att=None, side_cast_rec=None):
    xn = rmsnorm_rows(x2, norm_g) if maybe_xn is None else maybe_xn
    w_main, w_idx = _split_w_in(w_in)
    proj_main = matmul(xn, w_main, BF16)
    proj_idx = matmul(xn, w_idx, F32)
    qn, kn, v, qi, ki, w = attn_prep(proj_main, proj_idx, q_norm_g, k_norm_g, idx_q_norm_g,
                                     w_idx_q.astype(BF16), ln_g, ln_b)
    wt = w[:, IDX_HEAD_DIM:IDX_HEAD_DIM + IDX_HEADS].reshape(batch, seq, IDX_HEADS).transpose(0, 2, 1)
    vt = v.reshape(batch, seq // TK, TK, KV_W).transpose(0, 1, 3, 2)
    att, *cast_att = dsa_attention(qi, wt, qn, ki, kn, vt, bias, batch, seq, side_cast_att)
    rec, *cast_rec = hgrn2(proj_main, lb, hgrn_norm_g, batch, seq, side_cast_rec)
    outs = merge_out(att, rec, proj_main, x2, w_up_att.astype(BF16), w_up_hgrn.astype(BF16),
                     w_o.astype(BF16), norm_ffn_g, emit_hn)
    return outs, (cast_att[0] if cast_att else None), (cast_rec[0] if cast_rec else None)


def kernel(x, rel_bias, norm_mix_g, norm_ffn_g, w_in, q_norm_g, k_norm_g, idx_q_norm_g, w_idx_q,
           idx_k_ln_g, idx_k_ln_b, hgrn_lb_logits, hgrn_out_norm_g, w_up_att, w_up_hgrn, w_o,
           w1_dense, w3_dense, w2_dense, w_router, w1_moe, w3_moe, w2_moe):
    batch, seq, d = x.shape
    depth = w_in.shape[0]
    lb_all = jnp.cumsum(jax.nn.softmax(hgrn_lb_logits.astype(F32), axis=0), axis=0)
    lb_all = lb_all - lb_all[0:1]
    bias = bias_tables(rel_bias)
    x2 = x.reshape(batch * seq, d)
    maybe_xn = None
    n_split_ff = w1_moe.shape[-1] // 1024
    moe_bf16 = {}
    for l in range(depth):
        dense = l % 2 == 0
        j = l // 2
        side_att = side_rec = None
        if dense and l + 1 < depth:
            side_att = (w3_moe[j], 2, n_split_ff)
            side_rec = (w1_moe[j], 2, 2 * n_split_ff)
        elif not dense:
            side_att = (w2_moe[j], 1, n_split_ff)
        (h, *maybe_hn), cast_att, cast_rec = _mixer(
            x2, maybe_xn, batch, seq, norm_mix_g[l], w_in[l], q_norm_g[l], k_norm_g[l],
            idx_q_norm_g[l], w_idx_q[l], idx_k_ln_g[l], idx_k_ln_b[l], lb_all[l],
            hgrn_out_norm_g[l], w_up_att[l], w_up_hgrn[l], w_o[l], bias, norm_ffn_g[l],
            emit_hn=dense, side_cast_att=side_att, side_cast_rec=side_rec)
        maybe_xn = None
        if dense:
            more = l + 1 < depth
            moe_bf16["w3"], moe_bf16["w1"] = cast_att, cast_rec
            x2, *rest = ffn_dense(maybe_hn[0], h, w1_dense[j].astype(BF16), w3_dense[j].astype(BF16),
                                  w2_dense[j].astype(BF16), norm_mix_g[l + 1 if more else l], more)
            maybe_xn = rest[0] if more else None
        else:
            x2 = ffn_moe(h, norm_ffn_g[l], w_router[j], moe_bf16["w1"], moe_bf16["w3"], cast_att)
    return x2.reshape(batch, seq, d)
```

```python
import functools
import math

import numpy as np
import jax
import jax.numpy as jnp
from jax import lax
from jax.experimental import pallas as pl
from jax.experimental.pallas import tpu as pltpu

F32 = jnp.float32
BF16 = jnp.bfloat16
I32 = jnp.int32

EPS = 1e-6
ATT_HEADS = 8
ATT_KV_HEADS = 2
ATT_GROUP = ATT_HEADS // ATT_KV_HEADS
HEAD_DIM = 128
ATT_W = ATT_HEADS * HEAD_DIM
KV_W = ATT_KV_HEADS * HEAD_DIM
IDX_HEADS = 16
IDX_HEAD_DIM = 64
IDX_Q_RANK = 512
TOPK_MAX = 256
HGRN_HEADS = 8
HGRN_DIM = 128
HGRN_W = HGRN_HEADS * HGRN_DIM
REL_BUCKETS = 32
REL_MAX_DIST = 128
N_EXPERTS = 8
LANES = 128
INT_MIN = -(2 ** 31)
NEG_BIG = -1e30
LOG2E = math.log2(math.e)

TQ = 256
TK = 256
HC = 128
HGRN_LEVELS = (1, 2, 4, 8, 16, 32, 64)
N_ARG_GROUPS = len(HGRN_LEVELS) + 2

COL_Q, COL_KVC, COL_HQ, COL_HF, COL_HI, COL_HG, COL_GA, COL_GH = 0, 1, 2, 3, 4, 5, 6, 8
MAIN_W = 10 * 1024


def _cparams(sem, vmem_mb):
    return pltpu.CompilerParams(dimension_semantics=sem, vmem_limit_bytes=vmem_mb << 20)


def _dot(a, b):
    return jnp.dot(a, b, preferred_element_type=F32)


def _dot_nt(a, b):
    return lax.dot_general(a, b, (((1,), (1,)), ((), ())), preferred_element_type=F32)


def _sigmoid(x):
    return 1.0 / (1.0 + jnp.exp(-x))


def _rmsnorm_kernel(x_ref, g_ref, o_ref):
    x = x_ref[...]
    ms = jnp.mean(x * x, axis=-1, keepdims=True)
    o_ref[...] = (x * lax.rsqrt(ms + EPS) * g_ref[...]).astype(o_ref.dtype)


def rmsnorm_rows(x, g, tm=512):
    m, d = x.shape
    return pl.pallas_call(
        _rmsnorm_kernel,
        grid=(m // tm,),
        in_specs=[pl.BlockSpec((tm, d), lambda i: (i, 0)),
                  pl.BlockSpec((1, d), lambda i: (0, 0))],
        out_specs=pl.BlockSpec((tm, d), lambda i: (i, 0)),
        out_shape=jax.ShapeDtypeStruct((m, d), BF16),
        compiler_params=_cparams(("parallel",), 32),
        name="rmsnorm",
    )(x, g.reshape(1, d))


def _pack_bf16_pairs(x):
    c = x.shape[1] // 2
    lo = lax.bitcast_convert_type(x[:, :c].astype(BF16).astype(F32), I32)
    hi = lax.bitcast_convert_type(x[:, c:].astype(BF16).astype(F32), I32)
    return (hi & jnp.int32(-65536)) | lax.shift_right_logical(lo, 16)


def _unpack_bf16_pairs(p):
    lo = lax.bitcast_convert_type(lax.shift_left(p, 16), F32).astype(BF16)
    hi = lax.bitcast_convert_type(p & jnp.int32(-65536), F32).astype(BF16)
    return lo, hi


def _rmsnorm_router_kernel(x_ref, g_ref, wr_ref, tri_ref, o_ref, meta_ref, prob_ref, cnt_ref, carry_ref):
    @pl.when(pl.program_id(0) == 0)
    def _():
        carry_ref[...] = jnp.zeros_like(carry_ref)

    x = x_ref[...]
    ms = jnp.mean(x * x, axis=-1, keepdims=True)
    hn = x * lax.rsqrt(ms + EPS) * g_ref[...]
    o_ref[...] = _pack_bf16_pairs(hn)
    logits = _dot(hn.astype(BF16), wr_ref[...])
    lane = lax.broadcasted_iota(I32, logits.shape, 1)
    logits = jnp.where(lane < N_EXPERTS, logits, -jnp.inf)
    v1 = jnp.max(logits, axis=-1, keepdims=True)
    i1 = jnp.min(jnp.where(logits == v1, lane, LANES), axis=-1, keepdims=True)
    rest = jnp.where(lane == i1, -jnp.inf, logits)
    v2 = jnp.max(rest, axis=-1, keepdims=True)
    i2 = jnp.min(jnp.where(rest == v2, lane, LANES), axis=-1, keepdims=True)
    e = jnp.exp(v2 - v1)
    p1 = 1.0 / (1.0 + e)
    prob_ref[...] = jnp.where(lane == 0, p1, 0.0) + jnp.where(lane == 1, e * p1, 0.0)
    hot = jnp.where((lane == i1) | (lane == i2), 1.0, 0.0)
    rank = _dot(tri_ref[...], hot.astype(BF16)) + carry_ref[0:1, :]
    carry_ref[...] = carry_ref[...] + jnp.sum(hot, axis=0, keepdims=True)
    cnt_ref[...] = carry_ref[...]
    r1 = jnp.sum(jnp.where(lane == i1, rank, 0.0), axis=-1, keepdims=True)
    r2 = jnp.sum(jnp.where(lane == i2, rank, 0.0), axis=-1, keepdims=True)
    meta = (jnp.where(lane == 0, i1, 0) + jnp.where(lane == 1, i2, 0)
            + jnp.where(lane == 2, r1.astype(I32), 0) + jnp.where(lane == 3, r2.astype(I32), 0))
    meta_ref[...] = meta


def rmsnorm_router(x, g, w_router, tm=512):
    m, d = x.shape
    wr = jnp.zeros((d, LANES), BF16).at[:, :N_EXPERTS].set(w_router.astype(BF16))
    tri = jnp.asarray(np.tril(np.ones((tm, tm), np.float32), -1), BF16)
    return pl.pallas_call(
        _rmsnorm_router_kernel,
        grid=(m // tm,),
        in_specs=[pl.BlockSpec((tm, d), lambda i: (i, 0)),
                  pl.BlockSpec((1, d), lambda i: (0, 0)),
                  pl.BlockSpec((d, LANES), lambda i: (0, 0)),
                  pl.BlockSpec((tm, tm), lambda i: (0, 0))],
        out_specs=[pl.BlockSpec((tm, d // 2), lambda i: (i, 0)),
                   pl.BlockSpec((tm, LANES), lambda i: (i, 0)),
                   pl.BlockSpec((tm, LANES), lambda i: (i, 0)),
                   pl.BlockSpec((8, LANES), lambda i: (0, 0))],
        out_shape=[jax.ShapeDtypeStruct((m, d // 2), I32),
                   jax.ShapeDtypeStruct((m, LANES), I32),
                   jax.ShapeDtypeStruct((m, LANES), F32),
                   jax.ShapeDtypeStruct((8, LANES), F32)],
        scratch_shapes=[pltpu.VMEM((8, LANES), F32)],
        compiler_params=_cparams(("arbitrary",), 40),
        name="rmsnorm_router",
    )(x, g.reshape(1, d), wr, tri)


def _matmul_kernel(x_ref, w_ref, o_ref):
    o_ref[...] = _dot(x_ref[...], w_ref[...]).astype(o_ref.dtype)


def matmul(x, w, out_dtype, tm=1024, tn=1024):
    m, k = x.shape
    n = w.shape[1]
    tm, tn = min(tm, m), min(tn, n)
    return pl.pallas_call(
        _matmul_kernel,
        grid=(m // tm, n // tn),
        in_specs=[pl.BlockSpec((tm, k), lambda i, j: (i, 0)),
                  pl.BlockSpec((k, tn), lambda i, j: (0, j))],
        out_specs=pl.BlockSpec((tm, tn), lambda i, j: (i, j)),
        out_shape=jax.ShapeDtypeStruct((m, n), out_dtype),
        compiler_params=_cparams(("parallel", "parallel"), 48),
        name="matmul",
    )(x, w)


def _attn_prep_kernel(q_ref, kvc_ref, pi_ref, qg_ref, kg_ref, cg_ref, wiq_ref, lng_ref, lnb_ref,
                      qn_ref, kn_ref, vt_ref, qi_ref, ki_ref, wt_ref):
    def head_rms(x, g):
        x = x.astype(F32)
        return x * lax.rsqrt(jnp.mean(x * x, axis=-1, keepdims=True) + EPS) * g

    att_scale = HEAD_DIM ** -0.5 * LOG2E
    for h in range(ATT_HEADS):
        sl = slice(h * HEAD_DIM, (h + 1) * HEAD_DIM)
        qn_ref[:, sl] = (head_rms(q_ref[:, sl], qg_ref[...]) * att_scale).astype(BF16)
    for g in range(ATT_KV_HEADS):
        sl = slice(g * HEAD_DIM, (g + 1) * HEAD_DIM)
        kn_ref[:, sl] = head_rms(kvc_ref[:, sl], kg_ref[...]).astype(BF16)
    for c in range(vt_ref.shape[0]):
        vc = kvc_ref[c * TK:(c + 1) * TK, KV_W:2 * KV_W].astype(F32)
        vt_ref[c] = vc.T.astype(BF16)
    cq = kvc_ref[:, 2 * KV_W:2 * KV_W + IDX_Q_RANK]
    cqn = head_rms(cq, cg_ref[...]).astype(BF16)
    qi_ref[...] = _dot(cqn, wiq_ref[...]).astype(BF16)
    pi = pi_ref[...]
    is_key = lax.broadcasted_iota(I32, pi.shape, 1) < IDX_HEAD_DIM
    mu = jnp.sum(jnp.where(is_key, pi, 0.0), axis=-1, keepdims=True) * (1.0 / IDX_HEAD_DIM)
    cen = jnp.where(is_key, pi - mu, 0.0)
    var = jnp.sum(cen * cen, axis=-1, keepdims=True) * (1.0 / IDX_HEAD_DIM)
    kidn = cen * lax.rsqrt(var + EPS) * lng_ref[...] + lnb_ref[...]
    ki_ref[:, 0:LANES] = kidn.astype(BF16)
    ki_ref[:, LANES:2 * LANES] = pltpu.roll(kidn, IDX_HEAD_DIM, 1).astype(BF16)
    wt = (pi * (IDX_HEAD_DIM ** -0.5 * IDX_HEADS ** -0.5)).T
    wt_ref[0] = wt[IDX_HEAD_DIM:IDX_HEAD_DIM + IDX_HEADS, :]


def attn_prep(proj_main, proj_idx, q_norm_g, k_norm_g, idx_q_norm_g, w_idx_q, ln_g, ln_b, seq, tm=512):
    m = proj_main.shape[0]
    tm = min(tm, seq)
    per_seq = seq // tm
    row = lambda i: (i, 0)
    const = lambda i: (0, 0)
    lane_pad = lambda a: jnp.zeros((1, LANES), F32).at[0, :a.shape[0]].set(a)
    return pl.pallas_call(
        _attn_prep_kernel,
        grid=(m // tm,),
        in_specs=[pl.BlockSpec((tm, 1024), lambda i: (i, COL_Q)),
                  pl.BlockSpec((tm, 1024), lambda i: (i, COL_KVC)),
                  pl.BlockSpec((tm, LANES), row),
                  pl.BlockSpec((1, HEAD_DIM), const),
                  pl.BlockSpec((1, HEAD_DIM), const),
                  pl.BlockSpec((1, IDX_Q_RANK), const),
                  pl.BlockSpec((IDX_Q_RANK, IDX_HEADS * IDX_HEAD_DIM), const),
                  pl.BlockSpec((1, LANES), const),
                  pl.BlockSpec((1, LANES), const)],
        out_specs=[pl.BlockSpec((tm, ATT_W), row),
                   pl.BlockSpec((tm, KV_W), row),
                   pl.BlockSpec((tm // TK, KV_W, TK), lambda i: (i, 0, 0)),
                   pl.BlockSpec((tm, IDX_HEADS * IDX_HEAD_DIM), row),
                   pl.BlockSpec((tm, 2 * LANES), row),
                   pl.BlockSpec((1, IDX_HEADS, tm), lambda i: (i // per_seq, 0, i % per_seq))],
        out_shape=[jax.ShapeDtypeStruct((m, ATT_W), BF16),
                   jax.ShapeDtypeStruct((m, KV_W), BF16),
                   jax.ShapeDtypeStruct((m // TK, KV_W, TK), BF16),
                   jax.ShapeDtypeStruct((m, IDX_HEADS * IDX_HEAD_DIM), BF16),
                   jax.ShapeDtypeStruct((m, 2 * LANES), BF16),
                   jax.ShapeDtypeStruct((m // seq, IDX_HEADS, seq), F32)],
        compiler_params=_cparams(("parallel",), 40),
        name="attn_prep",
    )(proj_main, proj_main, proj_idx, q_norm_g.reshape(1, -1), k_norm_g.reshape(1, -1),
      idx_q_norm_g.reshape(1, -1), w_idx_q, lane_pad(ln_g), lane_pad(ln_b))


def _rel_bucket(dist):
    max_exact = REL_BUCKETS // 2
    n = jnp.maximum(dist, 0)
    nf = jnp.maximum(n, 1).astype(F32)
    large = max_exact + (jnp.log(nf / max_exact) / math.log(REL_MAX_DIST / max_exact)
                         * (REL_BUCKETS - max_exact)).astype(I32)
    large = jnp.minimum(large, REL_BUCKETS - 1)
    return jnp.where(n < max_exact, n, large)


def _bias_kernel(rb_ref, bkt_ref, o_ref):
    h = pl.program_id(0)
    for kind in range(3):
        b = bkt_ref[kind]
        acc = jnp.zeros(b.shape, F32)
        for n in range(REL_BUCKETS):
            acc = jnp.where(b == n, rb_ref[n, h], acc)
        o_ref[0, kind] = acc * LOG2E


def bias_tables(rel_bias):
    assert TK >= REL_MAX_DIST
    kpos = jnp.arange(TK, dtype=I32)[:, None]
    qpos = jnp.arange(TQ, dtype=I32)[None, :]
    dist = jnp.stack([qpos - kpos, qpos - kpos + TK, qpos - kpos + 2 * TK])
    bkt = _rel_bucket(dist)
    return pl.pallas_call(
        _bias_kernel,
        grid=(ATT_HEADS,),
        in_specs=[pl.BlockSpec(memory_space=pltpu.SMEM),
                  pl.BlockSpec((3, TK, TQ), lambda h: (0, 0, 0))],
        out_specs=pl.BlockSpec((1, 3, TK, TQ), lambda h: (h, 0, 0, 0)),
        out_shape=jax.ShapeDtypeStruct((ATT_HEADS, 3, TK, TQ), F32),
        compiler_params=_cparams(("arbitrary",), 32),
        name="bias_tables",
    )(rel_bias, bkt)


def _side_cast_specs(src, split_axis, n_split, n_steps, step_of):
    ne = src.shape[0]
    n_blocks = ne * n_split
    every = n_steps // n_blocks
    assert every >= 1, "not enough grid steps to walk the weight stack"
    block = [1, src.shape[1], src.shape[2]]
    block[split_axis] //= n_split

    def index_map(*grid_idx):
        t = jnp.minimum(step_of(*grid_idx) // every, n_blocks - 1)
        idx = [t // n_split, 0, 0]
        idx[split_axis] = t % n_split
        return tuple(idx)

    spec = pl.BlockSpec(tuple(block), index_map)
    return spec, spec, jax.ShapeDtypeStruct(src.shape, BF16)


def _dsa_kernel(qi_ref, wt_ref, qn_ref, ki_ref, kn_ref, vt_ref, bias_ref, *rest, n_sel, side_cast):
    if side_cast:
        wf_ref, o_ref, wb_ref, key_ref, m_ref, l_ref, a_ref, acc_ref, s_ref, p_ref = rest
        wb_ref[...] = wf_ref[...].astype(BF16)
    else:
        o_ref, key_ref, m_ref, l_ref, a_ref, acc_ref, s_ref, p_ref = rest
    i = pl.program_id(1)
    nch = i + 1
    qpos = i * TQ + lax.broadcasted_iota(I32, (TK, TQ), 1)

    def score_chunk(j, carry):
        k0 = pl.multiple_of(j * TK, TK)
        acc = jnp.zeros((TK, TQ), F32)
        for h in range(IDX_HEADS):
            par = h % 2
            kc = ki_ref[pl.ds(k0, TK), par * LANES:(par + 1) * LANES]
            qh = qi_ref[:, (h // 2) * LANES:(h // 2 + 1) * LANES]
            s = _dot_nt(kc, qh)
            acc = acc + jnp.maximum(s, 0.0) * wt_ref[0, h:h + 1, :]
        bits = lax.bitcast_convert_type(acc, I32)
        key = jnp.where(bits < 0, bits ^ jnp.int32(0x7FFFFFFF), bits)
        kpos = k0 + lax.broadcasted_iota(I32, (TK, TQ), 0)
        key_ref[pl.ds(k0, TK), :] = jnp.where(kpos <= qpos, key, INT_MIN)
        return carry

    lax.fori_loop(0, nch, score_chunk, 0)

    def count_ge(cand):
        def body(j, acc):
            k0 = pl.multiple_of(j * TK, TK)
            hit = jnp.where(key_ref[pl.ds(k0, TK), :] >= cand, 1, 0).astype(I32)
            return acc + jnp.sum(hit.reshape(TK // 8, 8, TQ), axis=0)
        acc = lax.fori_loop(0, nch, body, jnp.zeros((8, TQ), I32))
        return jnp.sum(acc, axis=0, keepdims=True)

    thr = jnp.where(count_ge(jnp.zeros((1, TQ), I32)) >= n_sel, 0, INT_MIN).astype(I32)

    def bit_body(b, thr):
        cand = thr | jnp.left_shift(jnp.int32(1), 30 - b)
        return jnp.where(count_ge(cand) >= n_sel, cand, thr)

    thr = lax.fori_loop(0, 31, bit_body, thr)
    thr = jnp.maximum(thr, INT_MIN + 1)

    m_ref[...] = jnp.full(m_ref.shape, NEG_BIG, F32)
    l_ref[...] = jnp.zeros(l_ref.shape, F32)
    acc_ref[...] = jnp.zeros(acc_ref.shape, F32)

    def att_chunk(j, carry):
        k0 = pl.multiple_of(j * TK, TK)
        neg = jnp.where(key_ref[pl.ds(k0, TK), :] >= thr, 0.0, NEG_BIG)
        kind = jnp.minimum(i - j, 2)
        for h in range(ATT_HEADS):
            g = h // ATT_GROUP
            kc = kn_ref[pl.ds(k0, TK), g * HEAD_DIM:(g + 1) * HEAD_DIM]
            qh = qn_ref[:, h * HEAD_DIM:(h + 1) * HEAD_DIM]
            lg = _dot_nt(kc, qh) + bias_ref[h, kind] + neg
            s_ref[h] = lg
            m_old = m_ref[h:h + 1, :]
            m_new = jnp.maximum(m_old, jnp.max(lg, axis=0, keepdims=True))
            a_ref[h:h + 1, :] = jnp.exp2(m_old - m_new)
            m_ref[h:h + 1, :] = m_new
        for h in range(ATT_HEADS):
            p = jnp.exp2(s_ref[h] - m_ref[h:h + 1, :])
            l_ref[h:h + 1, :] = (a_ref[h:h + 1, :] * l_ref[h:h + 1, :]
                                 + jnp.sum(p, axis=0, keepdims=True))
            p_ref[h] = p.astype(BF16)
        for h in range(ATT_HEADS):
            g = h // ATT_GROUP
            vc = vt_ref[0, j, g * HEAD_DIM:(g + 1) * HEAD_DIM, :]
            acc_ref[h] = a_ref[h:h + 1, :] * acc_ref[h] + _dot(vc, p_ref[h])
        return carry

    lax.fori_loop(0, nch, att_chunk, 0)
    for h in range(ATT_HEADS):
        o = acc_ref[h] * (1.0 / l_ref[h:h + 1, :])
        o_ref[:, h * HEAD_DIM:(h + 1) * HEAD_DIM] = o.T.astype(o_ref.dtype)


def dsa_attention(qi, wt, qn, ki, kn, vt, bias, batch, seq, side_cast=None):
    nq = seq // TQ
    n_sel = min(TOPK_MAX, seq // 4)
    qrow = lambda b, i: (b * nq + i, 0)
    brow = lambda b, i: (b, 0)
    in_specs = [pl.BlockSpec((TQ, IDX_HEADS * IDX_HEAD_DIM), qrow),
                pl.BlockSpec((1, IDX_HEADS, TQ), lambda b, i: (b, 0, i)),
                pl.BlockSpec((TQ, ATT_W), qrow),
                pl.BlockSpec((seq, 2 * LANES), brow),
                pl.BlockSpec((seq, KV_W), brow),
                pl.BlockSpec((1, seq // TK, KV_W, TK), lambda b, i: (b, 0, 0, 0)),
                pl.BlockSpec((ATT_HEADS, 3, TK, TQ), lambda b, i: (0, 0, 0, 0))]
    out_specs = [pl.BlockSpec((TQ, ATT_W), qrow)]
    out_shape = [jax.ShapeDtypeStruct((batch * seq, ATT_W), BF16)]
    operands = [qi, wt, qn, ki, kn, vt, bias]
    if side_cast is not None:
        src, split_axis, n_split = side_cast
        spec_in, spec_out, shape_out = _side_cast_specs(src, split_axis, n_split, batch * nq,
                                                        lambda b, i: b * nq + i)
        in_specs.append(spec_in)
        out_specs.append(spec_out)
        out_shape.append(shape_out)
        operands.append(src)
    return pl.pallas_call(
        functools.partial(_dsa_kernel, n_sel=n_sel, side_cast=side_cast is not None),
        grid=(batch, nq),
        in_specs=in_specs,
        out_specs=out_specs,
        out_shape=out_shape,
        scratch_shapes=[pltpu.VMEM((seq, TQ), I32),
                        pltpu.VMEM((ATT_HEADS, TQ), F32),
                        pltpu.VMEM((ATT_HEADS, TQ), F32),
                        pltpu.VMEM((ATT_HEADS, TQ), F32),
                        pltpu.VMEM((ATT_HEADS, HEAD_DIM, TQ), F32),
                        pltpu.VMEM((ATT_HEADS, TK, TQ), F32),
                        pltpu.VMEM((ATT_HEADS, TK, TQ), BF16)],
        compiler_params=_cparams(("arbitrary", "arbitrary"), 56),
        name="dsa_attention",
    )(*operands)


def _hgrn_constants():
    t = np.arange(HC)
    rows = []
    masks = [np.eye(HC, dtype=np.float32)]
    for m in HGRN_LEVELS:
        upper = (t // m) % 2 == 1
        start = (t // m) * m
        end = start + m - 1
        u = t[None, :]
        q_side = upper[:, None] & (u >= start[:, None]) & (u <= t[:, None])
        k_side = (~upper)[:, None] & (u > t[:, None]) & (u <= end[:, None])
        rows.append((q_side | k_side).astype(np.float32))
        same = (t[:, None] // (2 * m)) == (t[None, :] // (2 * m))
        masks.append((upper[:, None] & (~upper)[None, :] & same).astype(np.float32))
    u = t[None, :]
    rows.append((u <= t[:, None]).astype(np.float32))
    rows.append((u > t[:, None]).astype(np.float32))
    mat = np.concatenate(rows, axis=0)
    return np.concatenate([mat, mat], axis=1), np.stack(masks)


def _hgrn_kernel(hq_ref, hf_ref, hi_ref, hg_ref, lb_ref, gn_ref, mat_ref, mask_ref, *rest,
                 side_cast):
    if side_cast:
        wf_ref, o_ref, wb_ref, st_ref, arg_ref, kk_ref, lf_ref, zq_ref, zk_ref = rest
        wb_ref[...] = wf_ref[...].astype(BF16)
    else:
        o_ref, st_ref, arg_ref, kk_ref, lf_ref, zq_ref, zk_ref = rest
    c = pl.program_id(1)

    @pl.when(c == 0)
    def _():
        st_ref[...] = jnp.zeros_like(st_ref)

    lb = lb_ref[...]
    f = hf_ref[...].astype(F32)
    e = jnp.exp(-jnp.abs(f))
    r = 1.0 / (1.0 + e)
    log_sig = jnp.minimum(f, 0.0) - jnp.log(1.0 + e)
    la = jnp.log(lb)
    lc = jnp.log(1.0 - lb) + log_sig
    logf = (jnp.maximum(la, lc) + jnp.log(1.0 + jnp.exp(-jnp.abs(la - lc)))) * LOG2E
    kk_ref[...] = (1.0 - lb) * jnp.where(f >= 0, e * r, r)
    hi = logf.astype(BF16)
    lf_ref[0:HC, :] = hi
    lf_ref[HC:2 * HC, :] = (logf - hi.astype(F32)).astype(BF16)
    arg_ref[...] = _dot(mat_ref[...], lf_ref[...])

    scale = HGRN_DIM ** -0.5
    nl = len(HGRN_LEVELS)
    for h in range(HGRN_HEADS):
        sl = slice(h * HGRN_DIM, (h + 1) * HGRN_DIM)
        hq = hq_ref[:, sl].astype(F32)
        q = hq * _sigmoid(hq) * scale
        k = kk_ref[:, sl]
        zq_ref[h, 0] = q.astype(BF16)
        zk_ref[h, 0] = k.astype(BF16)
        for lv in range(nl):
            ex = jnp.exp2(arg_ref[lv * HC:(lv + 1) * HC, sl])
            zq_ref[h, lv + 1] = (q * ex).astype(BF16)
            zk_ref[h, lv + 1] = (k * ex).astype(BF16)
        zq_ref[h, nl + 1] = (q * jnp.exp2(arg_ref[nl * HC:(nl + 1) * HC, sl])).astype(BF16)
        zk_ref[h, nl + 1] = (k * jnp.exp2(arg_ref[(nl + 1) * HC:(nl + 2) * HC, sl])).astype(BF16)

    for h in range(HGRN_HEADS):
        sl = slice(h * HGRN_DIM, (h + 1) * HGRN_DIM)
        a = _dot_nt(zq_ref[h, 0], zk_ref[h, 0]) * mask_ref[0]
        for lv in range(nl):
            a = a + _dot_nt(zq_ref[h, lv + 1], zk_ref[h, lv + 1]) * mask_ref[lv + 1]
        v = hi_ref[:, sl].astype(BF16)
        st = st_ref[h]
        o = _dot(a.astype(BF16), v) + _dot_nt(zq_ref[h, nl + 1], st.astype(BF16))
        decay = jnp.exp2(arg_ref[(nl + 1) * HC - 1:(nl + 1) * HC, sl])
        st_ref[h] = decay * st + _dot(v.astype(F32).T.astype(BF16), zk_ref[h, nl + 1])
        on = o * lax.rsqrt(jnp.mean(o * o, axis=-1, keepdims=True) + EPS) * gn_ref[...]
        hg = hg_ref[:, sl].astype(F32)
        o_ref[:, sl] = (on * hg * _sigmoid(hg)).astype(o_ref.dtype)


def hgrn2(proj_main, lb, g_norm, batch, seq, side_cast=None):
    nc = seq // HC
    mat, masks = _hgrn_constants()
    col = lambda cb: (lambda b, c: (b * nc + c, cb))
    in_specs = [pl.BlockSpec((HC, HGRN_W), col(COL_HQ)),
                pl.BlockSpec((HC, HGRN_W), col(COL_HF)),
                pl.BlockSpec((HC, HGRN_W), col(COL_HI)),
                pl.BlockSpec((HC, HGRN_W), col(COL_HG)),
                pl.BlockSpec((1, HGRN_W), lambda b, c: (0, 0)),
                pl.BlockSpec((1, HGRN_DIM), lambda b, c: (0, 0)),
                pl.BlockSpec((N_ARG_GROUPS * HC, 2 * HC), lambda b, c: (0, 0)),
                pl.BlockSpec((len(HGRN_LEVELS) + 1, HC, HC), lambda b, c: (0, 0, 0))]
    operands = [proj_main, proj_main, proj_main, proj_main, lb.reshape(1, -1), g_norm.reshape(1, -1),
                jnp.asarray(mat, BF16), jnp.asarray(masks, F32)]
    out_specs = [pl.BlockSpec((HC, HGRN_W), lambda b, c: (b * nc + c, 0))]
    out_shape = [jax.ShapeDtypeStruct((batch * seq, HGRN_W), BF16)]
    if side_cast is not None:
        src, split_axis, n_split = side_cast
        spec_in, spec_out, shape_out = _side_cast_specs(src, split_axis, n_split, batch * nc,
                                                        lambda b, c: b * nc + c)
        in_specs.append(spec_in)
        operands.append(src)
        out_specs.append(spec_out)
        out_shape.append(shape_out)
    return pl.pallas_call(
        functools.partial(_hgrn_kernel, side_cast=side_cast is not None),
        grid=(batch, nc),
        in_specs=in_specs,
        out_specs=out_specs,
        out_shape=out_shape,
        scratch_shapes=[pltpu.VMEM((HGRN_HEADS, HGRN_DIM, HGRN_DIM), F32),
                        pltpu.VMEM((N_ARG_GROUPS * HC, HGRN_W), F32),
                        pltpu.VMEM((HC, HGRN_W), F32),
                        pltpu.VMEM((2 * HC, HGRN_W), BF16),
                        pltpu.VMEM((HGRN_HEADS, N_ARG_GROUPS, HC, HGRN_DIM), BF16),
                        pltpu.VMEM((HGRN_HEADS, N_ARG_GROUPS, HC, HGRN_DIM), BF16)],
        compiler_params=_cparams(("arbitrary", "arbitrary"), 48),
        name="hgrn2",
    )(*operands)


def _merge_out_kernel(att_ref, rec_ref, ga_ref, gh_ref, x_ref, wa_ref, wh_ref, wo_ref, g_ref,
                      h_ref, *maybe_hn_ref):
    j = pl.program_id(1)

    @pl.when(j == 0)
    def _():
        h_ref[...] = x_ref[...]

    a = _dot(att_ref[...], wa_ref[...])
    r = _dot(rec_ref[...], wh_ref[...])
    merged = (_sigmoid(ga_ref[...].astype(F32)) * a + _sigmoid(gh_ref[...].astype(F32)) * r).astype(BF16)
    h_ref[...] += _dot(merged, wo_ref[...])

    if maybe_hn_ref:
        @pl.when(j == pl.num_programs(1) - 1)
        def _():
            h = h_ref[...]
            ms = jnp.mean(h * h, axis=-1, keepdims=True)
            maybe_hn_ref[0][...] = (h * lax.rsqrt(ms + EPS) * g_ref[...]).astype(BF16)


def merge_out(att, rec, proj_main, x, w_up_att, w_up_hgrn, w_o, norm_g, emit_hn, tm=512, tn=1024):
    m, d = x.shape
    ga0, gh0 = COL_GA * 1024 // tn, COL_GH * 1024 // tn
    row = lambda i, j: (i, 0)
    out_specs = [pl.BlockSpec((tm, d), row)]
    out_shape = [jax.ShapeDtypeStruct((m, d), F32)]
    if emit_hn:
        out_specs.append(pl.BlockSpec((tm, d), row))
        out_shape.append(jax.ShapeDtypeStruct((m, d), BF16))
    return pl.pallas_call(
        _merge_out_kernel,
        grid=(m // tm, d // tn),
        in_specs=[pl.BlockSpec((tm, ATT_W), row),
                  pl.BlockSpec((tm, HGRN_W), row),
                  pl.BlockSpec((tm, tn), lambda i, j: (i, ga0 + j)),
                  pl.BlockSpec((tm, tn), lambda i, j: (i, gh0 + j)),
                  pl.BlockSpec((tm, d), row),
                  pl.BlockSpec((ATT_W, tn), lambda i, j: (0, j)),
                  pl.BlockSpec((HGRN_W, tn), lambda i, j: (0, j)),
                  pl.BlockSpec((tn, d), lambda i, j: (j, 0)),
                  pl.BlockSpec((1, d), lambda i, j: (0, 0))],
        out_specs=out_specs,
        out_shape=out_shape,
        compiler_params=_cparams(("parallel", "arbitrary"), 48),
        name="merge_out",
    )(att, rec, proj_main, proj_main, x, w_up_att, w_up_hgrn, w_o, norm_g.reshape(1, d))


def _ffn_kernel(hn_ref, h_ref, w1_ref, w3_ref, w2_ref, g_ref, o_ref, *maybe_xn_ref):
    j = pl.program_id(1)

    @pl.when(j == 0)
    def _():
        o_ref[...] = h_ref[...]

    x = hn_ref[...]
    a = _dot(x, w1_ref[...])
    b = _dot(x, w3_ref[...])
    act = (a * _sigmoid(a) * b).astype(BF16)
    o_ref[...] += _dot(act, w2_ref[...])

    if maybe_xn_ref:
        @pl.when(j == pl.num_programs(1) - 1)
        def _():
            y = o_ref[...]
            ms = jnp.mean(y * y, axis=-1, keepdims=True)
            maybe_xn_ref[0][...] = (y * lax.rsqrt(ms + EPS) * g_ref[...]).astype(BF16)


def ffn_dense(hn, h, w1, w3, w2, next_norm_g, emit_xn, tm=512, tf=512):
    m, d = hn.shape
    f = w1.shape[1]
    row = lambda i, j: (i, 0)
    out_specs = [pl.BlockSpec((tm, d), row)]
    out_shape = [jax.ShapeDtypeStruct((m, d), F32)]
    if emit_xn:
        out_specs.append(pl.BlockSpec((tm, d), row))
        out_shape.append(jax.ShapeDtypeStruct((m, d), BF16))
    return pl.pallas_call(
        _ffn_kernel,
        grid=(m // tm, f // tf),
        in_specs=[pl.BlockSpec((tm, d), row),
                  pl.BlockSpec((tm, d), row),
                  pl.BlockSpec((d, tf), lambda i, j: (0, j)),
                  pl.BlockSpec((d, tf), lambda i, j: (0, j)),
                  pl.BlockSpec((tf, d), lambda i, j: (j, 0)),
                  pl.BlockSpec((1, d), lambda i, j: (0, 0))],
        out_specs=out_specs,
        out_shape=out_shape,
        compiler_params=_cparams(("parallel", "arbitrary"), 52),
        name="ffn_dense",
    )(hn, h, w1, w3, w2, next_norm_g.reshape(1, d))


def _row_copy(src_ref, src_row, dst_ref, dst_row, sem):
    return pltpu.make_async_copy(src_ref.at[pl.ds(src_row, 1)], dst_ref.at[pl.ds(dst_row, 1)], sem)


def _moe_scatter_kernel(dest_ref, hn_ref, xs_init_ref, xs_ref, sem, *, tm, m):
    del xs_init_ref
    base = pl.program_id(0) * tm

    def issue(r, carry):
        for k in range(2):
            _row_copy(hn_ref, r, xs_ref, dest_ref[k * m + base + r], sem).start()
        return carry

    lax.fori_loop(0, tm, issue, 0, unroll=8)
    for k in range(2):
        pltpu.make_async_copy(hn_ref, xs_ref.at[pl.ds(0, tm)], sem).wait()


def moe_scatter(hn_packed, dest, n_rows, tm=512):
    m, c = hn_packed.shape
    xs_init = jnp.zeros((n_rows, c), I32)
    return pl.pallas_call(
        functools.partial(_moe_scatter_kernel, tm=tm, m=m),
        grid_spec=pltpu.PrefetchScalarGridSpec(
            num_scalar_prefetch=1,
            grid=(m // tm,),
            in_specs=[pl.BlockSpec((tm, c), lambda i, dest: (i, 0)),
                      pl.BlockSpec(memory_space=pl.ANY)],
            out_specs=pl.BlockSpec(memory_space=pl.ANY),
            scratch_shapes=[pltpu.SemaphoreType.DMA(())]),
        out_shape=jax.ShapeDtypeStruct((n_rows, c), I32),
        input_output_aliases={2: 0},
        compiler_params=_cparams(("arbitrary",), 32),
        name="moe_scatter",
    )(dest, hn_packed, xs_init)


def _moe_ffn_kernel(te_ref, tv_ref, xs_ref, w1_ref, w3_ref, w2_ref, o_ref, xb_ref):
    del te_ref
    i = pl.program_id(0)
    half = xs_ref.shape[1]

    @pl.when(pl.program_id(1) == 0)
    def _():
        o_ref[...] = jnp.zeros_like(o_ref)
        lo, hi = _unpack_bf16_pairs(xs_ref[...])
        xb_ref[:, 0:half] = lo
        xb_ref[:, half:2 * half] = hi

    @pl.when(tv_ref[i] == 1)
    def _():
        x = xb_ref[...]
        a = _dot(x, w1_ref[0])
        b = _dot(x, w3_ref[0])
        act = (a * _sigmoid(a) * b).astype(BF16)
        o_ref[...] += _dot(act, w2_ref[0])


def moe_ffn(xs, tile_expert, tile_valid, w1, w3, w2, tm=512, tf=1024):
    p, c = xs.shape
    ne, d, f = w1.shape
    nf = f // tf
    col = lambda i, j, te, tv: jnp.where(tv[i] == 1, j, nf - 1)
    return pl.pallas_call(
        _moe_ffn_kernel,
        grid_spec=pltpu.PrefetchScalarGridSpec(
            num_scalar_prefetch=2,
            grid=(p // tm, nf),
            in_specs=[pl.BlockSpec((tm, c), lambda i, j, te, tv: (i, 0)),
                      pl.BlockSpec((1, d, tf), lambda i, j, te, tv: (te[i], 0, col(i, j, te, tv))),
                      pl.BlockSpec((1, d, tf), lambda i, j, te, tv: (te[i], 0, col(i, j, te, tv))),
                      pl.BlockSpec((1, tf, d), lambda i, j, te, tv: (te[i], col(i, j, te, tv), 0))],
            out_specs=pl.BlockSpec((tm, d), lambda i, j, te, tv: (i, 0)),
            scratch_shapes=[pltpu.VMEM((tm, d), BF16)]),
        out_shape=jax.ShapeDtypeStruct((p, d), F32),
        compiler_params=_cparams(("arbitrary", "arbitrary"), 56),
        name="moe_ffn",
    )(tile_expert, tile_valid, xs, w1, w3, w2)


def _moe_combine_kernel(dest_ref, h_ref, prob_ref, ys_ref, o_ref, ybuf_ref, sem, *, tm, m):
    i = pl.program_id(0)
    slot = i % 2

    def gather_tile(tile, slot_):
        base = tile * tm

        def issue(r, carry):
            for k in range(2):
                _row_copy(ys_ref, dest_ref[k * m + base + r], ybuf_ref.at[slot_, k], r,
                          sem.at[slot_]).start()
            return carry

        lax.fori_loop(0, tm, issue, 0, unroll=8)

    @pl.when(i == 0)
    def _():
        gather_tile(0, 0)

    @pl.when(i + 1 < pl.num_programs(0))
    def _():
        gather_tile(i + 1, 1 - slot)

    for k in range(2):
        pltpu.make_async_copy(ys_ref.at[pl.ds(0, tm)], ybuf_ref.at[slot, k], sem.at[slot]).wait()
    prob = prob_ref[...]
    o_ref[...] = h_ref[...] + prob[:, 0:1] * ybuf_ref[slot, 0] + prob[:, 1:2] * ybuf_ref[slot, 1]


def moe_combine(h, prob, ys, dest, tm=512):
    m, d = h.shape
    return pl.pallas_call(
        functools.partial(_moe_combine_kernel, tm=tm, m=m),
        grid_spec=pltpu.PrefetchScalarGridSpec(
            num_scalar_prefetch=1,
            grid=(m // tm,),
            in_specs=[pl.BlockSpec((tm, d), lambda i, dest: (i, 0)),
                      pl.BlockSpec((tm, LANES), lambda i, dest: (i, 0)),
                      pl.BlockSpec(memory_space=pl.ANY)],
            out_specs=pl.BlockSpec((tm, d), lambda i, dest: (i, 0)),
            scratch_shapes=[pltpu.VMEM((2, 2, tm, d), F32), pltpu.SemaphoreType.DMA((2,))]),
        out_shape=jax.ShapeDtypeStruct((m, d), F32),
        compiler_params=_cparams(("arbitrary",), 48),
        name="moe_combine",
    )(dest, h, prob, ys)


def ffn_moe(h, norm_g, w_router, w1, w3, w2, tm=512):
    m, d = h.shape
    ne = w1.shape[0]
    hn_packed, meta, prob, cnt = rmsnorm_router(h, norm_g, w_router)
    counts = cnt[0, :ne].astype(I32)
    tiles_per = (counts + tm - 1) // tm
    tile_end = jnp.cumsum(tiles_per)
    offset = (tile_end - tiles_per) * tm
    experts = jnp.arange(ne, dtype=I32)

    def group_offset(e):
        return jnp.sum(jnp.where(e[:, None] == experts[None, :], offset[None, :], 0), axis=1)

    dest = jnp.concatenate([group_offset(meta[:, 0]) + meta[:, 2],
                            group_offset(meta[:, 1]) + meta[:, 3]])
    n_tiles = (2 * m) // tm + ne
    tile_ids = jnp.arange(n_tiles, dtype=I32)
    tile_valid = (tile_ids < tile_end[-1]).astype(I32)
    last_valid = jnp.minimum(tile_ids, tile_end[-1] - 1)
    tile_expert = jnp.sum((tile_end[None, :] <= last_valid[:, None]).astype(I32), axis=1)
    xs = moe_scatter(hn_packed, dest, n_tiles * tm)
    ys = moe_ffn(xs, tile_expert, tile_valid, w1, w3, w2, tm=tm)
    return moe_combine(h, prob, ys, dest)


def _w_in_prep_kernel(w_ref, main_ref, idx_ref):
    head = ATT_W + 2 * KV_W + IDX_Q_RANK
    small = IDX_HEAD_DIM + IDX_HEADS
    main_ref[:, 0:head] = w_ref[:, 0:head].astype(BF16)
    group = w_ref[:, head:head + LANES]
    lane = lax.broadcasted_iota(I32, group.shape, 1)
    idx_ref[...] = jnp.where(lane < small, group, 0.0).astype(BF16)
    step = 1024
    for c in range((MAIN_W - head) // step):
        src = head + small + c * step
        main_ref[:, head + c * step:head + (c + 1) * step] = w_ref[:, src:src + step].astype(BF16)


def _split_w_in(w_in, tr=256):
    d, n = w_in.shape
    assert n == MAIN_W + IDX_HEAD_DIM + IDX_HEADS
    return pl.pallas_call(
        _w_in_prep_kernel,
        grid=(d // tr,),
        in_specs=[pl.BlockSpec((tr, n), lambda i: (i, 0))],
        out_specs=[pl.BlockSpec((tr, MAIN_W), lambda i: (i, 0)),
                   pl.BlockSpec((tr, LANES), lambda i: (i, 0))],
        out_shape=[jax.ShapeDtypeStruct((d, MAIN_W), BF16),
                   jax.ShapeDtypeStruct((d, LANES), BF16)],
        compiler_params=_cparams(("parallel",), 48),
        name="w_in_prep",
    )(w_in)


def _mixer(x2, maybe_xn, batch, seq, norm_g, w_in, q_norm_g, k_norm_g, idx_q_norm_g, w_idx_q, ln_g,
           ln_b, lb, hgrn_norm_g, w_up_att, w_up_hgrn, w_o, bias, norm_ffn_g, emit_hn,
           side_cast_att=None, side_cast_rec=None):
    xn = rmsnorm_rows(x2, norm_g) if maybe_xn is None else maybe_xn
    w_main, w_idx = _split_w_in(w_in)
    proj_main = matmul(xn, w_main, BF16)
    proj_idx = matmul(xn, w_idx, F32)
    qn, kn, vt, qi, ki, wt = attn_prep(proj_main, proj_idx, q_norm_g, k_norm_g, idx_q_norm_g,
                                       w_idx_q.astype(BF16), ln_g, ln_b, seq)
    vt = vt.reshape(batch, seq // TK, KV_W, TK)
    att, *cast_att = dsa_attention(qi, wt, qn, ki, kn, vt, bias, batch, seq, side_cast_att)
    rec, *cast_rec = hgrn2(proj_main, lb, hgrn_norm_g, batch, seq, side_cast_rec)
    outs = merge_out(att, rec, proj_main, x2, w_up_att.astype(BF16), w_up_hgrn.astype(BF16),
                     w_o.astype(BF16), norm_ffn_g, emit_hn)
    return outs, (cast_att[0] if cast_att else None), (cast_rec[0] if cast_rec else None)


def kernel(x, rel_bias, norm_mix_g, norm_ffn_g, w_in, q_norm_g, k_norm_g, idx_q_norm_g, w_idx_q,
           idx_k_ln_g, idx_k_ln_b, hgrn_lb_logits, hgrn_out_norm_g, w_up_att, w_up_hgrn, w_o,
           w1_dense, w3_dense, w2_dense, w_router, w1_moe, w3_moe, w2_moe):
    batch, seq, d = x.shape
    depth = w_in.shape[0]
    lb_all = jnp.cumsum(jax.nn.softmax(hgrn_lb_logits.astype(F32), axis=0), axis=0)
    lb_all = lb_all - lb_all[0:1]
    bias = bias_tables(rel_bias)
    x2 = x.reshape(batch * seq, d)
    maybe_xn = None
    n_split_ff = w1_moe.shape[-1] // 1024
    moe_bf16 = {}
    for l in range(depth):
        dense = l % 2 == 0
        j = l // 2
        side_att = side_rec = None
        if dense and l + 1 < depth:
            side_att = (w3_moe[j], 2, n_split_ff)
            side_rec = (w1_moe[j], 2, 2 * n_split_ff)
        elif not dense:
            side_att = (w2_moe[j], 1, n_split_ff)
        (h, *maybe_hn), cast_att, cast_rec = _mixer(
            x2, maybe_xn, batch, seq, norm_mix_g[l], w_in[l], q_norm_g[l], k_norm_g[l],
            idx_q_norm_g[l], w_idx_q[l], idx_k_ln_g[l], idx_k_ln_b[l], lb_all[l],
            hgrn_out_norm_g[l], w_up_att[l], w_up_hgrn[l], w_o[l], bias, norm_ffn_g[l],
            emit_hn=dense, side_cast_att=side_att, side_cast_rec=side_rec)
        maybe_xn = None
        if dense:
            more = l + 1 < depth
            moe_bf16["w3"], moe_bf16["w1"] = cast_att, cast_rec
            x2, *rest = ffn_dense(maybe_hn[0], h, w1_dense[j].astype(BF16), w3_dense[j].astype(BF16),
                                  w2_dense[j].astype(BF16), norm_mix_g[l + 1 if more else l], more)
            maybe_xn = rest[0] if more else None
        else:
            x2 = ffn_moe(h, norm_ffn_g[l], w_router[j], moe_bf16["w1"], moe_bf16["w3"], cast_att)
    return x2.reshape(batch, seq, d)
```

```python
import functools
import math

import numpy as np
import jax
import jax.numpy as jnp
from jax import lax
from jax.experimental import pallas as pl
from jax.experimental.pallas import tpu as pltpu

F32 = jnp.float32
BF16 = jnp.bfloat16
I32 = jnp.int32

EPS = 1e-6
ATT_HEADS = 8
ATT_KV_HEADS = 2
ATT_GROUP = ATT_HEADS // ATT_KV_HEADS
HEAD_DIM = 128
ATT_W = ATT_HEADS * HEAD_DIM
KV_W = ATT_KV_HEADS * HEAD_DIM
IDX_HEADS = 16
IDX_HEAD_DIM = 64
IDX_Q_RANK = 512
TOPK_MAX = 256
HGRN_HEADS = 8
HGRN_DIM = 128
HGRN_W = HGRN_HEADS * HGRN_DIM
REL_BUCKETS = 32
REL_MAX_DIST = 128
N_EXPERTS = 8
LANES = 128
INT_MIN = -(2 ** 31)
NEG_BIG = -1e30
LOG2E = math.log2(math.e)

TQ = 256
TK = 256
HC = 128
HGRN_LEVELS = (1, 2, 4, 8, 16, 32, 64)
N_ARG_GROUPS = len(HGRN_LEVELS) + 2

COL_Q, COL_KVC, COL_HQ, COL_HF, COL_HI, COL_HG, COL_GA, COL_GH = 0, 1, 2, 3, 4, 5, 6, 8
MAIN_W = 10 * 1024


def _cparams(sem, vmem_mb):
    return pltpu.CompilerParams(dimension_semantics=sem, vmem_limit_bytes=vmem_mb << 20)


def _dot(a, b):
    return jnp.dot(a, b, preferred_element_type=F32)


def _dot_nt(a, b):
    return lax.dot_general(a, b, (((1,), (1,)), ((), ())), preferred_element_type=F32)


def _sigmoid(x):
    return 1.0 / (1.0 + jnp.exp(-x))


def _pack_bf16_pairs(x):
    c = x.shape[1] // 2
    lo = lax.bitcast_convert_type(x[:, :c].astype(BF16).astype(F32), I32)
    hi = lax.bitcast_convert_type(x[:, c:].astype(BF16).astype(F32), I32)
    return (hi & jnp.int32(-65536)) | lax.shift_right_logical(lo, 16)


def _unpack_bf16_pairs(p):
    lo = lax.bitcast_convert_type(lax.shift_left(p, 16), F32).astype(BF16)
    hi = lax.bitcast_convert_type(p & jnp.int32(-65536), F32).astype(BF16)
    return lo, hi


def _rmsnorm_router_kernel(x_ref, g_ref, wr_ref, tri_ref, o_ref, meta_ref, prob_ref, cnt_ref, carry_ref):
    @pl.when(pl.program_id(0) == 0)
    def _():
        carry_ref[...] = jnp.zeros_like(carry_ref)

    x = x_ref[...]
    ms = jnp.mean(x * x, axis=-1, keepdims=True)
    hn = x * lax.rsqrt(ms + EPS) * g_ref[...]
    o_ref[...] = _pack_bf16_pairs(hn)
    logits = _dot(hn.astype(BF16), wr_ref[...])
    lane = lax.broadcasted_iota(I32, logits.shape, 1)
    logits = jnp.where(lane < N_EXPERTS, logits, -jnp.inf)
    v1 = jnp.max(logits, axis=-1, keepdims=True)
    i1 = jnp.min(jnp.where(logits == v1, lane, LANES), axis=-1, keepdims=True)
    rest = jnp.where(lane == i1, -jnp.inf, logits)
    v2 = jnp.max(rest, axis=-1, keepdims=True)
    i2 = jnp.min(jnp.where(rest == v2, lane, LANES), axis=-1, keepdims=True)
    e = jnp.exp(v2 - v1)
    p1 = 1.0 / (1.0 + e)
    prob_ref[...] = jnp.where(lane == 0, p1, 0.0) + jnp.where(lane == 1, e * p1, 0.0)
    hot = jnp.where((lane == i1) | (lane == i2), 1.0, 0.0)
    rank = _dot(tri_ref[...], hot.astype(BF16)) + carry_ref[0:1, :]
    carry_ref[...] = carry_ref[...] + jnp.sum(hot, axis=0, keepdims=True)
    cnt_ref[...] = carry_ref[...]
    r1 = jnp.sum(jnp.where(lane == i1, rank, 0.0), axis=-1, keepdims=True)
    r2 = jnp.sum(jnp.where(lane == i2, rank, 0.0), axis=-1, keepdims=True)
    meta = (jnp.where(lane == 0, i1, 0) + jnp.where(lane == 1, i2, 0)
            + jnp.where(lane == 2, r1.astype(I32), 0) + jnp.where(lane == 3, r2.astype(I32), 0))
    meta_ref[...] = meta


def rmsnorm_router(x, g, w_router, tm=512):
    m, d = x.shape
    wr = jnp.zeros((d, LANES), BF16).at[:, :N_EXPERTS].set(w_router.astype(BF16))
    tri = jnp.asarray(np.tril(np.ones((tm, tm), np.float32), -1), BF16)
    return pl.pallas_call(
        _rmsnorm_router_kernel,
        grid=(m // tm,),
        in_specs=[pl.BlockSpec((tm, d), lambda i: (i, 0)),
                  pl.BlockSpec((1, d), lambda i: (0, 0)),
                  pl.BlockSpec((d, LANES), lambda i: (0, 0)),
                  pl.BlockSpec((tm, tm), lambda i: (0, 0))],
        out_specs=[pl.BlockSpec((tm, d // 2), lambda i: (i, 0)),
                   pl.BlockSpec((tm, LANES), lambda i: (i, 0)),
                   pl.BlockSpec((tm, LANES), lambda i: (i, 0)),
                   pl.BlockSpec((8, LANES), lambda i: (0, 0))],
        out_shape=[jax.ShapeDtypeStruct((m, d // 2), I32),
                   jax.ShapeDtypeStruct((m, LANES), I32),
                   jax.ShapeDtypeStruct((m, LANES), F32),
                   jax.ShapeDtypeStruct((8, LANES), F32)],
        scratch_shapes=[pltpu.VMEM((8, LANES), F32)],
        compiler_params=_cparams(("arbitrary",), 40),
        name="rmsnorm_router",
    )(x, g.reshape(1, d), wr, tri)


def _matmul_kernel(x_ref, w_ref, o_ref):
    o_ref[...] = _dot(x_ref[...], w_ref[...]).astype(o_ref.dtype)


def _rmsnorm_matmul_kernel(x_ref, g_ref, w_ref, o_ref, xn_o_ref, xn_ref):
    @pl.when(pl.program_id(1) == 0)
    def _():
        x = x_ref[...]
        ms = jnp.mean(x * x, axis=-1, keepdims=True)
        xn_ref[...] = (x * lax.rsqrt(ms + EPS) * g_ref[...]).astype(BF16)
        xn_o_ref[...] = xn_ref[...]

    o_ref[...] = _dot(xn_ref[...], w_ref[...]).astype(o_ref.dtype)


def rmsnorm_matmul(x, g, w, out_dtype, tm=1024, tn=1024):
    m, k = x.shape
    n = w.shape[1]
    tm, tn = min(tm, m), min(tn, n)
    return pl.pallas_call(
        _rmsnorm_matmul_kernel,
        grid=(m // tm, n // tn),
        in_specs=[pl.BlockSpec((tm, k), lambda i, j: (i, 0)),
                  pl.BlockSpec((1, k), lambda i, j: (0, 0)),
                  pl.BlockSpec((k, tn), lambda i, j: (0, j))],
        out_specs=[pl.BlockSpec((tm, tn), lambda i, j: (i, j)),
                   pl.BlockSpec((tm, k), lambda i, j: (i, 0))],
        out_shape=[jax.ShapeDtypeStruct((m, n), out_dtype),
                   jax.ShapeDtypeStruct((m, k), BF16)],
        scratch_shapes=[pltpu.VMEM((tm, k), BF16)],
        compiler_params=_cparams(("parallel", "arbitrary"), 48),
        name="rmsnorm_matmul",
    )(x, g.reshape(1, k), w)


def matmul(x, w, out_dtype, tm=1024, tn=1024):
    m, k = x.shape
    n = w.shape[1]
    tm, tn = min(tm, m), min(tn, n)
    return pl.pallas_call(
        _matmul_kernel,
        grid=(m // tm, n // tn),
        in_specs=[pl.BlockSpec((tm, k), lambda i, j: (i, 0)),
                  pl.BlockSpec((k, tn), lambda i, j: (0, j))],
        out_specs=pl.BlockSpec((tm, tn), lambda i, j: (i, j)),
        out_shape=jax.ShapeDtypeStruct((m, n), out_dtype),
        compiler_params=_cparams(("parallel", "parallel"), 48),
        name="matmul",
    )(x, w)


def _attn_prep_kernel(q_ref, kvc_ref, pi_ref, qg_ref, kg_ref, cg_ref, wiq_ref, lng_ref, lnb_ref,
                      qn_ref, kn_ref, vt_ref, qi_ref, ki_ref, wt_ref):
    def head_rms(x, g):
        x = x.astype(F32)
        return x * lax.rsqrt(jnp.mean(x * x, axis=-1, keepdims=True) + EPS) * g

    att_scale = HEAD_DIM ** -0.5 * LOG2E
    for h in range(ATT_HEADS):
        sl = slice(h * HEAD_DIM, (h + 1) * HEAD_DIM)
        qn_ref[:, sl] = (head_rms(q_ref[:, sl], qg_ref[...]) * att_scale).astype(BF16)
    for g in range(ATT_KV_HEADS):
        sl = slice(g * HEAD_DIM, (g + 1) * HEAD_DIM)
        kn_ref[:, sl] = head_rms(kvc_ref[:, sl], kg_ref[...]).astype(BF16)
    for c in range(vt_ref.shape[0]):
        vc = kvc_ref[c * TK:(c + 1) * TK, KV_W:2 * KV_W].astype(F32)
        vt_ref[c] = vc.T.astype(BF16)
    cq = kvc_ref[:, 2 * KV_W:2 * KV_W + IDX_Q_RANK]
    cqn = head_rms(cq, cg_ref[...]).astype(BF16)
    qi_ref[...] = _dot(cqn, wiq_ref[...]).astype(BF16)
    pi = pi_ref[...]
    is_key = lax.broadcasted_iota(I32, pi.shape, 1) < IDX_HEAD_DIM
    mu = jnp.sum(jnp.where(is_key, pi, 0.0), axis=-1, keepdims=True) * (1.0 / IDX_HEAD_DIM)
    cen = jnp.where(is_key, pi - mu, 0.0)
    var = jnp.sum(cen * cen, axis=-1, keepdims=True) * (1.0 / IDX_HEAD_DIM)
    kidn = cen * lax.rsqrt(var + EPS) * lng_ref[...] + lnb_ref[...]
    ki_ref[:, 0:LANES] = kidn.astype(BF16)
    ki_ref[:, LANES:2 * LANES] = pltpu.roll(kidn, IDX_HEAD_DIM, 1).astype(BF16)
    wt = (pi * (IDX_HEAD_DIM ** -0.5 * IDX_HEADS ** -0.5)).T
    wt_ref[0] = wt[IDX_HEAD_DIM:IDX_HEAD_DIM + IDX_HEADS, :]


def attn_prep(proj_main, proj_idx, q_norm_g, k_norm_g, idx_q_norm_g, w_idx_q, ln_g, ln_b, seq, tm=512):
    m = proj_main.shape[0]
    tm = min(tm, seq)
    per_seq = seq // tm
    row = lambda i: (i, 0)
    const = lambda i: (0, 0)
    lane_pad = lambda a: jnp.zeros((1, LANES), F32).at[0, :a.shape[0]].set(a)
    return pl.pallas_call(
        _attn_prep_kernel,
        grid=(m // tm,),
        in_specs=[pl.BlockSpec((tm, 1024), lambda i: (i, COL_Q)),
                  pl.BlockSpec((tm, 1024), lambda i: (i, COL_KVC)),
                  pl.BlockSpec((tm, LANES), row),
                  pl.BlockSpec((1, HEAD_DIM), const),
                  pl.BlockSpec((1, HEAD_DIM), const),
                  pl.BlockSpec((1, IDX_Q_RANK), const),
                  pl.BlockSpec((IDX_Q_RANK, IDX_HEADS * IDX_HEAD_DIM), const),
                  pl.BlockSpec((1, LANES), const),
                  pl.BlockSpec((1, LANES), const)],
        out_specs=[pl.BlockSpec((tm, ATT_W), row),
                   pl.BlockSpec((tm, KV_W), row),
                   pl.BlockSpec((tm // TK, KV_W, TK), lambda i: (i, 0, 0)),
                   pl.BlockSpec((tm, IDX_HEADS * IDX_HEAD_DIM), row),
                   pl.BlockSpec((tm, 2 * LANES), row),
                   pl.BlockSpec((1, IDX_HEADS, tm), lambda i: (i // per_seq, 0, i % per_seq))],
        out_shape=[jax.ShapeDtypeStruct((m, ATT_W), BF16),
                   jax.ShapeDtypeStruct((m, KV_W), BF16),
                   jax.ShapeDtypeStruct((m // TK, KV_W, TK), BF16),
                   jax.ShapeDtypeStruct((m, IDX_HEADS * IDX_HEAD_DIM), BF16),
                   jax.ShapeDtypeStruct((m, 2 * LANES), BF16),
                   jax.ShapeDtypeStruct((m // seq, IDX_HEADS, seq), F32)],
        compiler_params=_cparams(("parallel",), 40),
        name="attn_prep",
    )(proj_main, proj_main, proj_idx, q_norm_g.reshape(1, -1), k_norm_g.reshape(1, -1),
      idx_q_norm_g.reshape(1, -1), w_idx_q, lane_pad(ln_g), lane_pad(ln_b))


def _rel_bucket(dist):
    max_exact = REL_BUCKETS // 2
    n = jnp.maximum(dist, 0)
    nf = jnp.maximum(n, 1).astype(F32)
    large = max_exact + (jnp.log(nf / max_exact) / math.log(REL_MAX_DIST / max_exact)
                         * (REL_BUCKETS - max_exact)).astype(I32)
    large = jnp.minimum(large, REL_BUCKETS - 1)
    return jnp.where(n < max_exact, n, large)


def _bias_kernel(rb_ref, bkt_ref, o_ref):
    h = pl.program_id(0)
    for kind in range(3):
        b = bkt_ref[kind]
        acc = jnp.zeros(b.shape, F32)
        for n in range(REL_BUCKETS):
            acc = jnp.where(b == n, rb_ref[n, h], acc)
        o_ref[0, kind] = acc * LOG2E


def bias_tables(rel_bias):
    assert TK >= REL_MAX_DIST
    kpos = jnp.arange(TK, dtype=I32)[:, None]
    qpos = jnp.arange(TQ, dtype=I32)[None, :]
    dist = jnp.stack([qpos - kpos, qpos - kpos + TK, qpos - kpos + 2 * TK])
    bkt = _rel_bucket(dist)
    return pl.pallas_call(
        _bias_kernel,
        grid=(ATT_HEADS,),
        in_specs=[pl.BlockSpec(memory_space=pltpu.SMEM),
                  pl.BlockSpec((3, TK, TQ), lambda h: (0, 0, 0))],
        out_specs=pl.BlockSpec((1, 3, TK, TQ), lambda h: (h, 0, 0, 0)),
        out_shape=jax.ShapeDtypeStruct((ATT_HEADS, 3, TK, TQ), F32),
        compiler_params=_cparams(("arbitrary",), 32),
        name="bias_tables",
    )(rel_bias, bkt)


def _side_cast_specs(src, split_axis, n_split, n_steps, step_of):
    ne = src.shape[0]
    n_blocks = ne * n_split
    every = n_steps // n_blocks
    assert every >= 1, "not enough grid steps to walk the weight stack"
    block = [1, src.shape[1], src.shape[2]]
    block[split_axis] //= n_split

    def index_map(*grid_idx):
        t = jnp.minimum(step_of(*grid_idx) // every, n_blocks - 1)
        idx = [t // n_split, 0, 0]
        idx[split_axis] = t % n_split
        return tuple(idx)

    spec = pl.BlockSpec(tuple(block), index_map)
    return spec, spec, jax.ShapeDtypeStruct(src.shape, BF16)


def _dsa_kernel(qi_ref, wt_ref, qn_ref, ki_ref, kn_ref, vt_ref, bias_ref, *rest, n_sel, side_cast):
    if side_cast:
        wf_ref, o_ref, wb_ref, key_ref, m_ref, l_ref, a_ref, acc_ref, s_ref, p_ref = rest
        wb_ref[...] = wf_ref[...].astype(BF16)
    else:
        o_ref, key_ref, m_ref, l_ref, a_ref, acc_ref, s_ref, p_ref = rest
    i = pl.program_id(1)
    nch = i + 1
    qpos = i * TQ + lax.broadcasted_iota(I32, (TK, TQ), 1)

    def score_chunk(j, carry):
        k0 = pl.multiple_of(j * TK, TK)
        acc = jnp.zeros((TK, TQ), F32)
        for h in range(IDX_HEADS):
            par = h % 2
            kc = ki_ref[pl.ds(k0, TK), par * LANES:(par + 1) * LANES]
            qh = qi_ref[:, (h // 2) * LANES:(h // 2 + 1) * LANES]
            s = _dot_nt(kc, qh)
            acc = acc + jnp.maximum(s, 0.0) * wt_ref[0, h:h + 1, :]
        bits = lax.bitcast_convert_type(acc, I32)
        key = jnp.where(bits < 0, bits ^ jnp.int32(0x7FFFFFFF), bits)
        kpos = k0 + lax.broadcasted_iota(I32, (TK, TQ), 0)
        key_ref[pl.ds(k0, TK), :] = jnp.where(kpos <= qpos, key, INT_MIN)
        return carry

    lax.fori_loop(0, nch, score_chunk, 0)

    def count_ge(cand):
        def body(j, acc):
            k0 = pl.multiple_of(j * TK, TK)
            hit = jnp.where(key_ref[pl.ds(k0, TK), :] >= cand, 1, 0).astype(I32)
            return acc + jnp.sum(hit.reshape(TK // 8, 8, TQ), axis=0)
        acc = lax.fori_loop(0, nch, body, jnp.zeros((8, TQ), I32))
        return jnp.sum(acc, axis=0, keepdims=True)

    thr = jnp.where(count_ge(jnp.zeros((1, TQ), I32)) >= n_sel, 0, INT_MIN).astype(I32)

    def bit_body(b, thr):
        cand = thr | jnp.left_shift(jnp.int32(1), 30 - b)
        return jnp.where(count_ge(cand) >= n_sel, cand, thr)

    thr = lax.fori_loop(0, 31, bit_body, thr)
    thr = jnp.maximum(thr, INT_MIN + 1)

    m_ref[...] = jnp.full(m_ref.shape, NEG_BIG, F32)
    l_ref[...] = jnp.zeros(l_ref.shape, F32)
    acc_ref[...] = jnp.zeros(acc_ref.shape, F32)

    def att_chunk(j, carry):
        k0 = pl.multiple_of(j * TK, TK)
        neg = jnp.where(key_ref[pl.ds(k0, TK), :] >= thr, 0.0, NEG_BIG)
        kind = jnp.minimum(i - j, 2)
        for h in range(ATT_HEADS):
            g = h // ATT_GROUP
            kc = kn_ref[pl.ds(k0, TK), g * HEAD_DIM:(g + 1) * HEAD_DIM]
            qh = qn_ref[:, h * HEAD_DIM:(h + 1) * HEAD_DIM]
            lg = _dot_nt(kc, qh) + bias_ref[h, kind] + neg
            s_ref[h] = lg
            m_old = m_ref[h:h + 1, :]
            m_new = jnp.maximum(m_old, jnp.max(lg, axis=0, keepdims=True))
            a_ref[h:h + 1, :] = jnp.exp2(m_old - m_new)
            m_ref[h:h + 1, :] = m_new
        for h in range(ATT_HEADS):
            p = jnp.exp2(s_ref[h] - m_ref[h:h + 1, :])
            l_ref[h:h + 1, :] = (a_ref[h:h + 1, :] * l_ref[h:h + 1, :]
                                 + jnp.sum(p, axis=0, keepdims=True))
            p_ref[h] = p.astype(BF16)
        for h in range(ATT_HEADS):
            g = h // ATT_GROUP
            vc = vt_ref[0, j, g * HEAD_DIM:(g + 1) * HEAD_DIM, :]
            acc_ref[h] = a_ref[h:h + 1, :] * acc_ref[h] + _dot(vc, p_ref[h])
        return carry

    lax.fori_loop(0, nch, att_chunk, 0)
    for h in range(ATT_HEADS):
        o = acc_ref[h] * (1.0 / l_ref[h:h + 1, :])
        o_ref[:, h * HEAD_DIM:(h + 1) * HEAD_DIM] = o.T.astype(o_ref.dtype)


def dsa_attention(qi, wt, qn, ki, kn, vt, bias, batch, seq, side_cast=None):
    nq = seq // TQ
    n_sel = min(TOPK_MAX, seq // 4)
    qrow = lambda b, i: (b * nq + i, 0)
    brow = lambda b, i: (b, 0)
    in_specs = [pl.BlockSpec((TQ, IDX_HEADS * IDX_HEAD_DIM), qrow),
                pl.BlockSpec((1, IDX_HEADS, TQ), lambda b, i: (b, 0, i)),
                pl.BlockSpec((TQ, ATT_W), qrow),
                pl.BlockSpec((seq, 2 * LANES), brow),
                pl.BlockSpec((seq, KV_W), brow),
                pl.BlockSpec((1, seq // TK, KV_W, TK), lambda b, i: (b, 0, 0, 0)),
                pl.BlockSpec((ATT_HEADS, 3, TK, TQ), lambda b, i: (0, 0, 0, 0))]
    out_specs = [pl.BlockSpec((TQ, ATT_W), qrow)]
    out_shape = [jax.ShapeDtypeStruct((batch * seq, ATT_W), BF16)]
    operands = [qi, wt, qn, ki, kn, vt, bias]
    if side_cast is not None:
        src, split_axis, n_split = side_cast
        spec_in, spec_out, shape_out = _side_cast_specs(src, split_axis, n_split, batch * nq,
                                                        lambda b, i: b * nq + i)
        in_specs.append(spec_in)
        out_specs.append(spec_out)
        out_shape.append(shape_out)
        operands.append(src)
    return pl.pallas_call(
        functools.partial(_dsa_kernel, n_sel=n_sel, side_cast=side_cast is not None),
        grid=(batch, nq),
        in_specs=in_specs,
        out_specs=out_specs,
        out_shape=out_shape,
        scratch_shapes=[pltpu.VMEM((seq, TQ), I32),
                        pltpu.VMEM((ATT_HEADS, TQ), F32),
                        pltpu.VMEM((ATT_HEADS, TQ), F32),
                        pltpu.VMEM((ATT_HEADS, TQ), F32),
                        pltpu.VMEM((ATT_HEADS, HEAD_DIM, TQ), F32),
                        pltpu.VMEM((ATT_HEADS, TK, TQ), F32),
                        pltpu.VMEM((ATT_HEADS, TK, TQ), BF16)],
        compiler_params=_cparams(("arbitrary", "arbitrary"), 56),
        name="dsa_attention",
    )(*operands)


def _hgrn_constants():
    t = np.arange(HC)
    rows = []
    masks = [np.eye(HC, dtype=np.float32)]
    for m in HGRN_LEVELS:
        upper = (t // m) % 2 == 1
        start = (t // m) * m
        end = start + m - 1
        u = t[None, :]
        q_side = upper[:, None] & (u >= start[:, None]) & (u <= t[:, None])
        k_side = (~upper)[:, None] & (u > t[:, None]) & (u <= end[:, None])
        rows.append((q_side | k_side).astype(np.float32))
        same = (t[:, None] // (2 * m)) == (t[None, :] // (2 * m))
        masks.append((upper[:, None] & (~upper)[None, :] & same).astype(np.float32))
    u = t[None, :]
    rows.append((u <= t[:, None]).astype(np.float32))
    rows.append((u > t[:, None]).astype(np.float32))
    mat = np.concatenate(rows, axis=0)
    return np.concatenate([mat, mat], axis=1), np.stack(masks)


def _hgrn_kernel(hq_ref, hf_ref, hi_ref, hg_ref, lb_ref, gn_ref, mat_ref, mask_ref, *rest,
                 side_cast):
    if side_cast:
        wf_ref, o_ref, wb_ref, st_ref, arg_ref, kk_ref, lf_ref, zq_ref, zk_ref = rest
        wb_ref[...] = wf_ref[...].astype(BF16)
    else:
        o_ref, st_ref, arg_ref, kk_ref, lf_ref, zq_ref, zk_ref = rest
    c = pl.program_id(1)

    @pl.when(c == 0)
    def _():
        st_ref[...] = jnp.zeros_like(st_ref)

    lb = lb_ref[...]
    f = hf_ref[...].astype(F32)
    e = jnp.exp(-jnp.abs(f))
    r = 1.0 / (1.0 + e)
    log_sig = jnp.minimum(f, 0.0) - jnp.log(1.0 + e)
    la = jnp.log(lb)
    lc = jnp.log(1.0 - lb) + log_sig
    logf = (jnp.maximum(la, lc) + jnp.log(1.0 + jnp.exp(-jnp.abs(la - lc)))) * LOG2E
    kk_ref[...] = (1.0 - lb) * jnp.where(f >= 0, e * r, r)
    hi = logf.astype(BF16)
    lf_ref[0:HC, :] = hi
    lf_ref[HC:2 * HC, :] = (logf - hi.astype(F32)).astype(BF16)
    arg_ref[...] = _dot(mat_ref[...], lf_ref[...])

    scale = HGRN_DIM ** -0.5
    nl = len(HGRN_LEVELS)
    for h in range(HGRN_HEADS):
        sl = slice(h * HGRN_DIM, (h + 1) * HGRN_DIM)
        hq = hq_ref[:, sl].astype(F32)
        q = hq * _sigmoid(hq) * scale
        k = kk_ref[:, sl]
        zq_ref[h, 0] = q.astype(BF16)
        zk_ref[h, 0] = k.astype(BF16)
        for lv in range(nl):
            ex = jnp.exp2(arg_ref[lv * HC:(lv + 1) * HC, sl])
            zq_ref[h, lv + 1] = (q * ex).astype(BF16)
            zk_ref[h, lv + 1] = (k * ex).astype(BF16)
        zq_ref[h, nl + 1] = (q * jnp.exp2(arg_ref[nl * HC:(nl + 1) * HC, sl])).astype(BF16)
        zk_ref[h, nl + 1] = (k * jnp.exp2(arg_ref[(nl + 1) * HC:(nl + 2) * HC, sl])).astype(BF16)

    for h in range(HGRN_HEADS):
        sl = slice(h * HGRN_DIM, (h + 1) * HGRN_DIM)
        a = _dot_nt(zq_ref[h, 0], zk_ref[h, 0]) * mask_ref[0]
        for lv in range(nl):
            a = a + _dot_nt(zq_ref[h, lv + 1], zk_ref[h, lv + 1]) * mask_ref[lv + 1]
        v = hi_ref[:, sl].astype(BF16)
        st = st_ref[h]
        o = _dot(a.astype(BF16), v) + _dot_nt(zq_ref[h, nl + 1], st.astype(BF16))
        decay = jnp.exp2(arg_ref[(nl + 1) * HC - 1:(nl + 1) * HC, sl])
        st_ref[h] = decay * st + _dot(v.astype(F32).T.astype(BF16), zk_ref[h, nl + 1])
        on = o * lax.rsqrt(jnp.mean(o * o, axis=-1, keepdims=True) + EPS) * gn_ref[...]
        hg = hg_ref[:, sl].astype(F32)
        o_ref[:, sl] = (on * hg * _sigmoid(hg)).astype(o_ref.dtype)


def hgrn2(proj_main, lb, g_norm, batch, seq, side_cast=None):
    nc = seq // HC
    mat, masks = _hgrn_constants()
    col = lambda cb: (lambda b, c: (b * nc + c, cb))
    in_specs = [pl.BlockSpec((HC, HGRN_W), col(COL_HQ)),
                pl.BlockSpec((HC, HGRN_W), col(COL_HF)),
                pl.BlockSpec((HC, HGRN_W), col(COL_HI)),
                pl.BlockSpec((HC, HGRN_W), col(COL_HG)),
                pl.BlockSpec((1, HGRN_W), lambda b, c: (0, 0)),
                pl.BlockSpec((1, HGRN_DIM), lambda b, c: (0, 0)),
                pl.BlockSpec((N_ARG_GROUPS * HC, 2 * HC), lambda b, c: (0, 0)),
                pl.BlockSpec((len(HGRN_LEVELS) + 1, HC, HC), lambda b, c: (0, 0, 0))]
    operands = [proj_main, proj_main, proj_main, proj_main, lb.reshape(1, -1), g_norm.reshape(1, -1),
                jnp.asarray(mat, BF16), jnp.asarray(masks, F32)]
    out_specs = [pl.BlockSpec((HC, HGRN_W), lambda b, c: (b * nc + c, 0))]
    out_shape = [jax.ShapeDtypeStruct((batch * seq, HGRN_W), BF16)]
    if side_cast is not None:
        src, split_axis, n_split = side_cast
        spec_in, spec_out, shape_out = _side_cast_specs(src, split_axis, n_split, batch * nc,
                                                        lambda b, c: b * nc + c)
        in_specs.append(spec_in)
        operands.append(src)
        out_specs.append(spec_out)
        out_shape.append(shape_out)
    return pl.pallas_call(
        functools.partial(_hgrn_kernel, side_cast=side_cast is not None),
        grid=(batch, nc),
        in_specs=in_specs,
        out_specs=out_specs,
        out_shape=out_shape,
        scratch_shapes=[pltpu.VMEM((HGRN_HEADS, HGRN_DIM, HGRN_DIM), F32),
                        pltpu.VMEM((N_ARG_GROUPS * HC, HGRN_W), F32),
                        pltpu.VMEM((HC, HGRN_W), F32),
                        pltpu.VMEM((2 * HC, HGRN_W), BF16),
                        pltpu.VMEM((HGRN_HEADS, N_ARG_GROUPS, HC, HGRN_DIM), BF16),
                        pltpu.VMEM((HGRN_HEADS, N_ARG_GROUPS, HC, HGRN_DIM), BF16)],
        compiler_params=_cparams(("arbitrary", "arbitrary"), 48),
        name="hgrn2",
    )(*operands)


def _merge_out_kernel(att_ref, rec_ref, ga_ref, gh_ref, x_ref, wa_ref, wh_ref, wo_ref, g_ref,
                      h_ref, *maybe_hn_ref):
    j = pl.program_id(1)

    @pl.when(j == 0)
    def _():
        h_ref[...] = x_ref[...]

    a = _dot(att_ref[...], wa_ref[...])
    r = _dot(rec_ref[...], wh_ref[...])
    merged = (_sigmoid(ga_ref[...].astype(F32)) * a + _sigmoid(gh_ref[...].astype(F32)) * r).astype(BF16)
    h_ref[...] += _dot(merged, wo_ref[...])

    if maybe_hn_ref:
        @pl.when(j == pl.num_programs(1) - 1)
        def _():
            h = h_ref[...]
            ms = jnp.mean(h * h, axis=-1, keepdims=True)
            maybe_hn_ref[0][...] = (h * lax.rsqrt(ms + EPS) * g_ref[...]).astype(BF16)


def merge_out(att, rec, proj_main, x, w_up_att, w_up_hgrn, w_o, norm_g, emit_hn, tm=512, tn=1024):
    m, d = x.shape
    ga0, gh0 = COL_GA * 1024 // tn, COL_GH * 1024 // tn
    row = lambda i, j: (i, 0)
    out_specs = [pl.BlockSpec((tm, d), row)]
    out_shape = [jax.ShapeDtypeStruct((m, d), F32)]
    if emit_hn:
        out_specs.append(pl.BlockSpec((tm, d), row))
        out_shape.append(jax.ShapeDtypeStruct((m, d), BF16))
    return pl.pallas_call(
        _merge_out_kernel,
        grid=(m // tm, d // tn),
        in_specs=[pl.BlockSpec((tm, ATT_W), row),
                  pl.BlockSpec((tm, HGRN_W), row),
                  pl.BlockSpec((tm, tn), lambda i, j: (i, ga0 + j)),
                  pl.BlockSpec((tm, tn), lambda i, j: (i, gh0 + j)),
                  pl.BlockSpec((tm, d), row),
                  pl.BlockSpec((ATT_W, tn), lambda i, j: (0, j)),
                  pl.BlockSpec((HGRN_W, tn), lambda i, j: (0, j)),
                  pl.BlockSpec((tn, d), lambda i, j: (j, 0)),
                  pl.BlockSpec((1, d), lambda i, j: (0, 0))],
        out_specs=out_specs,
        out_shape=out_shape,
        compiler_params=_cparams(("parallel", "arbitrary"), 48),
        name="merge_out",
    )(att, rec, proj_main, proj_main, x, w_up_att, w_up_hgrn, w_o, norm_g.reshape(1, d))


def _ffn_kernel(hn_ref, h_ref, w1_ref, w3_ref, w2_ref, g_ref, o_ref, *maybe_xn_ref):
    j = pl.program_id(1)

    @pl.when(j == 0)
    def _():
        o_ref[...] = h_ref[...]

    x = hn_ref[...]
    a = _dot(x, w1_ref[...])
    b = _dot(x, w3_ref[...])
    act = (a * _sigmoid(a) * b).astype(BF16)
    o_ref[...] += _dot(act, w2_ref[...])

    if maybe_xn_ref:
        @pl.when(j == pl.num_programs(1) - 1)
        def _():
            y = o_ref[...]
            ms = jnp.mean(y * y, axis=-1, keepdims=True)
            maybe_xn_ref[0][...] = (y * lax.rsqrt(ms + EPS) * g_ref[...]).astype(BF16)


def ffn_dense(hn, h, w1, w3, w2, next_norm_g, emit_xn, tm=512, tf=512):
    m, d = hn.shape
    f = w1.shape[1]
    row = lambda i, j: (i, 0)
    out_specs = [pl.BlockSpec((tm, d), row)]
    out_shape = [jax.ShapeDtypeStruct((m, d), F32)]
    if emit_xn:
        out_specs.append(pl.BlockSpec((tm, d), row))
        out_shape.append(jax.ShapeDtypeStruct((m, d), BF16))
    return pl.pallas_call(
        _ffn_kernel,
        grid=(m // tm, f // tf),
        in_specs=[pl.BlockSpec((tm, d), row),
                  pl.BlockSpec((tm, d), row),
                  pl.BlockSpec((d, tf), lambda i, j: (0, j)),
                  pl.BlockSpec((d, tf), lambda i, j: (0, j)),
                  pl.BlockSpec((tf, d), lambda i, j: (j, 0)),
                  pl.BlockSpec((1, d), lambda i, j: (0, 0))],
        out_specs=out_specs,
        out_shape=out_shape,
        compiler_params=_cparams(("parallel", "arbitrary"), 52),
        name="ffn_dense",
    )(hn, h, w1, w3, w2, next_norm_g.reshape(1, d))


def _row_copy(src_ref, src_row, dst_ref, dst_row, sem):
    return pltpu.make_async_copy(src_ref.at[pl.ds(src_row, 1)], dst_ref.at[pl.ds(dst_row, 1)], sem)


def _moe_scatter_kernel(dest_ref, hn_ref, xs_init_ref, xs_ref, sem, *, tm, m):
    del xs_init_ref
    base = pl.program_id(0) * tm

    def issue(r, carry):
        for k in range(2):
            _row_copy(hn_ref, r, xs_ref, dest_ref[k * m + base + r], sem).start()
        return carry

    lax.fori_loop(0, tm, issue, 0, unroll=8)
    for k in range(2):
        pltpu.make_async_copy(hn_ref, xs_ref.at[pl.ds(0, tm)], sem).wait()


def moe_scatter(hn_packed, dest, n_rows, tm=512):
    m, c = hn_packed.shape
    xs_init = jnp.zeros((n_rows, c), I32)
    return pl.pallas_call(
        functools.partial(_moe_scatter_kernel, tm=tm, m=m),
        grid_spec=pltpu.PrefetchScalarGridSpec(
            num_scalar_prefetch=1,
            grid=(m // tm,),
            in_specs=[pl.BlockSpec((tm, c), lambda i, dest: (i, 0)),
                      pl.BlockSpec(memory_space=pl.ANY)],
            out_specs=pl.BlockSpec(memory_space=pl.ANY),
            scratch_shapes=[pltpu.SemaphoreType.DMA(())]),
        out_shape=jax.ShapeDtypeStruct((n_rows, c), I32),
        input_output_aliases={2: 0},
        compiler_params=_cparams(("arbitrary",), 32),
        name="moe_scatter",
    )(dest, hn_packed, xs_init)


def _moe_ffn_kernel(te_ref, tv_ref, xs_ref, w1_ref, w3_ref, w2_ref, o_ref, xb_ref):
    del te_ref
    i = pl.program_id(0)
    half = xs_ref.shape[1]

    @pl.when(pl.program_id(1) == 0)
    def _():
        o_ref[...] = jnp.zeros_like(o_ref)
        lo, hi = _unpack_bf16_pairs(xs_ref[...])
        xb_ref[:, 0:half] = lo
        xb_ref[:, half:2 * half] = hi

    @pl.when(tv_ref[i] == 1)
    def _():
        x = xb_ref[...]
        a = _dot(x, w1_ref[0])
        b = _dot(x, w3_ref[0])
        act = (a * _sigmoid(a) * b).astype(BF16)
        o_ref[...] += _dot(act, w2_ref[0])


def moe_ffn(xs, tile_expert, tile_valid, w1, w3, w2, tm=512, tf=1024):
    p, c = xs.shape
    ne, d, f = w1.shape
    nf = f // tf
    col = lambda i, j, te, tv: jnp.where(tv[i] == 1, j, nf - 1)
    return pl.pallas_call(
        _moe_ffn_kernel,
        grid_spec=pltpu.PrefetchScalarGridSpec(
            num_scalar_prefetch=2,
            grid=(p // tm, nf),
            in_specs=[pl.BlockSpec((tm, c), lambda i, j, te, tv: (i, 0)),
                      pl.BlockSpec((1, d, tf), lambda i, j, te, tv: (te[i], 0, col(i, j, te, tv))),
                      pl.BlockSpec((1, d, tf), lambda i, j, te, tv: (te[i], 0, col(i, j, te, tv))),
                      pl.BlockSpec((1, tf, d), lambda i, j, te, tv: (te[i], col(i, j, te, tv), 0))],
            out_specs=pl.BlockSpec((tm, d), lambda i, j, te, tv: (i, 0)),
            scratch_shapes=[pltpu.VMEM((tm, d), BF16)]),
        out_shape=jax.ShapeDtypeStruct((p, d), F32),
        compiler_params=_cparams(("arbitrary", "arbitrary"), 56),
        name="moe_ffn",
    )(tile_expert, tile_valid, xs, w1, w3, w2)


def _moe_combine_kernel(dest_ref, h_ref, prob_ref, ys_ref, o_ref, ybuf_ref, sem, *, tm, m):
    i = pl.program_id(0)
    slot = i % 2

    def gather_tile(tile, slot_):
        base = tile * tm

        def issue(r, carry):
            for k in range(2):
                _row_copy(ys_ref, dest_ref[k * m + base + r], ybuf_ref.at[slot_, k], r,
                          sem.at[slot_]).start()
            return carry

        lax.fori_loop(0, tm, issue, 0, unroll=8)

    @pl.when(i == 0)
    def _():
        gather_tile(0, 0)

    @pl.when(i + 1 < pl.num_programs(0))
    def _():
        gather_tile(i + 1, 1 - slot)

    for k in range(2):
        pltpu.make_async_copy(ys_ref.at[pl.ds(0, tm)], ybuf_ref.at[slot, k], sem.at[slot]).wait()
    prob = prob_ref[...]
    o_ref[...] = h_ref[...] + prob[:, 0:1] * ybuf_ref[slot, 0] + prob[:, 1:2] * ybuf_ref[slot, 1]


def moe_combine(h, prob, ys, dest, tm=512):
    m, d = h.shape
    return pl.pallas_call(
        functools.partial(_moe_combine_kernel, tm=tm, m=m),
        grid_spec=pltpu.PrefetchScalarGridSpec(
            num_scalar_prefetch=1,
            grid=(m // tm,),
            in_specs=[pl.BlockSpec((tm, d), lambda i, dest: (i, 0)),
                      pl.BlockSpec((tm, LANES), lambda i, dest: (i, 0)),
                      pl.BlockSpec(memory_space=pl.ANY)],
            out_specs=pl.BlockSpec((tm, d), lambda i, dest: (i, 0)),
            scratch_shapes=[pltpu.VMEM((2, 2, tm, d), F32), pltpu.SemaphoreType.DMA((2,))]),
        out_shape=jax.ShapeDtypeStruct((m, d), F32),
        compiler_params=_cparams(("arbitrary",), 48),
        name="moe_combine",
    )(dest, h, prob, ys)


def ffn_moe(h, norm_g, w_router, w1, w3, w2, tm=512):
    m, d = h.shape
    ne = w1.shape[0]
    hn_packed, meta, prob, cnt = rmsnorm_router(h, norm_g, w_router)
    counts = cnt[0, :ne].astype(I32)
    tiles_per = (counts + tm - 1) // tm
    tile_end = jnp.cumsum(tiles_per)
    offset = (tile_end - tiles_per) * tm
    experts = jnp.arange(ne, dtype=I32)

    def group_offset(e):
        return jnp.sum(jnp.where(e[:, None] == experts[None, :], offset[None, :], 0), axis=1)

    dest = jnp.concatenate([group_offset(meta[:, 0]) + meta[:, 2],
                            group_offset(meta[:, 1]) + meta[:, 3]])
    n_tiles = (2 * m) // tm + ne
    tile_ids = jnp.arange(n_tiles, dtype=I32)
    tile_valid = (tile_ids < tile_end[-1]).astype(I32)
    last_valid = jnp.minimum(tile_ids, tile_end[-1] - 1)
    tile_expert = jnp.sum((tile_end[None, :] <= last_valid[:, None]).astype(I32), axis=1)
    xs = moe_scatter(hn_packed, dest, n_tiles * tm)
    ys = moe_ffn(xs, tile_expert, tile_valid, w1, w3, w2, tm=tm)
    return moe_combine(h, prob, ys, dest)


def _w_in_prep_kernel(w_ref, main_ref, idx_ref):
    head = ATT_W + 2 * KV_W + IDX_Q_RANK
    small = IDX_HEAD_DIM + IDX_HEADS
    main_ref[:, 0:head] = w_ref[:, 0:head].astype(BF16)
    group = w_ref[:, head:head + LANES]
    lane = lax.broadcasted_iota(I32, group.shape, 1)
    idx_ref[...] = jnp.where(lane < small, group, 0.0).astype(BF16)
    step = 1024
    for c in range((MAIN_W - head) // step):
        src = head + small + c * step
        main_ref[:, head + c * step:head + (c + 1) * step] = w_ref[:, src:src + step].astype(BF16)


def _split_w_in(w_in, layer, tr=256):
    _, d, n = w_in.shape
    assert n == MAIN_W + IDX_HEAD_DIM + IDX_HEADS
    return pl.pallas_call(
        _w_in_prep_kernel,
        grid=(d // tr,),
        in_specs=[pl.BlockSpec((None, tr, n), lambda i: (layer, i, 0))],
        out_specs=[pl.BlockSpec((tr, MAIN_W), lambda i: (i, 0)),
                   pl.BlockSpec((tr, LANES), lambda i: (i, 0))],
        out_shape=[jax.ShapeDtypeStruct((d, MAIN_W), BF16),
                   jax.ShapeDtypeStruct((d, LANES), BF16)],
        compiler_params=_cparams(("parallel",), 48),
        name="w_in_prep",
    )(w_in)


def _mixer(x2, maybe_xn, batch, seq, norm_g, w_in, q_norm_g, k_norm_g, idx_q_norm_g, w_idx_q, ln_g,
           ln_b, lb, hgrn_norm_g, w_up_att, w_up_hgrn, w_o, bias, norm_ffn_g, emit_hn,
           side_cast_att=None, side_cast_rec=None):
    w_main, w_idx = _split_w_in(*w_in)
    if maybe_xn is None:
        proj_main, xn = rmsnorm_matmul(x2, norm_g, w_main, BF16)
    else:
        xn = maybe_xn
        proj_main = matmul(xn, w_main, BF16)
    proj_idx = matmul(xn, w_idx, F32)
    qn, kn, vt, qi, ki, wt = attn_prep(proj_main, proj_idx, q_norm_g, k_norm_g, idx_q_norm_g,
                                       w_idx_q.astype(BF16), ln_g, ln_b, seq)
    vt = vt.reshape(batch, seq // TK, KV_W, TK)
    att, *cast_att = dsa_attention(qi, wt, qn, ki, kn, vt, bias, batch, seq, side_cast_att)
    rec, *cast_rec = hgrn2(proj_main, lb, hgrn_norm_g, batch, seq, side_cast_rec)
    outs = merge_out(att, rec, proj_main, x2, w_up_att.astype(BF16), w_up_hgrn.astype(BF16),
                     w_o.astype(BF16), norm_ffn_g, emit_hn)
    return outs, (cast_att[0] if cast_att else None), (cast_rec[0] if cast_rec else None)


def kernel(x, rel_bias, norm_mix_g, norm_ffn_g, w_in, q_norm_g, k_norm_g, idx_q_norm_g, w_idx_q,
           idx_k_ln_g, idx_k_ln_b, hgrn_lb_logits, hgrn_out_norm_g, w_up_att, w_up_hgrn, w_o,
           w1_dense, w3_dense, w2_dense, w_router, w1_moe, w3_moe, w2_moe):
    batch, seq, d = x.shape
    depth = w_in.shape[0]
    lb_all = jnp.cumsum(jax.nn.softmax(hgrn_lb_logits.astype(F32), axis=0), axis=0)
    lb_all = lb_all - lb_all[0:1]
    bias = bias_tables(rel_bias)
    x2 = x.reshape(batch * seq, d)
    maybe_xn = None
    n_split_ff = w1_moe.shape[-1] // 1024
    moe_bf16 = {}
    for l in range(depth):
        dense = l % 2 == 0
        j = l // 2
        side_att = side_rec = None
        if dense and l + 1 < depth:
            side_att = (w3_moe[j], 2, n_split_ff)
            side_rec = (w1_moe[j], 2, 2 * n_split_ff)
        elif not dense:
            side_att = (w2_moe[j], 1, n_split_ff)
        (h, *maybe_hn), cast_att, cast_rec = _mixer(
            x2, maybe_xn, batch, seq, norm_mix_g[l], (w_in, l), q_norm_g[l], k_norm_g[l],
            idx_q_norm_g[l], w_idx_q[l], idx_k_ln_g[l], idx_k_ln_b[l], lb_all[l],
            hgrn_out_norm_g[l], w_up_att[l], w_up_hgrn[l], w_o[l], bias, norm_ffn_g[l],
            emit_hn=dense, side_cast_att=side_att, side_cast_rec=side_rec)
        maybe_xn = None
        if dense:
            more = l + 1 < depth
            moe_bf16["w3"], moe_bf16["w1"] = cast_att, cast_rec
            x2, *rest = ffn_dense(maybe_hn[0], h, w1_dense[j].astype(BF16), w3_dense[j].astype(BF16),
                                  w2_dense[j].astype(BF16), norm_mix_g[l + 1 if more else l], more)
            maybe_xn = rest[0] if more else None
        else:
            x2 = ffn_moe(h, norm_ffn_g[l], w_router[j], moe_bf16["w1"], moe_bf16["w3"], cast_att)
    return x2.reshape(batch, seq, d)
```

```python
import functools
import math

import numpy as np
import jax
import jax.numpy as jnp
from jax import lax
from jax.experimental import pallas as pl
from jax.experimental.pallas import tpu as pltpu

F32 = jnp.float32
BF16 = jnp.bfloat16
I32 = jnp.int32

EPS = 1e-6
ATT_HEADS = 8
ATT_KV_HEADS = 2
ATT_GROUP = ATT_HEADS // ATT_KV_HEADS
HEAD_DIM = 128
ATT_W = ATT_HEADS * HEAD_DIM
KV_W = ATT_KV_HEADS * HEAD_DIM
IDX_HEADS = 16
IDX_HEAD_DIM = 64
IDX_Q_RANK = 512
TOPK_MAX = 256
HGRN_HEADS = 8
HGRN_DIM = 128
HGRN_W = HGRN_HEADS * HGRN_DIM
REL_BUCKETS = 32
REL_MAX_DIST = 128
N_EXPERTS = 8
LANES = 128
INT_MIN = -(2 ** 31)
NEG_BIG = -1e30
LOG2E = math.log2(math.e)

TQ = 256
TK = 256
HC = 128
HGRN_LEVELS = (1, 2, 4, 8, 16, 32, 64)
N_ARG_GROUPS = len(HGRN_LEVELS) + 2

COL_Q, COL_KVC, COL_HQ, COL_HF, COL_HI, COL_HG, COL_GA, COL_GH = 0, 1, 2, 3, 4, 5, 6, 8
MAIN_W = 10 * 1024


def _cparams(sem, vmem_mb):
    return pltpu.CompilerParams(dimension_semantics=sem, vmem_limit_bytes=vmem_mb << 20)


def _dot(a, b):
    return jnp.dot(a, b, preferred_element_type=F32)


def _dot_nt(a, b):
    return lax.dot_general(a, b, (((1,), (1,)), ((), ())), preferred_element_type=F32)


def _sigmoid(x):
    return 1.0 / (1.0 + jnp.exp(-x))


def _pack_bf16_pairs(x):
    c = x.shape[1] // 2
    lo = lax.bitcast_convert_type(x[:, :c].astype(BF16).astype(F32), I32)
    hi = lax.bitcast_convert_type(x[:, c:].astype(BF16).astype(F32), I32)
    return (hi & jnp.int32(-65536)) | lax.shift_right_logical(lo, 16)


def _unpack_bf16_pairs(p):
    lo = lax.bitcast_convert_type(lax.shift_left(p, 16), F32).astype(BF16)
    hi = lax.bitcast_convert_type(p & jnp.int32(-65536), F32).astype(BF16)
    return lo, hi


def _rmsnorm_router_kernel(x_ref, g_ref, wr_ref, tri_ref, o_ref, meta_ref, prob_ref, cnt_ref, carry_ref):
    @pl.when(pl.program_id(0) == 0)
    def _():
        carry_ref[...] = jnp.zeros_like(carry_ref)

    x = x_ref[...]
    ms = jnp.mean(x * x, axis=-1, keepdims=True)
    hn = x * lax.rsqrt(ms + EPS) * g_ref[...]
    o_ref[...] = _pack_bf16_pairs(hn)
    logits = _dot(hn.astype(BF16), wr_ref[...])
    lane = lax.broadcasted_iota(I32, logits.shape, 1)
    logits = jnp.where(lane < N_EXPERTS, logits, -jnp.inf)
    v1 = jnp.max(logits, axis=-1, keepdims=True)
    i1 = jnp.min(jnp.where(logits == v1, lane, LANES), axis=-1, keepdims=True)
    rest = jnp.where(lane == i1, -jnp.inf, logits)
    v2 = jnp.max(rest, axis=-1, keepdims=True)
    i2 = jnp.min(jnp.where(rest == v2, lane, LANES), axis=-1, keepdims=True)
    e = jnp.exp(v2 - v1)
    p1 = 1.0 / (1.0 + e)
    prob_ref[...] = jnp.where(lane == 0, p1, 0.0) + jnp.where(lane == 1, e * p1, 0.0)
    hot = jnp.where((lane == i1) | (lane == i2), 1.0, 0.0)
    rank = _dot(tri_ref[...], hot.astype(BF16)) + carry_ref[0:1, :]
    carry_ref[...] = carry_ref[...] + jnp.sum(hot, axis=0, keepdims=True)
    cnt_ref[...] = carry_ref[...]
    r1 = jnp.sum(jnp.where(lane == i1, rank, 0.0), axis=-1, keepdims=True)
    r2 = jnp.sum(jnp.where(lane == i2, rank, 0.0), axis=-1, keepdims=True)
    meta = (jnp.where(lane == 0, i1, 0) + jnp.where(lane == 1, i2, 0)
            + jnp.where(lane == 2, r1.astype(I32), 0) + jnp.where(lane == 3, r2.astype(I32), 0))
    meta_ref[...] = meta


def rmsnorm_router(x, g, w_router, tm=512):
    m, d = x.shape
    wr = jnp.zeros((d, LANES), BF16).at[:, :N_EXPERTS].set(w_router.astype(BF16))
    tri = jnp.asarray(np.tril(np.ones((tm, tm), np.float32), -1), BF16)
    return pl.pallas_call(
        _rmsnorm_router_kernel,
        grid=(m // tm,),
        in_specs=[pl.BlockSpec((tm, d), lambda i: (i, 0)),
                  pl.BlockSpec((1, d), lambda i: (0, 0)),
                  pl.BlockSpec((d, LANES), lambda i: (0, 0)),
                  pl.BlockSpec((tm, tm), lambda i: (0, 0))],
        out_specs=[pl.BlockSpec((tm, d // 2), lambda i: (i, 0)),
                   pl.BlockSpec((tm, LANES), lambda i: (i, 0)),
                   pl.BlockSpec((tm, LANES), lambda i: (i, 0)),
                   pl.BlockSpec((8, LANES), lambda i: (0, 0))],
        out_shape=[jax.ShapeDtypeStruct((m, d // 2), I32),
                   jax.ShapeDtypeStruct((m, LANES), I32),
                   jax.ShapeDtypeStruct((m, LANES), F32),
                   jax.ShapeDtypeStruct((8, LANES), F32)],
        scratch_shapes=[pltpu.VMEM((8, LANES), F32)],
        compiler_params=_cparams(("arbitrary",), 40),
        name="rmsnorm_router",
    )(x, g.reshape(1, d), wr, tri)


def _matmul_kernel(x_ref, wt_ref, o_ref):
    o_ref[...] = _dot_nt(x_ref[...], wt_ref[...]).astype(o_ref.dtype)


def _rmsnorm_matmul_kernel(x_ref, g_ref, wt_ref, o_ref, xn_o_ref, xn_ref):
    @pl.when(pl.program_id(1) == 0)
    def _():
        x = x_ref[...]
        ms = jnp.mean(x * x, axis=-1, keepdims=True)
        xn_ref[...] = (x * lax.rsqrt(ms + EPS) * g_ref[...]).astype(BF16)
        xn_o_ref[...] = xn_ref[...]

    o_ref[...] = _dot_nt(xn_ref[...], wt_ref[...]).astype(o_ref.dtype)


def rmsnorm_matmul(x, g, wt, out_dtype, tm=1024, tn=1024):
    m, k = x.shape
    n = wt.shape[0]
    tm, tn = min(tm, m), min(tn, n)
    return pl.pallas_call(
        _rmsnorm_matmul_kernel,
        grid=(m // tm, n // tn),
        in_specs=[pl.BlockSpec((tm, k), lambda i, j: (i, 0)),
                  pl.BlockSpec((1, k), lambda i, j: (0, 0)),
                  pl.BlockSpec((tn, k), lambda i, j: (j, 0))],
        out_specs=[pl.BlockSpec((tm, tn), lambda i, j: (i, j)),
                   pl.BlockSpec((tm, k), lambda i, j: (i, 0))],
        out_shape=[jax.ShapeDtypeStruct((m, n), out_dtype),
                   jax.ShapeDtypeStruct((m, k), BF16)],
        scratch_shapes=[pltpu.VMEM((tm, k), BF16)],
        compiler_params=_cparams(("parallel", "arbitrary"), 48),
        name="rmsnorm_matmul",
    )(x, g.reshape(1, k), wt)


def matmul(x, wt, out_dtype, tm=1024, tn=1024):
    m, k = x.shape
    n = wt.shape[0]
    tm, tn = min(tm, m), min(tn, n)
    return pl.pallas_call(
        _matmul_kernel,
        grid=(m // tm, n // tn),
        in_specs=[pl.BlockSpec((tm, k), lambda i, j: (i, 0)),
                  pl.BlockSpec((tn, k), lambda i, j: (j, 0))],
        out_specs=pl.BlockSpec((tm, tn), lambda i, j: (i, j)),
        out_shape=jax.ShapeDtypeStruct((m, n), out_dtype),
        compiler_params=_cparams(("parallel", "parallel"), 48),
        name="matmul",
    )(x, wt)


def _attn_prep_kernel(q_ref, kvc_ref, pi_ref, qg_ref, kg_ref, cg_ref, wiq_ref, lng_ref, lnb_ref,
                      qn_ref, kn_ref, vt_ref, qi_ref, ki_ref, wt_ref):
    def head_rms(x, g):
        x = x.astype(F32)
        return x * lax.rsqrt(jnp.mean(x * x, axis=-1, keepdims=True) + EPS) * g

    att_scale = HEAD_DIM ** -0.5 * LOG2E
    for h in range(ATT_HEADS):
        sl = slice(h * HEAD_DIM, (h + 1) * HEAD_DIM)
        qn_ref[:, sl] = (head_rms(q_ref[:, sl], qg_ref[...]) * att_scale).astype(BF16)
    for g in range(ATT_KV_HEADS):
        sl = slice(g * HEAD_DIM, (g + 1) * HEAD_DIM)
        kn_ref[:, sl] = head_rms(kvc_ref[:, sl], kg_ref[...]).astype(BF16)
    for c in range(vt_ref.shape[0]):
        vc = kvc_ref[c * TK:(c + 1) * TK, KV_W:2 * KV_W].astype(F32)
        vt_ref[c] = vc.T.astype(BF16)
    cq = kvc_ref[:, 2 * KV_W:2 * KV_W + IDX_Q_RANK]
    cqn = head_rms(cq, cg_ref[...]).astype(BF16)
    qi_ref[...] = _dot(cqn, wiq_ref[...]).astype(BF16)
    pi = pi_ref[...]
    is_key = lax.broadcasted_iota(I32, pi.shape, 1) < IDX_HEAD_DIM
    mu = jnp.sum(jnp.where(is_key, pi, 0.0), axis=-1, keepdims=True) * (1.0 / IDX_HEAD_DIM)
    cen = jnp.where(is_key, pi - mu, 0.0)
    var = jnp.sum(cen * cen, axis=-1, keepdims=True) * (1.0 / IDX_HEAD_DIM)
    kidn = cen * lax.rsqrt(var + EPS) * lng_ref[...] + lnb_ref[...]
    ki_ref[:, 0:LANES] = kidn.astype(BF16)
    ki_ref[:, LANES:2 * LANES] = pltpu.roll(kidn, IDX_HEAD_DIM, 1).astype(BF16)
    wt = (pi * (IDX_HEAD_DIM ** -0.5 * IDX_HEADS ** -0.5)).T
    wt_ref[0] = wt[IDX_HEAD_DIM:IDX_HEAD_DIM + IDX_HEADS, :]


def attn_prep(proj_main, proj_idx, q_norm_g, k_norm_g, idx_q_norm_g, w_idx_q, ln_g, ln_b, seq, tm=512):
    m = proj_main.shape[0]
    tm = min(tm, seq)
    per_seq = seq // tm
    row = lambda i: (i, 0)
    const = lambda i: (0, 0)
    lane_pad = lambda a: jnp.zeros((1, LANES), F32).at[0, :a.shape[0]].set(a)
    return pl.pallas_call(
        _attn_prep_kernel,
        grid=(m // tm,),
        in_specs=[pl.BlockSpec((tm, 1024), lambda i: (i, COL_Q)),
                  pl.BlockSpec((tm, 1024), lambda i: (i, COL_KVC)),
                  pl.BlockSpec((tm, LANES), row),
                  pl.BlockSpec((1, HEAD_DIM), const),
                  pl.BlockSpec((1, HEAD_DIM), const),
                  pl.BlockSpec((1, IDX_Q_RANK), const),
                  pl.BlockSpec((IDX_Q_RANK, IDX_HEADS * IDX_HEAD_DIM), const),
                  pl.BlockSpec((1, LANES), const),
                  pl.BlockSpec((1, LANES), const)],
        out_specs=[pl.BlockSpec((tm, ATT_W), row),
                   pl.BlockSpec((tm, KV_W), row),
                   pl.BlockSpec((tm // TK, KV_W, TK), lambda i: (i, 0, 0)),
                   pl.BlockSpec((tm, IDX_HEADS * IDX_HEAD_DIM), row),
                   pl.BlockSpec((tm, 2 * LANES), row),
                   pl.BlockSpec((1, IDX_HEADS, tm), lambda i: (i // per_seq, 0, i % per_seq))],
        out_shape=[jax.ShapeDtypeStruct((m, ATT_W), BF16),
                   jax.ShapeDtypeStruct((m, KV_W), BF16),
                   jax.ShapeDtypeStruct((m // TK, KV_W, TK), BF16),
                   jax.ShapeDtypeStruct((m, IDX_HEADS * IDX_HEAD_DIM), BF16),
                   jax.ShapeDtypeStruct((m, 2 * LANES), BF16),
                   jax.ShapeDtypeStruct((m // seq, IDX_HEADS, seq), F32)],
        compiler_params=_cparams(("parallel",), 40),
        name="attn_prep",
    )(proj_main, proj_main, proj_idx, q_norm_g.reshape(1, -1), k_norm_g.reshape(1, -1),
      idx_q_norm_g.reshape(1, -1), w_idx_q, lane_pad(ln_g), lane_pad(ln_b))


def _rel_bucket(dist):
    max_exact = REL_BUCKETS // 2
    n = jnp.maximum(dist, 0)
    nf = jnp.maximum(n, 1).astype(F32)
    large = max_exact + (jnp.log(nf / max_exact) / math.log(REL_MAX_DIST / max_exact)
                         * (REL_BUCKETS - max_exact)).astype(I32)
    large = jnp.minimum(large, REL_BUCKETS - 1)
    return jnp.where(n < max_exact, n, large)


def _bias_kernel(rb_ref, bkt_ref, o_ref):
    h = pl.program_id(0)
    for kind in range(3):
        b = bkt_ref[kind]
        acc = jnp.zeros(b.shape, F32)
        for n in range(REL_BUCKETS):
            acc = jnp.where(b == n, rb_ref[n, h], acc)
        o_ref[0, kind] = acc * LOG2E


def bias_tables(rel_bias):
    assert TK >= REL_MAX_DIST
    kpos = jnp.arange(TK, dtype=I32)[:, None]
    qpos = jnp.arange(TQ, dtype=I32)[None, :]
    dist = jnp.stack([qpos - kpos, qpos - kpos + TK, qpos - kpos + 2 * TK])
    bkt = _rel_bucket(dist)
    return pl.pallas_call(
        _bias_kernel,
        grid=(ATT_HEADS,),
        in_specs=[pl.BlockSpec(memory_space=pltpu.SMEM),
                  pl.BlockSpec((3, TK, TQ), lambda h: (0, 0, 0))],
        out_specs=pl.BlockSpec((1, 3, TK, TQ), lambda h: (h, 0, 0, 0)),
        out_shape=jax.ShapeDtypeStruct((ATT_HEADS, 3, TK, TQ), F32),
        compiler_params=_cparams(("arbitrary",), 32),
        name="bias_tables",
    )(rel_bias, bkt)


def _side_cast_specs(src, split_axis, n_split, n_steps, step_of):
    ne = src.shape[0]
    n_blocks = ne * n_split
    every = n_steps // n_blocks
    assert every >= 1, "not enough grid steps to walk the weight stack"
    block = [1, src.shape[1], src.shape[2]]
    block[split_axis] //= n_split

    def index_map(*grid_idx):
        t = jnp.minimum(step_of(*grid_idx) // every, n_blocks - 1)
        idx = [t // n_split, 0, 0]
        idx[split_axis] = t % n_split
        return tuple(idx)

    spec = pl.BlockSpec(tuple(block), index_map)
    return spec, spec, jax.ShapeDtypeStruct(src.shape, BF16)


def _dsa_kernel(qi_ref, wt_ref, qn_ref, ki_ref, kn_ref, vt_ref, bias_ref, *rest, n_sel, side_cast):
    if side_cast:
        wf_ref, o_ref, wb_ref, key_ref, m_ref, l_ref, a_ref, acc_ref, s_ref, p_ref = rest
        wb_ref[...] = wf_ref[...].astype(BF16)
    else:
        o_ref, key_ref, m_ref, l_ref, a_ref, acc_ref, s_ref, p_ref = rest
    i = pl.program_id(1)
    nch = i + 1
    qpos = i * TQ + lax.broadcasted_iota(I32, (TK, TQ), 1)

    def score_chunk(j, carry):
        k0 = pl.multiple_of(j * TK, TK)
        acc = jnp.zeros((TK, TQ), F32)
        for h in range(IDX_HEADS):
            par = h % 2
            kc = ki_ref[pl.ds(k0, TK), par * LANES:(par + 1) * LANES]
            qh = qi_ref[:, (h // 2) * LANES:(h // 2 + 1) * LANES]
            s = _dot_nt(kc, qh)
            acc = acc + jnp.maximum(s, 0.0) * wt_ref[0, h:h + 1, :]
        bits = lax.bitcast_convert_type(acc, I32)
        key = jnp.where(bits < 0, bits ^ jnp.int32(0x7FFFFFFF), bits)
        kpos = k0 + lax.broadcasted_iota(I32, (TK, TQ), 0)
        key_ref[pl.ds(k0, TK), :] = jnp.where(kpos <= qpos, key, INT_MIN)
        return carry

    lax.fori_loop(0, nch, score_chunk, 0)

    def count_ge(cand):
        def body(j, acc):
            k0 = pl.multiple_of(j * TK, TK)
            hit = jnp.where(key_ref[pl.ds(k0, TK), :] >= cand, 1, 0).astype(I32)
            return acc + jnp.sum(hit.reshape(TK // 8, 8, TQ), axis=0)
        acc = lax.fori_loop(0, nch, body, jnp.zeros((8, TQ), I32))
        return jnp.sum(acc, axis=0, keepdims=True)

    thr = jnp.where(count_ge(jnp.zeros((1, TQ), I32)) >= n_sel, 0, INT_MIN).astype(I32)

    def bit_body(b, thr):
        cand = thr | jnp.left_shift(jnp.int32(1), 30 - b)
        return jnp.where(count_ge(cand) >= n_sel, cand, thr)

    thr = lax.fori_loop(0, 31, bit_body, thr)
    thr = jnp.maximum(thr, INT_MIN + 1)

    m_ref[...] = jnp.full(m_ref.shape, NEG_BIG, F32)
    l_ref[...] = jnp.zeros(l_ref.shape, F32)
    acc_ref[...] = jnp.zeros(acc_ref.shape, F32)

    def att_chunk(j, carry):
        k0 = pl.multiple_of(j * TK, TK)
        neg = jnp.where(key_ref[pl.ds(k0, TK), :] >= thr, 0.0, NEG_BIG)
        kind = jnp.minimum(i - j, 2)
        for h in range(ATT_HEADS):
            g = h // ATT_GROUP
            kc = kn_ref[pl.ds(k0, TK), g * HEAD_DIM:(g + 1) * HEAD_DIM]
            qh = qn_ref[:, h * HEAD_DIM:(h + 1) * HEAD_DIM]
            lg = _dot_nt(kc, qh) + bias_ref[h, kind] + neg
            s_ref[h] = lg
            m_old = m_ref[h:h + 1, :]
            m_new = jnp.maximum(m_old, jnp.max(lg, axis=0, keepdims=True))
            a_ref[h:h + 1, :] = jnp.exp2(m_old - m_new)
            m_ref[h:h + 1, :] = m_new
        for h in range(ATT_HEADS):
            p = jnp.exp2(s_ref[h] - m_ref[h:h + 1, :])
            l_ref[h:h + 1, :] = (a_ref[h:h + 1, :] * l_ref[h:h + 1, :]
                                 + jnp.sum(p, axis=0, keepdims=True))
            p_ref[h] = p.astype(BF16)
        for h in range(ATT_HEADS):
            g = h // ATT_GROUP
            vc = vt_ref[0, j, g * HEAD_DIM:(g + 1) * HEAD_DIM, :]
            acc_ref[h] = a_ref[h:h + 1, :] * acc_ref[h] + _dot(vc, p_ref[h])
        return carry

    lax.fori_loop(0, nch, att_chunk, 0)
    for h in range(ATT_HEADS):
        o = acc_ref[h] * (1.0 / l_ref[h:h + 1, :])
        o_ref[:, h * HEAD_DIM:(h + 1) * HEAD_DIM] = o.T.astype(o_ref.dtype)


def dsa_attention(qi, wt, qn, ki, kn, vt, bias, batch, seq, side_cast=None):
    nq = seq // TQ
    n_sel = min(TOPK_MAX, seq // 4)
    qrow = lambda b, i: (b * nq + i, 0)
    brow = lambda b, i: (b, 0)
    in_specs = [pl.BlockSpec((TQ, IDX_HEADS * IDX_HEAD_DIM), qrow),
                pl.BlockSpec((1, IDX_HEADS, TQ), lambda b, i: (b, 0, i)),
                pl.BlockSpec((TQ, ATT_W), qrow),
                pl.BlockSpec((seq, 2 * LANES), brow),
                pl.BlockSpec((seq, KV_W), brow),
                pl.BlockSpec((1, seq // TK, KV_W, TK), lambda b, i: (b, 0, 0, 0)),
                pl.BlockSpec((ATT_HEADS, 3, TK, TQ), lambda b, i: (0, 0, 0, 0))]
    out_specs = [pl.BlockSpec((TQ, ATT_W), qrow)]
    out_shape = [jax.ShapeDtypeStruct((batch * seq, ATT_W), BF16)]
    operands = [qi, wt, qn, ki, kn, vt, bias]
    if side_cast is not None:
        src, split_axis, n_split = side_cast
        spec_in, spec_out, shape_out = _side_cast_specs(src, split_axis, n_split, batch * nq,
                                                        lambda b, i: b * nq + i)
        in_specs.append(spec_in)
        out_specs.append(spec_out)
        out_shape.append(shape_out)
        operands.append(src)
    return pl.pallas_call(
        functools.partial(_dsa_kernel, n_sel=n_sel, side_cast=side_cast is not None),
        grid=(batch, nq),
        in_specs=in_specs,
        out_specs=out_specs,
        out_shape=out_shape,
        scratch_shapes=[pltpu.VMEM((seq, TQ), I32),
                        pltpu.VMEM((ATT_HEADS, TQ), F32),
                        pltpu.VMEM((ATT_HEADS, TQ), F32),
                        pltpu.VMEM((ATT_HEADS, TQ), F32),
                        pltpu.VMEM((ATT_HEADS, HEAD_DIM, TQ), F32),
                        pltpu.VMEM((ATT_HEADS, TK, TQ), F32),
                        pltpu.VMEM((ATT_HEADS, TK, TQ), BF16)],
        compiler_params=_cparams(("arbitrary", "arbitrary"), 56),
        name="dsa_attention",
    )(*operands)


def _hgrn_constants():
    t = np.arange(HC)
    rows = []
    masks = [np.eye(HC, dtype=np.float32)]
    for m in HGRN_LEVELS:
        upper = (t // m) % 2 == 1
        start = (t // m) * m
        end = start + m - 1
        u = t[None, :]
        q_side = upper[:, None] & (u >= start[:, None]) & (u <= t[:, None])
        k_side = (~upper)[:, None] & (u > t[:, None]) & (u <= end[:, None])
        rows.append((q_side | k_side).astype(np.float32))
        same = (t[:, None] // (2 * m)) == (t[None, :] // (2 * m))
        masks.append((upper[:, None] & (~upper)[None, :] & same).astype(np.float32))
    u = t[None, :]
    rows.append((u <= t[:, None]).astype(np.float32))
    rows.append((u > t[:, None]).astype(np.float32))
    mat = np.concatenate(rows, axis=0)
    return np.concatenate([mat, mat], axis=1), np.stack(masks)


def _hgrn_kernel(hq_ref, hf_ref, hi_ref, hg_ref, lb_ref, gn_ref, mat_ref, mask_ref, *rest,
                 side_cast):
    if side_cast:
        wf_ref, o_ref, wb_ref, st_ref, arg_ref, kk_ref, lf_ref, zq_ref, zk_ref = rest
        wb_ref[...] = wf_ref[...].astype(BF16)
    else:
        o_ref, st_ref, arg_ref, kk_ref, lf_ref, zq_ref, zk_ref = rest
    c = pl.program_id(1)

    @pl.when(c == 0)
    def _():
        st_ref[...] = jnp.zeros_like(st_ref)

    lb = lb_ref[...]
    f = hf_ref[...].astype(F32)
    e = jnp.exp(-jnp.abs(f))
    r = 1.0 / (1.0 + e)
    log_sig = jnp.minimum(f, 0.0) - jnp.log(1.0 + e)
    la = jnp.log(lb)
    lc = jnp.log(1.0 - lb) + log_sig
    logf = (jnp.maximum(la, lc) + jnp.log(1.0 + jnp.exp(-jnp.abs(la - lc)))) * LOG2E
    kk_ref[...] = (1.0 - lb) * jnp.where(f >= 0, e * r, r)
    hi = logf.astype(BF16)
    lf_ref[0:HC, :] = hi
    lf_ref[HC:2 * HC, :] = (logf - hi.astype(F32)).astype(BF16)
    arg_ref[...] = _dot(mat_ref[...], lf_ref[...])

    scale = HGRN_DIM ** -0.5
    nl = len(HGRN_LEVELS)
    for h in range(HGRN_HEADS):
        sl = slice(h * HGRN_DIM, (h + 1) * HGRN_DIM)
        hq = hq_ref[:, sl].astype(F32)
        q = hq * _sigmoid(hq) * scale
        k = kk_ref[:, sl]
        zq_ref[h, 0] = q.astype(BF16)
        zk_ref[h, 0] = k.astype(BF16)
        for lv in range(nl):
            ex = jnp.exp2(arg_ref[lv * HC:(lv + 1) * HC, sl])
            zq_ref[h, lv + 1] = (q * ex).astype(BF16)
            zk_ref[h, lv + 1] = (k * ex).astype(BF16)
        zq_ref[h, nl + 1] = (q * jnp.exp2(arg_ref[nl * HC:(nl + 1) * HC, sl])).astype(BF16)
        zk_ref[h, nl + 1] = (k * jnp.exp2(arg_ref[(nl + 1) * HC:(nl + 2) * HC, sl])).astype(BF16)

    for h in range(HGRN_HEADS):
        sl = slice(h * HGRN_DIM, (h + 1) * HGRN_DIM)
        a = _dot_nt(zq_ref[h, 0], zk_ref[h, 0]) * mask_ref[0]
        for lv in range(nl):
            a = a + _dot_nt(zq_ref[h, lv + 1], zk_ref[h, lv + 1]) * mask_ref[lv + 1]
        v = hi_ref[:, sl].astype(BF16)
        st = st_ref[h]
        o = _dot(a.astype(BF16), v) + _dot_nt(zq_ref[h, nl + 1], st.astype(BF16))
        decay = jnp.exp2(arg_ref[(nl + 1) * HC - 1:(nl + 1) * HC, sl])
        st_ref[h] = decay * st + _dot(v.astype(F32).T.astype(BF16), zk_ref[h, nl + 1])
        on = o * lax.rsqrt(jnp.mean(o * o, axis=-1, keepdims=True) + EPS) * gn_ref[...]
        hg = hg_ref[:, sl].astype(F32)
        o_ref[:, sl] = (on * hg * _sigmoid(hg)).astype(o_ref.dtype)


def hgrn2(proj_main, lb, g_norm, batch, seq, side_cast=None):
    nc = seq // HC
    mat, masks = _hgrn_constants()
    col = lambda cb: (lambda b, c: (b * nc + c, cb))
    in_specs = [pl.BlockSpec((HC, HGRN_W), col(COL_HQ)),
                pl.BlockSpec((HC, HGRN_W), col(COL_HF)),
                pl.BlockSpec((HC, HGRN_W), col(COL_HI)),
                pl.BlockSpec((HC, HGRN_W), col(COL_HG)),
                pl.BlockSpec((1, HGRN_W), lambda b, c: (0, 0)),
                pl.BlockSpec((1, HGRN_DIM), lambda b, c: (0, 0)),
                pl.BlockSpec((N_ARG_GROUPS * HC, 2 * HC), lambda b, c: (0, 0)),
                pl.BlockSpec((len(HGRN_LEVELS) + 1, HC, HC), lambda b, c: (0, 0, 0))]
    operands = [proj_main, proj_main, proj_main, proj_main, lb.reshape(1, -1), g_norm.reshape(1, -1),
                jnp.asarray(mat, BF16), jnp.asarray(masks, F32)]
    out_specs = [pl.BlockSpec((HC, HGRN_W), lambda b, c: (b * nc + c, 0))]
    out_shape = [jax.ShapeDtypeStruct((batch * seq, HGRN_W), BF16)]
    if side_cast is not None:
        src, split_axis, n_split = side_cast
        spec_in, spec_out, shape_out = _side_cast_specs(src, split_axis, n_split, batch * nc,
                                                        lambda b, c: b * nc + c)
        in_specs.append(spec_in)
        operands.append(src)
        out_specs.append(spec_out)
        out_shape.append(shape_out)
    return pl.pallas_call(
        functools.partial(_hgrn_kernel, side_cast=side_cast is not None),
        grid=(batch, nc),
        in_specs=in_specs,
        out_specs=out_specs,
        out_shape=out_shape,
        scratch_shapes=[pltpu.VMEM((HGRN_HEADS, HGRN_DIM, HGRN_DIM), F32),
                        pltpu.VMEM((N_ARG_GROUPS * HC, HGRN_W), F32),
                        pltpu.VMEM((HC, HGRN_W), F32),
                        pltpu.VMEM((2 * HC, HGRN_W), BF16),
                        pltpu.VMEM((HGRN_HEADS, N_ARG_GROUPS, HC, HGRN_DIM), BF16),
                        pltpu.VMEM((HGRN_HEADS, N_ARG_GROUPS, HC, HGRN_DIM), BF16)],
        compiler_params=_cparams(("arbitrary", "arbitrary"), 48),
        name="hgrn2",
    )(*operands)


def _merge_out_kernel(att_ref, rec_ref, ga_ref, gh_ref, x_ref, wa_ref, wh_ref, wo_ref, g_ref,
                      h_ref, *maybe_hn_ref):
    j = pl.program_id(1)

    @pl.when(j == 0)
    def _():
        h_ref[...] = x_ref[...]

    a = _dot(att_ref[...], wa_ref[...])
    r = _dot(rec_ref[...], wh_ref[...])
    merged = (_sigmoid(ga_ref[...].astype(F32)) * a + _sigmoid(gh_ref[...].astype(F32)) * r).astype(BF16)
    h_ref[...] += _dot(merged, wo_ref[...])

    if maybe_hn_ref:
        @pl.when(j == pl.num_programs(1) - 1)
        def _():
            h = h_ref[...]
            ms = jnp.mean(h * h, axis=-1, keepdims=True)
            maybe_hn_ref[0][...] = (h * lax.rsqrt(ms + EPS) * g_ref[...]).astype(BF16)


def merge_out(att, rec, proj_main, x, w_up_att, w_up_hgrn, w_o, norm_g, emit_hn, tm=512, tn=1024):
    m, d = x.shape
    ga0, gh0 = COL_GA * 1024 // tn, COL_GH * 1024 // tn
    row = lambda i, j: (i, 0)
    out_specs = [pl.BlockSpec((tm, d), row)]
    out_shape = [jax.ShapeDtypeStruct((m, d), F32)]
    if emit_hn:
        out_specs.append(pl.BlockSpec((tm, d), row))
        out_shape.append(jax.ShapeDtypeStruct((m, d), BF16))
    return pl.pallas_call(
        _merge_out_kernel,
        grid=(m // tm, d // tn),
        in_specs=[pl.BlockSpec((tm, ATT_W), row),
                  pl.BlockSpec((tm, HGRN_W), row),
                  pl.BlockSpec((tm, tn), lambda i, j: (i, ga0 + j)),
                  pl.BlockSpec((tm, tn), lambda i, j: (i, gh0 + j)),
                  pl.BlockSpec((tm, d), row),
                  pl.BlockSpec((ATT_W, tn), lambda i, j: (0, j)),
                  pl.BlockSpec((HGRN_W, tn), lambda i, j: (0, j)),
                  pl.BlockSpec((tn, d), lambda i, j: (j, 0)),
                  pl.BlockSpec((1, d), lambda i, j: (0, 0))],
        out_specs=out_specs,
        out_shape=out_shape,
        compiler_params=_cparams(("parallel", "arbitrary"), 48),
        name="merge_out",
    )(att, rec, proj_main, proj_main, x, w_up_att, w_up_hgrn, w_o, norm_g.reshape(1, d))


def _ffn_kernel(hn_ref, h_ref, w1_ref, w3_ref, w2_ref, g_ref, o_ref, *maybe_xn_ref):
    j = pl.program_id(1)

    @pl.when(j == 0)
    def _():
        o_ref[...] = h_ref[...]

    x = hn_ref[...]
    a = _dot(x, w1_ref[...])
    b = _dot(x, w3_ref[...])
    act = (a * _sigmoid(a) * b).astype(BF16)
    o_ref[...] += _dot(act, w2_ref[...])

    if maybe_xn_ref:
        @pl.when(j == pl.num_programs(1) - 1)
        def _():
            y = o_ref[...]
            ms = jnp.mean(y * y, axis=-1, keepdims=True)
            maybe_xn_ref[0][...] = (y * lax.rsqrt(ms + EPS) * g_ref[...]).astype(BF16)


def ffn_dense(hn, h, w1, w3, w2, next_norm_g, emit_xn, tm=512, tf=512):
    m, d = hn.shape
    f = w1.shape[1]
    row = lambda i, j: (i, 0)
    out_specs = [pl.BlockSpec((tm, d), row)]
    out_shape = [jax.ShapeDtypeStruct((m, d), F32)]
    if emit_xn:
        out_specs.append(pl.BlockSpec((tm, d), row))
        out_shape.append(jax.ShapeDtypeStruct((m, d), BF16))
    return pl.pallas_call(
        _ffn_kernel,
        grid=(m // tm, f // tf),
        in_specs=[pl.BlockSpec((tm, d), row),
                  pl.BlockSpec((tm, d), row),
                  pl.BlockSpec((d, tf), lambda i, j: (0, j)),
                  pl.BlockSpec((d, tf), lambda i, j: (0, j)),
                  pl.BlockSpec((tf, d), lambda i, j: (j, 0)),
                  pl.BlockSpec((1, d), lambda i, j: (0, 0))],
        out_specs=out_specs,
        out_shape=out_shape,
        compiler_params=_cparams(("parallel", "arbitrary"), 52),
        name="ffn_dense",
    )(hn, h, w1, w3, w2, next_norm_g.reshape(1, d))


def _row_copy(src_ref, src_row, dst_ref, dst_row, sem):
    return pltpu.make_async_copy(src_ref.at[pl.ds(src_row, 1)], dst_ref.at[pl.ds(dst_row, 1)], sem)


def _moe_scatter_kernel(dest_ref, hn_ref, xs_init_ref, xs_ref, sem, *, tm, m):
    del xs_init_ref
    base = pl.program_id(0) * tm

    def issue(r, carry):
        for k in range(2):
            _row_copy(hn_ref, r, xs_ref, dest_ref[k * m + base + r], sem).start()
        return carry

    lax.fori_loop(0, tm, issue, 0, unroll=8)
    for k in range(2):
        pltpu.make_async_copy(hn_ref, xs_ref.at[pl.ds(0, tm)], sem).wait()


def moe_scatter(hn_packed, dest, n_rows, tm=512):
    m, c = hn_packed.shape
    xs_init = jnp.zeros((n_rows, c), I32)
    return pl.pallas_call(
        functools.partial(_moe_scatter_kernel, tm=tm, m=m),
        grid_spec=pltpu.PrefetchScalarGridSpec(
            num_scalar_prefetch=1,
            grid=(m // tm,),
            in_specs=[pl.BlockSpec((tm, c), lambda i, dest: (i, 0)),
                      pl.BlockSpec(memory_space=pl.ANY)],
            out_specs=pl.BlockSpec(memory_space=pl.ANY),
            scratch_shapes=[pltpu.SemaphoreType.DMA(())]),
        out_shape=jax.ShapeDtypeStruct((n_rows, c), I32),
        input_output_aliases={2: 0},
        compiler_params=_cparams(("arbitrary",), 32),
        name="moe_scatter",
    )(dest, hn_packed, xs_init)


def _moe_ffn_kernel(te_ref, tv_ref, xs_ref, w1_ref, w3_ref, w2_ref, o_ref, xb_ref):
    del te_ref
    i = pl.program_id(0)
    half = xs_ref.shape[1]

    @pl.when(pl.program_id(1) == 0)
    def _():
        o_ref[...] = jnp.zeros_like(o_ref)
        lo, hi = _unpack_bf16_pairs(xs_ref[...])
        xb_ref[:, 0:half] = lo
        xb_ref[:, half:2 * half] = hi

    @pl.when(tv_ref[i] == 1)
    def _():
        x = xb_ref[...]
        a = _dot(x, w1_ref[0])
        b = _dot(x, w3_ref[0])
        act = (a * _sigmoid(a) * b).astype(BF16)
        o_ref[...] += _dot(act, w2_ref[0])


def moe_ffn(xs, tile_expert, tile_valid, w1, w3, w2, tm=512, tf=1024):
    p, c = xs.shape
    ne, d, f = w1.shape
    nf = f // tf
    col = lambda i, j, te, tv: jnp.where(tv[i] == 1, j, nf - 1)
    return pl.pallas_call(
        _moe_ffn_kernel,
        grid_spec=pltpu.PrefetchScalarGridSpec(
            num_scalar_prefetch=2,
            grid=(p // tm, nf),
            in_specs=[pl.BlockSpec((tm, c), lambda i, j, te, tv: (i, 0)),
                      pl.BlockSpec((1, d, tf), lambda i, j, te, tv: (te[i], 0, col(i, j, te, tv))),
                      pl.BlockSpec((1, d, tf), lambda i, j, te, tv: (te[i], 0, col(i, j, te, tv))),
                      pl.BlockSpec((1, tf, d), lambda i, j, te, tv: (te[i], col(i, j, te, tv), 0))],
            out_specs=pl.BlockSpec((tm, d), lambda i, j, te, tv: (i, 0)),
            scratch_shapes=[pltpu.VMEM((tm, d), BF16)]),
        out_shape=jax.ShapeDtypeStruct((p, d), F32),
        compiler_params=_cparams(("arbitrary", "arbitrary"), 56),
        name="moe_ffn",
    )(tile_expert, tile_valid, xs, w1, w3, w2)


def _moe_combine_kernel(dest_ref, h_ref, prob_ref, ys_ref, o_ref, ybuf_ref, sem, *, tm, m):
    i = pl.program_id(0)
    slot = i % 2

    def gather_tile(tile, slot_):
        base = tile * tm

        def issue(r, carry):
            for k in range(2):
                _row_copy(ys_ref, dest_ref[k * m + base + r], ybuf_ref.at[slot_, k], r,
                          sem.at[slot_]).start()
            return carry

        lax.fori_loop(0, tm, issue, 0, unroll=8)

    @pl.when(i == 0)
    def _():
        gather_tile(0, 0)

    @pl.when(i + 1 < pl.num_programs(0))
    def _():
        gather_tile(i + 1, 1 - slot)

    for k in range(2):
        pltpu.make_async_copy(ys_ref.at[pl.ds(0, tm)], ybuf_ref.at[slot, k], sem.at[slot]).wait()
    prob = prob_ref[...]
    o_ref[...] = h_ref[...] + prob[:, 0:1] * ybuf_ref[slot, 0] + prob[:, 1:2] * ybuf_ref[slot, 1]


def moe_combine(h, prob, ys, dest, tm=512):
    m, d = h.shape
    return pl.pallas_call(
        functools.partial(_moe_combine_kernel, tm=tm, m=m),
        grid_spec=pltpu.PrefetchScalarGridSpec(
            num_scalar_prefetch=1,
            grid=(m // tm,),
            in_specs=[pl.BlockSpec((tm, d), lambda i, dest: (i, 0)),
                      pl.BlockSpec((tm, LANES), lambda i, dest: (i, 0)),
                      pl.BlockSpec(memory_space=pl.ANY)],
            out_specs=pl.BlockSpec((tm, d), lambda i, dest: (i, 0)),
            scratch_shapes=[pltpu.VMEM((2, 2, tm, d), F32), pltpu.SemaphoreType.DMA((2,))]),
        out_shape=jax.ShapeDtypeStruct((m, d), F32),
        compiler_params=_cparams(("arbitrary",), 48),
        name="moe_combine",
    )(dest, h, prob, ys)


def ffn_moe(h, norm_g, w_router, w1, w3, w2, tm=512):
    m, d = h.shape
    ne = w1.shape[0]
    hn_packed, meta, prob, cnt = rmsnorm_router(h, norm_g, w_router)
    counts = cnt[0, :ne].astype(I32)
    tiles_per = (counts + tm - 1) // tm
    tile_end = jnp.cumsum(tiles_per)
    offset = (tile_end - tiles_per) * tm
    experts = jnp.arange(ne, dtype=I32)

    def group_offset(e):
        return jnp.sum(jnp.where(e[:, None] == experts[None, :], offset[None, :], 0), axis=1)

    dest = jnp.concatenate([group_offset(meta[:, 0]) + meta[:, 2],
                            group_offset(meta[:, 1]) + meta[:, 3]])
    n_tiles = (2 * m) // tm + ne
    tile_ids = jnp.arange(n_tiles, dtype=I32)
    tile_valid = (tile_ids < tile_end[-1]).astype(I32)
    last_valid = jnp.minimum(tile_ids, tile_end[-1] - 1)
    tile_expert = jnp.sum((tile_end[None, :] <= last_valid[:, None]).astype(I32), axis=1)
    xs = moe_scatter(hn_packed, dest, n_tiles * tm)
    ys = moe_ffn(xs, tile_expert, tile_valid, w1, w3, w2, tm=tm)
    return moe_combine(h, prob, ys, dest)


W_IN_HEAD = ATT_W + 2 * KV_W + IDX_Q_RANK
W_IN_SMALL = IDX_HEAD_DIM + IDX_HEADS
W_IN_ROWS = 1024


def _w_in_prep_kernel(wt_hbm, main_ref, idx_ref, buf_ref, sem, *, layer):
    c = pl.program_id(0)

    def fetch(row0, n_rows):
        cp = pltpu.make_async_copy(wt_hbm.at[layer, pl.ds(row0, n_rows)],
                                   buf_ref.at[pl.ds(0, n_rows)], sem)
        cp.start()
        cp.wait()

    @pl.when(c == 0)
    def _():
        fetch(W_IN_HEAD, LANES)
        rows = buf_ref[0:LANES, :]
        keep = lax.broadcasted_iota(I32, rows.shape, 0) < W_IN_SMALL
        idx_ref[...] = jnp.where(keep, rows, 0.0).astype(BF16)

    base = c * W_IN_ROWS
    row0 = pl.multiple_of(jnp.where(base < W_IN_HEAD, base, base + W_IN_SMALL), 8)
    fetch(row0, W_IN_ROWS)
    main_ref[...] = buf_ref[...].astype(BF16)


def _split_w_in(w_in, layer):
    _, n, d = w_in.shape
    assert n == MAIN_W + W_IN_SMALL and W_IN_HEAD % W_IN_ROWS == 0
    return pl.pallas_call(
        functools.partial(_w_in_prep_kernel, layer=layer),
        grid=(MAIN_W // W_IN_ROWS,),
        in_specs=[pl.BlockSpec(memory_space=pl.ANY)],
        out_specs=[pl.BlockSpec((W_IN_ROWS, d), lambda c: (c, 0)),
                   pl.BlockSpec((LANES, d), lambda c: (0, 0))],
        out_shape=[jax.ShapeDtypeStruct((MAIN_W, d), BF16),
                   jax.ShapeDtypeStruct((LANES, d), BF16)],
        scratch_shapes=[pltpu.VMEM((W_IN_ROWS, d), F32), pltpu.SemaphoreType.DMA(())],
        compiler_params=_cparams(("arbitrary",), 40),
        name="w_in_prep",
    )(w_in)


def _mixer(x2, maybe_xn, batch, seq, norm_g, w_in, q_norm_g, k_norm_g, idx_q_norm_g, w_idx_q, ln_g,
           ln_b, lb, hgrn_norm_g, w_up_att, w_up_hgrn, w_o, bias, norm_ffn_g, emit_hn,
           side_cast_att=None, side_cast_rec=None):
    w_main, w_idx = _split_w_in(*w_in)
    if maybe_xn is None:
        proj_main, xn = rmsnorm_matmul(x2, norm_g, w_main, BF16)
    else:
        xn = maybe_xn
        proj_main = matmul(xn, w_main, BF16)
    proj_idx = matmul(xn, w_idx, F32)
    qn, kn, vt, qi, ki, wt = attn_prep(proj_main, proj_idx, q_norm_g, k_norm_g, idx_q_norm_g,
                                       w_idx_q.astype(BF16), ln_g, ln_b, seq)
    vt = vt.reshape(batch, seq // TK, KV_W, TK)
    att, *cast_att = dsa_attention(qi, wt, qn, ki, kn, vt, bias, batch, seq, side_cast_att)
    rec, *cast_rec = hgrn2(proj_main, lb, hgrn_norm_g, batch, seq, side_cast_rec)
    outs = merge_out(att, rec, proj_main, x2, w_up_att.astype(BF16), w_up_hgrn.astype(BF16),
                     w_o.astype(BF16), norm_ffn_g, emit_hn)
    return outs, (cast_att[0] if cast_att else None), (cast_rec[0] if cast_rec else None)


def kernel(x, rel_bias, norm_mix_g, norm_ffn_g, w_in, q_norm_g, k_norm_g, idx_q_norm_g, w_idx_q,
           idx_k_ln_g, idx_k_ln_b, hgrn_lb_logits, hgrn_out_norm_g, w_up_att, w_up_hgrn, w_o,
           w1_dense, w3_dense, w2_dense, w_router, w1_moe, w3_moe, w2_moe):
    batch, seq, d = x.shape
    depth = w_in.shape[0]
    lb_all = jnp.cumsum(jax.nn.softmax(hgrn_lb_logits.astype(F32), axis=0), axis=0)
    lb_all = lb_all - lb_all[0:1]
    bias = bias_tables(rel_bias)
    w_in_t = jnp.swapaxes(w_in, 1, 2)
    x2 = x.reshape(batch * seq, d)
    maybe_xn = None
    n_split_ff = w1_moe.shape[-1] // 1024
    moe_bf16 = {}
    for l in range(depth):
        dense = l % 2 == 0
        j = l // 2
        side_att = side_rec = None
        if dense and l + 1 < depth:
            side_att = (w3_moe[j], 2, n_split_ff)
            side_rec = (w1_moe[j], 2, 2 * n_split_ff)
        elif not dense:
            side_att = (w2_moe[j], 1, n_split_ff)
        (h, *maybe_hn), cast_att, cast_rec = _mixer(
            x2, maybe_xn, batch, seq, norm_mix_g[l], (w_in_t, l), q_norm_g[l], k_norm_g[l],
            idx_q_norm_g[l], w_idx_q[l], idx_k_ln_g[l], idx_k_ln_b[l], lb_all[l],
            hgrn_out_norm_g[l], w_up_att[l], w_up_hgrn[l], w_o[l], bias, norm_ffn_g[l],
            emit_hn=dense, side_cast_att=side_att, side_cast_rec=side_rec)
        maybe_xn = None
        if dense:
            more = l + 1 < depth
            moe_bf16["w3"], moe_bf16["w1"] = cast_att, cast_rec
            x2, *rest = ffn_dense(maybe_hn[0], h, w1_dense[j].astype(BF16), w3_dense[j].astype(BF16),
                                  w2_dense[j].astype(BF16), norm_mix_g[l + 1 if more else l], more)
            maybe_xn = rest[0] if more else None
        else:
            x2 = ffn_moe(h, norm_ffn_g[l], w_router[j], moe_bf16["w1"], moe_bf16["w3"], cast_att)
    return x2.reshape(batch, seq, d)
```

```python
import functools
import math

import numpy as np
import jax
import jax.numpy as jnp
from jax import lax
from jax.experimental import pallas as pl
from jax.experimental.pallas import tpu as pltpu

F32 = jnp.float32
BF16 = jnp.bfloat16
I32 = jnp.int32

EPS = 1e-6
ATT_HEADS = 8
ATT_KV_HEADS = 2
ATT_GROUP = ATT_HEADS // ATT_KV_HEADS
HEAD_DIM = 128
ATT_W = ATT_HEADS * HEAD_DIM
KV_W = ATT_KV_HEADS * HEAD_DIM
IDX_HEADS = 16
IDX_HEAD_DIM = 64
IDX_Q_RANK = 512
TOPK_MAX = 256
HGRN_HEADS = 8
HGRN_DIM = 128
HGRN_W = HGRN_HEADS * HGRN_DIM
REL_BUCKETS = 32
REL_MAX_DIST = 128
N_EXPERTS = 8
LANES = 128
INT_MIN = -(2 ** 31)
NEG_BIG = -1e30
LOG2E = math.log2(math.e)

TQ = 256
TK = 256
HC = 128
HGRN_LEVELS = (1, 2, 4, 8, 16, 32, 64)
N_ARG_GROUPS = len(HGRN_LEVELS) + 2

COL_Q, COL_KVC, COL_HQ, COL_HF, COL_HI, COL_HG, COL_GA, COL_GH = 0, 1, 2, 3, 4, 5, 6, 8
MAIN_W = 10 * 1024


def _cparams(sem, vmem_mb):
    return pltpu.CompilerParams(dimension_semantics=sem, vmem_limit_bytes=vmem_mb << 20)


def _dot(a, b):
    return jnp.dot(a, b, preferred_element_type=F32)


def _dot_nt(a, b):
    return lax.dot_general(a, b, (((1,), (1,)), ((), ())), preferred_element_type=F32)


def _sigmoid(x):
    return 1.0 / (1.0 + jnp.exp(-x))


def _pack_bf16_pairs(x):
    c = x.shape[1] // 2
    lo = lax.bitcast_convert_type(x[:, :c].astype(BF16).astype(F32), I32)
    hi = lax.bitcast_convert_type(x[:, c:].astype(BF16).astype(F32), I32)
    return (hi & jnp.int32(-65536)) | lax.shift_right_logical(lo, 16)


def _unpack_bf16_pairs(p):
    lo = lax.bitcast_convert_type(lax.shift_left(p, 16), F32).astype(BF16)
    hi = lax.bitcast_convert_type(p & jnp.int32(-65536), F32).astype(BF16)
    return lo, hi


def _rmsnorm_router_kernel(x_ref, g_ref, wr_ref, tri_ref, o_ref, meta_ref, prob_ref, cnt_ref, carry_ref):
    @pl.when(pl.program_id(0) == 0)
    def _():
        carry_ref[...] = jnp.zeros_like(carry_ref)

    x = x_ref[...]
    ms = jnp.mean(x * x, axis=-1, keepdims=True)
    hn = x * lax.rsqrt(ms + EPS) * g_ref[...]
    o_ref[...] = _pack_bf16_pairs(hn)
    logits = _dot(hn.astype(BF16), wr_ref[...])
    lane = lax.broadcasted_iota(I32, logits.shape, 1)
    logits = jnp.where(lane < N_EXPERTS, logits, -jnp.inf)
    v1 = jnp.max(logits, axis=-1, keepdims=True)
    i1 = jnp.min(jnp.where(logits == v1, lane, LANES), axis=-1, keepdims=True)
    rest = jnp.where(lane == i1, -jnp.inf, logits)
    v2 = jnp.max(rest, axis=-1, keepdims=True)
    i2 = jnp.min(jnp.where(rest == v2, lane, LANES), axis=-1, keepdims=True)
    e = jnp.exp(v2 - v1)
    p1 = 1.0 / (1.0 + e)
    prob_ref[...] = jnp.where(lane == 0, p1, 0.0) + jnp.where(lane == 1, e * p1, 0.0)
    hot = jnp.where((lane == i1) | (lane == i2), 1.0, 0.0)
    rank = _dot(tri_ref[...], hot.astype(BF16)) + carry_ref[0:1, :]
    carry_ref[...] = carry_ref[...] + jnp.sum(hot, axis=0, keepdims=True)
    cnt_ref[...] = carry_ref[...]
    r1 = jnp.sum(jnp.where(lane == i1, rank, 0.0), axis=-1, keepdims=True)
    r2 = jnp.sum(jnp.where(lane == i2, rank, 0.0), axis=-1, keepdims=True)
    meta = (jnp.where(lane == 0, i1, 0) + jnp.where(lane == 1, i2, 0)
            + jnp.where(lane == 2, r1.astype(I32), 0) + jnp.where(lane == 3, r2.astype(I32), 0))
    meta_ref[...] = meta


def rmsnorm_router(x, g, w_router, tm=512):
    m, d = x.shape
    wr = jnp.zeros((d, LANES), BF16).at[:, :N_EXPERTS].set(w_router.astype(BF16))
    tri = jnp.asarray(np.tril(np.ones((tm, tm), np.float32), -1), BF16)
    return pl.pallas_call(
        _rmsnorm_router_kernel,
        grid=(m // tm,),
        in_specs=[pl.BlockSpec((tm, d), lambda i: (i, 0)),
                  pl.BlockSpec((1, d), lambda i: (0, 0)),
                  pl.BlockSpec((d, LANES), lambda i: (0, 0)),
                  pl.BlockSpec((tm, tm), lambda i: (0, 0))],
        out_specs=[pl.BlockSpec((tm, d // 2), lambda i: (i, 0)),
                   pl.BlockSpec((tm, LANES), lambda i: (i, 0)),
                   pl.BlockSpec((tm, LANES), lambda i: (i, 0)),
                   pl.BlockSpec((8, LANES), lambda i: (0, 0))],
        out_shape=[jax.ShapeDtypeStruct((m, d // 2), I32),
                   jax.ShapeDtypeStruct((m, LANES), I32),
                   jax.ShapeDtypeStruct((m, LANES), F32),
                   jax.ShapeDtypeStruct((8, LANES), F32)],
        scratch_shapes=[pltpu.VMEM((8, LANES), F32)],
        compiler_params=_cparams(("arbitrary",), 40),
        name="rmsnorm_router",
    )(x, g.reshape(1, d), wr, tri)


def _matmul_kernel(x_ref, wt_ref, o_ref):
    o_ref[...] = _dot_nt(x_ref[...], wt_ref[...]).astype(o_ref.dtype)


def _rmsnorm_matmul_kernel(x_ref, g_ref, wt_ref, o_ref, xn_o_ref, xn_ref):
    @pl.when(pl.program_id(1) == 0)
    def _():
        x = x_ref[...]
        ms = jnp.mean(x * x, axis=-1, keepdims=True)
        xn_ref[...] = (x * lax.rsqrt(ms + EPS) * g_ref[...]).astype(BF16)
        xn_o_ref[...] = xn_ref[...]

    o_ref[...] = _dot_nt(xn_ref[...], wt_ref[...]).astype(o_ref.dtype)


def rmsnorm_matmul(x, g, wt, out_dtype, tm=1024, tn=1024):
    m, k = x.shape
    n = wt.shape[0]
    tm, tn = min(tm, m), min(tn, n)
    return pl.pallas_call(
        _rmsnorm_matmul_kernel,
        grid=(m // tm, n // tn),
        in_specs=[pl.BlockSpec((tm, k), lambda i, j: (i, 0)),
                  pl.BlockSpec((1, k), lambda i, j: (0, 0)),
                  pl.BlockSpec((tn, k), lambda i, j: (j, 0))],
        out_specs=[pl.BlockSpec((tm, tn), lambda i, j: (i, j)),
                   pl.BlockSpec((tm, k), lambda i, j: (i, 0))],
        out_shape=[jax.ShapeDtypeStruct((m, n), out_dtype),
                   jax.ShapeDtypeStruct((m, k), BF16)],
        scratch_shapes=[pltpu.VMEM((tm, k), BF16)],
        compiler_params=_cparams(("parallel", "arbitrary"), 48),
        name="rmsnorm_matmul",
    )(x, g.reshape(1, k), wt)


def matmul(x, wt, out_dtype, tm=1024, tn=1024):
    m, k = x.shape
    n = wt.shape[0]
    tm, tn = min(tm, m), min(tn, n)
    return pl.pallas_call(
        _matmul_kernel,
        grid=(m // tm, n // tn),
        in_specs=[pl.BlockSpec((tm, k), lambda i, j: (i, 0)),
                  pl.BlockSpec((tn, k), lambda i, j: (j, 0))],
        out_specs=pl.BlockSpec((tm, tn), lambda i, j: (i, j)),
        out_shape=jax.ShapeDtypeStruct((m, n), out_dtype),
        compiler_params=_cparams(("parallel", "parallel"), 48),
        name="matmul",
    )(x, wt)


def _attn_prep_kernel(q_ref, kvc_ref, pi_ref, qg_ref, kg_ref, cg_ref, wiq_ref, lng_ref, lnb_ref,
                      qn_ref, kn_ref, vt_ref, qi_ref, ki_ref, wt_ref):
    def head_rms(x, g):
        x = x.astype(F32)
        return x * lax.rsqrt(jnp.mean(x * x, axis=-1, keepdims=True) + EPS) * g

    att_scale = HEAD_DIM ** -0.5 * LOG2E
    for h in range(ATT_HEADS):
        sl = slice(h * HEAD_DIM, (h + 1) * HEAD_DIM)
        qn_ref[:, sl] = (head_rms(q_ref[:, sl], qg_ref[...]) * att_scale).astype(BF16)
    for g in range(ATT_KV_HEADS):
        sl = slice(g * HEAD_DIM, (g + 1) * HEAD_DIM)
        kn_ref[:, sl] = head_rms(kvc_ref[:, sl], kg_ref[...]).astype(BF16)
    for c in range(vt_ref.shape[0]):
        vc = kvc_ref[c * TK:(c + 1) * TK, KV_W:2 * KV_W].astype(F32)
        vt_ref[c] = vc.T.astype(BF16)
    cq = kvc_ref[:, 2 * KV_W:2 * KV_W + IDX_Q_RANK]
    cqn = head_rms(cq, cg_ref[...]).astype(BF16)
    qi_ref[...] = _dot(cqn, wiq_ref[...]).astype(BF16)
    pi = pi_ref[...]
    is_key = lax.broadcasted_iota(I32, pi.shape, 1) < IDX_HEAD_DIM
    mu = jnp.sum(jnp.where(is_key, pi, 0.0), axis=-1, keepdims=True) * (1.0 / IDX_HEAD_DIM)
    cen = jnp.where(is_key, pi - mu, 0.0)
    var = jnp.sum(cen * cen, axis=-1, keepdims=True) * (1.0 / IDX_HEAD_DIM)
    kidn = cen * lax.rsqrt(var + EPS) * lng_ref[...] + lnb_ref[...]
    ki_ref[:, 0:LANES] = kidn.astype(BF16)
    ki_ref[:, LANES:2 * LANES] = pltpu.roll(kidn, IDX_HEAD_DIM, 1).astype(BF16)
    wt = (pi * (IDX_HEAD_DIM ** -0.5 * IDX_HEADS ** -0.5)).T
    wt_ref[0] = wt[IDX_HEAD_DIM:IDX_HEAD_DIM + IDX_HEADS, :]


def attn_prep(proj_main, proj_idx, q_norm_g, k_norm_g, idx_q_norm_g, w_idx_q, ln_g, ln_b, seq, tm=512):
    m = proj_main.shape[0]
    tm = min(tm, seq)
    per_seq = seq // tm
    row = lambda i: (i, 0)
    const = lambda i: (0, 0)
    lane_pad = lambda a: jnp.zeros((1, LANES), F32).at[0, :a.shape[0]].set(a)
    return pl.pallas_call(
        _attn_prep_kernel,
        grid=(m // tm,),
        in_specs=[pl.BlockSpec((tm, 1024), lambda i: (i, COL_Q)),
                  pl.BlockSpec((tm, 1024), lambda i: (i, COL_KVC)),
                  pl.BlockSpec((tm, LANES), row),
                  pl.BlockSpec((1, HEAD_DIM), const),
                  pl.BlockSpec((1, HEAD_DIM), const),
                  pl.BlockSpec((1, IDX_Q_RANK), const),
                  pl.BlockSpec((IDX_Q_RANK, IDX_HEADS * IDX_HEAD_DIM), const),
                  pl.BlockSpec((1, LANES), const),
                  pl.BlockSpec((1, LANES), const)],
        out_specs=[pl.BlockSpec((tm, ATT_W), row),
                   pl.BlockSpec((tm, KV_W), row),
                   pl.BlockSpec((tm // TK, KV_W, TK), lambda i: (i, 0, 0)),
                   pl.BlockSpec((tm, IDX_HEADS * IDX_HEAD_DIM), row),
                   pl.BlockSpec((tm, 2 * LANES), row),
                   pl.BlockSpec((1, IDX_HEADS, tm), lambda i: (i // per_seq, 0, i % per_seq))],
        out_shape=[jax.ShapeDtypeStruct((m, ATT_W), BF16),
                   jax.ShapeDtypeStruct((m, KV_W), BF16),
                   jax.ShapeDtypeStruct((m // TK, KV_W, TK), BF16),
                   jax.ShapeDtypeStruct((m, IDX_HEADS * IDX_HEAD_DIM), BF16),
                   jax.ShapeDtypeStruct((m, 2 * LANES), BF16),
                   jax.ShapeDtypeStruct((m // seq, IDX_HEADS, seq), F32)],
        compiler_params=_cparams(("parallel",), 40),
        name="attn_prep",
    )(proj_main, proj_main, proj_idx, q_norm_g.reshape(1, -1), k_norm_g.reshape(1, -1),
      idx_q_norm_g.reshape(1, -1), w_idx_q, lane_pad(ln_g), lane_pad(ln_b))


def _rel_bucket(dist):
    max_exact = REL_BUCKETS // 2
    n = jnp.maximum(dist, 0)
    nf = jnp.maximum(n, 1).astype(F32)
    large = max_exact + (jnp.log(nf / max_exact) / math.log(REL_MAX_DIST / max_exact)
                         * (REL_BUCKETS - max_exact)).astype(I32)
    large = jnp.minimum(large, REL_BUCKETS - 1)
    return jnp.where(n < max_exact, n, large)


def _bias_kernel(rb_ref, bkt_ref, o_ref):
    h = pl.program_id(0)
    for kind in range(3):
        b = bkt_ref[kind]
        acc = jnp.zeros(b.shape, F32)
        for n in range(REL_BUCKETS):
            acc = jnp.where(b == n, rb_ref[n, h], acc)
        o_ref[0, kind] = acc * LOG2E


def bias_tables(rel_bias):
    assert TK >= REL_MAX_DIST
    kpos = jnp.arange(TK, dtype=I32)[:, None]
    qpos = jnp.arange(TQ, dtype=I32)[None, :]
    dist = jnp.stack([qpos - kpos, qpos - kpos + TK, qpos - kpos + 2 * TK])
    bkt = _rel_bucket(dist)
    return pl.pallas_call(
        _bias_kernel,
        grid=(ATT_HEADS,),
        in_specs=[pl.BlockSpec(memory_space=pltpu.SMEM),
                  pl.BlockSpec((3, TK, TQ), lambda h: (0, 0, 0))],
        out_specs=pl.BlockSpec((1, 3, TK, TQ), lambda h: (h, 0, 0, 0)),
        out_shape=jax.ShapeDtypeStruct((ATT_HEADS, 3, TK, TQ), F32),
        compiler_params=_cparams(("arbitrary",), 32),
        name="bias_tables",
    )(rel_bias, bkt)


def _side_cast_specs(src, split_axis, n_split, n_steps, step_of):
    ne = src.shape[0]
    n_blocks = ne * n_split
    every = n_steps // n_blocks
    assert every >= 1, "not enough grid steps to walk the weight stack"
    block = [1, src.shape[1], src.shape[2]]
    block[split_axis] //= n_split

    def index_map(*grid_idx):
        t = jnp.minimum(step_of(*grid_idx) // every, n_blocks - 1)
        idx = [t // n_split, 0, 0]
        idx[split_axis] = t % n_split
        return tuple(idx)

    spec = pl.BlockSpec(tuple(block), index_map)
    return spec, spec, jax.ShapeDtypeStruct(src.shape, BF16)


def _dsa_kernel(qi_ref, wt_ref, qn_ref, ki_ref, kn_ref, vt_ref, bias_ref, *rest, n_sel, side_cast):
    if side_cast:
        wf_ref, o_ref, wb_ref, key_ref, m_ref, l_ref, a_ref, acc_ref, s_ref, p_ref = rest
        wb_ref[...] = wf_ref[...].astype(BF16)
    else:
        o_ref, key_ref, m_ref, l_ref, a_ref, acc_ref, s_ref, p_ref = rest
    i = pl.program_id(1)
    nch = i + 1
    qpos = i * TQ + lax.broadcasted_iota(I32, (TK, TQ), 1)

    def score_chunk(j, carry):
        k0 = pl.multiple_of(j * TK, TK)
        acc = jnp.zeros((TK, TQ), F32)
        for h in range(IDX_HEADS):
            par = h % 2
            kc = ki_ref[pl.ds(k0, TK), par * LANES:(par + 1) * LANES]
            qh = qi_ref[:, (h // 2) * LANES:(h // 2 + 1) * LANES]
            s = _dot_nt(kc, qh)
            acc = acc + jnp.maximum(s, 0.0) * wt_ref[0, h:h + 1, :]
        bits = lax.bitcast_convert_type(acc, I32)
        key = jnp.where(bits < 0, bits ^ jnp.int32(0x7FFFFFFF), bits)
        kpos = k0 + lax.broadcasted_iota(I32, (TK, TQ), 0)
        key_ref[pl.ds(k0, TK), :] = jnp.where(kpos <= qpos, key, INT_MIN)
        return carry

    lax.fori_loop(0, nch, score_chunk, 0)

    def count_ge(cand):
        def body(j, acc):
            k0 = pl.multiple_of(j * TK, TK)
            hit = jnp.where(key_ref[pl.ds(k0, TK), :] >= cand, 1, 0).astype(I32)
            return acc + jnp.sum(hit.reshape(TK // 8, 8, TQ), axis=0)
        acc = lax.fori_loop(0, nch, body, jnp.zeros((8, TQ), I32))
        return jnp.sum(acc, axis=0, keepdims=True)

    thr = jnp.where(count_ge(jnp.zeros((1, TQ), I32)) >= n_sel, 0, INT_MIN).astype(I32)

    def bit_body(b, thr):
        cand = thr | jnp.left_shift(jnp.int32(1), 30 - b)
        return jnp.where(count_ge(cand) >= n_sel, cand, thr)

    thr = lax.fori_loop(0, 31, bit_body, thr)
    thr = jnp.maximum(thr, INT_MIN + 1)

    m_ref[...] = jnp.full(m_ref.shape, NEG_BIG, F32)
    l_ref[...] = jnp.zeros(l_ref.shape, F32)
    acc_ref[...] = jnp.zeros(acc_ref.shape, F32)

    def att_chunk(j, carry):
        k0 = pl.multiple_of(j * TK, TK)
        neg = jnp.where(key_ref[pl.ds(k0, TK), :] >= thr, 0.0, NEG_BIG)
        kind = jnp.minimum(i - j, 2)
        for h in range(ATT_HEADS):
            g = h // ATT_GROUP
            kc = kn_ref[pl.ds(k0, TK), g * HEAD_DIM:(g + 1) * HEAD_DIM]
            qh = qn_ref[:, h * HEAD_DIM:(h + 1) * HEAD_DIM]
            lg = _dot_nt(kc, qh) + bias_ref[h, kind] + neg
            s_ref[h] = lg
            m_old = m_ref[h:h + 1, :]
            m_new = jnp.maximum(m_old, jnp.max(lg, axis=0, keepdims=True))
            a_ref[h:h + 1, :] = jnp.exp2(m_old - m_new)
            m_ref[h:h + 1, :] = m_new
        for h in range(ATT_HEADS):
            p = jnp.exp2(s_ref[h] - m_ref[h:h + 1, :])
            l_ref[h:h + 1, :] = (a_ref[h:h + 1, :] * l_ref[h:h + 1, :]
                                 + jnp.sum(p, axis=0, keepdims=True))
            p_ref[h] = p.astype(BF16)
        for h in range(ATT_HEADS):
            g = h // ATT_GROUP
            vc = vt_ref[0, j, g * HEAD_DIM:(g + 1) * HEAD_DIM, :]
            acc_ref[h] = a_ref[h:h + 1, :] * acc_ref[h] + _dot(vc, p_ref[h])
        return carry

    lax.fori_loop(0, nch, att_chunk, 0)
    for h in range(ATT_HEADS):
        o = acc_ref[h] * (1.0 / l_ref[h:h + 1, :])
        o_ref[:, h * HEAD_DIM:(h + 1) * HEAD_DIM] = o.T.astype(o_ref.dtype)


def dsa_attention(qi, wt, qn, ki, kn, vt, bias, batch, seq, side_cast=None):
    nq = seq // TQ
    n_sel = min(TOPK_MAX, seq // 4)
    qrow = lambda b, i: (b * nq + i, 0)
    brow = lambda b, i: (b, 0)
    in_specs = [pl.BlockSpec((TQ, IDX_HEADS * IDX_HEAD_DIM), qrow),
                pl.BlockSpec((1, IDX_HEADS, TQ), lambda b, i: (b, 0, i)),
                pl.BlockSpec((TQ, ATT_W), qrow),
                pl.BlockSpec((seq, 2 * LANES), brow),
                pl.BlockSpec((seq, KV_W), brow),
                pl.BlockSpec((1, seq // TK, KV_W, TK), lambda b, i: (b, 0, 0, 0)),
                pl.BlockSpec((ATT_HEADS, 3, TK, TQ), lambda b, i: (0, 0, 0, 0))]
    out_specs = [pl.BlockSpec((TQ, ATT_W), qrow)]
    out_shape = [jax.ShapeDtypeStruct((batch * seq, ATT_W), BF16)]
    operands = [qi, wt, qn, ki, kn, vt, bias]
    if side_cast is not None:
        src, split_axis, n_split = side_cast
        spec_in, spec_out, shape_out = _side_cast_specs(src, split_axis, n_split, batch * nq,
                                                        lambda b, i: b * nq + i)
        in_specs.append(spec_in)
        out_specs.append(spec_out)
        out_shape.append(shape_out)
        operands.append(src)
    return pl.pallas_call(
        functools.partial(_dsa_kernel, n_sel=n_sel, side_cast=side_cast is not None),
        grid=(batch, nq),
        in_specs=in_specs,
        out_specs=out_specs,
        out_shape=out_shape,
        scratch_shapes=[pltpu.VMEM((seq, TQ), I32),
                        pltpu.VMEM((ATT_HEADS, TQ), F32),
                        pltpu.VMEM((ATT_HEADS, TQ), F32),
                        pltpu.VMEM((ATT_HEADS, TQ), F32),
                        pltpu.VMEM((ATT_HEADS, HEAD_DIM, TQ), F32),
                        pltpu.VMEM((ATT_HEADS, TK, TQ), F32),
                        pltpu.VMEM((ATT_HEADS, TK, TQ), BF16)],
        compiler_params=_cparams(("arbitrary", "arbitrary"), 56),
        name="dsa_attention",
    )(*operands)


def _hgrn_constants():
    t = np.arange(HC)
    rows = []
    masks = [np.eye(HC, dtype=np.float32)]
    for m in HGRN_LEVELS:
        upper = (t // m) % 2 == 1
        start = (t // m) * m
        end = start + m - 1
        u = t[None, :]
        q_side = upper[:, None] & (u >= start[:, None]) & (u <= t[:, None])
        k_side = (~upper)[:, None] & (u > t[:, None]) & (u <= end[:, None])
        rows.append((q_side | k_side).astype(np.float32))
        same = (t[:, None] // (2 * m)) == (t[None, :] // (2 * m))
        masks.append((upper[:, None] & (~upper)[None, :] & same).astype(np.float32))
    u = t[None, :]
    rows.append((u <= t[:, None]).astype(np.float32))
    rows.append((u > t[:, None]).astype(np.float32))
    mat = np.concatenate(rows, axis=0)
    return np.concatenate([mat, mat], axis=1), np.stack(masks)


def _hgrn_kernel(hq_ref, hf_ref, hi_ref, hg_ref, lb_ref, gn_ref, mat_ref, mask_ref, *rest,
                 side_cast):
    if side_cast:
        wf_ref, o_ref, wb_ref, st_ref, arg_ref, kk_ref, lf_ref, zq_ref, zk_ref = rest
        wb_ref[...] = wf_ref[...].astype(BF16)
    else:
        o_ref, st_ref, arg_ref, kk_ref, lf_ref, zq_ref, zk_ref = rest
    c = pl.program_id(1)

    @pl.when(c == 0)
    def _():
        st_ref[...] = jnp.zeros_like(st_ref)

    lb = lb_ref[...]
    f = hf_ref[...].astype(F32)
    e = jnp.exp(-jnp.abs(f))
    r = 1.0 / (1.0 + e)
    log_sig = jnp.minimum(f, 0.0) - jnp.log(1.0 + e)
    la = jnp.log(lb)
    lc = jnp.log(1.0 - lb) + log_sig
    logf = (jnp.maximum(la, lc) + jnp.log(1.0 + jnp.exp(-jnp.abs(la - lc)))) * LOG2E
    kk_ref[...] = (1.0 - lb) * jnp.where(f >= 0, e * r, r)
    hi = logf.astype(BF16)
    lf_ref[0:HC, :] = hi
    lf_ref[HC:2 * HC, :] = (logf - hi.astype(F32)).astype(BF16)
    arg_ref[...] = _dot(mat_ref[...], lf_ref[...])

    scale = HGRN_DIM ** -0.5
    nl = len(HGRN_LEVELS)
    for h in range(HGRN_HEADS):
        sl = slice(h * HGRN_DIM, (h + 1) * HGRN_DIM)
        hq = hq_ref[:, sl].astype(F32)
        q = hq * _sigmoid(hq) * scale
        k = kk_ref[:, sl]
        zq_ref[h, 0] = q.astype(BF16)
        zk_ref[h, 0] = k.astype(BF16)
        for lv in range(nl):
            ex = jnp.exp2(arg_ref[lv * HC:(lv + 1) * HC, sl])
            zq_ref[h, lv + 1] = (q * ex).astype(BF16)
            zk_ref[h, lv + 1] = (k * ex).astype(BF16)
        zq_ref[h, nl + 1] = (q * jnp.exp2(arg_ref[nl * HC:(nl + 1) * HC, sl])).astype(BF16)
        zk_ref[h, nl + 1] = (k * jnp.exp2(arg_ref[(nl + 1) * HC:(nl + 2) * HC, sl])).astype(BF16)

    for h in range(HGRN_HEADS):
        sl = slice(h * HGRN_DIM, (h + 1) * HGRN_DIM)
        a = _dot_nt(zq_ref[h, 0], zk_ref[h, 0]) * mask_ref[0]
        for lv in range(nl):
            a = a + _dot_nt(zq_ref[h, lv + 1], zk_ref[h, lv + 1]) * mask_ref[lv + 1]
        v = hi_ref[:, sl].astype(BF16)
        st = st_ref[h]
        o = _dot(a.astype(BF16), v) + _dot_nt(zq_ref[h, nl + 1], st.astype(BF16))
        decay = jnp.exp2(arg_ref[(nl + 1) * HC - 1:(nl + 1) * HC, sl])
        st_ref[h] = decay * st + _dot(v.astype(F32).T.astype(BF16), zk_ref[h, nl + 1])
        on = o * lax.rsqrt(jnp.mean(o * o, axis=-1, keepdims=True) + EPS) * gn_ref[...]
        hg = hg_ref[:, sl].astype(F32)
        o_ref[:, sl] = (on * hg * _sigmoid(hg)).astype(o_ref.dtype)


def hgrn2(proj_main, lb, g_norm, batch, seq, side_cast=None):
    nc = seq // HC
    mat, masks = _hgrn_constants()
    col = lambda cb: (lambda b, c: (b * nc + c, cb))
    in_specs = [pl.BlockSpec((HC, HGRN_W), col(COL_HQ)),
                pl.BlockSpec((HC, HGRN_W), col(COL_HF)),
                pl.BlockSpec((HC, HGRN_W), col(COL_HI)),
                pl.BlockSpec((HC, HGRN_W), col(COL_HG)),
                pl.BlockSpec((1, HGRN_W), lambda b, c: (0, 0)),
                pl.BlockSpec((1, HGRN_DIM), lambda b, c: (0, 0)),
                pl.BlockSpec((N_ARG_GROUPS * HC, 2 * HC), lambda b, c: (0, 0)),
                pl.BlockSpec((len(HGRN_LEVELS) + 1, HC, HC), lambda b, c: (0, 0, 0))]
    operands = [proj_main, proj_main, proj_main, proj_main, lb.reshape(1, -1), g_norm.reshape(1, -1),
                jnp.asarray(mat, BF16), jnp.asarray(masks, F32)]
    out_specs = [pl.BlockSpec((HC, HGRN_W), lambda b, c: (b * nc + c, 0))]
    out_shape = [jax.ShapeDtypeStruct((batch * seq, HGRN_W), BF16)]
    if side_cast is not None:
        src, split_axis, n_split = side_cast
        spec_in, spec_out, shape_out = _side_cast_specs(src, split_axis, n_split, batch * nc,
                                                        lambda b, c: b * nc + c)
        in_specs.append(spec_in)
        operands.append(src)
        out_specs.append(spec_out)
        out_shape.append(shape_out)
    return pl.pallas_call(
        functools.partial(_hgrn_kernel, side_cast=side_cast is not None),
        grid=(batch, nc),
        in_specs=in_specs,
        out_specs=out_specs,
        out_shape=out_shape,
        scratch_shapes=[pltpu.VMEM((HGRN_HEADS, HGRN_DIM, HGRN_DIM), F32),
                        pltpu.VMEM((N_ARG_GROUPS * HC, HGRN_W), F32),
                        pltpu.VMEM((HC, HGRN_W), F32),
                        pltpu.VMEM((2 * HC, HGRN_W), BF16),
                        pltpu.VMEM((HGRN_HEADS, N_ARG_GROUPS, HC, HGRN_DIM), BF16),
                        pltpu.VMEM((HGRN_HEADS, N_ARG_GROUPS, HC, HGRN_DIM), BF16)],
        compiler_params=_cparams(("arbitrary", "arbitrary"), 48),
        name="hgrn2",
    )(*operands)


def _merge_out_kernel(att_ref, rec_ref, ga_ref, gh_ref, x_ref, wa_ref, wh_ref, wo_ref, g_ref,
                      h_ref, *maybe_hn_ref):
    j = pl.program_id(1)

    @pl.when(j == 0)
    def _():
        h_ref[...] = x_ref[...]

    a = _dot(att_ref[...], wa_ref[...])
    r = _dot(rec_ref[...], wh_ref[...])
    merged = (_sigmoid(ga_ref[...].astype(F32)) * a + _sigmoid(gh_ref[...].astype(F32)) * r).astype(BF16)
    h_ref[...] += _dot(merged, wo_ref[...])

    if maybe_hn_ref:
        @pl.when(j == pl.num_programs(1) - 1)
        def _():
            h = h_ref[...]
            ms = jnp.mean(h * h, axis=-1, keepdims=True)
            maybe_hn_ref[0][...] = (h * lax.rsqrt(ms + EPS) * g_ref[...]).astype(BF16)


def merge_out(att, rec, proj_main, x, w_up_att, w_up_hgrn, w_o, norm_g, emit_hn, tm=512, tn=1024):
    m, d = x.shape
    ga0, gh0 = COL_GA * 1024 // tn, COL_GH * 1024 // tn
    row = lambda i, j: (i, 0)
    out_specs = [pl.BlockSpec((tm, d), row)]
    out_shape = [jax.ShapeDtypeStruct((m, d), F32)]
    if emit_hn:
        out_specs.append(pl.BlockSpec((tm, d), row))
        out_shape.append(jax.ShapeDtypeStruct((m, d), BF16))
    return pl.pallas_call(
        _merge_out_kernel,
        grid=(m // tm, d // tn),
        in_specs=[pl.BlockSpec((tm, ATT_W), row),
                  pl.BlockSpec((tm, HGRN_W), row),
                  pl.BlockSpec((tm, tn), lambda i, j: (i, ga0 + j)),
                  pl.BlockSpec((tm, tn), lambda i, j: (i, gh0 + j)),
                  pl.BlockSpec((tm, d), row),
                  pl.BlockSpec((ATT_W, tn), lambda i, j: (0, j)),
                  pl.BlockSpec((HGRN_W, tn), lambda i, j: (0, j)),
                  pl.BlockSpec((tn, d), lambda i, j: (j, 0)),
                  pl.BlockSpec((1, d), lambda i, j: (0, 0))],
        out_specs=out_specs,
        out_shape=out_shape,
        compiler_params=_cparams(("parallel", "arbitrary"), 48),
        name="merge_out",
    )(att, rec, proj_main, proj_main, x, w_up_att, w_up_hgrn, w_o, norm_g.reshape(1, d))


def _ffn_kernel(hn_ref, h_ref, w1_ref, w3_ref, w2_ref, g_ref, o_ref, *maybe_xn_ref):
    j = pl.program_id(1)

    @pl.when(j == 0)
    def _():
        o_ref[...] = h_ref[...]

    x = hn_ref[...]
    a = _dot(x, w1_ref[...])
    b = _dot(x, w3_ref[...])
    act = (a * _sigmoid(a) * b).astype(BF16)
    o_ref[...] += _dot(act, w2_ref[...])

    if maybe_xn_ref:
        @pl.when(j == pl.num_programs(1) - 1)
        def _():
            y = o_ref[...]
            ms = jnp.mean(y * y, axis=-1, keepdims=True)
            maybe_xn_ref[0][...] = (y * lax.rsqrt(ms + EPS) * g_ref[...]).astype(BF16)


def ffn_dense(hn, h, w1, w3, w2, next_norm_g, emit_xn, tm=512, tf=512):
    m, d = hn.shape
    f = w1.shape[1]
    row = lambda i, j: (i, 0)
    out_specs = [pl.BlockSpec((tm, d), row)]
    out_shape = [jax.ShapeDtypeStruct((m, d), F32)]
    if emit_xn:
        out_specs.append(pl.BlockSpec((tm, d), row))
        out_shape.append(jax.ShapeDtypeStruct((m, d), BF16))
    return pl.pallas_call(
        _ffn_kernel,
        grid=(m // tm, f // tf),
        in_specs=[pl.BlockSpec((tm, d), row),
                  pl.BlockSpec((tm, d), row),
                  pl.BlockSpec((d, tf), lambda i, j: (0, j)),
                  pl.BlockSpec((d, tf), lambda i, j: (0, j)),
                  pl.BlockSpec((tf, d), lambda i, j: (j, 0)),
                  pl.BlockSpec((1, d), lambda i, j: (0, 0))],
        out_specs=out_specs,
        out_shape=out_shape,
        compiler_params=_cparams(("parallel", "arbitrary"), 52),
        name="ffn_dense",
    )(hn, h, w1, w3, w2, next_norm_g.reshape(1, d))


def _row_copy(src_ref, src_row, dst_ref, dst_row, sem):
    return pltpu.make_async_copy(src_ref.at[pl.ds(src_row, 1)], dst_ref.at[pl.ds(dst_row, 1)], sem)


def _moe_scatter_kernel(dest_ref, hn_ref, xs_init_ref, xs_ref, sem, *, tm, m):
    del xs_init_ref
    base = pl.program_id(0) * tm

    def issue(r, carry):
        for k in range(2):
            _row_copy(hn_ref, r, xs_ref, dest_ref[k * m + base + r], sem).start()
        return carry

    lax.fori_loop(0, tm, issue, 0, unroll=8)
    for k in range(2):
        pltpu.make_async_copy(hn_ref, xs_ref.at[pl.ds(0, tm)], sem).wait()


def moe_scatter(hn_packed, dest, n_rows, tm=512):
    m, c = hn_packed.shape
    xs_init = jnp.zeros((n_rows, c), I32)
    return pl.pallas_call(
        functools.partial(_moe_scatter_kernel, tm=tm, m=m),
        grid_spec=pltpu.PrefetchScalarGridSpec(
            num_scalar_prefetch=1,
            grid=(m // tm,),
            in_specs=[pl.BlockSpec((tm, c), lambda i, dest: (i, 0)),
                      pl.BlockSpec(memory_space=pl.ANY)],
            out_specs=pl.BlockSpec(memory_space=pl.ANY),
            scratch_shapes=[pltpu.SemaphoreType.DMA(())]),
        out_shape=jax.ShapeDtypeStruct((n_rows, c), I32),
        input_output_aliases={2: 0},
        compiler_params=_cparams(("arbitrary",), 32),
        name="moe_scatter",
    )(dest, hn_packed, xs_init)


MOE_ROW_STEP = 128


def _moe_ffn_kernel(te_ref, tr_ref, xs_ref, w1_ref, w3_ref, w2_ref, o_ref, xb_ref):
    del te_ref
    i = pl.program_id(0)
    half = xs_ref.shape[1]
    tm = xs_ref.shape[0]

    @pl.when(pl.program_id(1) == 0)
    def _():
        o_ref[...] = jnp.zeros_like(o_ref)
        lo, hi = _unpack_bf16_pairs(xs_ref[...])
        xb_ref[:, 0:half] = lo
        xb_ref[:, half:2 * half] = hi

    groups = (tr_ref[i] + MOE_ROW_STEP - 1) // MOE_ROW_STEP
    for g in range(1, tm // MOE_ROW_STEP + 1):
        rows = g * MOE_ROW_STEP

        @pl.when(groups == g)
        def _():
            x = xb_ref[0:rows, :]
            a = _dot(x, w1_ref[0])
            b = _dot(x, w3_ref[0])
            act = (a * _sigmoid(a) * b).astype(BF16)
            o_ref[0:rows, :] += _dot(act, w2_ref[0])


def moe_ffn(xs, tile_expert, tile_rows, w1, w3, w2, tm=512, tf=1024):
    p, c = xs.shape
    ne, d, f = w1.shape
    nf = f // tf
    col = lambda i, j, te, tr: jnp.where(tr[i] > 0, j, nf - 1)
    return pl.pallas_call(
        _moe_ffn_kernel,
        grid_spec=pltpu.PrefetchScalarGridSpec(
            num_scalar_prefetch=2,
            grid=(p // tm, nf),
            in_specs=[pl.BlockSpec((tm, c), lambda i, j, te, tr: (i, 0)),
                      pl.BlockSpec((1, d, tf), lambda i, j, te, tr: (te[i], 0, col(i, j, te, tr))),
                      pl.BlockSpec((1, d, tf), lambda i, j, te, tr: (te[i], 0, col(i, j, te, tr))),
                      pl.BlockSpec((1, tf, d), lambda i, j, te, tr: (te[i], col(i, j, te, tr), 0))],
            out_specs=pl.BlockSpec((tm, d), lambda i, j, te, tr: (i, 0)),
            scratch_shapes=[pltpu.VMEM((tm, d), BF16)]),
        out_shape=jax.ShapeDtypeStruct((p, d), F32),
        compiler_params=_cparams(("arbitrary", "arbitrary"), 56),
        name="moe_ffn",
    )(tile_expert, tile_rows, xs, w1, w3, w2)


def _moe_combine_kernel(dest_ref, h_ref, prob_ref, ys_ref, o_ref, ybuf_ref, sem, *, tm, m):
    i = pl.program_id(0)
    slot = i % 2

    def gather_tile(tile, slot_):
        base = tile * tm

        def issue(r, carry):
            for k in range(2):
                _row_copy(ys_ref, dest_ref[k * m + base + r], ybuf_ref.at[slot_, k], r,
                          sem.at[slot_]).start()
            return carry

        lax.fori_loop(0, tm, issue, 0, unroll=8)

    @pl.when(i == 0)
    def _():
        gather_tile(0, 0)

    @pl.when(i + 1 < pl.num_programs(0))
    def _():
        gather_tile(i + 1, 1 - slot)

    for k in range(2):
        pltpu.make_async_copy(ys_ref.at[pl.ds(0, tm)], ybuf_ref.at[slot, k], sem.at[slot]).wait()
    prob = prob_ref[...]
    o_ref[...] = h_ref[...] + prob[:, 0:1] * ybuf_ref[slot, 0] + prob[:, 1:2] * ybuf_ref[slot, 1]


def moe_combine(h, prob, ys, dest, tm=512):
    m, d = h.shape
    return pl.pallas_call(
        functools.partial(_moe_combine_kernel, tm=tm, m=m),
        grid_spec=pltpu.PrefetchScalarGridSpec(
            num_scalar_prefetch=1,
            grid=(m // tm,),
            in_specs=[pl.BlockSpec((tm, d), lambda i, dest: (i, 0)),
                      pl.BlockSpec((tm, LANES), lambda i, dest: (i, 0)),
                      pl.BlockSpec(memory_space=pl.ANY)],
            out_specs=pl.BlockSpec((tm, d), lambda i, dest: (i, 0)),
            scratch_shapes=[pltpu.VMEM((2, 2, tm, d), F32), pltpu.SemaphoreType.DMA((2,))]),
        out_shape=jax.ShapeDtypeStruct((m, d), F32),
        compiler_params=_cparams(("arbitrary",), 48),
        name="moe_combine",
    )(dest, h, prob, ys)


def ffn_moe(h, norm_g, w_router, w1, w3, w2, tm=512):
    m, d = h.shape
    ne = w1.shape[0]
    hn_packed, meta, prob, cnt = rmsnorm_router(h, norm_g, w_router)
    counts = cnt[0, :ne].astype(I32)
    tiles_per = (counts + tm - 1) // tm
    tile_end = jnp.cumsum(tiles_per)
    offset = (tile_end - tiles_per) * tm
    experts = jnp.arange(ne, dtype=I32)

    def group_offset(e):
        return jnp.sum(jnp.where(e[:, None] == experts[None, :], offset[None, :], 0), axis=1)

    dest = jnp.concatenate([group_offset(meta[:, 0]) + meta[:, 2],
                            group_offset(meta[:, 1]) + meta[:, 3]])
    n_tiles = (2 * m) // tm + ne
    tile_ids = jnp.arange(n_tiles, dtype=I32)
    last_valid = jnp.minimum(tile_ids, tile_end[-1] - 1)
    tile_expert = jnp.sum((tile_end[None, :] <= last_valid[:, None]).astype(I32), axis=1)
    mine = tile_expert[:, None] == experts[None, :]
    first_tile = jnp.sum(jnp.where(mine, (tile_end - tiles_per)[None, :], 0), axis=1)
    count = jnp.sum(jnp.where(mine, counts[None, :], 0), axis=1)
    tile_rows = jnp.where(tile_ids < tile_end[-1],
                          jnp.clip(count - (tile_ids - first_tile) * tm, 0, tm), 0).astype(I32)
    xs = moe_scatter(hn_packed, dest, n_tiles * tm)
    ys = moe_ffn(xs, tile_expert, tile_rows, w1, w3, w2, tm=tm)
    return moe_combine(h, prob, ys, dest)


W_IN_HEAD = ATT_W + 2 * KV_W + IDX_Q_RANK
W_IN_SMALL = IDX_HEAD_DIM + IDX_HEADS
W_IN_ROWS = 1024


def _w_in_prep_kernel(wt_hbm, main_ref, idx_ref, buf_ref, sem, *, layer):
    c = pl.program_id(0)

    def fetch(row0, n_rows):
        cp = pltpu.make_async_copy(wt_hbm.at[layer, pl.ds(row0, n_rows)],
                                   buf_ref.at[pl.ds(0, n_rows)], sem)
        cp.start()
        cp.wait()

    @pl.when(c == 0)
    def _():
        fetch(W_IN_HEAD, LANES)
        rows = buf_ref[0:LANES, :]
        keep = lax.broadcasted_iota(I32, rows.shape, 0) < W_IN_SMALL
        idx_ref[...] = jnp.where(keep, rows, 0.0).astype(BF16)

    base = c * W_IN_ROWS
    row0 = pl.multiple_of(jnp.where(base < W_IN_HEAD, base, base + W_IN_SMALL), 8)
    fetch(row0, W_IN_ROWS)
    main_ref[...] = buf_ref[...].astype(BF16)


def _split_w_in(w_in, layer):
    _, n, d = w_in.shape
    assert n == MAIN_W + W_IN_SMALL and W_IN_HEAD % W_IN_ROWS == 0
    return pl.pallas_call(
        functools.partial(_w_in_prep_kernel, layer=layer),
        grid=(MAIN_W // W_IN_ROWS,),
        in_specs=[pl.BlockSpec(memory_space=pl.ANY)],
        out_specs=[pl.BlockSpec((W_IN_ROWS, d), lambda c: (c, 0)),
                   pl.BlockSpec((LANES, d), lambda c: (0, 0))],
        out_shape=[jax.ShapeDtypeStruct((MAIN_W, d), BF16),
                   jax.ShapeDtypeStruct((LANES, d), BF16)],
        scratch_shapes=[pltpu.VMEM((W_IN_ROWS, d), F32), pltpu.SemaphoreType.DMA(())],
        compiler_params=_cparams(("arbitrary",), 40),
        name="w_in_prep",
    )(w_in)


def _mixer(x2, maybe_xn, batch, seq, norm_g, w_in, q_norm_g, k_norm_g, idx_q_norm_g, w_idx_q, ln_g,
           ln_b, lb, hgrn_norm_g, w_up_att, w_up_hgrn, w_o, bias, norm_ffn_g, emit_hn,
           side_cast_att=None, side_cast_rec=None):
    w_main, w_idx = _split_w_in(*w_in)
    if maybe_xn is None:
        proj_main, xn = rmsnorm_matmul(x2, norm_g, w_main, BF16)
    else:
        xn = maybe_xn
        proj_main = matmul(xn, w_main, BF16)
    proj_idx = matmul(xn, w_idx, F32)
    qn, kn, vt, qi, ki, wt = attn_prep(proj_main, proj_idx, q_norm_g, k_norm_g, idx_q_norm_g,
                                       w_idx_q.astype(BF16), ln_g, ln_b, seq)
    vt = vt.reshape(batch, seq // TK, KV_W, TK)
    att, *cast_att = dsa_attention(qi, wt, qn, ki, kn, vt, bias, batch, seq, side_cast_att)
    rec, *cast_rec = hgrn2(proj_main, lb, hgrn_norm_g, batch, seq, side_cast_rec)
    outs = merge_out(att, rec, proj_main, x2, w_up_att.astype(BF16), w_up_hgrn.astype(BF16),
                     w_o.astype(BF16), norm_ffn_g, emit_hn)
    return outs, (cast_att[0] if cast_att else None), (cast_rec[0] if cast_rec else None)


def kernel(x, rel_bias, norm_mix_g, norm_ffn_g, w_in, q_norm_g, k_norm_g, idx_q_norm_g, w_idx_q,
           idx_k_ln_g, idx_k_ln_b, hgrn_lb_logits, hgrn_out_norm_g, w_up_att, w_up_hgrn, w_o,
           w1_dense, w3_dense, w2_dense, w_router, w1_moe, w3_moe, w2_moe):
    batch, seq, d = x.shape
    depth = w_in.shape[0]
    lb_all = jnp.cumsum(jax.nn.softmax(hgrn_lb_logits.astype(F32), axis=0), axis=0)
    lb_all = lb_all - lb_all[0:1]
    bias = bias_tables(rel_bias)
    w_in_t = jnp.swapaxes(w_in, 1, 2)
    x2 = x.reshape(batch * seq, d)
    maybe_xn = None
    n_split_ff = w1_moe.shape[-1] // 1024
    moe_bf16 = {}
    for l in range(depth):
        dense = l % 2 == 0
        j = l // 2
        side_att = side_rec = None
        if dense and l + 1 < depth:
            side_att = (w3_moe[j], 2, n_split_ff)
            side_rec = (w1_moe[j], 2, 2 * n_split_ff)
        elif not dense:
            side_att = (w2_moe[j], 1, n_split_ff)
        (h, *maybe_hn), cast_att, cast_rec = _mixer(
            x2, maybe_xn, batch, seq, norm_mix_g[l], (w_in_t, l), q_norm_g[l], k_norm_g[l],
            idx_q_norm_g[l], w_idx_q[l], idx_k_ln_g[l], idx_k_ln_b[l], lb_all[l],
            hgrn_out_norm_g[l], w_up_att[l], w_up_hgrn[l], w_o[l], bias, norm_ffn_g[l],
            emit_hn=dense, side_cast_att=side_att, side_cast_rec=side_rec)
        maybe_xn = None
        if dense:
            more = l + 1 < depth
            moe_bf16["w3"], moe_bf16["w1"] = cast_att, cast_rec
            x2, *rest = ffn_dense(maybe_hn[0], h, w1_dense[j].astype(BF16), w3_dense[j].astype(BF16),
                                  w2_dense[j].astype(BF16), norm_mix_g[l + 1 if more else l], more)
            maybe_xn = rest[0] if more else None
        else:
            x2 = ffn_moe(h, norm_ffn_g[l], w_router[j], moe_bf16["w1"], moe_bf16["w3"], cast_att)
    return x2.reshape(batch, seq, d)
```

```python
import functools
import math

import numpy as np
import jax
import jax.numpy as jnp
from jax import lax
from jax.experimental import pallas as pl
from jax.experimental.pallas import tpu as pltpu

F32 = jnp.float32
BF16 = jnp.bfloat16
I32 = jnp.int32

EPS = 1e-6
ATT_HEADS = 8
ATT_KV_HEADS = 2
ATT_GROUP = ATT_HEADS // ATT_KV_HEADS
HEAD_DIM = 128
ATT_W = ATT_HEADS * HEAD_DIM
KV_W = ATT_KV_HEADS * HEAD_DIM
IDX_HEADS = 16
IDX_HEAD_DIM = 64
IDX_Q_RANK = 512
TOPK_MAX = 256
HGRN_HEADS = 8
HGRN_DIM = 128
HGRN_W = HGRN_HEADS * HGRN_DIM
REL_BUCKETS = 32
REL_MAX_DIST = 128
N_EXPERTS = 8
LANES = 128
INT_MIN = -(2 ** 31)
NEG_BIG = -1e30
LOG2E = math.log2(math.e)

TQ = 256
TK = 256
HC = 128
HGRN_LEVELS = (1, 2, 4, 8, 16, 32, 64)
N_ARG_GROUPS = len(HGRN_LEVELS) + 2

COL_Q, COL_KVC, COL_HQ, COL_HF, COL_HI, COL_HG, COL_GA, COL_GH = 0, 1, 2, 3, 4, 5, 6, 8
MAIN_W = 10 * 1024


def _cparams(sem, vmem_mb):
    return pltpu.CompilerParams(dimension_semantics=sem, vmem_limit_bytes=vmem_mb << 20)


def _dot(a, b):
    return jnp.dot(a, b, preferred_element_type=F32)


def _dot_nt(a, b):
    return lax.dot_general(a, b, (((1,), (1,)), ((), ())), preferred_element_type=F32)


def _sigmoid(x):
    return 1.0 / (1.0 + jnp.exp(-x))


def _pack_bf16_pairs(x):
    c = x.shape[1] // 2
    lo = lax.bitcast_convert_type(x[:, :c].astype(BF16).astype(F32), I32)
    hi = lax.bitcast_convert_type(x[:, c:].astype(BF16).astype(F32), I32)
    return (hi & jnp.int32(-65536)) | lax.shift_right_logical(lo, 16)


def _unpack_bf16_pairs(p):
    lo = lax.bitcast_convert_type(lax.shift_left(p, 16), F32).astype(BF16)
    hi = lax.bitcast_convert_type(p & jnp.int32(-65536), F32).astype(BF16)
    return lo, hi


def _rmsnorm_router_kernel(x_ref, g_ref, wr_ref, tri_ref, o_ref, meta_ref, prob_ref, cnt_ref, carry_ref):
    @pl.when(pl.program_id(0) == 0)
    def _():
        carry_ref[...] = jnp.zeros_like(carry_ref)

    x = x_ref[...]
    ms = jnp.mean(x * x, axis=-1, keepdims=True)
    hn = x * lax.rsqrt(ms + EPS) * g_ref[...]
    o_ref[...] = _pack_bf16_pairs(hn)
    logits = _dot(hn.astype(BF16), wr_ref[...])
    lane = lax.broadcasted_iota(I32, logits.shape, 1)
    logits = jnp.where(lane < N_EXPERTS, logits, -jnp.inf)
    v1 = jnp.max(logits, axis=-1, keepdims=True)
    i1 = jnp.min(jnp.where(logits == v1, lane, LANES), axis=-1, keepdims=True)
    rest = jnp.where(lane == i1, -jnp.inf, logits)
    v2 = jnp.max(rest, axis=-1, keepdims=True)
    i2 = jnp.min(jnp.where(rest == v2, lane, LANES), axis=-1, keepdims=True)
    e = jnp.exp(v2 - v1)
    p1 = 1.0 / (1.0 + e)
    prob_ref[...] = jnp.where(lane == 0, p1, 0.0) + jnp.where(lane == 1, e * p1, 0.0)
    hot = jnp.where((lane == i1) | (lane == i2), 1.0, 0.0)
    rank = _dot(tri_ref[...], hot.astype(BF16)) + carry_ref[0:1, :]
    carry_ref[...] = carry_ref[...] + jnp.sum(hot, axis=0, keepdims=True)
    cnt_ref[...] = carry_ref[...]
    r1 = jnp.sum(jnp.where(lane == i1, rank, 0.0), axis=-1, keepdims=True)
    r2 = jnp.sum(jnp.where(lane == i2, rank, 0.0), axis=-1, keepdims=True)
    meta = (jnp.where(lane == 0, i1, 0) + jnp.where(lane == 1, i2, 0)
            + jnp.where(lane == 2, r1.astype(I32), 0) + jnp.where(lane == 3, r2.astype(I32), 0))
    meta_ref[...] = meta


def rmsnorm_router(x, g, w_router, tm=512):
    m, d = x.shape
    wr = jnp.zeros((d, LANES), BF16).at[:, :N_EXPERTS].set(w_router.astype(BF16))
    tri = jnp.asarray(np.tril(np.ones((tm, tm), np.float32), -1), BF16)
    return pl.pallas_call(
        _rmsnorm_router_kernel,
        grid=(m // tm,),
        in_specs=[pl.BlockSpec((tm, d), lambda i: (i, 0)),
                  pl.BlockSpec((1, d), lambda i: (0, 0)),
                  pl.BlockSpec((d, LANES), lambda i: (0, 0)),
                  pl.BlockSpec((tm, tm), lambda i: (0, 0))],
        out_specs=[pl.BlockSpec((tm, d // 2), lambda i: (i, 0)),
                   pl.BlockSpec((tm, LANES), lambda i: (i, 0)),
                   pl.BlockSpec((tm, LANES), lambda i: (i, 0)),
                   pl.BlockSpec((8, LANES), lambda i: (0, 0))],
        out_shape=[jax.ShapeDtypeStruct((m, d // 2), I32),
                   jax.ShapeDtypeStruct((m, LANES), I32),
                   jax.ShapeDtypeStruct((m, LANES), F32),
                   jax.ShapeDtypeStruct((8, LANES), F32)],
        scratch_shapes=[pltpu.VMEM((8, LANES), F32)],
        compiler_params=_cparams(("arbitrary",), 40),
        name="rmsnorm_router",
    )(x, g.reshape(1, d), wr, tri)


def _matmul_kernel(x_ref, wt_ref, o_ref):
    o_ref[...] = _dot_nt(x_ref[...], wt_ref[...]).astype(o_ref.dtype)


def _rmsnorm_matmul_kernel(x_ref, g_ref, wt_ref, o_ref, xn_o_ref, xn_ref):
    @pl.when(pl.program_id(1) == 0)
    def _():
        x = x_ref[...]
        ms = jnp.mean(x * x, axis=-1, keepdims=True)
        xn_ref[...] = (x * lax.rsqrt(ms + EPS) * g_ref[...]).astype(BF16)
        xn_o_ref[...] = xn_ref[...]

    o_ref[...] = _dot_nt(xn_ref[...], wt_ref[...]).astype(o_ref.dtype)


def rmsnorm_matmul(x, g, wt, out_dtype, tm=1024, tn=1024):
    m, k = x.shape
    n = wt.shape[0]
    tm, tn = min(tm, m), min(tn, n)
    return pl.pallas_call(
        _rmsnorm_matmul_kernel,
        grid=(m // tm, n // tn),
        in_specs=[pl.BlockSpec((tm, k), lambda i, j: (i, 0)),
                  pl.BlockSpec((1, k), lambda i, j: (0, 0)),
                  pl.BlockSpec((tn, k), lambda i, j: (j, 0))],
        out_specs=[pl.BlockSpec((tm, tn), lambda i, j: (i, j)),
                   pl.BlockSpec((tm, k), lambda i, j: (i, 0))],
        out_shape=[jax.ShapeDtypeStruct((m, n), out_dtype),
                   jax.ShapeDtypeStruct((m, k), BF16)],
        scratch_shapes=[pltpu.VMEM((tm, k), BF16)],
        compiler_params=_cparams(("parallel", "arbitrary"), 48),
        name="rmsnorm_matmul",
    )(x, g.reshape(1, k), wt)


def matmul(x, wt, out_dtype, tm=1024, tn=1024):
    m, k = x.shape
    n = wt.shape[0]
    tm, tn = min(tm, m), min(tn, n)
    return pl.pallas_call(
        _matmul_kernel,
        grid=(m // tm, n // tn),
        in_specs=[pl.BlockSpec((tm, k), lambda i, j: (i, 0)),
                  pl.BlockSpec((tn, k), lambda i, j: (j, 0))],
        out_specs=pl.BlockSpec((tm, tn), lambda i, j: (i, j)),
        out_shape=jax.ShapeDtypeStruct((m, n), out_dtype),
        compiler_params=_cparams(("parallel", "parallel"), 48),
        name="matmul",
    )(x, wt)


def _attn_prep_kernel(q_ref, kvc_ref, pi_ref, qg_ref, kg_ref, cg_ref, wiq_ref, lng_ref, lnb_ref,
                      qn_ref, kn_ref, vt_ref, qi_ref, ki_ref, wt_ref):
    def head_rms(x, g):
        x = x.astype(F32)
        return x * lax.rsqrt(jnp.mean(x * x, axis=-1, keepdims=True) + EPS) * g

    att_scale = HEAD_DIM ** -0.5 * LOG2E
    for h in range(ATT_HEADS):
        sl = slice(h * HEAD_DIM, (h + 1) * HEAD_DIM)
        qn_ref[:, sl] = (head_rms(q_ref[:, sl], qg_ref[...]) * att_scale).astype(BF16)
    for g in range(ATT_KV_HEADS):
        sl = slice(g * HEAD_DIM, (g + 1) * HEAD_DIM)
        kn_ref[:, sl] = head_rms(kvc_ref[:, sl], kg_ref[...]).astype(BF16)
    for c in range(vt_ref.shape[0]):
        vc = kvc_ref[c * TK:(c + 1) * TK, KV_W:2 * KV_W].astype(F32)
        vt_ref[c] = vc.T.astype(BF16)
    cq = kvc_ref[:, 2 * KV_W:2 * KV_W + IDX_Q_RANK]
    cqn = head_rms(cq, cg_ref[...]).astype(BF16)
    qi_ref[...] = _dot(cqn, wiq_ref[...]).astype(BF16)
    pi = pi_ref[...]
    is_key = lax.broadcasted_iota(I32, pi.shape, 1) < IDX_HEAD_DIM
    mu = jnp.sum(jnp.where(is_key, pi, 0.0), axis=-1, keepdims=True) * (1.0 / IDX_HEAD_DIM)
    cen = jnp.where(is_key, pi - mu, 0.0)
    var = jnp.sum(cen * cen, axis=-1, keepdims=True) * (1.0 / IDX_HEAD_DIM)
    kidn = cen * lax.rsqrt(var + EPS) * lng_ref[...] + lnb_ref[...]
    ki_ref[:, 0:LANES] = kidn.astype(BF16)
    ki_ref[:, LANES:2 * LANES] = pltpu.roll(kidn, IDX_HEAD_DIM, 1).astype(BF16)
    wt = (pi * (IDX_HEAD_DIM ** -0.5 * IDX_HEADS ** -0.5)).T
    wt_ref[0] = wt[IDX_HEAD_DIM:IDX_HEAD_DIM + IDX_HEADS, :]


def attn_prep(proj_main, proj_idx, q_norm_g, k_norm_g, idx_q_norm_g, w_idx_q, ln_g, ln_b, seq, tm=512):
    m = proj_main.shape[0]
    tm = min(tm, seq)
    per_seq = seq // tm
    row = lambda i: (i, 0)
    const = lambda i: (0, 0)
    lane_pad = lambda a: jnp.zeros((1, LANES), F32).at[0, :a.shape[0]].set(a)
    return pl.pallas_call(
        _attn_prep_kernel,
        grid=(m // tm,),
        in_specs=[pl.BlockSpec((tm, 1024), lambda i: (i, COL_Q)),
                  pl.BlockSpec((tm, 1024), lambda i: (i, COL_KVC)),
                  pl.BlockSpec((tm, LANES), row),
                  pl.BlockSpec((1, HEAD_DIM), const),
                  pl.BlockSpec((1, HEAD_DIM), const),
                  pl.BlockSpec((1, IDX_Q_RANK), const),
                  pl.BlockSpec((IDX_Q_RANK, IDX_HEADS * IDX_HEAD_DIM), const),
                  pl.BlockSpec((1, LANES), const),
                  pl.BlockSpec((1, LANES), const)],
        out_specs=[pl.BlockSpec((tm, ATT_W), row),
                   pl.BlockSpec((tm, KV_W), row),
                   pl.BlockSpec((tm // TK, KV_W, TK), lambda i: (i, 0, 0)),
                   pl.BlockSpec((tm, IDX_HEADS * IDX_HEAD_DIM), row),
                   pl.BlockSpec((tm, 2 * LANES), row),
                   pl.BlockSpec((1, IDX_HEADS, tm), lambda i: (i // per_seq, 0, i % per_seq))],
        out_shape=[jax.ShapeDtypeStruct((m, ATT_W), BF16),
                   jax.ShapeDtypeStruct((m, KV_W), BF16),
                   jax.ShapeDtypeStruct((m // TK, KV_W, TK), BF16),
                   jax.ShapeDtypeStruct((m, IDX_HEADS * IDX_HEAD_DIM), BF16),
                   jax.ShapeDtypeStruct((m, 2 * LANES), BF16),
                   jax.ShapeDtypeStruct((m // seq, IDX_HEADS, seq), F32)],
        compiler_params=_cparams(("parallel",), 40),
        name="attn_prep",
    )(proj_main, proj_main, proj_idx, q_norm_g.reshape(1, -1), k_norm_g.reshape(1, -1),
      idx_q_norm_g.reshape(1, -1), w_idx_q, lane_pad(ln_g), lane_pad(ln_b))


def _rel_bucket(dist):
    max_exact = REL_BUCKETS // 2
    n = jnp.maximum(dist, 0)
    nf = jnp.maximum(n, 1).astype(F32)
    large = max_exact + (jnp.log(nf / max_exact) / math.log(REL_MAX_DIST / max_exact)
                         * (REL_BUCKETS - max_exact)).astype(I32)
    large = jnp.minimum(large, REL_BUCKETS - 1)
    return jnp.where(n < max_exact, n, large)


def _bias_kernel(rb_ref, bkt_ref, o_ref):
    h = pl.program_id(0)
    for kind in range(3):
        b = bkt_ref[kind]
        acc = jnp.zeros(b.shape, F32)
        for n in range(REL_BUCKETS):
            acc = jnp.where(b == n, rb_ref[n, h], acc)
        o_ref[0, kind] = acc * LOG2E


def bias_tables(rel_bias):
    assert TK >= REL_MAX_DIST
    kpos = jnp.arange(TK, dtype=I32)[:, None]
    qpos = jnp.arange(TQ, dtype=I32)[None, :]
    dist = jnp.stack([qpos - kpos, qpos - kpos + TK, qpos - kpos + 2 * TK])
    bkt = _rel_bucket(dist)
    return pl.pallas_call(
        _bias_kernel,
        grid=(ATT_HEADS,),
        in_specs=[pl.BlockSpec(memory_space=pltpu.SMEM),
                  pl.BlockSpec((3, TK, TQ), lambda h: (0, 0, 0))],
        out_specs=pl.BlockSpec((1, 3, TK, TQ), lambda h: (h, 0, 0, 0)),
        out_shape=jax.ShapeDtypeStruct((ATT_HEADS, 3, TK, TQ), F32),
        compiler_params=_cparams(("arbitrary",), 32),
        name="bias_tables",
    )(rel_bias, bkt)


def _side_cast_specs(src, split_axis, n_split, n_steps, step_of):
    ne = src.shape[0]
    n_blocks = ne * n_split
    every = n_steps // n_blocks
    assert every >= 1, "not enough grid steps to walk the weight stack"
    block = [1, src.shape[1], src.shape[2]]
    block[split_axis] //= n_split

    def index_map(*grid_idx):
        t = jnp.minimum(step_of(*grid_idx) // every, n_blocks - 1)
        idx = [t // n_split, 0, 0]
        idx[split_axis] = t % n_split
        return tuple(idx)

    spec = pl.BlockSpec(tuple(block), index_map)
    return spec, spec, jax.ShapeDtypeStruct(src.shape, BF16)


def _dsa_kernel(qi_ref, wt_ref, qn_ref, ki_ref, kn_ref, vt_ref, bias_ref, *rest, n_sel, side_cast):
    if side_cast:
        wf_ref, o_ref, wb_ref, key_ref, thr_ref, m_ref, l_ref, a_ref, acc_ref, s_ref, p_ref = rest
        wb_ref[...] = wf_ref[...].astype(BF16)
    else:
        o_ref, key_ref, thr_ref, m_ref, l_ref, a_ref, acc_ref, s_ref, p_ref = rest
    i = pl.program_id(1)
    nch = i + 1
    qpos = i * TQ + lax.broadcasted_iota(I32, (TK, TQ), 1)

    def score_chunk(j, carry):
        k0 = pl.multiple_of(j * TK, TK)
        acc = jnp.zeros((TK, TQ), F32)
        for h in range(IDX_HEADS):
            par = h % 2
            kc = ki_ref[pl.ds(k0, TK), par * LANES:(par + 1) * LANES]
            qh = qi_ref[:, (h // 2) * LANES:(h // 2 + 1) * LANES]
            s = _dot_nt(kc, qh)
            acc = acc + jnp.maximum(s, 0.0) * wt_ref[0, h:h + 1, :]
        bits = lax.bitcast_convert_type(acc, I32)
        key = jnp.where(bits < 0, bits ^ jnp.int32(0x7FFFFFFF), bits)
        kpos = k0 + lax.broadcasted_iota(I32, (TK, TQ), 0)
        key_ref[pl.ds(k0, TK), :] = jnp.where(kpos <= qpos, key, INT_MIN)
        return carry

    lax.fori_loop(0, nch, score_chunk, 0)

    def bisect(n_chunks):
        def count_ge(cand):
            acc = jnp.zeros((8, TQ), I32)
            for c in range(n_chunks):
                hit = jnp.where(key_ref[c * TK:(c + 1) * TK, :] >= cand, 1, 0).astype(I32)
                acc = acc + jnp.sum(hit.reshape(TK // 8, 8, TQ), axis=0)
            return jnp.sum(acc, axis=0, keepdims=True)

        def bit_body(b, thr):
            cand = thr | jnp.left_shift(jnp.int32(1), 30 - b)
            return jnp.where(count_ge(cand) >= n_sel, cand, thr)

        thr0 = jnp.where(count_ge(jnp.zeros((1, TQ), I32)) >= n_sel, 0, INT_MIN).astype(I32)
        return jnp.maximum(lax.fori_loop(0, 31, bit_body, thr0), INT_MIN + 1)

    for v in range(key_ref.shape[0] // TK):
        @pl.when(i == v)
        def _():
            thr_ref[...] = jnp.broadcast_to(bisect(v + 1), thr_ref.shape)

    thr = thr_ref[0:1, :]

    m_ref[...] = jnp.full(m_ref.shape, NEG_BIG, F32)
    l_ref[...] = jnp.zeros(l_ref.shape, F32)
    acc_ref[...] = jnp.zeros(acc_ref.shape, F32)

    def att_chunk(j, carry):
        k0 = pl.multiple_of(j * TK, TK)
        neg = jnp.where(key_ref[pl.ds(k0, TK), :] >= thr, 0.0, NEG_BIG)
        kind = jnp.minimum(i - j, 2)
        for h in range(ATT_HEADS):
            g = h // ATT_GROUP
            kc = kn_ref[pl.ds(k0, TK), g * HEAD_DIM:(g + 1) * HEAD_DIM]
            qh = qn_ref[:, h * HEAD_DIM:(h + 1) * HEAD_DIM]
            lg = _dot_nt(kc, qh) + bias_ref[h, kind] + neg
            s_ref[h] = lg
            m_old = m_ref[h:h + 1, :]
            m_new = jnp.maximum(m_old, jnp.max(lg, axis=0, keepdims=True))
            a_ref[h:h + 1, :] = jnp.exp2(m_old - m_new)
            m_ref[h:h + 1, :] = m_new
        for h in range(ATT_HEADS):
            p = jnp.exp2(s_ref[h] - m_ref[h:h + 1, :])
            l_ref[h:h + 1, :] = (a_ref[h:h + 1, :] * l_ref[h:h + 1, :]
                                 + jnp.sum(p, axis=0, keepdims=True))
            p_ref[h] = p.astype(BF16)
        for h in range(ATT_HEADS):
            g = h // ATT_GROUP
            vc = vt_ref[0, j, g * HEAD_DIM:(g + 1) * HEAD_DIM, :]
            acc_ref[h] = a_ref[h:h + 1, :] * acc_ref[h] + _dot(vc, p_ref[h])
        return carry

    lax.fori_loop(0, nch, att_chunk, 0)
    for h in range(ATT_HEADS):
        o = acc_ref[h] * (1.0 / l_ref[h:h + 1, :])
        o_ref[:, h * HEAD_DIM:(h + 1) * HEAD_DIM] = o.T.astype(o_ref.dtype)


def dsa_attention(qi, wt, qn, ki, kn, vt, bias, batch, seq, side_cast=None):
    nq = seq // TQ
    n_sel = min(TOPK_MAX, seq // 4)
    qrow = lambda b, i: (b * nq + i, 0)
    brow = lambda b, i: (b, 0)
    in_specs = [pl.BlockSpec((TQ, IDX_HEADS * IDX_HEAD_DIM), qrow),
                pl.BlockSpec((1, IDX_HEADS, TQ), lambda b, i: (b, 0, i)),
                pl.BlockSpec((TQ, ATT_W), qrow),
                pl.BlockSpec((seq, 2 * LANES), brow),
                pl.BlockSpec((seq, KV_W), brow),
                pl.BlockSpec((1, seq // TK, KV_W, TK), lambda b, i: (b, 0, 0, 0)),
                pl.BlockSpec((ATT_HEADS, 3, TK, TQ), lambda b, i: (0, 0, 0, 0))]
    out_specs = [pl.BlockSpec((TQ, ATT_W), qrow)]
    out_shape = [jax.ShapeDtypeStruct((batch * seq, ATT_W), BF16)]
    operands = [qi, wt, qn, ki, kn, vt, bias]
    if side_cast is not None:
        src, split_axis, n_split = side_cast
        spec_in, spec_out, shape_out = _side_cast_specs(src, split_axis, n_split, batch * nq,
                                                        lambda b, i: b * nq + i)
        in_specs.append(spec_in)
        out_specs.append(spec_out)
        out_shape.append(shape_out)
        operands.append(src)
    return pl.pallas_call(
        functools.partial(_dsa_kernel, n_sel=n_sel, side_cast=side_cast is not None),
        grid=(batch, nq),
        in_specs=in_specs,
        out_specs=out_specs,
        out_shape=out_shape,
        scratch_shapes=[pltpu.VMEM((seq, TQ), I32),
                        pltpu.VMEM((8, TQ), I32),
                        pltpu.VMEM((ATT_HEADS, TQ), F32),
                        pltpu.VMEM((ATT_HEADS, TQ), F32),
                        pltpu.VMEM((ATT_HEADS, TQ), F32),
                        pltpu.VMEM((ATT_HEADS, HEAD_DIM, TQ), F32),
                        pltpu.VMEM((ATT_HEADS, TK, TQ), F32),
                        pltpu.VMEM((ATT_HEADS, TK, TQ), BF16)],
        compiler_params=_cparams(("arbitrary", "arbitrary"), 56),
        name="dsa_attention",
    )(*operands)


def _hgrn_constants():
    t = np.arange(HC)
    rows = []
    masks = [np.eye(HC, dtype=np.float32)]
    for m in HGRN_LEVELS:
        upper = (t // m) % 2 == 1
        start = (t // m) * m
        end = start + m - 1
        u = t[None, :]
        q_side = upper[:, None] & (u >= start[:, None]) & (u <= t[:, None])
        k_side = (~upper)[:, None] & (u > t[:, None]) & (u <= end[:, None])
        rows.append((q_side | k_side).astype(np.float32))
        same = (t[:, None] // (2 * m)) == (t[None, :] // (2 * m))
        masks.append((upper[:, None] & (~upper)[None, :] & same).astype(np.float32))
    u = t[None, :]
    rows.append((u <= t[:, None]).astype(np.float32))
    rows.append((u > t[:, None]).astype(np.float32))
    mat = np.concatenate(rows, axis=0)
    return np.concatenate([mat, mat], axis=1), np.stack(masks)


def _hgrn_kernel(hq_ref, hf_ref, hi_ref, hg_ref, lb_ref, gn_ref, mat_ref, mask_ref, *rest,
                 side_cast):
    if side_cast:
        wf_ref, o_ref, wb_ref, st_ref, arg_ref, kk_ref, lf_ref, zq_ref, zk_ref = rest
        wb_ref[...] = wf_ref[...].astype(BF16)
    else:
        o_ref, st_ref, arg_ref, kk_ref, lf_ref, zq_ref, zk_ref = rest
    c = pl.program_id(1)

    @pl.when(c == 0)
    def _():
        st_ref[...] = jnp.zeros_like(st_ref)

    lb = lb_ref[...]
    f = hf_ref[...].astype(F32)
    e = jnp.exp(-jnp.abs(f))
    r = 1.0 / (1.0 + e)
    log_sig = jnp.minimum(f, 0.0) - jnp.log(1.0 + e)
    la = jnp.log(lb)
    lc = jnp.log(1.0 - lb) + log_sig
    logf = (jnp.maximum(la, lc) + jnp.log(1.0 + jnp.exp(-jnp.abs(la - lc)))) * LOG2E
    kk_ref[...] = (1.0 - lb) * jnp.where(f >= 0, e * r, r)
    hi = logf.astype(BF16)
    lf_ref[0:HC, :] = hi
    lf_ref[HC:2 * HC, :] = (logf - hi.astype(F32)).astype(BF16)
    arg_ref[...] = _dot(mat_ref[...], lf_ref[...])

    scale = HGRN_DIM ** -0.5
    nl = len(HGRN_LEVELS)
    for h in range(HGRN_HEADS):
        sl = slice(h * HGRN_DIM, (h + 1) * HGRN_DIM)
        hq = hq_ref[:, sl].astype(F32)
        q = hq * _sigmoid(hq) * scale
        k = kk_ref[:, sl]
        zq_ref[h, 0] = q.astype(BF16)
        zk_ref[h, 0] = k.astype(BF16)
        for lv in range(nl):
            ex = jnp.exp2(arg_ref[lv * HC:(lv + 1) * HC, sl])
            zq_ref[h, lv + 1] = (q * ex).astype(BF16)
            zk_ref[h, lv + 1] = (k * ex).astype(BF16)
        zq_ref[h, nl + 1] = (q * jnp.exp2(arg_ref[nl * HC:(nl + 1) * HC, sl])).astype(BF16)
        zk_ref[h, nl + 1] = (k * jnp.exp2(arg_ref[(nl + 1) * HC:(nl + 2) * HC, sl])).astype(BF16)

    for h in range(HGRN_HEADS):
        sl = slice(h * HGRN_DIM, (h + 1) * HGRN_DIM)
        a = _dot_nt(zq_ref[h, 0], zk_ref[h, 0]) * mask_ref[0]
        for lv in range(nl):
            a = a + _dot_nt(zq_ref[h, lv + 1], zk_ref[h, lv + 1]) * mask_ref[lv + 1]
        v = hi_ref[:, sl].astype(BF16)
        st = st_ref[h]
        o = _dot(a.astype(BF16), v) + _dot_nt(zq_ref[h, nl + 1], st.astype(BF16))
        decay = jnp.exp2(arg_ref[(nl + 1) * HC - 1:(nl + 1) * HC, sl])
        st_ref[h] = decay * st + _dot(v.astype(F32).T.astype(BF16), zk_ref[h, nl + 1])
        on = o * lax.rsqrt(jnp.mean(o * o, axis=-1, keepdims=True) + EPS) * gn_ref[...]
        hg = hg_ref[:, sl].astype(F32)
        o_ref[:, sl] = (on * hg * _sigmoid(hg)).astype(o_ref.dtype)


def hgrn2(proj_main, lb, g_norm, batch, seq, side_cast=None):
    nc = seq // HC
    mat, masks = _hgrn_constants()
    col = lambda cb: (lambda b, c: (b * nc + c, cb))
    in_specs = [pl.BlockSpec((HC, HGRN_W), col(COL_HQ)),
                pl.BlockSpec((HC, HGRN_W), col(COL_HF)),
                pl.BlockSpec((HC, HGRN_W), col(COL_HI)),
                pl.BlockSpec((HC, HGRN_W), col(COL_HG)),
                pl.BlockSpec((1, HGRN_W), lambda b, c: (0, 0)),
                pl.BlockSpec((1, HGRN_DIM), lambda b, c: (0, 0)),
                pl.BlockSpec((N_ARG_GROUPS * HC, 2 * HC), lambda b, c: (0, 0)),
                pl.BlockSpec((len(HGRN_LEVELS) + 1, HC, HC), lambda b, c: (0, 0, 0))]
    operands = [proj_main, proj_main, proj_main, proj_main, lb.reshape(1, -1), g_norm.reshape(1, -1),
                jnp.asarray(mat, BF16), jnp.asarray(masks, F32)]
    out_specs = [pl.BlockSpec((HC, HGRN_W), lambda b, c: (b * nc + c, 0))]
    out_shape = [jax.ShapeDtypeStruct((batch * seq, HGRN_W), BF16)]
    if side_cast is not None:
        src, split_axis, n_split = side_cast
        spec_in, spec_out, shape_out = _side_cast_specs(src, split_axis, n_split, batch * nc,
                                                        lambda b, c: b * nc + c)
        in_specs.append(spec_in)
        operands.append(src)
        out_specs.append(spec_out)
        out_shape.append(shape_out)
    return pl.pallas_call(
        functools.partial(_hgrn_kernel, side_cast=side_cast is not None),
        grid=(batch, nc),
        in_specs=in_specs,
        out_specs=out_specs,
        out_shape=out_shape,
        scratch_shapes=[pltpu.VMEM((HGRN_HEADS, HGRN_DIM, HGRN_DIM), F32),
                        pltpu.VMEM((N_ARG_GROUPS * HC, HGRN_W), F32),
                        pltpu.VMEM((HC, HGRN_W), F32),
                        pltpu.VMEM((2 * HC, HGRN_W), BF16),
                        pltpu.VMEM((HGRN_HEADS, N_ARG_GROUPS, HC, HGRN_DIM), BF16),
                        pltpu.VMEM((HGRN_HEADS, N_ARG_GROUPS, HC, HGRN_DIM), BF16)],
        compiler_params=_cparams(("arbitrary", "arbitrary"), 48),
        name="hgrn2",
    )(*operands)


def _merge_out_kernel(att_ref, rec_ref, ga_ref, gh_ref, x_ref, wa_ref, wh_ref, wo_ref, g_ref,
                      h_ref, *maybe_hn_ref):
    j = pl.program_id(1)

    @pl.when(j == 0)
    def _():
        h_ref[...] = x_ref[...]

    a = _dot(att_ref[...], wa_ref[...])
    r = _dot(rec_ref[...], wh_ref[...])
    merged = (_sigmoid(ga_ref[...].astype(F32)) * a + _sigmoid(gh_ref[...].astype(F32)) * r).astype(BF16)
    h_ref[...] += _dot(merged, wo_ref[...])

    if maybe_hn_ref:
        @pl.when(j == pl.num_programs(1) - 1)
        def _():
            h = h_ref[...]
            ms = jnp.mean(h * h, axis=-1, keepdims=True)
            maybe_hn_ref[0][...] = (h * lax.rsqrt(ms + EPS) * g_ref[...]).astype(BF16)


def merge_out(att, rec, proj_main, x, w_up_att, w_up_hgrn, w_o, norm_g, emit_hn, tm=512, tn=1024):
    m, d = x.shape
    ga0, gh0 = COL_GA * 1024 // tn, COL_GH * 1024 // tn
    row = lambda i, j: (i, 0)
    out_specs = [pl.BlockSpec((tm, d), row)]
    out_shape = [jax.ShapeDtypeStruct((m, d), F32)]
    if emit_hn:
        out_specs.append(pl.BlockSpec((tm, d), row))
        out_shape.append(jax.ShapeDtypeStruct((m, d), BF16))
    return pl.pallas_call(
        _merge_out_kernel,
        grid=(m // tm, d // tn),
        in_specs=[pl.BlockSpec((tm, ATT_W), row),
                  pl.BlockSpec((tm, HGRN_W), row),
                  pl.BlockSpec((tm, tn), lambda i, j: (i, ga0 + j)),
                  pl.BlockSpec((tm, tn), lambda i, j: (i, gh0 + j)),
                  pl.BlockSpec((tm, d), row),
                  pl.BlockSpec((ATT_W, tn), lambda i, j: (0, j)),
                  pl.BlockSpec((HGRN_W, tn), lambda i, j: (0, j)),
                  pl.BlockSpec((tn, d), lambda i, j: (j, 0)),
                  pl.BlockSpec((1, d), lambda i, j: (0, 0))],
        out_specs=out_specs,
        out_shape=out_shape,
        compiler_params=_cparams(("parallel", "arbitrary"), 48),
        name="merge_out",
    )(att, rec, proj_main, proj_main, x, w_up_att, w_up_hgrn, w_o, norm_g.reshape(1, d))


def _ffn_kernel(hn_ref, h_ref, w1_ref, w3_ref, w2_ref, g_ref, o_ref, *maybe_xn_ref):
    j = pl.program_id(1)

    @pl.when(j == 0)
    def _():
        o_ref[...] = h_ref[...]

    x = hn_ref[...]
    a = _dot(x, w1_ref[...])
    b = _dot(x, w3_ref[...])
    act = (a * _sigmoid(a) * b).astype(BF16)
    o_ref[...] += _dot(act, w2_ref[...])

    if maybe_xn_ref:
        @pl.when(j == pl.num_programs(1) - 1)
        def _():
            y = o_ref[...]
            ms = jnp.mean(y * y, axis=-1, keepdims=True)
            maybe_xn_ref[0][...] = (y * lax.rsqrt(ms + EPS) * g_ref[...]).astype(BF16)


def ffn_dense(hn, h, w1, w3, w2, next_norm_g, emit_xn, tm=512, tf=512):
    m, d = hn.shape
    f = w1.shape[1]
    row = lambda i, j: (i, 0)
    out_specs = [pl.BlockSpec((tm, d), row)]
    out_shape = [jax.ShapeDtypeStruct((m, d), F32)]
    if emit_xn:
        out_specs.append(pl.BlockSpec((tm, d), row))
        out_shape.append(jax.ShapeDtypeStruct((m, d), BF16))
    return pl.pallas_call(
        _ffn_kernel,
        grid=(m // tm, f // tf),
        in_specs=[pl.BlockSpec((tm, d), row),
                  pl.BlockSpec((tm, d), row),
                  pl.BlockSpec((d, tf), lambda i, j: (0, j)),
                  pl.BlockSpec((d, tf), lambda i, j: (0, j)),
                  pl.BlockSpec((tf, d), lambda i, j: (j, 0)),
                  pl.BlockSpec((1, d), lambda i, j: (0, 0))],
        out_specs=out_specs,
        out_shape=out_shape,
        compiler_params=_cparams(("parallel", "arbitrary"), 52),
        name="ffn_dense",
    )(hn, h, w1, w3, w2, next_norm_g.reshape(1, d))


def _row_copy(src_ref, src_row, dst_ref, dst_row, sem):
    return pltpu.make_async_copy(src_ref.at[pl.ds(src_row, 1)], dst_ref.at[pl.ds(dst_row, 1)], sem)


def _moe_scatter_kernel(dest_ref, hn_ref, xs_init_ref, xs_ref, sem, *, tm, m):
    del xs_init_ref
    base = pl.program_id(0) * tm

    def issue(r, carry):
        for k in range(2):
            _row_copy(hn_ref, r, xs_ref, dest_ref[k * m + base + r], sem).start()
        return carry

    lax.fori_loop(0, tm, issue, 0, unroll=8)
    for k in range(2):
        pltpu.make_async_copy(hn_ref, xs_ref.at[pl.ds(0, tm)], sem).wait()


def moe_scatter(hn_packed, dest, n_rows, tm=512):
    m, c = hn_packed.shape
    xs_init = jnp.zeros((n_rows, c), I32)
    return pl.pallas_call(
        functools.partial(_moe_scatter_kernel, tm=tm, m=m),
        grid_spec=pltpu.PrefetchScalarGridSpec(
            num_scalar_prefetch=1,
            grid=(m // tm,),
            in_specs=[pl.BlockSpec((tm, c), lambda i, dest: (i, 0)),
                      pl.BlockSpec(memory_space=pl.ANY)],
            out_specs=pl.BlockSpec(memory_space=pl.ANY),
            scratch_shapes=[pltpu.SemaphoreType.DMA(())]),
        out_shape=jax.ShapeDtypeStruct((n_rows, c), I32),
        input_output_aliases={2: 0},
        compiler_params=_cparams(("arbitrary",), 32),
        name="moe_scatter",
    )(dest, hn_packed, xs_init)


MOE_ROW_STEP = 128


def _moe_ffn_kernel(te_ref, tr_ref, xs_ref, w1_ref, w3_ref, w2_ref, o_ref, xb_ref):
    del te_ref
    i = pl.program_id(0)
    half = xs_ref.shape[1]
    tm = xs_ref.shape[0]

    @pl.when(pl.program_id(1) == 0)
    def _():
        o_ref[...] = jnp.zeros_like(o_ref)
        lo, hi = _unpack_bf16_pairs(xs_ref[...])
        xb_ref[:, 0:half] = lo
        xb_ref[:, half:2 * half] = hi

    groups = (tr_ref[i] + MOE_ROW_STEP - 1) // MOE_ROW_STEP
    for g in range(1, tm // MOE_ROW_STEP + 1):
        rows = g * MOE_ROW_STEP

        @pl.when(groups == g)
        def _():
            x = xb_ref[0:rows, :]
            a = _dot(x, w1_ref[0])
            b = _dot(x, w3_ref[0])
            act = (a * _sigmoid(a) * b).astype(BF16)
            o_ref[0:rows, :] += _dot(act, w2_ref[0])


def moe_ffn(xs, tile_expert, tile_rows, w1, w3, w2, tm=512, tf=1024):
    p, c = xs.shape
    ne, d, f = w1.shape
    nf = f // tf
    col = lambda i, j, te, tr: jnp.where(tr[i] > 0, j, nf - 1)
    return pl.pallas_call(
        _moe_ffn_kernel,
        grid_spec=pltpu.PrefetchScalarGridSpec(
            num_scalar_prefetch=2,
            grid=(p // tm, nf),
            in_specs=[pl.BlockSpec((tm, c), lambda i, j, te, tr: (i, 0)),
                      pl.BlockSpec((1, d, tf), lambda i, j, te, tr: (te[i], 0, col(i, j, te, tr))),
                      pl.BlockSpec((1, d, tf), lambda i, j, te, tr: (te[i], 0, col(i, j, te, tr))),
                      pl.BlockSpec((1, tf, d), lambda i, j, te, tr: (te[i], col(i, j, te, tr), 0))],
            out_specs=pl.BlockSpec((tm, d), lambda i, j, te, tr: (i, 0)),
            scratch_shapes=[pltpu.VMEM((tm, d), BF16)]),
        out_shape=jax.ShapeDtypeStruct((p, d), F32),
        compiler_params=_cparams(("arbitrary", "arbitrary"), 56),
        name="moe_ffn",
    )(tile_expert, tile_rows, xs, w1, w3, w2)


def _moe_combine_kernel(dest_ref, h_ref, prob_ref, ys_ref, o_ref, ybuf_ref, sem, *, tm, m):
    i = pl.program_id(0)
    slot = i % 2

    def gather_tile(tile, slot_):
        base = tile * tm

        def issue(r, carry):
            for k in range(2):
                _row_copy(ys_ref, dest_ref[k * m + base + r], ybuf_ref.at[slot_, k], r,
                          sem.at[slot_]).start()
            return carry

        lax.fori_loop(0, tm, issue, 0, unroll=8)

    @pl.when(i == 0)
    def _():
        gather_tile(0, 0)

    @pl.when(i + 1 < pl.num_programs(0))
    def _():
        gather_tile(i + 1, 1 - slot)

    for k in range(2):
        pltpu.make_async_copy(ys_ref.at[pl.ds(0, tm)], ybuf_ref.at[slot, k], sem.at[slot]).wait()
    prob = prob_ref[...]
    o_ref[...] = h_ref[...] + prob[:, 0:1] * ybuf_ref[slot, 0] + prob[:, 1:2] * ybuf_ref[slot, 1]


def moe_combine(h, prob, ys, dest, tm=512):
    m, d = h.shape
    return pl.pallas_call(
        functools.partial(_moe_combine_kernel, tm=tm, m=m),
        grid_spec=pltpu.PrefetchScalarGridSpec(
            num_scalar_prefetch=1,
            grid=(m // tm,),
            in_specs=[pl.BlockSpec((tm, d), lambda i, dest: (i, 0)),
                      pl.BlockSpec((tm, LANES), lambda i, dest: (i, 0)),
                      pl.BlockSpec(memory_space=pl.ANY)],
            out_specs=pl.BlockSpec((tm, d), lambda i, dest: (i, 0)),
            scratch_shapes=[pltpu.VMEM((2, 2, tm, d), F32), pltpu.SemaphoreType.DMA((2,))]),
        out_shape=jax.ShapeDtypeStruct((m, d), F32),
        compiler_params=_cparams(("arbitrary",), 48),
        name="moe_combine",
    )(dest, h, prob, ys)


def ffn_moe(h, norm_g, w_router, w1, w3, w2, tm=512):
    m, d = h.shape
    ne = w1.shape[0]
    hn_packed, meta, prob, cnt = rmsnorm_router(h, norm_g, w_router)
    counts = cnt[0, :ne].astype(I32)
    tiles_per = (counts + tm - 1) // tm
    tile_end = jnp.cumsum(tiles_per)
    offset = (tile_end - tiles_per) * tm
    experts = jnp.arange(ne, dtype=I32)

    def group_offset(e):
        return jnp.sum(jnp.where(e[:, None] == experts[None, :], offset[None, :], 0), axis=1)

    dest = jnp.concatenate([group_offset(meta[:, 0]) + meta[:, 2],
                            group_offset(meta[:, 1]) + meta[:, 3]])
    n_tiles = (2 * m) // tm + ne
    tile_ids = jnp.arange(n_tiles, dtype=I32)
    last_valid = jnp.minimum(tile_ids, tile_end[-1] - 1)
    tile_expert = jnp.sum((tile_end[None, :] <= last_valid[:, None]).astype(I32), axis=1)
    mine = tile_expert[:, None] == experts[None, :]
    first_tile = jnp.sum(jnp.where(mine, (tile_end - tiles_per)[None, :], 0), axis=1)
    count = jnp.sum(jnp.where(mine, counts[None, :], 0), axis=1)
    tile_rows = jnp.where(tile_ids < tile_end[-1],
                          jnp.clip(count - (tile_ids - first_tile) * tm, 0, tm), 0).astype(I32)
    xs = moe_scatter(hn_packed, dest, n_tiles * tm)
    ys = moe_ffn(xs, tile_expert, tile_rows, w1, w3, w2, tm=tm)
    return moe_combine(h, prob, ys, dest)


W_IN_HEAD = ATT_W + 2 * KV_W + IDX_Q_RANK
W_IN_SMALL = IDX_HEAD_DIM + IDX_HEADS
W_IN_ROWS = 1024


def _w_in_prep_kernel(wt_hbm, main_ref, idx_ref, buf_ref, sem, *, layer):
    c = pl.program_id(0)

    def fetch(row0, n_rows):
        cp = pltpu.make_async_copy(wt_hbm.at[layer, pl.ds(row0, n_rows)],
                                   buf_ref.at[pl.ds(0, n_rows)], sem)
        cp.start()
        cp.wait()

    @pl.when(c == 0)
    def _():
        fetch(W_IN_HEAD, LANES)
        rows = buf_ref[0:LANES, :]
        keep = lax.broadcasted_iota(I32, rows.shape, 0) < W_IN_SMALL
        idx_ref[...] = jnp.where(keep, rows, 0.0).astype(BF16)

    base = c * W_IN_ROWS
    row0 = pl.multiple_of(jnp.where(base < W_IN_HEAD, base, base + W_IN_SMALL), 8)
    fetch(row0, W_IN_ROWS)
    main_ref[...] = buf_ref[...].astype(BF16)


def _split_w_in(w_in, layer):
    _, n, d = w_in.shape
    assert n == MAIN_W + W_IN_SMALL and W_IN_HEAD % W_IN_ROWS == 0
    return pl.pallas_call(
        functools.partial(_w_in_prep_kernel, layer=layer),
        grid=(MAIN_W // W_IN_ROWS,),
        in_specs=[pl.BlockSpec(memory_space=pl.ANY)],
        out_specs=[pl.BlockSpec((W_IN_ROWS, d), lambda c: (c, 0)),
                   pl.BlockSpec((LANES, d), lambda c: (0, 0))],
        out_shape=[jax.ShapeDtypeStruct((MAIN_W, d), BF16),
                   jax.ShapeDtypeStruct((LANES, d), BF16)],
        scratch_shapes=[pltpu.VMEM((W_IN_ROWS, d), F32), pltpu.SemaphoreType.DMA(())],
        compiler_params=_cparams(("arbitrary",), 40),
        name="w_in_prep",
    )(w_in)


def _mixer(x2, maybe_xn, batch, seq, norm_g, w_in, q_norm_g, k_norm_g, idx_q_norm_g, w_idx_q, ln_g,
           ln_b, lb, hgrn_norm_g, w_up_att, w_up_hgrn, w_o, bias, norm_ffn_g, emit_hn,
           side_cast_att=None, side_cast_rec=None):
    w_main, w_idx = _split_w_in(*w_in)
    if maybe_xn is None:
        proj_main, xn = rmsnorm_matmul(x2, norm_g, w_main, BF16)
    else:
        xn = maybe_xn
        proj_main = matmul(xn, w_main, BF16)
    proj_idx = matmul(xn, w_idx, F32)
    qn, kn, vt, qi, ki, wt = attn_prep(proj_main, proj_idx, q_norm_g, k_norm_g, idx_q_norm_g,
                                       w_idx_q.astype(BF16), ln_g, ln_b, seq)
    vt = vt.reshape(batch, seq // TK, KV_W, TK)
    att, *cast_att = dsa_attention(qi, wt, qn, ki, kn, vt, bias, batch, seq, side_cast_att)
    rec, *cast_rec = hgrn2(proj_main, lb, hgrn_norm_g, batch, seq, side_cast_rec)
    outs = merge_out(att, rec, proj_main, x2, w_up_att.astype(BF16), w_up_hgrn.astype(BF16),
                     w_o.astype(BF16), norm_ffn_g, emit_hn)
    return outs, (cast_att[0] if cast_att else None), (cast_rec[0] if cast_rec else None)


def kernel(x, rel_bias, norm_mix_g, norm_ffn_g, w_in, q_norm_g, k_norm_g, idx_q_norm_g, w_idx_q,
           idx_k_ln_g, idx_k_ln_b, hgrn_lb_logits, hgrn_out_norm_g, w_up_att, w_up_hgrn, w_o,
           w1_dense, w3_dense, w2_dense, w_router, w1_moe, w3_moe, w2_moe):
    batch, seq, d = x.shape
    depth = w_in.shape[0]
    lb_all = jnp.cumsum(jax.nn.softmax(hgrn_lb_logits.astype(F32), axis=0), axis=0)
    lb_all = lb_all - lb_all[0:1]
    bias = bias_tables(rel_bias)
    w_in_t = jnp.swapaxes(w_in, 1, 2)
    x2 = x.reshape(batch * seq, d)
    maybe_xn = None
    n_split_ff = w1_moe.shape[-1] // 1024
    moe_bf16 = {}
    for l in range(depth):
        dense = l % 2 == 0
        j = l // 2
        side_att = side_rec = None
        if dense and l + 1 < depth:
            side_att = (w3_moe[j], 2, n_split_ff)
            side_rec = (w1_moe[j], 2, 2 * n_split_ff)
        elif not dense:
            side_att = (w2_moe[j], 1, n_split_ff)
        (h, *maybe_hn), cast_att, cast_rec = _mixer(
            x2, maybe_xn, batch, seq, norm_mix_g[l], (w_in_t, l), q_norm_g[l], k_norm_g[l],
            idx_q_norm_g[l], w_idx_q[l], idx_k_ln_g[l], idx_k_ln_b[l], lb_all[l],
            hgrn_out_norm_g[l], w_up_att[l], w_up_hgrn[l], w_o[l], bias, norm_ffn_g[l],
            emit_hn=dense, side_cast_att=side_att, side_cast_rec=side_rec)
        maybe_xn = None
        if dense:
            more = l + 1 < depth
            moe_bf16["w3"], moe_bf16["w1"] = cast_att, cast_rec
            x2, *rest = ffn_dense(maybe_hn[0], h, w1_dense[j].astype(BF16), w3_dense[j].astype(BF16),
                                  w2_dense[j].astype(BF16), norm_mix_g[l + 1 if more else l], more)
            maybe_xn = rest[0] if more else None
        else:
            x2 = ffn_moe(h, norm_ffn_g[l], w_router[j], moe_bf16["w1"], moe_bf16["w3"], cast_att)
    return x2.reshape(batch, seq, d)
```

```python
import functools
import math

import numpy as np
import jax
import jax.numpy as jnp
from jax import lax
from jax.experimental import pallas as pl
from jax.experimental.pallas import tpu as pltpu

F32 = jnp.float32
BF16 = jnp.bfloat16
I32 = jnp.int32

EPS = 1e-6
ATT_HEADS = 8
ATT_KV_HEADS = 2
ATT_GROUP = ATT_HEADS // ATT_KV_HEADS
HEAD_DIM = 128
ATT_W = ATT_HEADS * HEAD_DIM
KV_W = ATT_KV_HEADS * HEAD_DIM
IDX_HEADS = 16
IDX_HEAD_DIM = 64
IDX_Q_RANK = 512
TOPK_MAX = 256
HGRN_HEADS = 8
HGRN_DIM = 128
HGRN_W = HGRN_HEADS * HGRN_DIM
REL_BUCKETS = 32
REL_MAX_DIST = 128
N_EXPERTS = 8
LANES = 128
INT_MIN = -(2 ** 31)
NEG_BIG = -1e30
LOG2E = math.log2(math.e)

TQ = 256
TK = 256
HC = 128
HGRN_LEVELS = (1, 2, 4, 8, 16, 32, 64)
N_ARG_GROUPS = len(HGRN_LEVELS) + 2

COL_Q, COL_KVC, COL_HQ, COL_HF, COL_HI, COL_HG, COL_GA, COL_GH = 0, 1, 2, 3, 4, 5, 6, 8
MAIN_W = 10 * 1024


def _cparams(sem, vmem_mb):
    return pltpu.CompilerParams(dimension_semantics=sem, vmem_limit_bytes=vmem_mb << 20)


def _dot(a, b):
    return jnp.dot(a, b, preferred_element_type=F32)


def _dot_nt(a, b):
    return lax.dot_general(a, b, (((1,), (1,)), ((), ())), preferred_element_type=F32)


def _sigmoid(x):
    return 1.0 / (1.0 + jnp.exp(-x))


def _pack_bf16_pairs(x):
    c = x.shape[1] // 2
    lo = lax.bitcast_convert_type(x[:, :c].astype(BF16).astype(F32), I32)
    hi = lax.bitcast_convert_type(x[:, c:].astype(BF16).astype(F32), I32)
    return (hi & jnp.int32(-65536)) | lax.shift_right_logical(lo, 16)


def _unpack_bf16_pairs(p):
    lo = lax.bitcast_convert_type(lax.shift_left(p, 16), F32).astype(BF16)
    hi = lax.bitcast_convert_type(p & jnp.int32(-65536), F32).astype(BF16)
    return lo, hi


def _rmsnorm_router_kernel(x_ref, g_ref, wr_ref, tri_ref, o_ref, meta_ref, prob_ref, cnt_ref, carry_ref):
    @pl.when(pl.program_id(0) == 0)
    def _():
        carry_ref[...] = jnp.zeros_like(carry_ref)

    x = x_ref[...]
    ms = jnp.mean(x * x, axis=-1, keepdims=True)
    hn = x * lax.rsqrt(ms + EPS) * g_ref[...]
    o_ref[...] = _pack_bf16_pairs(hn)
    logits = _dot(hn.astype(BF16), wr_ref[...])
    lane = lax.broadcasted_iota(I32, logits.shape, 1)
    logits = jnp.where(lane < N_EXPERTS, logits, -jnp.inf)
    v1 = jnp.max(logits, axis=-1, keepdims=True)
    i1 = jnp.min(jnp.where(logits == v1, lane, LANES), axis=-1, keepdims=True)
    rest = jnp.where(lane == i1, -jnp.inf, logits)
    v2 = jnp.max(rest, axis=-1, keepdims=True)
    i2 = jnp.min(jnp.where(rest == v2, lane, LANES), axis=-1, keepdims=True)
    e = jnp.exp(v2 - v1)
    p1 = 1.0 / (1.0 + e)
    prob_ref[...] = jnp.where(lane == 0, p1, 0.0) + jnp.where(lane == 1, e * p1, 0.0)
    hot = jnp.where((lane == i1) | (lane == i2), 1.0, 0.0)
    rank = _dot(tri_ref[...], hot.astype(BF16)) + carry_ref[0:1, :]
    carry_ref[...] = carry_ref[...] + jnp.sum(hot, axis=0, keepdims=True)
    cnt_ref[...] = carry_ref[...]
    r1 = jnp.sum(jnp.where(lane == i1, rank, 0.0), axis=-1, keepdims=True)
    r2 = jnp.sum(jnp.where(lane == i2, rank, 0.0), axis=-1, keepdims=True)
    meta = (jnp.where(lane == 0, i1, 0) + jnp.where(lane == 1, i2, 0)
            + jnp.where(lane == 2, r1.astype(I32), 0) + jnp.where(lane == 3, r2.astype(I32), 0))
    meta_ref[...] = meta


def rmsnorm_router(x, g, w_router, tm=512):
    m, d = x.shape
    wr = jnp.zeros((d, LANES), BF16).at[:, :N_EXPERTS].set(w_router.astype(BF16))
    tri = jnp.asarray(np.tril(np.ones((tm, tm), np.float32), -1), BF16)
    return pl.pallas_call(
        _rmsnorm_router_kernel,
        grid=(m // tm,),
        in_specs=[pl.BlockSpec((tm, d), lambda i: (i, 0)),
                  pl.BlockSpec((1, d), lambda i: (0, 0)),
                  pl.BlockSpec((d, LANES), lambda i: (0, 0)),
                  pl.BlockSpec((tm, tm), lambda i: (0, 0))],
        out_specs=[pl.BlockSpec((tm, d // 2), lambda i: (i, 0)),
                   pl.BlockSpec((tm, LANES), lambda i: (i, 0)),
                   pl.BlockSpec((tm, LANES), lambda i: (i, 0)),
                   pl.BlockSpec((8, LANES), lambda i: (0, 0))],
        out_shape=[jax.ShapeDtypeStruct((m, d // 2), I32),
                   jax.ShapeDtypeStruct((m, LANES), I32),
                   jax.ShapeDtypeStruct((m, LANES), F32),
                   jax.ShapeDtypeStruct((8, LANES), F32)],
        scratch_shapes=[pltpu.VMEM((8, LANES), F32)],
        compiler_params=_cparams(("arbitrary",), 40),
        name="rmsnorm_router",
    )(x, g.reshape(1, d), wr, tri)


def _matmul_kernel(x_ref, wt_ref, o_ref):
    o_ref[...] = _dot_nt(x_ref[...], wt_ref[...]).astype(o_ref.dtype)


def _rmsnorm_matmul_kernel(x_ref, g_ref, wt_ref, o_ref, xn_o_ref, xn_ref):
    @pl.when(pl.program_id(1) == 0)
    def _():
        x = x_ref[...]
        ms = jnp.mean(x * x, axis=-1, keepdims=True)
        xn_ref[...] = (x * lax.rsqrt(ms + EPS) * g_ref[...]).astype(BF16)
        xn_o_ref[...] = xn_ref[...]

    o_ref[...] = _dot_nt(xn_ref[...], wt_ref[...]).astype(o_ref.dtype)


def rmsnorm_matmul(x, g, wt, out_dtype, tm=1024, tn=1024):
    m, k = x.shape
    n = wt.shape[0]
    tm, tn = min(tm, m), min(tn, n)
    return pl.pallas_call(
        _rmsnorm_matmul_kernel,
        grid=(m // tm, n // tn),
        in_specs=[pl.BlockSpec((tm, k), lambda i, j: (i, 0)),
                  pl.BlockSpec((1, k), lambda i, j: (0, 0)),
                  pl.BlockSpec((tn, k), lambda i, j: (j, 0))],
        out_specs=[pl.BlockSpec((tm, tn), lambda i, j: (i, j)),
                   pl.BlockSpec((tm, k), lambda i, j: (i, 0))],
        out_shape=[jax.ShapeDtypeStruct((m, n), out_dtype),
                   jax.ShapeDtypeStruct((m, k), BF16)],
        scratch_shapes=[pltpu.VMEM((tm, k), BF16)],
        compiler_params=_cparams(("parallel", "arbitrary"), 56),
        name="rmsnorm_matmul",
    )(x, g.reshape(1, k), wt)


def matmul(x, wt, out_dtype, tm=1024, tn=1024):
    m, k = x.shape
    n = wt.shape[0]
    tm, tn = min(tm, m), min(tn, n)
    return pl.pallas_call(
        _matmul_kernel,
        grid=(m // tm, n // tn),
        in_specs=[pl.BlockSpec((tm, k), lambda i, j: (i, 0)),
                  pl.BlockSpec((tn, k), lambda i, j: (j, 0))],
        out_specs=pl.BlockSpec((tm, tn), lambda i, j: (i, j)),
        out_shape=jax.ShapeDtypeStruct((m, n), out_dtype),
        compiler_params=_cparams(("parallel", "parallel"), 48),
        name="matmul",
    )(x, wt)


def _attn_prep_kernel(q_ref, kvc_ref, pi_ref, qg_ref, kg_ref, cg_ref, wiq_ref, lng_ref, lnb_ref,
                      qn_ref, kn_ref, vt_ref, qi_ref, ki_ref, wt_ref):
    def head_rms(x, g):
        x = x.astype(F32)
        return x * lax.rsqrt(jnp.mean(x * x, axis=-1, keepdims=True) + EPS) * g

    att_scale = HEAD_DIM ** -0.5 * LOG2E
    for h in range(ATT_HEADS):
        sl = slice(h * HEAD_DIM, (h + 1) * HEAD_DIM)
        qn_ref[:, sl] = (head_rms(q_ref[:, sl], qg_ref[...]) * att_scale).astype(BF16)
    for g in range(ATT_KV_HEADS):
        sl = slice(g * HEAD_DIM, (g + 1) * HEAD_DIM)
        kn_ref[:, sl] = head_rms(kvc_ref[:, sl], kg_ref[...]).astype(BF16)
    for c in range(vt_ref.shape[0]):
        vc = kvc_ref[c * TK:(c + 1) * TK, KV_W:2 * KV_W].astype(F32)
        vt_ref[c] = vc.T.astype(BF16)
    cq = kvc_ref[:, 2 * KV_W:2 * KV_W + IDX_Q_RANK]
    cqn = head_rms(cq, cg_ref[...]).astype(BF16)
    qi_ref[...] = _dot(cqn, wiq_ref[...]).astype(BF16)
    pi = pi_ref[...]
    is_key = lax.broadcasted_iota(I32, pi.shape, 1) < IDX_HEAD_DIM
    mu = jnp.sum(jnp.where(is_key, pi, 0.0), axis=-1, keepdims=True) * (1.0 / IDX_HEAD_DIM)
    cen = jnp.where(is_key, pi - mu, 0.0)
    var = jnp.sum(cen * cen, axis=-1, keepdims=True) * (1.0 / IDX_HEAD_DIM)
    kidn = cen * lax.rsqrt(var + EPS) * lng_ref[...] + lnb_ref[...]
    ki_ref[:, 0:LANES] = kidn.astype(BF16)
    ki_ref[:, LANES:2 * LANES] = pltpu.roll(kidn, IDX_HEAD_DIM, 1).astype(BF16)
    wt = (pi * (IDX_HEAD_DIM ** -0.5 * IDX_HEADS ** -0.5)).T
    wt_ref[0] = wt[IDX_HEAD_DIM:IDX_HEAD_DIM + IDX_HEADS, :]


def attn_prep(proj_main, proj_idx, q_norm_g, k_norm_g, idx_q_norm_g, w_idx_q, ln_g, ln_b, seq, tm=512):
    m = proj_main.shape[0]
    tm = min(tm, seq)
    per_seq = seq // tm
    row = lambda i: (i, 0)
    const = lambda i: (0, 0)
    lane_pad = lambda a: jnp.zeros((1, LANES), F32).at[0, :a.shape[0]].set(a)
    return pl.pallas_call(
        _attn_prep_kernel,
        grid=(m // tm,),
        in_specs=[pl.BlockSpec((tm, 1024), lambda i: (i, COL_Q)),
                  pl.BlockSpec((tm, 1024), lambda i: (i, COL_KVC)),
                  pl.BlockSpec((tm, LANES), row),
                  pl.BlockSpec((1, HEAD_DIM), const),
                  pl.BlockSpec((1, HEAD_DIM), const),
                  pl.BlockSpec((1, IDX_Q_RANK), const),
                  pl.BlockSpec((IDX_Q_RANK, IDX_HEADS * IDX_HEAD_DIM), const),
                  pl.BlockSpec((1, LANES), const),
                  pl.BlockSpec((1, LANES), const)],
        out_specs=[pl.BlockSpec((tm, ATT_W), row),
                   pl.BlockSpec((tm, KV_W), row),
                   pl.BlockSpec((tm // TK, KV_W, TK), lambda i: (i, 0, 0)),
                   pl.BlockSpec((tm, IDX_HEADS * IDX_HEAD_DIM), row),
                   pl.BlockSpec((tm, 2 * LANES), row),
                   pl.BlockSpec((1, IDX_HEADS, tm), lambda i: (i // per_seq, 0, i % per_seq))],
        out_shape=[jax.ShapeDtypeStruct((m, ATT_W), BF16),
                   jax.ShapeDtypeStruct((m, KV_W), BF16),
                   jax.ShapeDtypeStruct((m // TK, KV_W, TK), BF16),
                   jax.ShapeDtypeStruct((m, IDX_HEADS * IDX_HEAD_DIM), BF16),
                   jax.ShapeDtypeStruct((m, 2 * LANES), BF16),
                   jax.ShapeDtypeStruct((m // seq, IDX_HEADS, seq), F32)],
        compiler_params=_cparams(("parallel",), 40),
        name="attn_prep",
    )(proj_main, proj_main, proj_idx, q_norm_g.reshape(1, -1), k_norm_g.reshape(1, -1),
      idx_q_norm_g.reshape(1, -1), w_idx_q, lane_pad(ln_g), lane_pad(ln_b))


def _rel_bucket(dist):
    max_exact = REL_BUCKETS // 2
    n = jnp.maximum(dist, 0)
    nf = jnp.maximum(n, 1).astype(F32)
    large = max_exact + (jnp.log(nf / max_exact) / math.log(REL_MAX_DIST / max_exact)
                         * (REL_BUCKETS - max_exact)).astype(I32)
    large = jnp.minimum(large, REL_BUCKETS - 1)
    return jnp.where(n < max_exact, n, large)


def _bias_kernel(rb_ref, bkt_ref, o_ref):
    h = pl.program_id(0)
    for kind in range(3):
        b = bkt_ref[kind]
        acc = jnp.zeros(b.shape, F32)
        for n in range(REL_BUCKETS):
            acc = jnp.where(b == n, rb_ref[n, h], acc)
        o_ref[0, kind] = acc * LOG2E


def bias_tables(rel_bias):
    assert TK >= REL_MAX_DIST
    kpos = jnp.arange(TK, dtype=I32)[:, None]
    qpos = jnp.arange(TQ, dtype=I32)[None, :]
    dist = jnp.stack([qpos - kpos, qpos - kpos + TK, qpos - kpos + 2 * TK])
    bkt = _rel_bucket(dist)
    return pl.pallas_call(
        _bias_kernel,
        grid=(ATT_HEADS,),
        in_specs=[pl.BlockSpec(memory_space=pltpu.SMEM),
                  pl.BlockSpec((3, TK, TQ), lambda h: (0, 0, 0))],
        out_specs=pl.BlockSpec((1, 3, TK, TQ), lambda h: (h, 0, 0, 0)),
        out_shape=jax.ShapeDtypeStruct((ATT_HEADS, 3, TK, TQ), F32),
        compiler_params=_cparams(("arbitrary",), 32),
        name="bias_tables",
    )(rel_bias, bkt)


def _side_cast_specs(src, split_axis, n_split, n_steps, step_of):
    ne = src.shape[0]
    n_blocks = ne * n_split
    every = n_steps // n_blocks
    assert every >= 1, "not enough grid steps to walk the weight stack"
    block = [1, src.shape[1], src.shape[2]]
    block[split_axis] //= n_split

    def index_map(*grid_idx):
        t = jnp.minimum(step_of(*grid_idx) // every, n_blocks - 1)
        idx = [t // n_split, 0, 0]
        idx[split_axis] = t % n_split
        return tuple(idx)

    spec = pl.BlockSpec(tuple(block), index_map)
    return spec, spec, jax.ShapeDtypeStruct(src.shape, BF16)


def _dsa_kernel(qi_ref, wt_ref, qn_ref, ki_ref, kn_ref, vt_ref, bias_ref, *rest, n_sel, side_cast):
    if side_cast:
        wf_ref, o_ref, wb_ref, key_ref, thr_ref, m_ref, l_ref, a_ref, acc_ref, s_ref, p_ref = rest
        wb_ref[...] = wf_ref[...].astype(BF16)
    else:
        o_ref, key_ref, thr_ref, m_ref, l_ref, a_ref, acc_ref, s_ref, p_ref = rest
    i = pl.program_id(1)
    nch = i + 1
    qpos = i * TQ + lax.broadcasted_iota(I32, (TK, TQ), 1)

    def score_chunk(j, carry):
        k0 = pl.multiple_of(j * TK, TK)
        acc = jnp.zeros((TK, TQ), F32)
        for h in range(IDX_HEADS):
            par = h % 2
            kc = ki_ref[pl.ds(k0, TK), par * LANES:(par + 1) * LANES]
            qh = qi_ref[:, (h // 2) * LANES:(h // 2 + 1) * LANES]
            s = _dot_nt(kc, qh)
            acc = acc + jnp.maximum(s, 0.0) * wt_ref[0, h:h + 1, :]
        bits = lax.bitcast_convert_type(acc, I32)
        key = jnp.where(bits < 0, bits ^ jnp.int32(0x7FFFFFFF), bits)
        kpos = k0 + lax.broadcasted_iota(I32, (TK, TQ), 0)
        key_ref[pl.ds(k0, TK), :] = jnp.where(kpos <= qpos, key, INT_MIN)
        return carry

    lax.fori_loop(0, nch, score_chunk, 0)

    def bisect(n_chunks):
        def count_ge(cand):
            acc = jnp.zeros((8, TQ), I32)
            for c in range(n_chunks):
                hit = jnp.where(key_ref[c * TK:(c + 1) * TK, :] >= cand, 1, 0).astype(I32)
                acc = acc + jnp.sum(hit.reshape(TK // 8, 8, TQ), axis=0)
            return jnp.sum(acc, axis=0, keepdims=True)

        def bit_body(b, thr):
            cand = thr | jnp.left_shift(jnp.int32(1), 30 - b)
            return jnp.where(count_ge(cand) >= n_sel, cand, thr)

        thr0 = jnp.where(count_ge(jnp.zeros((1, TQ), I32)) >= n_sel, 0, INT_MIN).astype(I32)
        return jnp.maximum(lax.fori_loop(0, 31, bit_body, thr0), INT_MIN + 1)

    for v in range(key_ref.shape[0] // TK):
        @pl.when(i == v)
        def _():
            thr_ref[...] = jnp.broadcast_to(bisect(v + 1), thr_ref.shape)

    thr = thr_ref[0:1, :]

    m_ref[...] = jnp.full(m_ref.shape, NEG_BIG, F32)
    l_ref[...] = jnp.zeros(l_ref.shape, F32)
    acc_ref[...] = jnp.zeros(acc_ref.shape, F32)

    def att_chunk(j, carry):
        k0 = pl.multiple_of(j * TK, TK)
        neg = jnp.where(key_ref[pl.ds(k0, TK), :] >= thr, 0.0, NEG_BIG)
        kind = jnp.minimum(i - j, 2)
        for h in range(ATT_HEADS):
            g = h // ATT_GROUP
            kc = kn_ref[pl.ds(k0, TK), g * HEAD_DIM:(g + 1) * HEAD_DIM]
            qh = qn_ref[:, h * HEAD_DIM:(h + 1) * HEAD_DIM]
            lg = _dot_nt(kc, qh) + bias_ref[h, kind] + neg
            s_ref[h] = lg
            m_old = m_ref[h:h + 1, :]
            m_new = jnp.maximum(m_old, jnp.max(lg, axis=0, keepdims=True))
            a_ref[h:h + 1, :] = jnp.exp2(m_old - m_new)
            m_ref[h:h + 1, :] = m_new
        for h in range(ATT_HEADS):
            p = jnp.exp2(s_ref[h] - m_ref[h:h + 1, :])
            l_ref[h:h + 1, :] = (a_ref[h:h + 1, :] * l_ref[h:h + 1, :]
                                 + jnp.sum(p, axis=0, keepdims=True))
            p_ref[h] = p.astype(BF16)
        for h in range(ATT_HEADS):
            g = h // ATT_GROUP
            vc = vt_ref[0, j, g * HEAD_DIM:(g + 1) * HEAD_DIM, :]
            acc_ref[h] = a_ref[h:h + 1, :] * acc_ref[h] + _dot(vc, p_ref[h])
        return carry

    lax.fori_loop(0, nch, att_chunk, 0)
    for h in range(ATT_HEADS):
        o = acc_ref[h] * (1.0 / l_ref[h:h + 1, :])
        o_ref[:, h * HEAD_DIM:(h + 1) * HEAD_DIM] = o.T.astype(o_ref.dtype)


def dsa_attention(qi, wt, qn, ki, kn, vt, bias, batch, seq, side_cast=None):
    nq = seq // TQ
    n_sel = min(TOPK_MAX, seq // 4)
    qrow = lambda b, i: (b * nq + i, 0)
    brow = lambda b, i: (b, 0)
    in_specs = [pl.BlockSpec((TQ, IDX_HEADS * IDX_HEAD_DIM), qrow),
                pl.BlockSpec((1, IDX_HEADS, TQ), lambda b, i: (b, 0, i)),
                pl.BlockSpec((TQ, ATT_W), qrow),
                pl.BlockSpec((seq, 2 * LANES), brow),
                pl.BlockSpec((seq, KV_W), brow),
                pl.BlockSpec((1, seq // TK, KV_W, TK), lambda b, i: (b, 0, 0, 0)),
                pl.BlockSpec((ATT_HEADS, 3, TK, TQ), lambda b, i: (0, 0, 0, 0))]
    out_specs = [pl.BlockSpec((TQ, ATT_W), qrow)]
    out_shape = [jax.ShapeDtypeStruct((batch * seq, ATT_W), BF16)]
    operands = [qi, wt, qn, ki, kn, vt, bias]
    if side_cast is not None:
        src, split_axis, n_split = side_cast
        spec_in, spec_out, shape_out = _side_cast_specs(src, split_axis, n_split, batch * nq,
                                                        lambda b, i: b * nq + i)
        in_specs.append(spec_in)
        out_specs.append(spec_out)
        out_shape.append(shape_out)
        operands.append(src)
    return pl.pallas_call(
        functools.partial(_dsa_kernel, n_sel=n_sel, side_cast=side_cast is not None),
        grid=(batch, nq),
        in_specs=in_specs,
        out_specs=out_specs,
        out_shape=out_shape,
        scratch_shapes=[pltpu.VMEM((seq, TQ), I32),
                        pltpu.VMEM((8, TQ), I32),
                        pltpu.VMEM((ATT_HEADS, TQ), F32),
                        pltpu.VMEM((ATT_HEADS, TQ), F32),
                        pltpu.VMEM((ATT_HEADS, TQ), F32),
                        pltpu.VMEM((ATT_HEADS, HEAD_DIM, TQ), F32),
                        pltpu.VMEM((ATT_HEADS, TK, TQ), F32),
                        pltpu.VMEM((ATT_HEADS, TK, TQ), BF16)],
        compiler_params=_cparams(("arbitrary", "arbitrary"), 56),
        name="dsa_attention",
    )(*operands)


def _hgrn_constants():
    t = np.arange(HC)
    rows = []
    masks = [np.eye(HC, dtype=np.float32)]
    for m in HGRN_LEVELS:
        upper = (t // m) % 2 == 1
        start = (t // m) * m
        end = start + m - 1
        u = t[None, :]
        q_side = upper[:, None] & (u >= start[:, None]) & (u <= t[:, None])
        k_side = (~upper)[:, None] & (u > t[:, None]) & (u <= end[:, None])
        rows.append((q_side | k_side).astype(np.float32))
        same = (t[:, None] // (2 * m)) == (t[None, :] // (2 * m))
        masks.append((upper[:, None] & (~upper)[None, :] & same).astype(np.float32))
    u = t[None, :]
    rows.append((u <= t[:, None]).astype(np.float32))
    rows.append((u > t[:, None]).astype(np.float32))
    mat = np.concatenate(rows, axis=0)
    return np.concatenate([mat, mat], axis=1), np.stack(masks)


def _hgrn_kernel(hq_ref, hf_ref, hi_ref, hg_ref, lb_ref, gn_ref, mat_ref, mask_ref, *rest,
                 side_cast):
    if side_cast:
        wf_ref, o_ref, wb_ref, st_ref, arg_ref, kk_ref, lf_ref, zq_ref, zk_ref = rest
        wb_ref[...] = wf_ref[...].astype(BF16)
    else:
        o_ref, st_ref, arg_ref, kk_ref, lf_ref, zq_ref, zk_ref = rest
    c = pl.program_id(1)

    @pl.when(c == 0)
    def _():
        st_ref[...] = jnp.zeros_like(st_ref)

    lb = lb_ref[...]
    f = hf_ref[...].astype(F32)
    e = jnp.exp(-jnp.abs(f))
    r = 1.0 / (1.0 + e)
    log_sig = jnp.minimum(f, 0.0) - jnp.log(1.0 + e)
    la = jnp.log(lb)
    lc = jnp.log(1.0 - lb) + log_sig
    logf = (jnp.maximum(la, lc) + jnp.log(1.0 + jnp.exp(-jnp.abs(la - lc)))) * LOG2E
    kk_ref[...] = (1.0 - lb) * jnp.where(f >= 0, e * r, r)
    hi = logf.astype(BF16)
    lf_ref[0:HC, :] = hi
    lf_ref[HC:2 * HC, :] = (logf - hi.astype(F32)).astype(BF16)
    arg_ref[...] = _dot(mat_ref[...], lf_ref[...])

    scale = HGRN_DIM ** -0.5
    nl = len(HGRN_LEVELS)
    for h in range(HGRN_HEADS):
        sl = slice(h * HGRN_DIM, (h + 1) * HGRN_DIM)
        hq = hq_ref[:, sl].astype(F32)
        q = hq * _sigmoid(hq) * scale
        k = kk_ref[:, sl]
        zq_ref[h, 0] = q.astype(BF16)
        zk_ref[h, 0] = k.astype(BF16)
        for lv in range(nl):
            ex = jnp.exp2(arg_ref[lv * HC:(lv + 1) * HC, sl])
            zq_ref[h, lv + 1] = (q * ex).astype(BF16)
            zk_ref[h, lv + 1] = (k * ex).astype(BF16)
        zq_ref[h, nl + 1] = (q * jnp.exp2(arg_ref[nl * HC:(nl + 1) * HC, sl])).astype(BF16)
        zk_ref[h, nl + 1] = (k * jnp.exp2(arg_ref[(nl + 1) * HC:(nl + 2) * HC, sl])).astype(BF16)

    for h in range(HGRN_HEADS):
        sl = slice(h * HGRN_DIM, (h + 1) * HGRN_DIM)
        a = _dot_nt(zq_ref[h, 0], zk_ref[h, 0]) * mask_ref[0]
        for lv in range(nl):
            a = a + _dot_nt(zq_ref[h, lv + 1], zk_ref[h, lv + 1]) * mask_ref[lv + 1]
        v = hi_ref[:, sl].astype(BF16)
        st = st_ref[h]
        o = _dot(a.astype(BF16), v) + _dot_nt(zq_ref[h, nl + 1], st.astype(BF16))
        decay = jnp.exp2(arg_ref[(nl + 1) * HC - 1:(nl + 1) * HC, sl])
        st_ref[h] = decay * st + _dot(v.astype(F32).T.astype(BF16), zk_ref[h, nl + 1])
        on = o * lax.rsqrt(jnp.mean(o * o, axis=-1, keepdims=True) + EPS) * gn_ref[...]
        hg = hg_ref[:, sl].astype(F32)
        o_ref[:, sl] = (on * hg * _sigmoid(hg)).astype(o_ref.dtype)


def hgrn2(proj_main, lb, g_norm, batch, seq, side_cast=None):
    nc = seq // HC
    mat, masks = _hgrn_constants()
    col = lambda cb: (lambda b, c: (b * nc + c, cb))
    in_specs = [pl.BlockSpec((HC, HGRN_W), col(COL_HQ)),
                pl.BlockSpec((HC, HGRN_W), col(COL_HF)),
                pl.BlockSpec((HC, HGRN_W), col(COL_HI)),
                pl.BlockSpec((HC, HGRN_W), col(COL_HG)),
                pl.BlockSpec((1, HGRN_W), lambda b, c: (0, 0)),
                pl.BlockSpec((1, HGRN_DIM), lambda b, c: (0, 0)),
                pl.BlockSpec((N_ARG_GROUPS * HC, 2 * HC), lambda b, c: (0, 0)),
                pl.BlockSpec((len(HGRN_LEVELS) + 1, HC, HC), lambda b, c: (0, 0, 0))]
    operands = [proj_main, proj_main, proj_main, proj_main, lb.reshape(1, -1), g_norm.reshape(1, -1),
                jnp.asarray(mat, BF16), jnp.asarray(masks, F32)]
    out_specs = [pl.BlockSpec((HC, HGRN_W), lambda b, c: (b * nc + c, 0))]
    out_shape = [jax.ShapeDtypeStruct((batch * seq, HGRN_W), BF16)]
    if side_cast is not None:
        src, split_axis, n_split = side_cast
        spec_in, spec_out, shape_out = _side_cast_specs(src, split_axis, n_split, batch * nc,
                                                        lambda b, c: b * nc + c)
        in_specs.append(spec_in)
        operands.append(src)
        out_specs.append(spec_out)
        out_shape.append(shape_out)
    return pl.pallas_call(
        functools.partial(_hgrn_kernel, side_cast=side_cast is not None),
        grid=(batch, nc),
        in_specs=in_specs,
        out_specs=out_specs,
        out_shape=out_shape,
        scratch_shapes=[pltpu.VMEM((HGRN_HEADS, HGRN_DIM, HGRN_DIM), F32),
                        pltpu.VMEM((N_ARG_GROUPS * HC, HGRN_W), F32),
                        pltpu.VMEM((HC, HGRN_W), F32),
                        pltpu.VMEM((2 * HC, HGRN_W), BF16),
                        pltpu.VMEM((HGRN_HEADS, N_ARG_GROUPS, HC, HGRN_DIM), BF16),
                        pltpu.VMEM((HGRN_HEADS, N_ARG_GROUPS, HC, HGRN_DIM), BF16)],
        compiler_params=_cparams(("arbitrary", "arbitrary"), 48),
        name="hgrn2",
    )(*operands)


def _merge_out_kernel(att_ref, rec_ref, ga_ref, gh_ref, x_ref, wa_ref, wh_ref, wo_ref, g_ref,
                      h_ref, *maybe_hn_ref):
    j = pl.program_id(1)

    @pl.when(j == 0)
    def _():
        h_ref[...] = x_ref[...]

    a = _dot(att_ref[...], wa_ref[...])
    r = _dot(rec_ref[...], wh_ref[...])
    merged = (_sigmoid(ga_ref[...].astype(F32)) * a + _sigmoid(gh_ref[...].astype(F32)) * r).astype(BF16)
    h_ref[...] += _dot(merged, wo_ref[...])

    if maybe_hn_ref:
        @pl.when(j == pl.num_programs(1) - 1)
        def _():
            h = h_ref[...]
            ms = jnp.mean(h * h, axis=-1, keepdims=True)
            maybe_hn_ref[0][...] = (h * lax.rsqrt(ms + EPS) * g_ref[...]).astype(BF16)


def merge_out(att, rec, proj_main, x, w_up_att, w_up_hgrn, w_o, norm_g, emit_hn, tm=512, tn=1024):
    m, d = x.shape
    ga0, gh0 = COL_GA * 1024 // tn, COL_GH * 1024 // tn
    row = lambda i, j: (i, 0)
    out_specs = [pl.BlockSpec((tm, d), row)]
    out_shape = [jax.ShapeDtypeStruct((m, d), F32)]
    if emit_hn:
        out_specs.append(pl.BlockSpec((tm, d), row))
        out_shape.append(jax.ShapeDtypeStruct((m, d), BF16))
    return pl.pallas_call(
        _merge_out_kernel,
        grid=(m // tm, d // tn),
        in_specs=[pl.BlockSpec((tm, ATT_W), row),
                  pl.BlockSpec((tm, HGRN_W), row),
                  pl.BlockSpec((tm, tn), lambda i, j: (i, ga0 + j)),
                  pl.BlockSpec((tm, tn), lambda i, j: (i, gh0 + j)),
                  pl.BlockSpec((tm, d), row),
                  pl.BlockSpec((ATT_W, tn), lambda i, j: (0, j)),
                  pl.BlockSpec((HGRN_W, tn), lambda i, j: (0, j)),
                  pl.BlockSpec((tn, d), lambda i, j: (j, 0)),
                  pl.BlockSpec((1, d), lambda i, j: (0, 0))],
        out_specs=out_specs,
        out_shape=out_shape,
        compiler_params=_cparams(("parallel", "arbitrary"), 48),
        name="merge_out",
    )(att, rec, proj_main, proj_main, x, w_up_att, w_up_hgrn, w_o, norm_g.reshape(1, d))


def _ffn_kernel(hn_ref, h_ref, w1_ref, w3_ref, w2_ref, g_ref, o_ref, *maybe_xn_ref):
    j = pl.program_id(1)

    @pl.when(j == 0)
    def _():
        o_ref[...] = h_ref[...]

    x = hn_ref[...]
    a = _dot(x, w1_ref[...])
    b = _dot(x, w3_ref[...])
    act = (a * _sigmoid(a) * b).astype(BF16)
    o_ref[...] += _dot(act, w2_ref[...])

    if maybe_xn_ref:
        @pl.when(j == pl.num_programs(1) - 1)
        def _():
            y = o_ref[...]
            ms = jnp.mean(y * y, axis=-1, keepdims=True)
            maybe_xn_ref[0][...] = (y * lax.rsqrt(ms + EPS) * g_ref[...]).astype(BF16)


def ffn_dense(hn, h, w1, w3, w2, next_norm_g, emit_xn, tm=512, tf=512):
    m, d = hn.shape
    f = w1.shape[1]
    row = lambda i, j: (i, 0)
    out_specs = [pl.BlockSpec((tm, d), row)]
    out_shape = [jax.ShapeDtypeStruct((m, d), F32)]
    if emit_xn:
        out_specs.append(pl.BlockSpec((tm, d), row))
        out_shape.append(jax.ShapeDtypeStruct((m, d), BF16))
    return pl.pallas_call(
        _ffn_kernel,
        grid=(m // tm, f // tf),
        in_specs=[pl.BlockSpec((tm, d), row),
                  pl.BlockSpec((tm, d), row),
                  pl.BlockSpec((d, tf), lambda i, j: (0, j)),
                  pl.BlockSpec((d, tf), lambda i, j: (0, j)),
                  pl.BlockSpec((tf, d), lambda i, j: (j, 0)),
                  pl.BlockSpec((1, d), lambda i, j: (0, 0))],
        out_specs=out_specs,
        out_shape=out_shape,
        compiler_params=_cparams(("parallel", "arbitrary"), 52),
        name="ffn_dense",
    )(hn, h, w1, w3, w2, next_norm_g.reshape(1, d))


def _row_copy(src_ref, src_row, dst_ref, dst_row, sem):
    return pltpu.make_async_copy(src_ref.at[pl.ds(src_row, 1)], dst_ref.at[pl.ds(dst_row, 1)], sem)


def _moe_scatter_kernel(dest_ref, hn_ref, xs_init_ref, xs_ref, sem, *, tm, m):
    del xs_init_ref
    base = pl.program_id(0) * tm

    def issue(r, carry):
        for k in range(2):
            _row_copy(hn_ref, r, xs_ref, dest_ref[k * m + base + r], sem).start()
        return carry

    lax.fori_loop(0, tm, issue, 0, unroll=8)
    for k in range(2):
        pltpu.make_async_copy(hn_ref, xs_ref.at[pl.ds(0, tm)], sem).wait()


def moe_scatter(hn_packed, dest, n_rows, tm=512):
    m, c = hn_packed.shape
    xs_init = jnp.zeros((n_rows, c), I32)
    return pl.pallas_call(
        functools.partial(_moe_scatter_kernel, tm=tm, m=m),
        grid_spec=pltpu.PrefetchScalarGridSpec(
            num_scalar_prefetch=1,
            grid=(m // tm,),
            in_specs=[pl.BlockSpec((tm, c), lambda i, dest: (i, 0)),
                      pl.BlockSpec(memory_space=pl.ANY)],
            out_specs=pl.BlockSpec(memory_space=pl.ANY),
            scratch_shapes=[pltpu.SemaphoreType.DMA(())]),
        out_shape=jax.ShapeDtypeStruct((n_rows, c), I32),
        input_output_aliases={2: 0},
        compiler_params=_cparams(("arbitrary",), 32),
        name="moe_scatter",
    )(dest, hn_packed, xs_init)


MOE_ROW_STEP = 128


def _moe_ffn_kernel(te_ref, tr_ref, xs_ref, w1_ref, w3_ref, w2_ref, o_ref, xb_ref):
    del te_ref
    i = pl.program_id(0)
    half = xs_ref.shape[1]
    tm = xs_ref.shape[0]

    @pl.when(pl.program_id(1) == 0)
    def _():
        o_ref[...] = jnp.zeros_like(o_ref)
        lo, hi = _unpack_bf16_pairs(xs_ref[...])
        xb_ref[:, 0:half] = lo
        xb_ref[:, half:2 * half] = hi

    groups = (tr_ref[i] + MOE_ROW_STEP - 1) // MOE_ROW_STEP
    for g in range(1, tm // MOE_ROW_STEP + 1):
        rows = g * MOE_ROW_STEP

        @pl.when(groups == g)
        def _():
            x = xb_ref[0:rows, :]
            a = _dot(x, w1_ref[0])
            b = _dot(x, w3_ref[0])
            act = (a * _sigmoid(a) * b).astype(BF16)
            o_ref[0:rows, :] += _dot(act, w2_ref[0])


def moe_ffn(xs, tile_expert, tile_rows, w1, w3, w2, tm=512, tf=1024):
    p, c = xs.shape
    ne, d, f = w1.shape
    nf = f // tf
    col = lambda i, j, te, tr: jnp.where(tr[i] > 0, j, nf - 1)
    return pl.pallas_call(
        _moe_ffn_kernel,
        grid_spec=pltpu.PrefetchScalarGridSpec(
            num_scalar_prefetch=2,
            grid=(p // tm, nf),
            in_specs=[pl.BlockSpec((tm, c), lambda i, j, te, tr: (i, 0)),
                      pl.BlockSpec((1, d, tf), lambda i, j, te, tr: (te[i], 0, col(i, j, te, tr))),
                      pl.BlockSpec((1, d, tf), lambda i, j, te, tr: (te[i], 0, col(i, j, te, tr))),
                      pl.BlockSpec((1, tf, d), lambda i, j, te, tr: (te[i], col(i, j, te, tr), 0))],
            out_specs=pl.BlockSpec((tm, d), lambda i, j, te, tr: (i, 0)),
            scratch_shapes=[pltpu.VMEM((tm, d), BF16)]),
        out_shape=jax.ShapeDtypeStruct((p, d), F32),
        compiler_params=_cparams(("arbitrary", "arbitrary"), 56),
        name="moe_ffn",
    )(tile_expert, tile_rows, xs, w1, w3, w2)


def _moe_combine_kernel(dest_ref, h_ref, prob_ref, ys_ref, o_ref, ybuf_ref, sem, *, tm, m):
    i = pl.program_id(0)
    slot = i % 2

    def gather_tile(tile, slot_):
        base = tile * tm

        def issue(r, carry):
            for k in range(2):
                _row_copy(ys_ref, dest_ref[k * m + base + r], ybuf_ref.at[slot_, k], r,
                          sem.at[slot_]).start()
            return carry

        lax.fori_loop(0, tm, issue, 0, unroll=8)

    @pl.when(i == 0)
    def _():
        gather_tile(0, 0)

    @pl.when(i + 1 < pl.num_programs(0))
    def _():
        gather_tile(i + 1, 1 - slot)

    for k in range(2):
        pltpu.make_async_copy(ys_ref.at[pl.ds(0, tm)], ybuf_ref.at[slot, k], sem.at[slot]).wait()
    prob = prob_ref[...]
    o_ref[...] = h_ref[...] + prob[:, 0:1] * ybuf_ref[slot, 0] + prob[:, 1:2] * ybuf_ref[slot, 1]


def moe_combine(h, prob, ys, dest, tm=512):
    m, d = h.shape
    return pl.pallas_call(
        functools.partial(_moe_combine_kernel, tm=tm, m=m),
        grid_spec=pltpu.PrefetchScalarGridSpec(
            num_scalar_prefetch=1,
            grid=(m // tm,),
            in_specs=[pl.BlockSpec((tm, d), lambda i, dest: (i, 0)),
                      pl.BlockSpec((tm, LANES), lambda i, dest: (i, 0)),
                      pl.BlockSpec(memory_space=pl.ANY)],
            out_specs=pl.BlockSpec((tm, d), lambda i, dest: (i, 0)),
            scratch_shapes=[pltpu.VMEM((2, 2, tm, d), F32), pltpu.SemaphoreType.DMA((2,))]),
        out_shape=jax.ShapeDtypeStruct((m, d), F32),
        compiler_params=_cparams(("arbitrary",), 48),
        name="moe_combine",
    )(dest, h, prob, ys)


def ffn_moe(h, norm_g, w_router, w1, w3, w2, tm=512):
    m, d = h.shape
    ne = w1.shape[0]
    hn_packed, meta, prob, cnt = rmsnorm_router(h, norm_g, w_router)
    counts = cnt[0, :ne].astype(I32)
    tiles_per = (counts + tm - 1) // tm
    tile_end = jnp.cumsum(tiles_per)
    offset = (tile_end - tiles_per) * tm
    experts = jnp.arange(ne, dtype=I32)

    def group_offset(e):
        return jnp.sum(jnp.where(e[:, None] == experts[None, :], offset[None, :], 0), axis=1)

    dest = jnp.concatenate([group_offset(meta[:, 0]) + meta[:, 2],
                            group_offset(meta[:, 1]) + meta[:, 3]])
    n_tiles = (2 * m) // tm + ne
    tile_ids = jnp.arange(n_tiles, dtype=I32)
    last_valid = jnp.minimum(tile_ids, tile_end[-1] - 1)
    tile_expert = jnp.sum((tile_end[None, :] <= last_valid[:, None]).astype(I32), axis=1)
    mine = tile_expert[:, None] == experts[None, :]
    first_tile = jnp.sum(jnp.where(mine, (tile_end - tiles_per)[None, :], 0), axis=1)
    count = jnp.sum(jnp.where(mine, counts[None, :], 0), axis=1)
    tile_rows = jnp.where(tile_ids < tile_end[-1],
                          jnp.clip(count - (tile_ids - first_tile) * tm, 0, tm), 0).astype(I32)
    xs = moe_scatter(hn_packed, dest, n_tiles * tm)
    ys = moe_ffn(xs, tile_expert, tile_rows, w1, w3, w2, tm=tm)
    return moe_combine(h, prob, ys, dest)


W_IN_HEAD = ATT_W + 2 * KV_W + IDX_Q_RANK
W_IN_SMALL = IDX_HEAD_DIM + IDX_HEADS
W_IN_ROWS = 1024


def _w_in_prep_kernel(wt_hbm, main_ref, idx_ref, buf_ref, sem, *, layer):
    c = pl.program_id(0)

    def fetch(row0, n_rows):
        cp = pltpu.make_async_copy(wt_hbm.at[layer, pl.ds(row0, n_rows)],
                                   buf_ref.at[pl.ds(0, n_rows)], sem)
        cp.start()
        cp.wait()

    @pl.when(c == 0)
    def _():
        fetch(W_IN_HEAD, LANES)
        rows = buf_ref[0:LANES, :]
        keep = lax.broadcasted_iota(I32, rows.shape, 0) < W_IN_SMALL
        idx_ref[...] = jnp.where(keep, rows, 0.0).astype(BF16)

    base = c * W_IN_ROWS
    row0 = pl.multiple_of(jnp.where(base < W_IN_HEAD, base, base + W_IN_SMALL), 8)
    fetch(row0, W_IN_ROWS)
    main_ref[...] = buf_ref[...].astype(BF16)


def _split_w_in(w_in, layer):
    _, n, d = w_in.shape
    assert n == MAIN_W + W_IN_SMALL and W_IN_HEAD % W_IN_ROWS == 0
    return pl.pallas_call(
        functools.partial(_w_in_prep_kernel, layer=layer),
        grid=(MAIN_W // W_IN_ROWS,),
        in_specs=[pl.BlockSpec(memory_space=pl.ANY)],
        out_specs=[pl.BlockSpec((W_IN_ROWS, d), lambda c: (c, 0)),
                   pl.BlockSpec((LANES, d), lambda c: (0, 0))],
        out_shape=[jax.ShapeDtypeStruct((MAIN_W, d), BF16),
                   jax.ShapeDtypeStruct((LANES, d), BF16)],
        scratch_shapes=[pltpu.VMEM((W_IN_ROWS, d), F32), pltpu.SemaphoreType.DMA(())],
        compiler_params=_cparams(("arbitrary",), 40),
        name="w_in_prep",
    )(w_in)


def _mixer(x2, maybe_xn, batch, seq, norm_g, w_in, q_norm_g, k_norm_g, idx_q_norm_g, w_idx_q, ln_g,
           ln_b, lb, hgrn_norm_g, w_up_att, w_up_hgrn, w_o, bias, norm_ffn_g, emit_hn,
           side_cast_att=None, side_cast_rec=None):
    w_main, w_idx = _split_w_in(*w_in)
    if maybe_xn is None:
        proj_main, xn = rmsnorm_matmul(x2, norm_g, w_main, BF16, tn=2048)
    else:
        xn = maybe_xn
        proj_main = matmul(xn, w_main, BF16, tn=2048)
    proj_idx = matmul(xn, w_idx, F32)
    qn, kn, vt, qi, ki, wt = attn_prep(proj_main, proj_idx, q_norm_g, k_norm_g, idx_q_norm_g,
                                       w_idx_q.astype(BF16), ln_g, ln_b, seq)
    vt = vt.reshape(batch, seq // TK, KV_W, TK)
    att, *cast_att = dsa_attention(qi, wt, qn, ki, kn, vt, bias, batch, seq, side_cast_att)
    rec, *cast_rec = hgrn2(proj_main, lb, hgrn_norm_g, batch, seq, side_cast_rec)
    outs = merge_out(att, rec, proj_main, x2, w_up_att.astype(BF16), w_up_hgrn.astype(BF16),
                     w_o.astype(BF16), norm_ffn_g, emit_hn)
    return outs, (cast_att[0] if cast_att else None), (cast_rec[0] if cast_rec else None)


def kernel(x, rel_bias, norm_mix_g, norm_ffn_g, w_in, q_norm_g, k_norm_g, idx_q_norm_g, w_idx_q,
           idx_k_ln_g, idx_k_ln_b, hgrn_lb_logits, hgrn_out_norm_g, w_up_att, w_up_hgrn, w_o,
           w1_dense, w3_dense, w2_dense, w_router, w1_moe, w3_moe, w2_moe):
    batch, seq, d = x.shape
    depth = w_in.shape[0]
    lb_all = jnp.cumsum(jax.nn.softmax(hgrn_lb_logits.astype(F32), axis=0), axis=0)
    lb_all = lb_all - lb_all[0:1]
    bias = bias_tables(rel_bias)
    w_in_t = jnp.swapaxes(w_in, 1, 2)
    x2 = x.reshape(batch * seq, d)
    maybe_xn = None
    n_split_ff = w1_moe.shape[-1] // 1024
    moe_bf16 = {}
    for l in range(depth):
        dense = l % 2 == 0
        j = l // 2
        side_att = side_rec = None
        if dense and l + 1 < depth:
            side_att = (w3_moe[j], 2, n_split_ff)
            side_rec = (w1_moe[j], 2, 2 * n_split_ff)
        elif not dense:
            side_att = (w2_moe[j], 1, n_split_ff)
        (h, *maybe_hn), cast_att, cast_rec = _mixer(
            x2, maybe_xn, batch, seq, norm_mix_g[l], (w_in_t, l), q_norm_g[l], k_norm_g[l],
            idx_q_norm_g[l], w_idx_q[l], idx_k_ln_g[l], idx_k_ln_b[l], lb_all[l],
            hgrn_out_norm_g[l], w_up_att[l], w_up_hgrn[l], w_o[l], bias, norm_ffn_g[l],
            emit_hn=dense, side_cast_att=side_att, side_cast_rec=side_rec)
        maybe_xn = None
        if dense:
            more = l + 1 < depth
            moe_bf16["w3"], moe_bf16["w1"] = cast_att, cast_rec
            x2, *rest = ffn_dense(maybe_hn[0], h, w1_dense[j].astype(BF16), w3_dense[j].astype(BF16),
                                  w2_dense[j].astype(BF16), norm_mix_g[l + 1 if more else l], more)
            maybe_xn = rest[0] if more else None
        else:
            x2 = ffn_moe(h, norm_ffn_g[l], w_router[j], moe_bf16["w1"], moe_bf16["w3"], cast_att)
    return x2.reshape(batch, seq, d)
```
